```python
import jax, jax.numpy as jnp
from jax import lax
import numpy as np

D_MODEL = 2048
BATCH = 16
SEQ = 256
DEPTH = 4
DEC_BATCH = 4
DEC_SEQ = 1024
PAST_LEN = 256

GRID_W = 64
ROPE_BASE = 10000.0
EPS = 1e-6
NEG_INF = -1e30
Q_BLOCK = 128
H_A = 8
DK_A = 128
DV_A = 128
CHUNK = 32
H_B = 8
Q_LORA = 512
KV_LORA = 256
NOPE_B = 128
ROPE_B = 64
V_B = 128
H_C = 8
KVH_C = 2
HD_C = 128
WINDOW = 128
N_BRANCH = 3
D_FF = 5504
CONV_W = 3

IN_SIZES = (H_A * DK_A, H_A * DK_A, H_A * DK_A, H_A * DV_A, H_A * DV_A,
            Q_LORA, KV_LORA, ROPE_B,
            H_C * HD_C, KVH_C * HD_C, KVH_C * HD_C,
            N_BRANCH * D_MODEL)
IN_TOTAL = sum(IN_SIZES)
IN_SPLITS = tuple(sum(IN_SIZES[:i + 1]) for i in range(len(IN_SIZES) - 1))

kernel_name = "hybrid_diffusion_hgrn2_mla_swa_step"


def rms_norm(x, w):
    xf = x.astype(jnp.float32)
    y = xf * lax.rsqrt(jnp.mean(xf * xf, axis=-1, keepdims=True) + EPS)
    return y.astype(x.dtype) * w


def modulation(cvec, w_mod, b_mod):
    return (jax.nn.silu(cvec) @ w_mod + b_mod)[:, None, :]


def axial_rope(x):
    n, d = x.shape[1], x.shape[-1]
    rows = n // GRID_W
    row = jnp.repeat(jnp.arange(rows), GRID_W).astype(jnp.float32)
    col = jnp.tile(jnp.arange(GRID_W), rows).astype(jnp.float32)
    half = d // 2
    inv = ROPE_BASE ** (-jnp.arange(0, half, 2, dtype=jnp.float32) / half)

    def rot(xa, pos):
        ang = pos[:, None] * inv[None, :]
        cos = jnp.cos(ang)[:, None, :].astype(x.dtype)
        sin = jnp.sin(ang)[:, None, :].astype(x.dtype)
        x1, x2 = xa[..., :half // 2], xa[..., half // 2:]
        return jnp.concatenate([x1 * cos - x2 * sin, x1 * sin + x2 * cos], axis=-1)

    return jnp.concatenate([rot(x[..., :half], row), rot(x[..., half:], col)], axis=-1)


def softmax_with_sink(s, sink):
    if sink is None:
        return jax.nn.softmax(s, axis=-1)
    sk = sink.astype(jnp.float32)[:, :, None, None]
    m = jnp.maximum(s.max(axis=-1, keepdims=True), sk)
    p = jnp.exp(s - m)
    return p / (p.sum(axis=-1, keepdims=True) + jnp.exp(sk - m))


def dense_attention(q, k, v, sink):
    b, n, g, r, dk = q.shape
    scale = dk ** -0.5
    qb = jnp.moveaxis(q.reshape(b, n // Q_BLOCK, Q_BLOCK, g, r, dk), 1, 0)

    def one_block(qi):
        s = jnp.einsum('bqgrd,bkgd->bgrqk', qi, k).astype(jnp.float32) * scale
        p = softmax_with_sink(s, sink)
        return jnp.einsum('bgrqk,bkgd->bqgrd', p, v)

    out = lax.map(one_block, qb)
    return jnp.moveaxis(out, 0, 1).reshape(b, n, g, r, v.shape[-1])


def window_attention(q, k, v, k_ctx, v_ctx, sink):
    b, n, g, r, d = q.shape
    w = WINDOW
    nb = n // w
    scale = d ** -0.5
    pad = ((0, 0), (w, w), (0, 0), (0, 0))
    kp, vp = jnp.pad(k, pad), jnp.pad(v, pad)

    def band(a):
        return jnp.concatenate([a[:, o * w:o * w + n].reshape(b, nb, w, g, d) for o in range(3)], axis=2)

    kb, vb = band(kp), band(vp)
    qb = q.reshape(b, nb, w, g, r, d)
    s_loc = jnp.einsum('bnqgrd,bnkgd->bngrqk', qb, kb).astype(jnp.float32) * scale
    qpos = jnp.arange(nb)[:, None, None] * w + jnp.arange(w)[None, :, None]
    kpos = jnp.arange(nb)[:, None, None] * w - w + jnp.arange(3 * w)[None, None, :]
    valid = (jnp.abs(qpos - kpos) <= w) & (kpos >= 0) & (kpos < n)
    s_loc = jnp.where(valid[None, :, None, None], s_loc, NEG_INF)
    s_ctx = jnp.einsum('bnqgrd,bkgd->bngrqk', qb, k_ctx).astype(jnp.float32) * scale
    p = softmax_with_sink(jnp.concatenate([s_loc, s_ctx], axis=-1), sink)
    o = (jnp.einsum('bngrqk,bnkgd->bnqgrd', p[..., :3 * w], vb)
         + jnp.einsum('bngrqk,bkgd->bnqgrd', p[..., 3 * w:], v_ctx))
    return o.reshape(b, n, g, r, d)


def hgrn_lower_bounds(lb_param):
    cs = jnp.cumsum(jax.nn.softmax(lb_param.astype(jnp.float32), axis=0), axis=0)
    return cs - cs[0]


def forget_gate(x_f, lb, b, n):
    xf = x_f.astype(jnp.float32)
    f = lb + (1.0 - lb) * jax.nn.sigmoid(xf)
    k = (1.0 - lb) * jax.nn.sigmoid(-xf)
    return jnp.log(f).reshape(b, n, H_A, DK_A), k.reshape(b, n, H_A, DK_A)


def hgrn_scan(q, k, v, logf, s0):
    b, n, h, _ = q.shape
    nc = n // CHUNK

    def to_chunks(a):
        return jnp.moveaxis(a.astype(jnp.float32).reshape(b, nc, CHUNK, h, a.shape[-1]), 1, 0)

    mask = jnp.tril(jnp.ones((CHUNK, CHUNK), dtype=bool))

    def step(s, xs):
        qc, kc, vc, gc = xs
        g_cum = jnp.cumsum(gc, axis=1)
        o_inter = jnp.einsum('bthd,bhdv->bthv', qc * jnp.exp(g_cum), s)
        diff = g_cum[:, :, None] - g_cum[:, None, :]
        decay = jnp.where(mask[None, :, :, None, None], jnp.exp(jnp.minimum(diff, 0.0)), 0.0)
        a = jnp.einsum('bthd,bshd,btshd->bhts', qc, kc, decay)
        o_intra = jnp.einsum('bhts,bshv->bthv', a, vc)
        g_last = g_cum[:, -1]
        s_new = (jnp.exp(g_last)[..., None] * s
                 + jnp.einsum('bshd,bshv->bhdv', kc * jnp.exp(g_last[:, None] - g_cum), vc))
        return s_new, o_inter + o_intra

    s_fin, o = lax.scan(step, s0.astype(jnp.float32), (to_chunks(q), to_chunks(k), to_chunks(v), to_chunks(logf)))
    o = jnp.moveaxis(o, 0, 1).reshape(b, n, h, v.shape[-1])
    return o.astype(q.dtype), s_fin


def mla_kv(ckv, krope, w_ukv):
    b, l, _ = ckv.shape
    kv = (ckv @ w_ukv).reshape(b, l, H_B, NOPE_B + V_B)
    k = jnp.concatenate([kv[..., :NOPE_B], jnp.broadcast_to(krope[:, :, None, :], (b, l, H_B, ROPE_B))], axis=-1)
    return k, kv[..., NOPE_B:]


def token_mixer(h, lw, ctx):
    (w_in, lb, g_norm, g_q, w_uq, g_kv, w_ukv, sink, w_a, w_b, w_c, w_o) = lw
    latent = ctx is not None
    b, n, _ = h.shape
    z = h @ w_in
    (aq, af_fwd, af_bwd, ai, ag, bq, bkv, bkr, cq, ck, cv, gts) = jnp.split(z, IN_SPLITS, axis=-1)

    q_a = aq.reshape(b, n, H_A, DK_A)
    v_a = ai.reshape(b, n, H_A, DV_A)
    logf_f, k_f = forget_gate(af_fwd, lb[0], b, n)
    logf_b, k_b = forget_gate(af_bwd, lb[1], b, n)
    if latent:
        s0_f, s0_b = ctx[0][:, 0], ctx[0][:, 1]
    else:
        s0_f = jnp.zeros((b, H_A, DK_A, DV_A), jnp.float32)
        s0_b = s0_f
    o_f, s_f = hgrn_scan(q_a, k_f, v_a, logf_f, s0_f)
    o_b, s_b = hgrn_scan(q_a[:, ::-1], k_b[:, ::-1], v_a[:, ::-1], logf_b[:, ::-1], s0_b)
    o_a = rms_norm(o_f + o_b[:, ::-1], g_norm).reshape(b, n, H_A * DV_A) * jax.nn.silu(ag)

    q_b = (rms_norm(bq, g_q) @ w_uq).reshape(b, n, H_B, NOPE_B + ROPE_B)
    q_nope, q_rope = q_b[..., :NOPE_B], q_b[..., NOPE_B:]
    ckv = rms_norm(bkv, g_kv)
    krope = bkr
    if latent:
        q_rope = axial_rope(q_rope)
        k_lat, v_lat = mla_kv(ckv, axial_rope(krope[:, :, None])[:, :, 0], w_ukv)
        k_cx, v_cx = mla_kv(ctx[1], ctx[2], w_ukv)
        k_all = jnp.concatenate([k_lat, k_cx], axis=1)
        v_all = jnp.concatenate([v_lat, v_cx], axis=1)
    else:
        k_all, v_all = mla_kv(ckv, krope, w_ukv)
    q_full = jnp.concatenate([q_nope, q_rope], axis=-1)[:, :, :, None]
    o_b = dense_attention(q_full, k_all, v_all, None).reshape(b, n, H_B * V_B)

    q_c = cq.reshape(b, n, H_C, HD_C)
    k_c = ck.reshape(b, n, KVH_C, HD_C)
    v_c = cv.reshape(b, n, KVH_C, HD_C)
    sink_c = sink.reshape(KVH_C, H_C // KVH_C)
    if latent:
        q_c = axial_rope(q_c).reshape(b, n, KVH_C, H_C // KVH_C, HD_C)
        o_c = window_attention(q_c, axial_rope(k_c), v_c, ctx[3], ctx[4], sink_c)
    else:
        o_c = dense_attention(q_c.reshape(b, n, KVH_C, H_C // KVH_C, HD_C), k_c, v_c, sink_c)
    o_c = o_c.reshape(b, n, H_C * HD_C)

    gates = jax.nn.sigmoid(gts.reshape(b, n, N_BRANCH, D_MODEL))
    merged = gates[:, :, 0] * (o_a @ w_a) + gates[:, :, 1] * (o_b @ w_b) + gates[:, :, 2] * (o_c @ w_c)
    out = merged @ w_o
    new_ctx = None if latent else (jnp.stack([s_f, s_b], axis=1), ckv, krope, k_c, v_c)
    return out, new_ctx


def conv_ffn(h, w_up, w_conv, w_down):
    n = h.shape[1]
    u = h @ w_up
    half = CONV_W // 2
    up = jnp.pad(u, ((0, 0), (half, half), (0, 0)))
    u = sum(w_conv[j] * up[:, j:j + n] for j in range(CONV_W))
    a, g = jnp.split(u, 2, axis=-1)
    return (a * jax.nn.gelu(g)) @ w_down


def trunk_layer(x, mod, norms, lw, fw, ctx):
    n_pre_a, n_post_a, n_pre_f, n_post_f = norms
    shift1, scale1, gate1, shift2, scale2, gate2 = jnp.split(mod, 6, axis=-1)
    h = rms_norm(x, n_pre_a) * (1.0 + scale1) + shift1
    y, new_ctx = token_mixer(h, lw, ctx)
    x = x + gate1 * rms_norm(y, n_post_a)
    h = rms_norm(x, n_pre_f) * (1.0 + scale2) + shift2
    x = x + gate2 * rms_norm(conv_ffn(h, *fw), n_post_f)
    return x, new_ctx


def setup_inputs(seed: int = 0) -> dict:
    key = jax.random.key(seed)
    ks = iter(jax.random.split(key, 40))

    def nrm(shape, scale):
        return jax.random.normal(next(ks), shape, jnp.float32) * scale

    def gain(shape):
        return 1.0 + nrm(shape, 0.01)

    d = D_MODEL
    return {
        "x_prompt": nrm((BATCH, SEQ, d), 1.0),
        "x_sample": nrm((DEC_BATCH, DEC_SEQ, d), 1.0),
        "state_hgrn": nrm((DEC_BATCH, DEPTH, 2, H_A, DK_A, DV_A), 0.5),
        "cache_mla_ckv": nrm((DEC_BATCH, DEPTH, PAST_LEN, KV_LORA), 1.0),
        "cache_mla_krope": nrm((DEC_BATCH, DEPTH, PAST_LEN, ROPE_B), 1.0),
        "cache_swa_k": nrm((DEC_BATCH, DEPTH, PAST_LEN, KVH_C, HD_C), 1.0),
        "cache_swa_v": nrm((DEC_BATCH, DEPTH, PAST_LEN, KVH_C, HD_C), 1.0),
        "c": nrm((DEC_BATCH, d), 1.0),
        "c_ctx": nrm((d,), 1.0),
        "w_mod": nrm((DEPTH, d, 6 * d), 0.5 * d ** -0.5),
        "b_mod": nrm((DEPTH, 6 * d), 0.01),
        "norm_pre_attn": gain((DEPTH, d)),
        "norm_post_attn": gain((DEPTH, d)),
        "norm_pre_ffn": gain((DEPTH, d)),
        "norm_post_ffn": gain((DEPTH, d)),
        "w_in": nrm((DEPTH, d, IN_TOTAL), d ** -0.5),
        "hgrn_lb": nrm((DEPTH, 2, H_A * DK_A), 0.5),
        "hgrn_gnorm": gain((DEPTH, DV_A)),
        "mla_gq": gain((DEPTH, Q_LORA)),
        "mla_w_uq": nrm((DEPTH, Q_LORA, H_B * (NOPE_B + ROPE_B)), Q_LORA ** -0.5),
        "mla_gkv": gain((DEPTH, KV_LORA)),
        "mla_w_ukv": nrm((DEPTH, KV_LORA, H_B * (NOPE_B + V_B)), KV_LORA ** -0.5),
        "swa_sink": nrm((DEPTH, H_C), 1.0),
        "w_branch_a": nrm((DEPTH, H_A * DV_A, d), (H_A * DV_A) ** -0.5),
        "w_branch_b": nrm((DEPTH, H_B * V_B, d), (H_B * V_B) ** -0.5),
        "w_branch_c": nrm((DEPTH, H_C * HD_C, d), (H_C * HD_C) ** -0.5),
        "w_out": nrm((DEPTH, d, d), d ** -0.5),
        "ffn_w_up": nrm((DEPTH, d, 2 * D_FF), d ** -0.5),
        "ffn_conv": nrm((DEPTH, CONV_W, 2 * D_FF), CONV_W ** -0.5),
        "ffn_w_down": nrm((DEPTH, D_FF, d), D_FF ** -0.5),
    }


def reference(x_prompt, x_sample, state_hgrn, cache_mla_ckv, cache_mla_krope, cache_swa_k, cache_swa_v,
              c, c_ctx, w_mod, b_mod, norm_pre_attn, norm_post_attn, norm_pre_ffn, norm_post_ffn,
              w_in, hgrn_lb, hgrn_gnorm, mla_gq, mla_w_uq, mla_gkv, mla_w_ukv, swa_sink,
              w_branch_a, w_branch_b, w_branch_c, w_out, ffn_w_up, ffn_conv, ffn_w_down):
    lb_all = hgrn_lower_bounds(hgrn_lb)
    xp, xs = x_prompt, x_sample
    new_hgrn, new_ckv, new_krope, new_k, new_v = [], [], [], [], []
    for l in range(DEPTH):
        lw = (w_in[l], lb_all[l], hgrn_gnorm[l], mla_gq[l], mla_w_uq[l], mla_gkv[l], mla_w_ukv[l],
              swa_sink[l], w_branch_a[l], w_branch_b[l], w_branch_c[l], w_out[l])
        fw = (ffn_w_up[l], ffn_conv[l], ffn_w_down[l])
        norms = (norm_pre_attn[l], norm_post_attn[l], norm_pre_ffn[l], norm_post_ffn[l])
        mod_ctx = modulation(c_ctx[None, :], w_mod[l], b_mod[l])
        xp, ctx_l = trunk_layer(xp, mod_ctx, norms, lw, fw, None)
        new_hgrn.append(ctx_l[0])
        new_ckv.append(ctx_l[1])
        new_krope.append(ctx_l[2])
        new_k.append(ctx_l[3])
        new_v.append(ctx_l[4])
        mod_lat = modulation(c, w_mod[l], b_mod[l])
        cache_l = (state_hgrn[:, l], cache_mla_ckv[:, l], cache_mla_krope[:, l], cache_swa_k[:, l], cache_swa_v[:, l])
        xs, _ = trunk_layer(xs, mod_lat, norms, lw, fw, cache_l)
    return (xp, xs, jnp.stack(new_hgrn, axis=1), jnp.stack(new_ckv, axis=1), jnp.stack(new_krope, axis=1),
            jnp.stack(new_k, axis=1), jnp.stack(new_v, axis=1))
```

```python
import functools

import jax
import jax.numpy as jnp
from jax import lax
from jax.experimental import pallas as pl
from jax.experimental.pallas import tpu as pltpu

F32 = jnp.float32
BF16 = jnp.bfloat16

D_MODEL = 2048
BATCH = 16
SEQ = 256
DEPTH = 4
DEC_BATCH = 4
DEC_SEQ = 1024
PAST_LEN = 256
GRID_W = 64
ROPE_BASE = 10000.0
EPS = 1e-6
NEG_INF = -1e30
H_A, DK_A, DV_A = 8, 128, 128
H_B, Q_LORA, KV_LORA, NOPE_B, ROPE_B, V_B = 8, 512, 256, 128, 64, 128
H_C, KVH_C, HD_C, WINDOW = 8, 2, 128, 128
N_BRANCH = 3
D_FF = 5504
CONV_W = 3

T_CTX = BATCH * SEQ
T_LAT = DEC_BATCH * DEC_SEQ
T_ALL = T_CTX + T_LAT
MOD_ROWS = 8
LANE = 128
D_FF_PAD = 5632
HGRN_CHUNK = 16
VMEM_LIMIT = 56 * 1024 * 1024


def _params(*sem):
    return pltpu.CompilerParams(dimension_semantics=sem, vmem_limit_bytes=VMEM_LIMIT)


def _mod_row(i, tm):
    return jnp.where(i * tm < T_CTX, 0, 1 + (i * tm - T_CTX) // DEC_SEQ)


def _rms(x):
    return x * lax.rsqrt(jnp.mean(x * x, axis=-1, keepdims=True) + EPS)


def _mod_kernel(c_ref, w_ref, b_ref, o_ref):
    cv = c_ref[...]
    s = (cv * jax.nn.sigmoid(cv)).astype(BF16)
    o_ref[0] = jnp.dot(s, w_ref[0].astype(BF16), preferred_element_type=F32) + b_ref[0]


def _modulation(cvec, w_mod, b_mod):
    tn = 1024
    n = 6 * D_MODEL
    return pl.pallas_call(
        _mod_kernel,
        grid=(DEPTH, n // tn),
        in_specs=[pl.BlockSpec((MOD_ROWS, D_MODEL), lambda l, j: (0, 0)),
                  pl.BlockSpec((1, D_MODEL, tn), lambda l, j: (l, 0, j)),
                  pl.BlockSpec((1, 1, tn), lambda l, j: (l, 0, j))],
        out_specs=pl.BlockSpec((1, MOD_ROWS, tn), lambda l, j: (l, 0, j)),
        out_shape=jax.ShapeDtypeStruct((DEPTH, MOD_ROWS, n), F32),
        compiler_params=_params("parallel", "parallel"),
        name="modulation",
    )(cvec, w_mod, b_mod.reshape(DEPTH, 1, n))


def _norm_kernel(*refs, has_y, has_h):
    it = iter(refs)
    x_ref = next(it)
    if has_y:
        y_ref, gate_ref, npost_ref = next(it), next(it), next(it)
    if has_h:
        npre_ref, scale_ref, shift_ref = next(it), next(it), next(it)
    x = x_ref[...]
    if has_y:
        xnew_ref = next(it)
        x = x + gate_ref[0, 0] * (_rms(y_ref[...]) * npost_ref[...])
        xnew_ref[...] = x
    if has_h:
        h_ref = next(it)
        h = (_rms(x) * npre_ref[...]) * (1.0 + scale_ref[0, 0]) + shift_ref[0, 0]
        h_ref[...] = h.astype(BF16)


def _norm(x, *, y=None, mod_post=None, gate_idx=None, npost=None,
          npre=None, mod_pre=None, scale_idx=None, shift_idx=None):
    tm = 512
    has_y, has_h = y is not None, npre is not None
    row = pl.BlockSpec((tm, D_MODEL), lambda i: (i, 0))
    vec = pl.BlockSpec((1, D_MODEL), lambda i: (0, 0))

    def modspec(k):
        return pl.BlockSpec((1, 1, 1, D_MODEL), lambda i: (_mod_row(i, tm), k, 0, 0))

    args, specs, out_shape, out_specs = [x], [row], [], []
    if has_y:
        args += [y, mod_post, npost.reshape(1, D_MODEL)]
        specs += [row, modspec(gate_idx), vec]
        out_shape.append(jax.ShapeDtypeStruct((T_ALL, D_MODEL), F32))
        out_specs.append(row)
    if has_h:
        args += [npre.reshape(1, D_MODEL), mod_pre, mod_pre]
        specs += [vec, modspec(scale_idx), modspec(shift_idx)]
        out_shape.append(jax.ShapeDtypeStruct((T_ALL, D_MODEL), BF16))
        out_specs.append(row)
    outs = pl.pallas_call(
        functools.partial(_norm_kernel, has_y=has_y, has_h=has_h),
        grid=(T_ALL // tm,),
        in_specs=specs, out_specs=out_specs, out_shape=out_shape,
        compiler_params=_params("parallel"),
        name="norm_y%d_h%d" % (has_y, has_h),
    )(*args)
    return outs


def _mm_kernel(x_ref, w_ref, o_ref):
    o_ref[...] = jnp.dot(x_ref[...].astype(BF16), w_ref[...],
                         preferred_element_type=F32).astype(o_ref.dtype)


def _mm(x, w, out_dtype, name):
    m, k = x.shape
    n = w.shape[1]
    tm = min(m, 1024)
    tn = min(n, 512)
    return pl.pallas_call(
        _mm_kernel,
        grid=(m // tm, n // tn),
        in_specs=[pl.BlockSpec((tm, k), lambda i, j: (i, 0)),
                  pl.BlockSpec((k, tn), lambda i, j: (0, j))],
        out_specs=pl.BlockSpec((tm, tn), lambda i, j: (i, j)),
        out_shape=jax.ShapeDtypeStruct((m, n), out_dtype),
        compiler_params=_params("parallel", "parallel"),
        name=name,
    )(x, w)


def _norm_mm_kernel(x_ref, g_ref, w_ref, *out_refs, emit_normed):
    xn = _rms(x_ref[...]) * g_ref[...]
    if emit_normed:
        out_refs[0][...] = xn
    out_refs[-1][...] = jnp.dot(xn.astype(BF16), w_ref[...],
                                preferred_element_type=F32).astype(out_refs[-1].dtype)


def _norm_mm(z, col_block, k, gain, w, out_dtype, *, emit_normed, name):
    m = z.shape[0]
    n = w.shape[1]
    tm = 512
    out_shape = [jax.ShapeDtypeStruct((m, n), out_dtype)]
    out_specs = [pl.BlockSpec((tm, n), lambda i: (i, 0))]
    if emit_normed:
        out_shape.insert(0, jax.ShapeDtypeStruct((m, k), F32))
        out_specs.insert(0, pl.BlockSpec((tm, k), lambda i: (i, 0)))
    return pl.pallas_call(
        functools.partial(_norm_mm_kernel, emit_normed=emit_normed),
        grid=(m // tm,),
        in_specs=[pl.BlockSpec((tm, k), lambda i: (i, col_block)),
                  pl.BlockSpec((1, k), lambda i: (0, 0)),
                  pl.BlockSpec((k, n), lambda i: (0, 0))],
        out_specs=out_specs, out_shape=out_shape,
        compiler_params=_params("parallel"),
        name=name,
    )(z, gain.reshape(1, k), w)


def _rope_tables(head_dim):
    n = DEC_SEQ
    half = head_dim // 2
    quarter = half // 2
    row = jnp.repeat(jnp.arange(n // GRID_W), GRID_W).astype(F32)
    col = jnp.tile(jnp.arange(GRID_W), n // GRID_W).astype(F32)
    inv = ROPE_BASE ** (-jnp.arange(0, half, 2, dtype=F32) / half)
    lane = jnp.arange(LANE)
    m = lane % half
    pos = jnp.where((lane // half)[None, :] == 0, row[:, None], col[:, None])
    ang = pos * inv[m % quarter][None, :]
    valid = (lane < head_dim)[None, :]
    cos = jnp.where(valid, jnp.cos(ang), 0.0)
    sin = jnp.where(valid, jnp.where(m < quarter, -1.0, 1.0)[None, :] * jnp.sin(ang), 0.0)
    return cos.astype(F32), sin.astype(F32)


def _rope(x, cos, sin, head_dim):
    quarter = head_dim // 4
    lane = lax.broadcasted_iota(jnp.int32, x.shape, 1)
    first = (lane % (2 * quarter)) < quarter
    partner = jnp.where(first, pltpu.roll(x, LANE - quarter, 1), pltpu.roll(x, quarter, 1))
    return x * cos + partner * sin


def _mla_attn_kernel(*refs, latent):
    if latent:
        q_ref, kv_ref, kr_ref, kvc_ref, krc_ref, cq_ref, sq_ref, ck_ref, sk_ref, o_ref = refs
    else:
        q_ref, kv_ref, kr_ref, o_ref = refs
    scale = (NOPE_B + ROPE_B) ** -0.5
    nt = (((1,), (1,)), ((), ()))
    kr = kr_ref[...]
    if latent:
        kr = _rope(kr, ck_ref[...], sk_ref[...], ROPE_B)
        krc = krc_ref[...].astype(BF16)
    kr = kr.astype(BF16)
    for h in range(H_B):
        c0 = h * 2 * LANE
        qn = q_ref[:, c0:c0 + LANE]
        qr = q_ref[:, c0 + LANE:c0 + 2 * LANE]
        if latent:
            qr = _rope(qr, cq_ref[...], sq_ref[...], ROPE_B)
        qh = jnp.concatenate([qn.astype(BF16), qr.astype(BF16)], axis=-1)
        kh = jnp.concatenate([kv_ref[:, c0:c0 + LANE], kr], axis=-1)
        vh = kv_ref[:, c0 + LANE:c0 + 2 * LANE]
        s = lax.dot_general(qh, kh, nt, preferred_element_type=F32) * scale
        m = jnp.max(s, axis=-1, keepdims=True)
        if latent:
            khc = jnp.concatenate([kvc_ref[:, c0:c0 + LANE], krc], axis=-1)
            vhc = kvc_ref[:, c0 + LANE:c0 + 2 * LANE]
            sc = lax.dot_general(qh, khc, nt, preferred_element_type=F32) * scale
            m = jnp.maximum(m, jnp.max(sc, axis=-1, keepdims=True))
        p = jnp.exp(s - m)
        l = jnp.sum(p, axis=-1, keepdims=True)
        o = jnp.dot(p.astype(BF16), vh, preferred_element_type=F32)
        if latent:
            pc = jnp.exp(sc - m)
            l = l + jnp.sum(pc, axis=-1, keepdims=True)
            o = o + jnp.dot(pc.astype(BF16), vhc, preferred_element_type=F32)
        o_ref[:, h * LANE:(h + 1) * LANE] = (o / l).astype(BF16)


def _mla_attn(q, kv, zb, *, latent, kvc=None, krc=None, tables=None):
    nb, n = (DEC_BATCH, DEC_SEQ) if latent else (BATCH, SEQ)
    tq = 256
    nq = n // tq
    off = T_CTX // n if latent else 0
    offq = T_CTX // tq if latent else 0
    w = H_B * 2 * LANE
    specs = [pl.BlockSpec((tq, w), lambda b, i: (offq + b * nq + i, 0)),
             pl.BlockSpec((n, w), lambda b, i: (off + b, 0)),
             pl.BlockSpec((n, LANE), lambda b, i: (off + b, 6))]
    args = [q, kv, zb]
    if latent:
        cos, sin = tables
        specs += [pl.BlockSpec((PAST_LEN, w), lambda b, i: (b, 0)),
                  pl.BlockSpec((PAST_LEN, LANE), lambda b, i: (b, 0)),
                  pl.BlockSpec((tq, LANE), lambda b, i: (i, 0)),
                  pl.BlockSpec((tq, LANE), lambda b, i: (i, 0)),
                  pl.BlockSpec((n, LANE), lambda b, i: (0, 0)),
                  pl.BlockSpec((n, LANE), lambda b, i: (0, 0))]
        args += [kvc, krc, cos, sin, cos, sin]
    return pl.pallas_call(
        functools.partial(_mla_attn_kernel, latent=latent),
        grid=(nb, nq),
        in_specs=specs,
        out_specs=pl.BlockSpec((tq, H_B * V_B), lambda b, i: (b * nq + i, 0)),
        out_shape=jax.ShapeDtypeStruct((nb * n, H_B * V_B), BF16),
        compiler_params=_params("parallel", "parallel"),
        name="mla_attn_lat" if latent else "mla_attn_ctx",
    )(*args)


def _gqa_attn_kernel(*refs, latent, tq):
    if latent:
        q_ref, k_ref, v_ref, kc_ref, vc_ref, sink_ref, cq_ref, sq_ref, ck_ref, sk_ref, o_ref = refs
    else:
        q_ref, k_ref, v_ref, sink_ref, o_ref = refs
    scale = HD_C ** -0.5
    nt = (((1,), (1,)), ((), ()))
    rep = H_C // KVH_C
    n = k_ref.shape[0]
    if latent:
        qpos = pl.program_id(1) * tq + lax.broadcasted_iota(jnp.int32, (tq, n), 0)
        kpos = lax.broadcasted_iota(jnp.int32, (tq, n), 1)
        band = jnp.abs(qpos - kpos) <= WINDOW
    for g in range(KVH_C):
        kg = k_ref[:, g * LANE:(g + 1) * LANE]
        if latent:
            kg = _rope(kg, ck_ref[...], sk_ref[...], HD_C)
            kcg = kc_ref[:, g * LANE:(g + 1) * LANE].astype(BF16)
            vcg = vc_ref[:, g * LANE:(g + 1) * LANE].astype(BF16)
        kg = kg.astype(BF16)
        vg = v_ref[:, g * LANE:(g + 1) * LANE].astype(BF16)
        for r in range(rep):
            h = g * rep + r
            qh = q_ref[:, h * LANE:(h + 1) * LANE]
            if latent:
                qh = _rope(qh, cq_ref[...], sq_ref[...], HD_C)
            qh = qh.astype(BF16)
            sk = sink_ref[h:h + 1, 0:1]
            s = lax.dot_general(qh, kg, nt, preferred_element_type=F32) * scale
            if latent:
                s = jnp.where(band, s, NEG_INF)
            m = jnp.maximum(jnp.max(s, axis=-1, keepdims=True), sk)
            if latent:
                sc = lax.dot_general(qh, kcg, nt, preferred_element_type=F32) * scale
                m = jnp.maximum(m, jnp.max(sc, axis=-1, keepdims=True))
            p = jnp.exp(s - m)
            l = jnp.sum(p, axis=-1, keepdims=True) + jnp.exp(sk - m)
            o = jnp.dot(p.astype(BF16), vg, preferred_element_type=F32)
            if latent:
                pc = jnp.exp(sc - m)
                l = l + jnp.sum(pc, axis=-1, keepdims=True)
                o = o + jnp.dot(pc.astype(BF16), vcg, preferred_element_type=F32)
            o_ref[:, h * LANE:(h + 1) * LANE] = (o / l).astype(BF16)


def _gqa_attn(zc, sink_b, *, latent, kc=None, vc=None, tables=None):
    nb, n = (DEC_BATCH, DEC_SEQ) if latent else (BATCH, SEQ)
    tq = 256
    nq = n // tq
    off = T_CTX // n if latent else 0
    offq = T_CTX // tq if latent else 0
    wq, wk = H_C * HD_C, KVH_C * HD_C
    specs = [pl.BlockSpec((tq, wq), lambda b, i: (offq + b * nq + i, 0)),
             pl.BlockSpec((n, wk), lambda b, i: (off + b, wq // wk)),
             pl.BlockSpec((n, wk), lambda b, i: (off + b, wq // wk + 1))]
    args = [zc, zc, zc]
    if latent:
        specs += [pl.BlockSpec((PAST_LEN, wk), lambda b, i: (b, 0)),
                  pl.BlockSpec((PAST_LEN, wk), lambda b, i: (b, 0))]
        args += [kc, vc]
    specs.append(pl.BlockSpec((H_C, LANE), lambda b, i: (0, 0)))
    args.append(sink_b)
    if latent:
        cos, sin = tables
        specs += [pl.BlockSpec((tq, LANE), lambda b, i: (i, 0)),
                  pl.BlockSpec((tq, LANE), lambda b, i: (i, 0)),
                  pl.BlockSpec((n, LANE), lambda b, i: (0, 0)),
                  pl.BlockSpec((n, LANE), lambda b, i: (0, 0))]
        args += [cos, sin, cos, sin]
    return pl.pallas_call(
        functools.partial(_gqa_attn_kernel, latent=latent, tq=tq),
        grid=(nb, nq),
        in_specs=specs,
        out_specs=pl.BlockSpec((tq, wq), lambda b, i: (b * nq + i, 0)),
        out_shape=jax.ShapeDtypeStruct((nb * n, wq), BF16),
        compiler_params=_params("parallel", "parallel"),
        name="gqa_attn_lat" if latent else "gqa_attn_ctx",
    )(*args)


def _hgrn_kernel(*refs, n, has_s0, emit_state):
    it = iter(refs)
    q_ref, xf_ref, xb_ref, v_ref, ag_ref, lb_ref, gn_ref = (next(it) for _ in range(7))
    s0_ref = next(it) if has_s0 else None
    o_ref = next(it)
    sfin_ref = next(it) if emit_state else None
    of_scr, ob_scr, gf_scr, kf_scr, gb_scr, kb_scr, stf_scr, stb_scr = (next(it) for _ in range(8))

    c = HGRN_CHUNK
    nc = n // c
    nt = (((1,), (1,)), ((), ()))
    tn = (((0,), (0,)), ((), ()))

    for d, (x_ref, g_scr, k_scr) in enumerate(((xf_ref, gf_scr, kf_scr), (xb_ref, gb_scr, kb_scr))):
        x = x_ref[...]
        lb = lb_ref[d:d + 1, :]
        g_scr[...] = jnp.log(lb + (1.0 - lb) * jax.nn.sigmoid(x))
        k_scr[...] = (1.0 - lb) * jax.nn.sigmoid(-x)

    if has_s0:
        stf_scr[...] = s0_ref[0, 0, 0].T
        stb_scr[...] = s0_ref[0, 1, 0].T
    else:
        stf_scr[...] = jnp.zeros((DV_A, DK_A), F32)
        stb_scr[...] = jnp.zeros((DV_A, DK_A), F32)

    ti = lax.broadcasted_iota(jnp.int32, (c, c), 0)
    si = lax.broadcasted_iota(jnp.int32, (c, c), 1)
    tril = (si <= ti).astype(F32)
    triu = (si >= ti).astype(F32)
    trow = lax.broadcasted_iota(jnp.int32, (c, 1), 0)

    def chunk(r0, g_scr, k_scr, st_scr, o_scr, forward):
        rows = pl.ds(r0, c)
        g = g_scr[rows, :]
        k = k_scr[rows, :]
        q = q_ref[rows, :]
        v = v_ref[rows, :]
        gc = jnp.dot(tril if forward else triu, g, precision=lax.Precision.HIGHEST,
                     preferred_element_type=F32)
        g_end = gc[c - 1:c, :] if forward else gc[0:1, :]
        st = st_scr[...]
        o = lax.dot_general((q * jnp.exp(gc)).astype(BF16), st.astype(BF16), nt,
                            preferred_element_type=F32)
        for s in range(c):
            decay = jnp.exp(jnp.minimum(gc - gc[s:s + 1, :], 0.0))
            a = jnp.sum(q * k[s:s + 1, :] * decay, axis=-1, keepdims=True)
            a = jnp.where((trow >= s) if forward else (trow <= s), a, 0.0)
            o = o + a * v[s:s + 1, :]
        o_scr[rows, :] = o
        kd = k * jnp.exp(g_end - gc)
        st_scr[...] = st * jnp.exp(g_end) + lax.dot_general(
            v.astype(BF16), kd.astype(BF16), tn, preferred_element_type=F32)

    def body(i, carry):
        chunk(pl.multiple_of(i * c, c), gf_scr, kf_scr, stf_scr, of_scr, True)
        chunk(pl.multiple_of((nc - 1 - i) * c, c), gb_scr, kb_scr, stb_scr, ob_scr, False)
        return carry

    lax.fori_loop(0, nc, body, 0)

    o = of_scr[...] + ob_scr[...]
    ag = ag_ref[...]
    o_ref[...] = (_rms(o) * gn_ref[...] * (ag * jax.nn.sigmoid(ag))).astype(BF16)
    if emit_state:
        sfin_ref[0, 0, 0] = stf_scr[...].T
        sfin_ref[0, 1, 0] = stb_scr[...].T


def _hgrn(za, lb_l, gnorm, *, latent, s0=None):
    nb, n = (DEC_BATCH, DEC_SEQ) if latent else (BATCH, SEQ)
    off = T_CTX // n if latent else 0
    emit_state = not latent

    def zspec(k):
        return pl.BlockSpec((n, LANE), lambda b, h: (off + b, k * H_A + h))

    specs = [zspec(0), zspec(1), zspec(2), zspec(3), zspec(4),
             pl.BlockSpec((2, DK_A), lambda b, h: (0, h)),
             pl.BlockSpec((1, DV_A), lambda b, h: (0, 0))]
    args = [za, za, za, za, za, lb_l, gnorm.reshape(1, DV_A)]
    if latent:
        specs.append(pl.BlockSpec((1, 2, 1, DK_A, DV_A), lambda b, h: (b, 0, h, 0, 0)))
        args.append(s0)
    out_shape = [jax.ShapeDtypeStruct((nb * n, H_A * DV_A), BF16)]
    out_specs = [pl.BlockSpec((n, DV_A), lambda b, h: (b, h))]
    if emit_state:
        out_shape.append(jax.ShapeDtypeStruct((nb, 2, H_A, DK_A, DV_A), F32))
        out_specs.append(pl.BlockSpec((1, 2, 1, DK_A, DV_A), lambda b, h: (b, 0, h, 0, 0)))
    seq = pltpu.VMEM((n, LANE), F32)
    st = pltpu.VMEM((DV_A, DK_A), F32)
    return pl.pallas_call(
        functools.partial(_hgrn_kernel, n=n, has_s0=latent, emit_state=emit_state),
        grid=(nb, H_A),
        in_specs=specs, out_specs=out_specs, out_shape=out_shape,
        scratch_shapes=[seq, seq, seq, seq, seq, seq, st, st],
        compiler_params=_params("parallel", "parallel"),
        name="hgrn_lat" if latent else "hgrn_ctx",
    )(*args)


def _merge_kernel(oa_ref, ob_ref, oc_ref, wa_ref, wb_ref, wc_ref, g0_ref, g1_ref, g2_ref, wo_ref, y_ref):
    j = pl.program_id(1)
    merged = (jax.nn.sigmoid(g0_ref[...]) * jnp.dot(oa_ref[...], wa_ref[...], preferred_element_type=F32)
              + jax.nn.sigmoid(g1_ref[...]) * jnp.dot(ob_ref[...], wb_ref[...], preferred_element_type=F32)
              + jax.nn.sigmoid(g2_ref[...]) * jnp.dot(oc_ref[...], wc_ref[...], preferred_element_type=F32))
    part = jnp.dot(merged.astype(BF16), wo_ref[...], preferred_element_type=F32)

    @pl.when(j == 0)
    def _():
        y_ref[...] = part

    @pl.when(j > 0)
    def _():
        y_ref[...] += part


def _merge(oa, ob, oc, wa, wb, wc, zg, wo):
    tm, tn = 512, 256
    nj = D_MODEL // tn
    kb = H_A * DV_A
    o_spec = pl.BlockSpec((tm, kb), lambda i, j: (i, 0))
    w_spec = pl.BlockSpec((kb, tn), lambda i, j: (0, j))

    def gspec(k):
        return pl.BlockSpec((tm, tn), lambda i, j: (i, k * nj + j))

    return pl.pallas_call(
        _merge_kernel,
        grid=(T_ALL // tm, nj),
        in_specs=[o_spec, o_spec, o_spec, w_spec, w_spec, w_spec, gspec(0), gspec(1), gspec(2),
                  pl.BlockSpec((tn, D_MODEL), lambda i, j: (j, 0))],
        out_specs=pl.BlockSpec((tm, D_MODEL), lambda i, j: (i, 0)),
        out_shape=jax.ShapeDtypeStruct((T_ALL, D_MODEL), F32),
        compiler_params=_params("parallel", "arbitrary"),
        name="merge_out",
    )(oa, ob, oc, wa, wb, wc, zg, zg, zg, wo)


def _ffn_kernel(h_ref, wa_ref, wg_ref, ca_ref, cg_ref, wd_ref, y_ref, *, tm):
    i = pl.program_id(0)
    j = pl.program_id(1)
    h = h_ref[...]
    seq_len = jnp.where(i * tm < T_CTX, SEQ, DEC_SEQ)
    pos = lax.broadcasted_iota(jnp.int32, (tm, 1), 0) & (seq_len - 1)
    has_prev = pos != 0
    has_next = pos != seq_len - 1

    def conv(u, c_ref):
        prev = jnp.where(has_prev, pltpu.roll(u, 1, 0), 0.0)
        nxt = jnp.where(has_next, pltpu.roll(u, tm - 1, 0), 0.0)
        return c_ref[0:1, :] * prev + c_ref[1:2, :] * u + c_ref[2:3, :] * nxt

    a = conv(jnp.dot(h, wa_ref[...], preferred_element_type=F32), ca_ref)
    g = conv(jnp.dot(h, wg_ref[...], preferred_element_type=F32), cg_ref)
    act = (a * jax.nn.gelu(g)).astype(BF16)
    part = jnp.dot(act, wd_ref[...], preferred_element_type=F32)

    @pl.when(j == 0)
    def _():
        y_ref[...] = part

    @pl.when(j > 0)
    def _():
        y_ref[...] += part


def _ffn(h, w_up, conv, w_down):
    tm, tf = 1024, 256
    nj = D_FF_PAD // tf
    return pl.pallas_call(
        functools.partial(_ffn_kernel, tm=tm),
        grid=(T_ALL // tm, nj),
        in_specs=[pl.BlockSpec((tm, D_MODEL), lambda i, j: (i, 0)),
                  pl.BlockSpec((D_MODEL, tf), lambda i, j: (0, j)),
                  pl.BlockSpec((D_MODEL, tf), lambda i, j: (0, nj + j)),
                  pl.BlockSpec((CONV_W, tf), lambda i, j: (0, j)),
                  pl.BlockSpec((CONV_W, tf), lambda i, j: (0, nj + j)),
                  pl.BlockSpec((tf, D_MODEL), lambda i, j: (j, 0))],
        out_specs=pl.BlockSpec((tm, D_MODEL), lambda i, j: (i, 0)),
        out_shape=jax.ShapeDtypeStruct((T_ALL, D_MODEL), F32),
        compiler_params=_params("parallel", "arbitrary"),
        name="conv_ffn",
    )(h, w_up, w_up, conv, conv, w_down)


def _pad_cols(w, n):
    return jnp.pad(w, [(0, 0)] * (w.ndim - 1) + [(0, n - w.shape[-1])])


def _prep_weights(w_in, mla_w_uq, mla_w_ukv, w_branch_a, w_branch_b, w_branch_c, w_out,
                  ffn_w_up, ffn_conv, ffn_w_down):
    a_end = 5 * H_A * DK_A
    b_end = a_end + Q_LORA + KV_LORA + ROPE_B
    c_end = b_end + (H_C + 2 * KVH_C) * HD_C
    w_za = w_in[:, :, :a_end].astype(BF16)
    w_zb = _pad_cols(w_in[:, :, a_end:b_end], 1024).astype(BF16)
    w_zc = w_in[:, :, b_end:c_end].astype(BF16)
    w_zg = w_in[:, :, c_end:].astype(BF16)
    w_uq = _pad_cols(mla_w_uq.reshape(DEPTH, Q_LORA, H_B, NOPE_B + ROPE_B), 2 * LANE)
    w_uq = w_uq.reshape(DEPTH, Q_LORA, H_B * 2 * LANE).astype(BF16)
    w_up = jnp.concatenate([_pad_cols(ffn_w_up[:, :, :D_FF], D_FF_PAD),
                            _pad_cols(ffn_w_up[:, :, D_FF:], D_FF_PAD)], axis=-1).astype(BF16)
    conv = jnp.concatenate([_pad_cols(ffn_conv[:, :, :D_FF], D_FF_PAD),
                            _pad_cols(ffn_conv[:, :, D_FF:], D_FF_PAD)], axis=-1)
    w_down = jnp.pad(ffn_w_down, ((0, 0), (0, D_FF_PAD - D_FF), (0, 0))).astype(BF16)
    return dict(w_za=w_za, w_zb=w_zb, w_zc=w_zc, w_zg=w_zg, w_uq=w_uq, w_ukv=mla_w_ukv.astype(BF16),
                w_a=w_branch_a.astype(BF16), w_b=w_branch_b.astype(BF16), w_c=w_branch_c.astype(BF16),
                w_o=w_out.astype(BF16), w_up=w_up, conv=conv, w_down=w_down)


def kernel(x_prompt, x_sample, state_hgrn, cache_mla_ckv, cache_mla_krope, cache_swa_k, cache_swa_v,
           c, c_ctx, w_mod, b_mod, norm_pre_attn, norm_post_attn, norm_pre_ffn, norm_post_ffn,
           w_in, hgrn_lb, hgrn_gnorm, mla_gq, mla_w_uq, mla_gkv, mla_w_ukv, swa_sink,
           w_branch_a, w_branch_b, w_branch_c, w_out, ffn_w_up, ffn_conv, ffn_w_down):
    wts = _prep_weights(w_in, mla_w_uq, mla_w_ukv, w_branch_a, w_branch_b, w_branch_c, w_out,
                        ffn_w_up, ffn_conv, ffn_w_down)
    cs = jnp.cumsum(jax.nn.softmax(hgrn_lb.astype(F32), axis=0), axis=0)
    lb_all = cs - cs[0]

    cvec = jnp.concatenate([c_ctx[None, :], c, jnp.zeros((MOD_ROWS - 1 - DEC_BATCH, D_MODEL), F32)], axis=0)
    mod = _modulation(cvec, w_mod, b_mod).reshape(DEPTH, MOD_ROWS, 6, 1, D_MODEL)

    rope_b = _rope_tables(ROPE_B)
    rope_c = _rope_tables(HD_C)
    sink_b = jnp.broadcast_to(swa_sink[:, :, None], (DEPTH, H_C, LANE))

    x = jnp.concatenate([x_prompt.reshape(T_CTX, D_MODEL), x_sample.reshape(T_LAT, D_MODEL)], axis=0)
    new_hgrn, new_ckv, new_krope, new_k, new_v = [], [], [], [], []
    y = None
    for l in range(DEPTH):
        mod_l = mod[l]
        if l == 0:
            (h,) = _norm(x, npre=norm_pre_attn[l], mod_pre=mod_l, scale_idx=1, shift_idx=0)
        else:
            x, h = _norm(x, y=y, mod_post=mod[l - 1], gate_idx=5, npost=norm_post_ffn[l - 1],
                         npre=norm_pre_attn[l], mod_pre=mod_l, scale_idx=1, shift_idx=0)
        za = _mm(h, wts["w_za"][l], F32, "in_proj_a")
        zb = _mm(h, wts["w_zb"][l], F32, "in_proj_b")
        zc = _mm(h, wts["w_zc"][l], F32, "in_proj_c")
        zg = _mm(h, wts["w_zg"][l], F32, "in_proj_g")

        oa_ctx, s_ctx = _hgrn(za, lb_all[l], hgrn_gnorm[l], latent=False)
        (oa_lat,) = _hgrn(za, lb_all[l], hgrn_gnorm[l], latent=True, s0=state_hgrn[:, l])
        new_hgrn.append(s_ctx)

        (qb,) = _norm_mm(zb, 0, Q_LORA, mla_gq[l], wts["w_uq"][l], F32, emit_normed=False, name="mla_q_proj")
        ckv, kvb = _norm_mm(zb, 2, KV_LORA, mla_gkv[l], wts["w_ukv"][l], BF16, emit_normed=True,
                            name="mla_kv_proj")
        kv_cache = _mm(cache_mla_ckv[:, l].reshape(DEC_BATCH * PAST_LEN, KV_LORA), wts["w_ukv"][l], BF16,
                       "mla_kv_cache")
        kr_cache = _pad_cols(cache_mla_krope[:, l].reshape(DEC_BATCH * PAST_LEN, ROPE_B), LANE)
        ob_ctx = _mla_attn(qb, kvb, zb, latent=False)
        ob_lat = _mla_attn(qb, kvb, zb, latent=True, kvc=kv_cache, krc=kr_cache, tables=rope_b)
        new_ckv.append(ckv[:T_CTX].reshape(BATCH, SEQ, KV_LORA))
        new_krope.append(zb[:T_CTX, Q_LORA + KV_LORA:Q_LORA + KV_LORA + ROPE_B].reshape(BATCH, SEQ, ROPE_B))

        oc_ctx = _gqa_attn(zc, sink_b[l], latent=False)
        oc_lat = _gqa_attn(zc, sink_b[l], latent=True,
                           kc=cache_swa_k[:, l].reshape(DEC_BATCH * PAST_LEN, KVH_C * HD_C),
                           vc=cache_swa_v[:, l].reshape(DEC_BATCH * PAST_LEN, KVH_C * HD_C), tables=rope_c)
        kq = H_C * HD_C
        new_k.append(zc[:T_CTX, kq:kq + KVH_C * HD_C].reshape(BATCH, SEQ, KVH_C, HD_C))
        new_v.append(zc[:T_CTX, kq + KVH_C * HD_C:].reshape(BATCH, SEQ, KVH_C, HD_C))

        oa = jnp.concatenate([oa_ctx, oa_lat], axis=0)
        ob = jnp.concatenate([ob_ctx, ob_lat], axis=0)
        oc = jnp.concatenate([oc_ctx, oc_lat], axis=0)
        y = _merge(oa, ob, oc, wts["w_a"][l], wts["w_b"][l], wts["w_c"][l], zg, wts["w_o"][l])

        x, h = _norm(x, y=y, mod_post=mod_l, gate_idx=2, npost=norm_post_attn[l],
                     npre=norm_pre_ffn[l], mod_pre=mod_l, scale_idx=4, shift_idx=3)
        y = _ffn(h, wts["w_up"][l], wts["conv"][l], wts["w_down"][l])

    (x,) = _norm(x, y=y, mod_post=mod[DEPTH - 1], gate_idx=5, npost=norm_post_ffn[DEPTH - 1])
    return (x[:T_CTX].reshape(BATCH, SEQ, D_MODEL), x[T_CTX:].reshape(DEC_BATCH, DEC_SEQ, D_MODEL),
            jnp.stack(new_hgrn, axis=1), jnp.stack(new_ckv, axis=1), jnp.stack(new_krope, axis=1),
            jnp.stack(new_k, axis=1), jnp.stack(new_v, axis=1))
```

```python
import functools

import jax
import jax.numpy as jnp
import numpy as np
from jax import lax
from jax.experimental import pallas as pl
from jax.experimental.pallas import tpu as pltpu

F32 = jnp.float32
BF16 = jnp.bfloat16

D_MODEL = 2048
BATCH = 16
SEQ = 256
DEPTH = 4
DEC_BATCH = 4
DEC_SEQ = 1024
PAST_LEN = 256
GRID_W = 64
ROPE_BASE = 10000.0
EPS = 1e-6
NEG_INF = -1e30
H_A, DK_A, DV_A = 8, 128, 128
H_B, Q_LORA, KV_LORA, NOPE_B, ROPE_B, V_B = 8, 512, 256, 128, 64, 128
H_C, KVH_C, HD_C, WINDOW = 8, 2, 128, 128
N_BRANCH = 3
D_FF = 5504
CONV_W = 3

T_CTX = BATCH * SEQ
T_LAT = DEC_BATCH * DEC_SEQ
T_ALL = T_CTX + T_LAT
MOD_ROWS = 8
LANE = 128
D_FF_PAD = 5632
HGRN_CHUNK = 64
VMEM_LIMIT = 56 * 1024 * 1024


def _params(*sem):
    return pltpu.CompilerParams(dimension_semantics=sem, vmem_limit_bytes=VMEM_LIMIT)


def _mod_row(i, tm):
    return jnp.where(i * tm < T_CTX, 0, 1 + (i * tm - T_CTX) // DEC_SEQ)


def _rms(x):
    return x * lax.rsqrt(jnp.mean(x * x, axis=-1, keepdims=True) + EPS)


def _mod_kernel(c_ref, w_ref, b_ref, o_ref):
    cv = c_ref[...]
    s = (cv * jax.nn.sigmoid(cv)).astype(BF16)
    o_ref[0] = jnp.dot(s, w_ref[0].astype(BF16), preferred_element_type=F32) + b_ref[0]


def _modulation(cvec, w_mod, b_mod):
    tn = 1024
    n = 6 * D_MODEL
    return pl.pallas_call(
        _mod_kernel,
        grid=(DEPTH, n // tn),
        in_specs=[pl.BlockSpec((MOD_ROWS, D_MODEL), lambda l, j: (0, 0)),
                  pl.BlockSpec((1, D_MODEL, tn), lambda l, j: (l, 0, j)),
                  pl.BlockSpec((1, 1, tn), lambda l, j: (l, 0, j))],
        out_specs=pl.BlockSpec((1, MOD_ROWS, tn), lambda l, j: (l, 0, j)),
        out_shape=jax.ShapeDtypeStruct((DEPTH, MOD_ROWS, n), F32),
        compiler_params=_params("parallel", "parallel"),
        name="modulation",
    )(cvec, w_mod, b_mod.reshape(DEPTH, 1, n))


def _norm_kernel(*refs, has_y, has_h):
    it = iter(refs)
    x_ref = next(it)
    if has_y:
        y_ref, gate_ref, npost_ref = next(it), next(it), next(it)
    if has_h:
        npre_ref, scale_ref, shift_ref = next(it), next(it), next(it)
    x = x_ref[...]
    if has_y:
        xnew_ref = next(it)
        x = x + gate_ref[0, 0] * (_rms(y_ref[...]) * npost_ref[...])
        xnew_ref[...] = x
    if has_h:
        h_ref = next(it)
        h = (_rms(x) * npre_ref[...]) * (1.0 + scale_ref[0, 0]) + shift_ref[0, 0]
        h_ref[...] = h.astype(BF16)


def _norm(x, *, y=None, mod_post=None, gate_idx=None, npost=None,
          npre=None, mod_pre=None, scale_idx=None, shift_idx=None):
    tm = 512
    has_y, has_h = y is not None, npre is not None
    row = pl.BlockSpec((tm, D_MODEL), lambda i: (i, 0))
    vec = pl.BlockSpec((1, D_MODEL), lambda i: (0, 0))

    def modspec(k):
        return pl.BlockSpec((1, 1, 1, D_MODEL), lambda i: (_mod_row(i, tm), k, 0, 0))

    args, specs, out_shape, out_specs = [x], [row], [], []
    if has_y:
        args += [y, mod_post, npost.reshape(1, D_MODEL)]
        specs += [row, modspec(gate_idx), vec]
        out_shape.append(jax.ShapeDtypeStruct((T_ALL, D_MODEL), F32))
        out_specs.append(row)
    if has_h:
        args += [npre.reshape(1, D_MODEL), mod_pre, mod_pre]
        specs += [vec, modspec(scale_idx), modspec(shift_idx)]
        out_shape.append(jax.ShapeDtypeStruct((T_ALL, D_MODEL), BF16))
        out_specs.append(row)
    outs = pl.pallas_call(
        functools.partial(_norm_kernel, has_y=has_y, has_h=has_h),
        grid=(T_ALL // tm,),
        in_specs=specs, out_specs=out_specs, out_shape=out_shape,
        compiler_params=_params("parallel"),
        name="norm_y%d_h%d" % (has_y, has_h),
    )(*args)
    return outs


def _mm_kernel(x_ref, w_ref, o_ref):
    o_ref[...] = jnp.dot(x_ref[...].astype(BF16), w_ref[...],
                         preferred_element_type=F32).astype(o_ref.dtype)


def _mm(x, w, out_dtype, name):
    m, k = x.shape
    n = w.shape[1]
    tm = min(m, 1024)
    tn = min(n, 512)
    return pl.pallas_call(
        _mm_kernel,
        grid=(m // tm, n // tn),
        in_specs=[pl.BlockSpec((tm, k), lambda i, j: (i, 0)),
                  pl.BlockSpec((k, tn), lambda i, j: (0, j))],
        out_specs=pl.BlockSpec((tm, tn), lambda i, j: (i, j)),
        out_shape=jax.ShapeDtypeStruct((m, n), out_dtype),
        compiler_params=_params("parallel", "parallel"),
        name=name,
    )(x, w)


def _norm_mm_kernel(x_ref, g_ref, w_ref, *out_refs, emit_normed):
    xn = _rms(x_ref[...]) * g_ref[...]
    if emit_normed:
        out_refs[0][...] = xn
    out_refs[-1][...] = jnp.dot(xn.astype(BF16), w_ref[...],
                                preferred_element_type=F32).astype(out_refs[-1].dtype)


def _norm_mm(z, col_block, k, gain, w, out_dtype, *, emit_normed, name):
    m = z.shape[0]
    n = w.shape[1]
    tm = 512
    out_shape = [jax.ShapeDtypeStruct((m, n), out_dtype)]
    out_specs = [pl.BlockSpec((tm, n), lambda i: (i, 0))]
    if emit_normed:
        out_shape.insert(0, jax.ShapeDtypeStruct((m, k), F32))
        out_specs.insert(0, pl.BlockSpec((tm, k), lambda i: (i, 0)))
    return pl.pallas_call(
        functools.partial(_norm_mm_kernel, emit_normed=emit_normed),
        grid=(m // tm,),
        in_specs=[pl.BlockSpec((tm, k), lambda i: (i, col_block)),
                  pl.BlockSpec((1, k), lambda i: (0, 0)),
                  pl.BlockSpec((k, n), lambda i: (0, 0))],
        out_specs=out_specs, out_shape=out_shape,
        compiler_params=_params("parallel"),
        name=name,
    )(z, gain.reshape(1, k), w)


def _rope_tables(head_dim):
    n = DEC_SEQ
    half = head_dim // 2
    quarter = half // 2
    row = jnp.repeat(jnp.arange(n // GRID_W), GRID_W).astype(F32)
    col = jnp.tile(jnp.arange(GRID_W), n // GRID_W).astype(F32)
    inv = ROPE_BASE ** (-jnp.arange(0, half, 2, dtype=F32) / half)
    lane = jnp.arange(LANE)
    m = lane % half
    pos = jnp.where((lane // half)[None, :] == 0, row[:, None], col[:, None])
    ang = pos * inv[m % quarter][None, :]
    valid = (lane < head_dim)[None, :]
    cos = jnp.where(valid, jnp.cos(ang), 0.0)
    sin = jnp.where(valid, jnp.where(m < quarter, -1.0, 1.0)[None, :] * jnp.sin(ang), 0.0)
    return cos.astype(F32), sin.astype(F32)


def _rope(x, cos, sin, head_dim):
    quarter = head_dim // 4
    lane = lax.broadcasted_iota(jnp.int32, x.shape, 1)
    first = (lane % (2 * quarter)) < quarter
    partner = jnp.where(first, pltpu.roll(x, LANE - quarter, 1), pltpu.roll(x, quarter, 1))
    return x * cos + partner * sin


def _mla_attn_kernel(*refs, latent):
    if latent:
        q_ref, kv_ref, kr_ref, kvc_ref, krc_ref, cq_ref, sq_ref, ck_ref, sk_ref, o_ref = refs
    else:
        q_ref, kv_ref, kr_ref, o_ref = refs
    scale = (NOPE_B + ROPE_B) ** -0.5
    nt = (((1,), (1,)), ((), ()))
    kr = kr_ref[...]
    if latent:
        kr = _rope(kr, ck_ref[...], sk_ref[...], ROPE_B)
        krc = krc_ref[...].astype(BF16)
    kr = kr.astype(BF16)
    for h in range(H_B):
        c0 = h * 2 * LANE
        qn = q_ref[:, c0:c0 + LANE]
        qr = q_ref[:, c0 + LANE:c0 + 2 * LANE]
        if latent:
            qr = _rope(qr, cq_ref[...], sq_ref[...], ROPE_B)
        qh = jnp.concatenate([qn.astype(BF16), qr.astype(BF16)], axis=-1)
        kh = jnp.concatenate([kv_ref[:, c0:c0 + LANE], kr], axis=-1)
        vh = kv_ref[:, c0 + LANE:c0 + 2 * LANE]
        s = lax.dot_general(qh, kh, nt, preferred_element_type=F32) * scale
        m = jnp.max(s, axis=-1, keepdims=True)
        if latent:
            khc = jnp.concatenate([kvc_ref[:, c0:c0 + LANE], krc], axis=-1)
            vhc = kvc_ref[:, c0 + LANE:c0 + 2 * LANE]
            sc = lax.dot_general(qh, khc, nt, preferred_element_type=F32) * scale
            m = jnp.maximum(m, jnp.max(sc, axis=-1, keepdims=True))
        p = jnp.exp(s - m)
        l = jnp.sum(p, axis=-1, keepdims=True)
        o = jnp.dot(p.astype(BF16), vh, preferred_element_type=F32)
        if latent:
            pc = jnp.exp(sc - m)
            l = l + jnp.sum(pc, axis=-1, keepdims=True)
            o = o + jnp.dot(pc.astype(BF16), vhc, preferred_element_type=F32)
        o_ref[:, h * LANE:(h + 1) * LANE] = (o / l).astype(BF16)


def _mla_attn(q, kv, zb, *, latent, kvc=None, krc=None, tables=None):
    nb, n = (DEC_BATCH, DEC_SEQ) if latent else (BATCH, SEQ)
    tq = 256
    nq = n // tq
    off = T_CTX // n if latent else 0
    offq = T_CTX // tq if latent else 0
    w = H_B * 2 * LANE
    specs = [pl.BlockSpec((tq, w), lambda b, i: (offq + b * nq + i, 0)),
             pl.BlockSpec((n, w), lambda b, i: (off + b, 0)),
             pl.BlockSpec((n, LANE), lambda b, i: (off + b, 6))]
    args = [q, kv, zb]
    if latent:
        cos, sin = tables
        specs += [pl.BlockSpec((PAST_LEN, w), lambda b, i: (b, 0)),
                  pl.BlockSpec((PAST_LEN, LANE), lambda b, i: (b, 0)),
                  pl.BlockSpec((tq, LANE), lambda b, i: (i, 0)),
                  pl.BlockSpec((tq, LANE), lambda b, i: (i, 0)),
                  pl.BlockSpec((n, LANE), lambda b, i: (0, 0)),
                  pl.BlockSpec((n, LANE), lambda b, i: (0, 0))]
        args += [kvc, krc, cos, sin, cos, sin]
    return pl.pallas_call(
        functools.partial(_mla_attn_kernel, latent=latent),
        grid=(nb, nq),
        in_specs=specs,
        out_specs=pl.BlockSpec((tq, H_B * V_B), lambda b, i: (b * nq + i, 0)),
        out_shape=jax.ShapeDtypeStruct((nb * n, H_B * V_B), BF16),
        compiler_params=_params("parallel", "parallel"),
        name="mla_attn_lat" if latent else "mla_attn_ctx",
    )(*args)


def _gqa_attn_kernel(*refs, latent, tq):
    if latent:
        q_ref, k_ref, v_ref, kc_ref, vc_ref, sink_ref, cq_ref, sq_ref, ck_ref, sk_ref, o_ref = refs
    else:
        q_ref, k_ref, v_ref, sink_ref, o_ref = refs
    scale = HD_C ** -0.5
    nt = (((1,), (1,)), ((), ()))
    rep = H_C // KVH_C
    n = k_ref.shape[0]
    if latent:
        qpos = pl.program_id(1) * tq + lax.broadcasted_iota(jnp.int32, (tq, n), 0)
        kpos = lax.broadcasted_iota(jnp.int32, (tq, n), 1)
        band = jnp.abs(qpos - kpos) <= WINDOW
    for g in range(KVH_C):
        kg = k_ref[:, g * LANE:(g + 1) * LANE]
        if latent:
            kg = _rope(kg, ck_ref[...], sk_ref[...], HD_C)
            kcg = kc_ref[:, g * LANE:(g + 1) * LANE].astype(BF16)
            vcg = vc_ref[:, g * LANE:(g + 1) * LANE].astype(BF16)
        kg = kg.astype(BF16)
        vg = v_ref[:, g * LANE:(g + 1) * LANE].astype(BF16)
        for r in range(rep):
            h = g * rep + r
            qh = q_ref[:, h * LANE:(h + 1) * LANE]
            if latent:
                qh = _rope(qh, cq_ref[...], sq_ref[...], HD_C)
            qh = qh.astype(BF16)
            sk = sink_ref[h:h + 1, 0:1]
            s = lax.dot_general(qh, kg, nt, preferred_element_type=F32) * scale
            if latent:
                s = jnp.where(band, s, NEG_INF)
            m = jnp.maximum(jnp.max(s, axis=-1, keepdims=True), sk)
            if latent:
                sc = lax.dot_general(qh, kcg, nt, preferred_element_type=F32) * scale
                m = jnp.maximum(m, jnp.max(sc, axis=-1, keepdims=True))
            p = jnp.exp(s - m)
            l = jnp.sum(p, axis=-1, keepdims=True) + jnp.exp(sk - m)
            o = jnp.dot(p.astype(BF16), vg, preferred_element_type=F32)
            if latent:
                pc = jnp.exp(sc - m)
                l = l + jnp.sum(pc, axis=-1, keepdims=True)
                o = o + jnp.dot(pc.astype(BF16), vcg, preferred_element_type=F32)
            o_ref[:, h * LANE:(h + 1) * LANE] = (o / l).astype(BF16)


def _gqa_attn(zc, sink_b, *, latent, kc=None, vc=None, tables=None):
    nb, n = (DEC_BATCH, DEC_SEQ) if latent else (BATCH, SEQ)
    tq = 256
    nq = n // tq
    off = T_CTX // n if latent else 0
    offq = T_CTX // tq if latent else 0
    wq, wk = H_C * HD_C, KVH_C * HD_C
    specs = [pl.BlockSpec((tq, wq), lambda b, i: (offq + b * nq + i, 0)),
             pl.BlockSpec((n, wk), lambda b, i: (off + b, wq // wk)),
             pl.BlockSpec((n, wk), lambda b, i: (off + b, wq // wk + 1))]
    args = [zc, zc, zc]
    if latent:
        specs += [pl.BlockSpec((PAST_LEN, wk), lambda b, i: (b, 0)),
                  pl.BlockSpec((PAST_LEN, wk), lambda b, i: (b, 0))]
        args += [kc, vc]
    specs.append(pl.BlockSpec((H_C, LANE), lambda b, i: (0, 0)))
    args.append(sink_b)
    if latent:
        cos, sin = tables
        specs += [pl.BlockSpec((tq, LANE), lambda b, i: (i, 0)),
                  pl.BlockSpec((tq, LANE), lambda b, i: (i, 0)),
                  pl.BlockSpec((n, LANE), lambda b, i: (0, 0)),
                  pl.BlockSpec((n, LANE), lambda b, i: (0, 0))]
        args += [cos, sin, cos, sin]
    return pl.pallas_call(
        functools.partial(_gqa_attn_kernel, latent=latent, tq=tq),
        grid=(nb, nq),
        in_specs=specs,
        out_specs=pl.BlockSpec((tq, wq), lambda b, i: (b * nq + i, 0)),
        out_shape=jax.ShapeDtypeStruct((nb * n, wq), BF16),
        compiler_params=_params("parallel", "parallel"),
        name="gqa_attn_lat" if latent else "gqa_attn_ctx",
    )(*args)


def _hgrn_tables():
    c = HGRN_CHUNK
    halves = [c >> (i + 1) for i in range(c.bit_length() - 1)]
    out = []
    for forward in (True, False):
        sums = np.zeros((len(halves) + 1, c, c), np.float32)
        level = np.full((c, c), -1, np.int32)
        level[np.arange(c), np.arange(c)] = 0
        for li, m in enumerate(halves):
            for r in range(c):
                pos = r % (2 * m)
                mid = r - pos + m
                late = pos >= m
                if forward:
                    lo, hi = (mid, r + 1) if late else (r + 1, mid)
                else:
                    lo, hi = (mid, r) if late else (r, mid)
                sums[li, r, lo:hi] = 1.0
                for s in range(r - pos, r - pos + 2 * m):
                    s_late = (s % (2 * m)) >= m
                    if (late and not s_late) if forward else (not late and s_late):
                        level[r, s] = li + 1
        for r in range(c):
            if forward:
                sums[-1, r, :r + 1] = 1.0
            else:
                sums[-1, r, r:] = 1.0
        sums = sums.reshape(-1, c)
        out.append((jnp.asarray(np.concatenate([sums, sums, sums], axis=1), BF16),
                    jnp.asarray(np.concatenate([level, level], axis=1))))
    return out


def _hgrn_kernel(*refs, n, has_s0, emit_state):
    it = iter(refs)
    q_ref, xf_ref, xb_ref, v_ref, ag_ref, lb_ref, gn_ref = (next(it) for _ in range(7))
    sums_refs = (next(it), next(it))
    level_refs = (next(it), next(it))
    s0_ref = next(it) if has_s0 else None
    o_ref = next(it)
    sfin_ref = next(it) if emit_state else None
    o_scr, g_scr, k_scr, qe_scr, u_scr, e_scr, st_scr = (next(it) for _ in range(7))

    c = HGRN_CHUNK
    nc = n // c
    nlev = c.bit_length() - 1
    nt = (((1,), (1,)), ((), ()))
    tn = (((0,), (0,)), ((), ()))
    zero = jnp.zeros((c, LANE), BF16)

    def blockdiag(x):
        return jnp.concatenate([jnp.concatenate([x[:, :LANE], zero], axis=1),
                                jnp.concatenate([zero, x[:, LANE:]], axis=1)], axis=0)

    for d, x_ref in enumerate((xf_ref, xb_ref)):
        x = x_ref[...]
        lb = lb_ref[d:d + 1, :]
        g_scr[d] = jnp.log(lb + (1.0 - lb) * jax.nn.sigmoid(x))
        k_scr[d] = (1.0 - lb) * jax.nn.sigmoid(-x)

    for d in range(2):
        for hh in range(2):
            st_scr[d, hh] = s0_ref[0, d, hh].T if has_s0 else jnp.zeros((DV_A, DK_A), F32)

    def intra(i, carry):
        rows = pl.ds(pl.multiple_of(i * c, c), c)
        q = q_ref[rows, :]
        v = v_ref[rows, :].astype(BF16)
        for d in range(2):
            g = g_scr[d, rows, :]
            k = k_scr[d, rows, :]
            g_hi = g.astype(BF16)
            rem = g - g_hi.astype(F32)
            g_mid = rem.astype(BF16)
            g_lo = (rem - g_mid.astype(F32)).astype(BF16)
            dall = jnp.dot(sums_refs[d][...], jnp.concatenate([g_hi, g_mid, g_lo], axis=0),
                           preferred_element_type=F32)
            level = level_refs[d][...]
            r = lax.dot_general(q.astype(BF16), blockdiag(k.astype(BF16)), nt, preferred_element_type=F32)
            a = jnp.where(level == 0, r, 0.0)
            for li in range(nlev):
                e = jnp.exp(dall[li * c:(li + 1) * c, :])
                r = lax.dot_general((q * e).astype(BF16), blockdiag((k * e).astype(BF16)), nt,
                                    preferred_element_type=F32)
                a = jnp.where(level == li + 1, r, a)
            o_scr[d, rows, :] = jnp.dot(a.astype(BF16), blockdiag(v), preferred_element_type=F32)
            gc = dall[nlev * c:, :]
            g_end = gc[c - 1:c, :] if d == 0 else gc[0:1, :]
            qe_scr[d, rows, :] = (q * jnp.exp(gc)).astype(BF16)
            kd = (k * jnp.exp(g_end - gc)).astype(BF16)
            e_scr[d, i] = jnp.broadcast_to(jnp.exp(g_end), (8, 2 * LANE))
            for hh in range(2):
                hl = slice(hh * LANE, (hh + 1) * LANE)
                u_scr[d, i, hh] = lax.dot_general(v[:, hl], kd[:, hl], tn, preferred_element_type=F32)
        return carry

    lax.fori_loop(0, nc, intra, 0)

    def inter(i, carry):
        for d in range(2):
            ci = i if d == 0 else nc - 1 - i
            rows = pl.ds(pl.multiple_of(ci * c, c), c)
            e = e_scr[d, ci]
            for hh in range(2):
                hl = slice(hh * LANE, (hh + 1) * LANE)
                st = st_scr[d, hh]
                o_scr[d, rows, hl] += lax.dot_general(qe_scr[d, rows, hl], st.astype(BF16), nt,
                                                      preferred_element_type=F32)
                st_scr[d, hh] = st * e[0:1, hl] + u_scr[d, ci, hh]
        return carry

    lax.fori_loop(0, nc, inter, 0)

    o = o_scr[0] + o_scr[1]
    o = jnp.concatenate([_rms(o[:, :LANE]), _rms(o[:, LANE:])], axis=1)
    ag = ag_ref[...]
    o_ref[...] = (o * gn_ref[...] * (ag * jax.nn.sigmoid(ag))).astype(BF16)
    if emit_state:
        for d in range(2):
            for hh in range(2):
                sfin_ref[0, d, hh] = st_scr[d, hh].T


def _hgrn(za, lb_l, gnorm, tables, *, latent, s0=None):
    nb, n = (DEC_BATCH, DEC_SEQ) if latent else (BATCH, SEQ)
    off = T_CTX // n if latent else 0
    emit_state = not latent
    w = 2 * LANE
    pairs = H_A // 2
    c = HGRN_CHUNK
    nc = n // c

    def zspec(k):
        return pl.BlockSpec((n, w), lambda b, p: (off + b, k * pairs + p))

    def const(x):
        return pl.BlockSpec(x.shape, lambda b, p: (0, 0))

    (sums_f, level_f), (sums_b, level_b) = tables
    specs = [zspec(0), zspec(1), zspec(2), zspec(3), zspec(4),
             pl.BlockSpec((2, w), lambda b, p: (0, p)),
             pl.BlockSpec((1, w), lambda b, p: (0, 0)),
             const(sums_f), const(sums_b), const(level_f), const(level_b)]
    args = [za, za, za, za, za, lb_l, jnp.tile(gnorm.reshape(1, DV_A), (1, 2)),
            sums_f, sums_b, level_f, level_b]
    if latent:
        specs.append(pl.BlockSpec((1, 2, 2, DK_A, DV_A), lambda b, p: (b, 0, p, 0, 0)))
        args.append(s0)
    out_shape = [jax.ShapeDtypeStruct((nb * n, H_A * DV_A), BF16)]
    out_specs = [pl.BlockSpec((n, w), lambda b, p: (b, p))]
    if emit_state:
        out_shape.append(jax.ShapeDtypeStruct((nb, 2, H_A, DK_A, DV_A), F32))
        out_specs.append(pl.BlockSpec((1, 2, 2, DK_A, DV_A), lambda b, p: (b, 0, p, 0, 0)))
    scratch = [pltpu.VMEM((2, n, w), F32),
               pltpu.VMEM((2, n, w), F32),
               pltpu.VMEM((2, n, w), F32),
               pltpu.VMEM((2, n, w), BF16),
               pltpu.VMEM((2, nc, 2, DV_A, DK_A), F32),
               pltpu.VMEM((2, nc, 8, w), F32),
               pltpu.VMEM((2, 2, DV_A, DK_A), F32)]
    return pl.pallas_call(
        functools.partial(_hgrn_kernel, n=n, has_s0=latent, emit_state=emit_state),
        grid=(nb, pairs),
        in_specs=specs, out_specs=out_specs, out_shape=out_shape,
        scratch_shapes=scratch,
        compiler_params=_params("parallel", "parallel"),
        name="hgrn_lat" if latent else "hgrn_ctx",
    )(*args)


def _merge_kernel(oa_ref, ob_ref, oc_ref, wa_ref, wb_ref, wc_ref, g0_ref, g1_ref, g2_ref, wo_ref, y_ref):
    j = pl.program_id(1)
    merged = (jax.nn.sigmoid(g0_ref[...]) * jnp.dot(oa_ref[...], wa_ref[...], preferred_element_type=F32)
              + jax.nn.sigmoid(g1_ref[...]) * jnp.dot(ob_ref[...], wb_ref[...], preferred_element_type=F32)
              + jax.nn.sigmoid(g2_ref[...]) * jnp.dot(oc_ref[...], wc_ref[...], preferred_element_type=F32))
    part = jnp.dot(merged.astype(BF16), wo_ref[...], preferred_element_type=F32)

    @pl.when(j == 0)
    def _():
        y_ref[...] = part

    @pl.when(j > 0)
    def _():
        y_ref[...] += part


def _merge(oa, ob, oc, wa, wb, wc, zg, wo):
    tm, tn = 512, 256
    nj = D_MODEL // tn
    kb = H_A * DV_A
    o_spec = pl.BlockSpec((tm, kb), lambda i, j: (i, 0))
    w_spec = pl.BlockSpec((kb, tn), lambda i, j: (0, j))

    def gspec(k):
        return pl.BlockSpec((tm, tn), lambda i, j: (i, k * nj + j))

    return pl.pallas_call(
        _merge_kernel,
        grid=(T_ALL // tm, nj),
        in_specs=[o_spec, o_spec, o_spec, w_spec, w_spec, w_spec, gspec(0), gspec(1), gspec(2),
                  pl.BlockSpec((tn, D_MODEL), lambda i, j: (j, 0))],
        out_specs=pl.BlockSpec((tm, D_MODEL), lambda i, j: (i, 0)),
        out_shape=jax.ShapeDtypeStruct((T_ALL, D_MODEL), F32),
        compiler_params=_params("parallel", "arbitrary"),
        name="merge_out",
    )(oa, ob, oc, wa, wb, wc, zg, zg, zg, wo)


def _ffn_kernel(h_ref, wa_ref, wg_ref, ca_ref, cg_ref, wd_ref, y_ref, *, tm):
    i = pl.program_id(0)
    j = pl.program_id(1)
    h = h_ref[...]
    seq_len = jnp.where(i * tm < T_CTX, SEQ, DEC_SEQ)
    pos = lax.broadcasted_iota(jnp.int32, (tm, 1), 0) & (seq_len - 1)
    has_prev = pos != 0
    has_next = pos != seq_len - 1

    def conv(u, c_ref):
        prev = jnp.where(has_prev, pltpu.roll(u, 1, 0), 0.0)
        nxt = jnp.where(has_next, pltpu.roll(u, tm - 1, 0), 0.0)
        return c_ref[0:1, :] * prev + c_ref[1:2, :] * u + c_ref[2:3, :] * nxt

    a = conv(jnp.dot(h, wa_ref[...], preferred_element_type=F32), ca_ref)
    g = conv(jnp.dot(h, wg_ref[...], preferred_element_type=F32), cg_ref)
    act = (a * jax.nn.gelu(g)).astype(BF16)
    part = jnp.dot(act, wd_ref[...], preferred_element_type=F32)

    @pl.when(j == 0)
    def _():
        y_ref[...] = part

    @pl.when(j > 0)
    def _():
        y_ref[...] += part


def _ffn(h, w_up, conv, w_down):
    tm, tf = 1024, 256
    nj = D_FF_PAD // tf
    return pl.pallas_call(
        functools.partial(_ffn_kernel, tm=tm),
        grid=(T_ALL // tm, nj),
        in_specs=[pl.BlockSpec((tm, D_MODEL), lambda i, j: (i, 0)),
                  pl.BlockSpec((D_MODEL, tf), lambda i, j: (0, j)),
                  pl.BlockSpec((D_MODEL, tf), lambda i, j: (0, nj + j)),
                  pl.BlockSpec((CONV_W, tf), lambda i, j: (0, j)),
                  pl.BlockSpec((CONV_W, tf), lambda i, j: (0, nj + j)),
                  pl.BlockSpec((tf, D_MODEL), lambda i, j: (j, 0))],
        out_specs=pl.BlockSpec((tm, D_MODEL), lambda i, j: (i, 0)),
        out_shape=jax.ShapeDtypeStruct((T_ALL, D_MODEL), F32),
        compiler_params=_params("parallel", "arbitrary"),
        name="conv_ffn",
    )(h, w_up, w_up, conv, conv, w_down)


def _pad_cols(w, n):
    return jnp.pad(w, [(0, 0)] * (w.ndim - 1) + [(0, n - w.shape[-1])])


def _prep_weights(w_in, mla_w_uq, mla_w_ukv, w_branch_a, w_branch_b, w_branch_c, w_out,
                  ffn_w_up, ffn_conv, ffn_w_down):
    a_end = 5 * H_A * DK_A
    b_end = a_end + Q_LORA + KV_LORA + ROPE_B
    c_end = b_end + (H_C + 2 * KVH_C) * HD_C
    w_za = w_in[:, :, :a_end].astype(BF16)
    w_zb = _pad_cols(w_in[:, :, a_end:b_end], 1024).astype(BF16)
    w_zc = w_in[:, :, b_end:c_end].astype(BF16)
    w_zg = w_in[:, :, c_end:].astype(BF16)
    w_uq = _pad_cols(mla_w_uq.reshape(DEPTH, Q_LORA, H_B, NOPE_B + ROPE_B), 2 * LANE)
    w_uq = w_uq.reshape(DEPTH, Q_LORA, H_B * 2 * LANE).astype(BF16)
    w_up = jnp.concatenate([_pad_cols(ffn_w_up[:, :, :D_FF], D_FF_PAD),
                            _pad_cols(ffn_w_up[:, :, D_FF:], D_FF_PAD)], axis=-1).astype(BF16)
    conv = jnp.concatenate([_pad_cols(ffn_conv[:, :, :D_FF], D_FF_PAD),
                            _pad_cols(ffn_conv[:, :, D_FF:], D_FF_PAD)], axis=-1)
    w_down = jnp.pad(ffn_w_down, ((0, 0), (0, D_FF_PAD - D_FF), (0, 0))).astype(BF16)
    return dict(w_za=w_za, w_zb=w_zb, w_zc=w_zc, w_zg=w_zg, w_uq=w_uq, w_ukv=mla_w_ukv.astype(BF16),
                w_a=w_branch_a.astype(BF16), w_b=w_branch_b.astype(BF16), w_c=w_branch_c.astype(BF16),
                w_o=w_out.astype(BF16), w_up=w_up, conv=conv, w_down=w_down)


def kernel(x_prompt, x_sample, state_hgrn, cache_mla_ckv, cache_mla_krope, cache_swa_k, cache_swa_v,
           c, c_ctx, w_mod, b_mod, norm_pre_attn, norm_post_attn, norm_pre_ffn, norm_post_ffn,
           w_in, hgrn_lb, hgrn_gnorm, mla_gq, mla_w_uq, mla_gkv, mla_w_ukv, swa_sink,
           w_branch_a, w_branch_b, w_branch_c, w_out, ffn_w_up, ffn_conv, ffn_w_down):
    wts = _prep_weights(w_in, mla_w_uq, mla_w_ukv, w_branch_a, w_branch_b, w_branch_c, w_out,
                        ffn_w_up, ffn_conv, ffn_w_down)
    cs = jnp.cumsum(jax.nn.softmax(hgrn_lb.astype(F32), axis=0), axis=0)
    lb_all = cs - cs[0]

    cvec = jnp.concatenate([c_ctx[None, :], c, jnp.zeros((MOD_ROWS - 1 - DEC_BATCH, D_MODEL), F32)], axis=0)
    mod = _modulation(cvec, w_mod, b_mod).reshape(DEPTH, MOD_ROWS, 6, 1, D_MODEL)

    hgrn_tables = _hgrn_tables()
    rope_b = _rope_tables(ROPE_B)
    rope_c = _rope_tables(HD_C)
    sink_b = jnp.broadcast_to(swa_sink[:, :, None], (DEPTH, H_C, LANE))

    x = jnp.concatenate([x_prompt.reshape(T_CTX, D_MODEL), x_sample.reshape(T_LAT, D_MODEL)], axis=0)
    new_hgrn, new_ckv, new_krope, new_k, new_v = [], [], [], [], []
    y = None
    for l in range(DEPTH):
        mod_l = mod[l]
        if l == 0:
            (h,) = _norm(x, npre=norm_pre_attn[l], mod_pre=mod_l, scale_idx=1, shift_idx=0)
        else:
            x, h = _norm(x, y=y, mod_post=mod[l - 1], gate_idx=5, npost=norm_post_ffn[l - 1],
                         npre=norm_pre_attn[l], mod_pre=mod_l, scale_idx=1, shift_idx=0)
        za = _mm(h, wts["w_za"][l], F32, "in_proj_a")
        zb = _mm(h, wts["w_zb"][l], F32, "in_proj_b")
        zc = _mm(h, wts["w_zc"][l], F32, "in_proj_c")
        zg = _mm(h, wts["w_zg"][l], F32, "in_proj_g")

        oa_ctx, s_ctx = _hgrn(za, lb_all[l], hgrn_gnorm[l], hgrn_tables, latent=False)
        (oa_lat,) = _hgrn(za, lb_all[l], hgrn_gnorm[l], hgrn_tables, latent=True, s0=state_hgrn[:, l])
        new_hgrn.append(s_ctx)

        (qb,) = _norm_mm(zb, 0, Q_LORA, mla_gq[l], wts["w_uq"][l], F32, emit_normed=False, name="mla_q_proj")
        ckv, kvb = _norm_mm(zb, 2, KV_LORA, mla_gkv[l], wts["w_ukv"][l], BF16, emit_normed=True,
                            name="mla_kv_proj")
        kv_cache = _mm(cache_mla_ckv[:, l].reshape(DEC_BATCH * PAST_LEN, KV_LORA), wts["w_ukv"][l], BF16,
                       "mla_kv_cache")
        kr_cache = _pad_cols(cache_mla_krope[:, l].reshape(DEC_BATCH * PAST_LEN, ROPE_B), LANE)
        ob_ctx = _mla_attn(qb, kvb, zb, latent=False)
        ob_lat = _mla_attn(qb, kvb, zb, latent=True, kvc=kv_cache, krc=kr_cache, tables=rope_b)
        new_ckv.append(ckv[:T_CTX].reshape(BATCH, SEQ, KV_LORA))
        new_krope.append(zb[:T_CTX, Q_LORA + KV_LORA:Q_LORA + KV_LORA + ROPE_B].reshape(BATCH, SEQ, ROPE_B))

        oc_ctx = _gqa_attn(zc, sink_b[l], latent=False)
        oc_lat = _gqa_attn(zc, sink_b[l], latent=True,
                           kc=cache_swa_k[:, l].reshape(DEC_BATCH * PAST_LEN, KVH_C * HD_C),
                           vc=cache_swa_v[:, l].reshape(DEC_BATCH * PAST_LEN, KVH_C * HD_C), tables=rope_c)
        kq = H_C * HD_C
        new_k.append(zc[:T_CTX, kq:kq + KVH_C * HD_C].reshape(BATCH, SEQ, KVH_C, HD_C))
        new_v.append(zc[:T_CTX, kq + KVH_C * HD_C:].reshape(BATCH, SEQ, KVH_C, HD_C))

        oa = jnp.concatenate([oa_ctx, oa_lat], axis=0)
        ob = jnp.concatenate([ob_ctx, ob_lat], axis=0)
        oc = jnp.concatenate([oc_ctx, oc_lat], axis=0)
        y = _merge(oa, ob, oc, wts["w_a"][l], wts["w_b"][l], wts["w_c"][l], zg, wts["w_o"][l])

        x, h = _norm(x, y=y, mod_post=mod_l, gate_idx=2, npost=norm_post_attn[l],
                     npre=norm_pre_ffn[l], mod_pre=mod_l, scale_idx=4, shift_idx=3)
        y = _ffn(h, wts["w_up"][l], wts["conv"][l], wts["w_down"][l])

    (x,) = _norm(x, y=y, mod_post=mod[DEPTH - 1], gate_idx=5, npost=norm_post_ffn[DEPTH - 1])
    return (x[:T_CTX].reshape(BATCH, SEQ, D_MODEL), x[T_CTX:].reshape(DEC_BATCH, DEC_SEQ, D_MODEL),
            jnp.stack(new_hgrn, axis=1), jnp.stack(new_ckv, axis=1), jnp.stack(new_krope, axis=1),
            jnp.stack(new_k, axis=1), jnp.stack(new_v, axis=1))
```

```python
import functools

import jax
import jax.numpy as jnp
import numpy as np
from jax import lax
from jax.experimental import pallas as pl
from jax.experimental.pallas import tpu as pltpu

F32 = jnp.float32
BF16 = jnp.bfloat16

D_MODEL = 2048
BATCH = 16
SEQ = 256
DEPTH = 4
DEC_BATCH = 4
DEC_SEQ = 1024
PAST_LEN = 256
GRID_W = 64
ROPE_BASE = 10000.0
EPS = 1e-6
NEG_INF = -1e30
H_A, DK_A, DV_A = 8, 128, 128
H_B, Q_LORA, KV_LORA, NOPE_B, ROPE_B, V_B = 8, 512, 256, 128, 64, 128
H_C, KVH_C, HD_C, WINDOW = 8, 2, 128, 128
N_BRANCH = 3
D_FF = 5504
CONV_W = 3

T_CTX = BATCH * SEQ
T_LAT = DEC_BATCH * DEC_SEQ
T_ALL = T_CTX + T_LAT
MOD_ROWS = 8
LANE = 128
D_FF_PAD = 5632
HGRN_CHUNK = 64
VMEM_LIMIT = 56 * 1024 * 1024


def _params(*sem):
    return pltpu.CompilerParams(dimension_semantics=sem, vmem_limit_bytes=VMEM_LIMIT)


def _mod_row(i, tm):
    return jnp.where(i * tm < T_CTX, 0, 1 + (i * tm - T_CTX) // DEC_SEQ)


def _rms(x):
    return x * lax.rsqrt(jnp.mean(x * x, axis=-1, keepdims=True) + EPS)


def _mod_kernel(c_ref, w_ref, b_ref, o_ref):
    cv = c_ref[...]
    s = (cv * jax.nn.sigmoid(cv)).astype(BF16)
    o_ref[0] = jnp.dot(s, w_ref[0].astype(BF16), preferred_element_type=F32) + b_ref[0]


def _modulation(cvec, w_mod, b_mod):
    tn = 1024
    n = 6 * D_MODEL
    return pl.pallas_call(
        _mod_kernel,
        grid=(DEPTH, n // tn),
        in_specs=[pl.BlockSpec((MOD_ROWS, D_MODEL), lambda l, j: (0, 0)),
                  pl.BlockSpec((1, D_MODEL, tn), lambda l, j: (l, 0, j)),
                  pl.BlockSpec((1, 1, tn), lambda l, j: (l, 0, j))],
        out_specs=pl.BlockSpec((1, MOD_ROWS, tn), lambda l, j: (l, 0, j)),
        out_shape=jax.ShapeDtypeStruct((DEPTH, MOD_ROWS, n), F32),
        compiler_params=_params("parallel", "parallel"),
        name="modulation",
    )(cvec, w_mod, b_mod.reshape(DEPTH, 1, n))


def _norm_kernel(*refs, has_y, has_h):
    it = iter(refs)
    x_ref = next(it)
    if has_y:
        y_ref, gate_ref, npost_ref = next(it), next(it), next(it)
    if has_h:
        npre_ref, scale_ref, shift_ref = next(it), next(it), next(it)
    x = x_ref[...]
    if has_y:
        xnew_ref = next(it)
        x = x + gate_ref[0, 0] * (_rms(y_ref[...]) * npost_ref[...])
        xnew_ref[...] = x
    if has_h:
        h_ref = next(it)
        h = (_rms(x) * npre_ref[...]) * (1.0 + scale_ref[0, 0]) + shift_ref[0, 0]
        h_ref[...] = h.astype(BF16)


def _norm(x, *, y=None, mod_post=None, gate_idx=None, npost=None,
          npre=None, mod_pre=None, scale_idx=None, shift_idx=None):
    tm = 512
    has_y, has_h = y is not None, npre is not None
    row = pl.BlockSpec((tm, D_MODEL), lambda i: (i, 0))
    vec = pl.BlockSpec((1, D_MODEL), lambda i: (0, 0))

    def modspec(k):
        return pl.BlockSpec((1, 1, 1, D_MODEL), lambda i: (_mod_row(i, tm), k, 0, 0))

    args, specs, out_shape, out_specs = [x], [row], [], []
    if has_y:
        args += [y, mod_post, npost.reshape(1, D_MODEL)]
        specs += [row, modspec(gate_idx), vec]
        out_shape.append(jax.ShapeDtypeStruct((T_ALL, D_MODEL), F32))
        out_specs.append(row)
    if has_h:
        args += [npre.reshape(1, D_MODEL), mod_pre, mod_pre]
        specs += [vec, modspec(scale_idx), modspec(shift_idx)]
        out_shape.append(jax.ShapeDtypeStruct((T_ALL, D_MODEL), BF16))
        out_specs.append(row)
    outs = pl.pallas_call(
        functools.partial(_norm_kernel, has_y=has_y, has_h=has_h),
        grid=(T_ALL // tm,),
        in_specs=specs, out_specs=out_specs, out_shape=out_shape,
        compiler_params=_params("parallel"),
        name="norm_y%d_h%d" % (has_y, has_h),
    )(*args)
    return outs


def _mm_kernel(x_ref, w_ref, o_ref):
    o_ref[...] = jnp.dot(x_ref[...].astype(BF16), w_ref[...],
                         preferred_element_type=F32).astype(o_ref.dtype)


def _mm(x, w, out_dtype, name):
    m, k = x.shape
    n = w.shape[1]
    tm = min(m, 1024)
    tn = min(n, 512)
    return pl.pallas_call(
        _mm_kernel,
        grid=(m // tm, n // tn),
        in_specs=[pl.BlockSpec((tm, k), lambda i, j: (i, 0)),
                  pl.BlockSpec((k, tn), lambda i, j: (0, j))],
        out_specs=pl.BlockSpec((tm, tn), lambda i, j: (i, j)),
        out_shape=jax.ShapeDtypeStruct((m, n), out_dtype),
        compiler_params=_params("parallel", "parallel"),
        name=name,
    )(x, w)


def _norm_mm_kernel(x_ref, g_ref, w_ref, *out_refs, emit_normed):
    xn = _rms(x_ref[...]) * g_ref[...]
    if emit_normed:
        out_refs[0][...] = xn
    out_refs[-1][...] = jnp.dot(xn.astype(BF16), w_ref[...],
                                preferred_element_type=F32).astype(out_refs[-1].dtype)


def _norm_mm(z, col_block, k, gain, w, out_dtype, *, emit_normed, name):
    m = z.shape[0]
    n = w.shape[1]
    tm = 512
    out_shape = [jax.ShapeDtypeStruct((m, n), out_dtype)]
    out_specs = [pl.BlockSpec((tm, n), lambda i: (i, 0))]
    if emit_normed:
        out_shape.insert(0, jax.ShapeDtypeStruct((m, k), F32))
        out_specs.insert(0, pl.BlockSpec((tm, k), lambda i: (i, 0)))
    return pl.pallas_call(
        functools.partial(_norm_mm_kernel, emit_normed=emit_normed),
        grid=(m // tm,),
        in_specs=[pl.BlockSpec((tm, k), lambda i: (i, col_block)),
                  pl.BlockSpec((1, k), lambda i: (0, 0)),
                  pl.BlockSpec((k, n), lambda i: (0, 0))],
        out_specs=out_specs, out_shape=out_shape,
        compiler_params=_params("parallel"),
        name=name,
    )(z, gain.reshape(1, k), w)


def _rope_tables(head_dim):
    n = DEC_SEQ
    half = head_dim // 2
    quarter = half // 2
    row = jnp.repeat(jnp.arange(n // GRID_W), GRID_W).astype(F32)
    col = jnp.tile(jnp.arange(GRID_W), n // GRID_W).astype(F32)
    inv = ROPE_BASE ** (-jnp.arange(0, half, 2, dtype=F32) / half)
    lane = jnp.arange(LANE)
    m = lane % half
    pos = jnp.where((lane // half)[None, :] == 0, row[:, None], col[:, None])
    ang = pos * inv[m % quarter][None, :]
    valid = (lane < head_dim)[None, :]
    cos = jnp.where(valid, jnp.cos(ang), 0.0)
    sin = jnp.where(valid, jnp.where(m < quarter, -1.0, 1.0)[None, :] * jnp.sin(ang), 0.0)
    return cos.astype(F32), sin.astype(F32)


def _rope(x, cos, sin, head_dim):
    quarter = head_dim // 4
    lane = lax.broadcasted_iota(jnp.int32, x.shape, 1)
    first = (lane % (2 * quarter)) < quarter
    partner = jnp.where(first, pltpu.roll(x, LANE - quarter, 1), pltpu.roll(x, quarter, 1))
    return x * cos + partner * sin


def _mla_attn_kernel(*refs, latent):
    if latent:
        q_ref, kv_ref, kr_ref, kvc_ref, krc_ref, cq_ref, sq_ref, ck_ref, sk_ref, _, o_ref = refs
    else:
        q_ref, kv_ref, kr_ref, o_ref = refs
    scale = (NOPE_B + ROPE_B) ** -0.5
    nt = (((1,), (1,)), ((), ()))
    kr = kr_ref[...]
    if latent:
        kr = _rope(kr, ck_ref[...], sk_ref[...], ROPE_B)
        krc = krc_ref[...].astype(BF16)
    kr = kr.astype(BF16)
    for h in range(H_B):
        c0 = h * 2 * LANE
        qn = q_ref[:, c0:c0 + LANE]
        qr = q_ref[:, c0 + LANE:c0 + 2 * LANE]
        if latent:
            qr = _rope(qr, cq_ref[...], sq_ref[...], ROPE_B)
        qh = jnp.concatenate([qn.astype(BF16), qr.astype(BF16)], axis=-1)
        kh = jnp.concatenate([kv_ref[:, c0:c0 + LANE], kr], axis=-1)
        vh = kv_ref[:, c0 + LANE:c0 + 2 * LANE]
        s = lax.dot_general(qh, kh, nt, preferred_element_type=F32) * scale
        m = jnp.max(s, axis=-1, keepdims=True)
        if latent:
            khc = jnp.concatenate([kvc_ref[:, c0:c0 + LANE], krc], axis=-1)
            vhc = kvc_ref[:, c0 + LANE:c0 + 2 * LANE]
            sc = lax.dot_general(qh, khc, nt, preferred_element_type=F32) * scale
            m = jnp.maximum(m, jnp.max(sc, axis=-1, keepdims=True))
        p = jnp.exp(s - m)
        l = jnp.sum(p, axis=-1, keepdims=True)
        o = jnp.dot(p.astype(BF16), vh, preferred_element_type=F32)
        if latent:
            pc = jnp.exp(sc - m)
            l = l + jnp.sum(pc, axis=-1, keepdims=True)
            o = o + jnp.dot(pc.astype(BF16), vhc, preferred_element_type=F32)
        o_ref[:, h * LANE:(h + 1) * LANE] = (o / l).astype(BF16)


def _rows_of(specs, args, prev):
    if prev is None:
        return specs, args, {}
    return specs + [pl.BlockSpec(memory_space=pl.ANY)], args + [prev], {len(args): 0}


def _mla_attn(q, kv, zb, *, latent, kvc=None, krc=None, tables=None, prev=None):
    nb, n = (DEC_BATCH, DEC_SEQ) if latent else (BATCH, SEQ)
    tq = 256
    nq = n // tq
    off = T_CTX // n if latent else 0
    offq = T_CTX // tq if latent else 0
    w = H_B * 2 * LANE
    specs = [pl.BlockSpec((tq, w), lambda b, i: (offq + b * nq + i, 0)),
             pl.BlockSpec((n, w), lambda b, i: (off + b, 0)),
             pl.BlockSpec((n, LANE), lambda b, i: (off + b, 6))]
    args = [q, kv, zb]
    if latent:
        cos, sin = tables
        specs += [pl.BlockSpec((PAST_LEN, w), lambda b, i: (b, 0)),
                  pl.BlockSpec((PAST_LEN, LANE), lambda b, i: (b, 0)),
                  pl.BlockSpec((tq, LANE), lambda b, i: (i, 0)),
                  pl.BlockSpec((tq, LANE), lambda b, i: (i, 0)),
                  pl.BlockSpec((n, LANE), lambda b, i: (0, 0)),
                  pl.BlockSpec((n, LANE), lambda b, i: (0, 0))]
        args += [kvc, krc, cos, sin, cos, sin]
    specs, args, aliases = _rows_of(specs, args, prev)
    return pl.pallas_call(
        functools.partial(_mla_attn_kernel, latent=latent),
        grid=(nb, nq),
        in_specs=specs,
        out_specs=pl.BlockSpec((tq, H_B * V_B), lambda b, i: (offq + b * nq + i, 0)),
        out_shape=jax.ShapeDtypeStruct((T_ALL, H_B * V_B), BF16),
        input_output_aliases=aliases,
        compiler_params=_params("parallel", "parallel"),
        name="mla_attn_lat" if latent else "mla_attn_ctx",
    )(*args)


def _gqa_attn_kernel(*refs, latent, tq):
    if latent:
        q_ref, k_ref, v_ref, kc_ref, vc_ref, sink_ref, cq_ref, sq_ref, ck_ref, sk_ref, _, o_ref = refs
    else:
        q_ref, k_ref, v_ref, sink_ref, o_ref = refs
    scale = HD_C ** -0.5
    nt = (((1,), (1,)), ((), ()))
    rep = H_C // KVH_C
    n = k_ref.shape[0]
    if latent:
        qpos = pl.program_id(1) * tq + lax.broadcasted_iota(jnp.int32, (tq, n), 0)
        kpos = lax.broadcasted_iota(jnp.int32, (tq, n), 1)
        band = jnp.abs(qpos - kpos) <= WINDOW
    for g in range(KVH_C):
        kg = k_ref[:, g * LANE:(g + 1) * LANE]
        if latent:
            kg = _rope(kg, ck_ref[...], sk_ref[...], HD_C)
            kcg = kc_ref[:, g * LANE:(g + 1) * LANE].astype(BF16)
            vcg = vc_ref[:, g * LANE:(g + 1) * LANE].astype(BF16)
        kg = kg.astype(BF16)
        vg = v_ref[:, g * LANE:(g + 1) * LANE].astype(BF16)
        for r in range(rep):
            h = g * rep + r
            qh = q_ref[:, h * LANE:(h + 1) * LANE]
            if latent:
                qh = _rope(qh, cq_ref[...], sq_ref[...], HD_C)
            qh = qh.astype(BF16)
            sk = sink_ref[h:h + 1, 0:1]
            s = lax.dot_general(qh, kg, nt, preferred_element_type=F32) * scale
            if latent:
                s = jnp.where(band, s, NEG_INF)
            m = jnp.maximum(jnp.max(s, axis=-1, keepdims=True), sk)
            if latent:
                sc = lax.dot_general(qh, kcg, nt, preferred_element_type=F32) * scale
                m = jnp.maximum(m, jnp.max(sc, axis=-1, keepdims=True))
            p = jnp.exp(s - m)
            l = jnp.sum(p, axis=-1, keepdims=True) + jnp.exp(sk - m)
            o = jnp.dot(p.astype(BF16), vg, preferred_element_type=F32)
            if latent:
                pc = jnp.exp(sc - m)
                l = l + jnp.sum(pc, axis=-1, keepdims=True)
                o = o + jnp.dot(pc.astype(BF16), vcg, preferred_element_type=F32)
            o_ref[:, h * LANE:(h + 1) * LANE] = (o / l).astype(BF16)


def _gqa_attn(zc, sink_b, *, latent, kc=None, vc=None, tables=None, prev=None):
    nb, n = (DEC_BATCH, DEC_SEQ) if latent else (BATCH, SEQ)
    tq = 256
    nq = n // tq
    off = T_CTX // n if latent else 0
    offq = T_CTX // tq if latent else 0
    wq, wk = H_C * HD_C, KVH_C * HD_C
    specs = [pl.BlockSpec((tq, wq), lambda b, i: (offq + b * nq + i, 0)),
             pl.BlockSpec((n, wk), lambda b, i: (off + b, wq // wk)),
             pl.BlockSpec((n, wk), lambda b, i: (off + b, wq // wk + 1))]
    args = [zc, zc, zc]
    if latent:
        specs += [pl.BlockSpec((PAST_LEN, wk), lambda b, i: (b, 0)),
                  pl.BlockSpec((PAST_LEN, wk), lambda b, i: (b, 0))]
        args += [kc, vc]
    specs.append(pl.BlockSpec((H_C, LANE), lambda b, i: (0, 0)))
    args.append(sink_b)
    if latent:
        cos, sin = tables
        specs += [pl.BlockSpec((tq, LANE), lambda b, i: (i, 0)),
                  pl.BlockSpec((tq, LANE), lambda b, i: (i, 0)),
                  pl.BlockSpec((n, LANE), lambda b, i: (0, 0)),
                  pl.BlockSpec((n, LANE), lambda b, i: (0, 0))]
        args += [cos, sin, cos, sin]
    specs, args, aliases = _rows_of(specs, args, prev)
    return pl.pallas_call(
        functools.partial(_gqa_attn_kernel, latent=latent, tq=tq),
        grid=(nb, nq),
        in_specs=specs,
        out_specs=pl.BlockSpec((tq, wq), lambda b, i: (offq + b * nq + i, 0)),
        out_shape=jax.ShapeDtypeStruct((T_ALL, wq), BF16),
        input_output_aliases=aliases,
        compiler_params=_params("parallel", "parallel"),
        name="gqa_attn_lat" if latent else "gqa_attn_ctx",
    )(*args)


def _hgrn_tables():
    c = HGRN_CHUNK
    halves = [c >> (i + 1) for i in range(c.bit_length() - 1)]
    out = []
    for forward in (True, False):
        sums = np.zeros((len(halves) + 1, c, c), np.float32)
        level = np.full((c, c), -1, np.int32)
        level[np.arange(c), np.arange(c)] = 0
        for li, m in enumerate(halves):
            for r in range(c):
                pos = r % (2 * m)
                mid = r - pos + m
                late = pos >= m
                if forward:
                    lo, hi = (mid, r + 1) if late else (r + 1, mid)
                else:
                    lo, hi = (mid, r) if late else (r, mid)
                sums[li, r, lo:hi] = 1.0
                for s in range(r - pos, r - pos + 2 * m):
                    s_late = (s % (2 * m)) >= m
                    if (late and not s_late) if forward else (not late and s_late):
                        level[r, s] = li + 1
        for r in range(c):
            if forward:
                sums[-1, r, :r + 1] = 1.0
            else:
                sums[-1, r, r:] = 1.0
        sums = sums.reshape(-1, c)
        out.append((jnp.asarray(np.concatenate([sums, sums, sums], axis=1), BF16),
                    jnp.asarray(np.concatenate([level, level], axis=1))))
    return out


def _hgrn_kernel(*refs, n, has_s0, emit_state):
    it = iter(refs)
    q_ref, xf_ref, xb_ref, v_ref, ag_ref, lb_ref, gn_ref = (next(it) for _ in range(7))
    sums_refs = (next(it), next(it))
    level_refs = (next(it), next(it))
    s0_ref = next(it) if has_s0 else None
    if has_s0:
        next(it)
    o_ref = next(it)
    sfin_ref = next(it) if emit_state else None
    o_scr, qe_scr, u_scr, e_scr, st_scr = (next(it) for _ in range(5))

    c = HGRN_CHUNK
    nc = n // c
    nlev = c.bit_length() - 1
    nt = (((1,), (1,)), ((), ()))
    tn = (((0,), (0,)), ((), ()))
    zero = jnp.zeros((c, LANE), BF16)

    def blockdiag(x):
        return jnp.concatenate([jnp.concatenate([x[:, :LANE], zero], axis=1),
                                jnp.concatenate([zero, x[:, LANE:]], axis=1)], axis=0)

    def gates(x, lb):
        e = jnp.exp(-jnp.abs(x))
        big = 1.0 / (1.0 + e)
        small = e * big
        pos = x >= 0.0
        return jnp.log(lb + (1.0 - lb) * jnp.where(pos, big, small)), (1.0 - lb) * jnp.where(pos, small, big)

    for d in range(2):
        for hh in range(2):
            st_scr[d, hh] = s0_ref[0, d, hh].T if has_s0 else jnp.zeros((DV_A, DK_A), F32)

    group = 4

    def intra(t, carry):
        jobs = [(u, d) for u in range(group) for d in range(2)]
        chunk_of = [t * group + u for u in range(group)]
        rows = {u: pl.ds(pl.multiple_of(chunk_of[u] * c, c), c) for u in range(group)}
        q = {ci: q_ref[rows[ci], :] for ci, _ in jobs}
        v = {ci: v_ref[rows[ci], :].astype(BF16) for ci, _ in jobs}
        k = {}

        dall = {}
        for ci, d in jobs:
            g, k[ci, d] = gates((xf_ref, xb_ref)[d][rows[ci], :], lb_ref[d:d + 1, :])
            g_hi = g.astype(BF16)
            rem = g - g_hi.astype(F32)
            g_mid = rem.astype(BF16)
            g_lo = (rem - g_mid.astype(F32)).astype(BF16)
            dall[ci, d] = jnp.dot(sums_refs[d][...], jnp.concatenate([g_hi, g_mid, g_lo], axis=0),
                                  preferred_element_type=F32)

        scores = {}
        for ci, d in jobs:
            rs = [lax.dot_general(q[ci].astype(BF16), blockdiag(k[ci, d].astype(BF16)), nt,
                                  preferred_element_type=F32)]
            for li in range(nlev):
                e = jnp.exp(dall[ci, d][li * c:(li + 1) * c, :])
                rs.append(lax.dot_general((q[ci] * e).astype(BF16), blockdiag((k[ci, d] * e).astype(BF16)), nt,
                                          preferred_element_type=F32))
            scores[ci, d] = rs

        for ci, d in jobs:
            level = level_refs[d][...]
            a = jnp.where(level == 0, scores[ci, d][0], 0.0)
            for li in range(nlev):
                a = jnp.where(level == li + 1, scores[ci, d][li + 1], a)
            o_scr[d, rows[ci], :] = jnp.dot(a.astype(BF16), blockdiag(v[ci]), preferred_element_type=F32)
            gc = dall[ci, d][nlev * c:, :]
            g_end = gc[c - 1:c, :] if d == 0 else gc[0:1, :]
            qe_scr[d, rows[ci], :] = (q[ci] * jnp.exp(gc)).astype(BF16)
            kd = (k[ci, d] * jnp.exp(g_end - gc)).astype(BF16)
            e_scr[d, chunk_of[ci]] = jnp.broadcast_to(jnp.exp(g_end), (8, 2 * LANE))
            for hh in range(2):
                hl = slice(hh * LANE, (hh + 1) * LANE)
                u_scr[d, chunk_of[ci], hh] = lax.dot_general(v[ci][:, hl], kd[:, hl], tn,
                                                             preferred_element_type=F32)
        return carry

    lax.fori_loop(0, nc // group, intra, 0)

    def inter(i, carry):
        for d in range(2):
            ci = i if d == 0 else nc - 1 - i
            rows = pl.ds(pl.multiple_of(ci * c, c), c)
            e = e_scr[d, ci]
            for hh in range(2):
                hl = slice(hh * LANE, (hh + 1) * LANE)
                st = st_scr[d, hh]
                o_scr[d, rows, hl] += lax.dot_general(qe_scr[d, rows, hl], st.astype(BF16), nt,
                                                      preferred_element_type=F32)
                st_scr[d, hh] = st * e[0:1, hl] + u_scr[d, ci, hh]
        return carry

    lax.fori_loop(0, nc, inter, 0, unroll=4)

    def finish(i, carry):
        rows = pl.ds(pl.multiple_of(i * c, c), c)
        o = o_scr[0, rows, :] + o_scr[1, rows, :]
        o = jnp.concatenate([_rms(o[:, :LANE]), _rms(o[:, LANE:])], axis=1)
        ag = ag_ref[rows, :]
        o_ref[rows, :] = (o * gn_ref[...] * (ag * jax.nn.sigmoid(ag))).astype(BF16)
        return carry

    lax.fori_loop(0, nc, finish, 0, unroll=4)
    if emit_state:
        for d in range(2):
            for hh in range(2):
                sfin_ref[0, d, hh] = st_scr[d, hh].T


def _hgrn(za, lb_l, gnorm, tables, *, latent, s0=None, prev=None):
    nb, n = (DEC_BATCH, DEC_SEQ) if latent else (BATCH, SEQ)
    off = T_CTX // n if latent else 0
    emit_state = not latent
    w = 2 * LANE
    pairs = H_A // 2
    c = HGRN_CHUNK
    nc = n // c

    def zspec(k):
        return pl.BlockSpec((n, w), lambda b, p: (off + b, k * pairs + p))

    def const(x):
        return pl.BlockSpec(x.shape, lambda b, p: (0, 0))

    (sums_f, level_f), (sums_b, level_b) = tables
    specs = [zspec(0), zspec(1), zspec(2), zspec(3), zspec(4),
             pl.BlockSpec((2, w), lambda b, p: (0, p)),
             pl.BlockSpec((1, w), lambda b, p: (0, 0)),
             const(sums_f), const(sums_b), const(level_f), const(level_b)]
    args = [za, za, za, za, za, lb_l, jnp.tile(gnorm.reshape(1, DV_A), (1, 2)),
            sums_f, sums_b, level_f, level_b]
    if latent:
        specs.append(pl.BlockSpec((1, 2, 2, DK_A, DV_A), lambda b, p: (b, 0, p, 0, 0)))
        args.append(s0)
    specs, args, aliases = _rows_of(specs, args, prev)
    out_shape = [jax.ShapeDtypeStruct((T_ALL, H_A * DV_A), BF16)]
    out_specs = [pl.BlockSpec((n, w), lambda b, p: (off + b, p))]
    if emit_state:
        out_shape.append(jax.ShapeDtypeStruct((nb, 2, H_A, DK_A, DV_A), F32))
        out_specs.append(pl.BlockSpec((1, 2, 2, DK_A, DV_A), lambda b, p: (b, 0, p, 0, 0)))
    scratch = [pltpu.VMEM((2, n, w), F32),
               pltpu.VMEM((2, n, w), BF16),
               pltpu.VMEM((2, nc, 2, DV_A, DK_A), F32),
               pltpu.VMEM((2, nc, 8, w), F32),
               pltpu.VMEM((2, 2, DV_A, DK_A), F32)]
    return pl.pallas_call(
        functools.partial(_hgrn_kernel, n=n, has_s0=latent, emit_state=emit_state),
        grid=(nb, pairs),
        in_specs=specs, out_specs=out_specs, out_shape=out_shape,
        scratch_shapes=scratch,
        input_output_aliases=aliases,
        compiler_params=_params("parallel", "parallel"),
        name="hgrn_lat" if latent else "hgrn_ctx",
    )(*args)


def _merge_kernel(oa_ref, ob_ref, oc_ref, wa_ref, wb_ref, wc_ref, g0_ref, g1_ref, g2_ref, wo_ref, y_ref):
    @pl.when(pl.program_id(1) == 0)
    def _():
        y_ref[...] = jnp.zeros_like(y_ref)

    merged = (jax.nn.sigmoid(g0_ref[...]) * jnp.dot(oa_ref[...], wa_ref[...], preferred_element_type=F32)
              + jax.nn.sigmoid(g1_ref[...]) * jnp.dot(ob_ref[...], wb_ref[...], preferred_element_type=F32)
              + jax.nn.sigmoid(g2_ref[...]) * jnp.dot(oc_ref[...], wc_ref[...], preferred_element_type=F32))
    y_ref[...] += jnp.dot(merged.astype(BF16), wo_ref[...], preferred_element_type=F32)


def _merge(oa, ob, oc, wa, wb, wc, zg, wo):
    tm, tn = 512, 512
    nj = D_MODEL // tn
    kb = H_A * DV_A
    o_spec = pl.BlockSpec((tm, kb), lambda i, j: (i, 0))
    w_spec = pl.BlockSpec((kb, tn), lambda i, j: (0, j))

    def gspec(k):
        return pl.BlockSpec((tm, tn), lambda i, j: (i, k * nj + j))

    return pl.pallas_call(
        _merge_kernel,
        grid=(T_ALL // tm, nj),
        in_specs=[o_spec, o_spec, o_spec, w_spec, w_spec, w_spec, gspec(0), gspec(1), gspec(2),
                  pl.BlockSpec((tn, D_MODEL), lambda i, j: (j, 0))],
        out_specs=pl.BlockSpec((tm, D_MODEL), lambda i, j: (i, 0)),
        out_shape=jax.ShapeDtypeStruct((T_ALL, D_MODEL), F32),
        compiler_params=_params("parallel", "arbitrary"),
        name="merge_out",
    )(oa, ob, oc, wa, wb, wc, zg, zg, zg, wo)


def _ffn_kernel(h_ref, wa_ref, wg_ref, ca_ref, cg_ref, wd_ref, y_ref, *, tm):
    i = pl.program_id(0)

    @pl.when(pl.program_id(1) == 0)
    def _():
        y_ref[...] = jnp.zeros_like(y_ref)

    h = h_ref[...]
    seq_len = jnp.where(i * tm < T_CTX, SEQ, DEC_SEQ)
    pos = lax.broadcasted_iota(jnp.int32, (tm, 1), 0) & (seq_len - 1)
    has_prev = pos != 0
    has_next = pos != seq_len - 1

    def conv(u, c_ref):
        prev = jnp.where(has_prev, pltpu.roll(u, 1, 0), 0.0)
        nxt = jnp.where(has_next, pltpu.roll(u, tm - 1, 0), 0.0)
        return c_ref[0:1, :] * prev + c_ref[1:2, :] * u + c_ref[2:3, :] * nxt

    a = conv(jnp.dot(h, wa_ref[...], preferred_element_type=F32), ca_ref)
    g = conv(jnp.dot(h, wg_ref[...], preferred_element_type=F32), cg_ref)
    act = (a * jax.nn.gelu(g)).astype(BF16)
    y_ref[...] += jnp.dot(act, wd_ref[...], preferred_element_type=F32)


def _ffn(h, w_up, conv, w_down):
    tm, tf = 1024, 512
    nj = D_FF_PAD // tf
    return pl.pallas_call(
        functools.partial(_ffn_kernel, tm=tm),
        grid=(T_ALL // tm, nj),
        in_specs=[pl.BlockSpec((tm, D_MODEL), lambda i, j: (i, 0)),
                  pl.BlockSpec((D_MODEL, tf), lambda i, j: (0, j)),
                  pl.BlockSpec((D_MODEL, tf), lambda i, j: (0, nj + j)),
                  pl.BlockSpec((CONV_W, tf), lambda i, j: (0, j)),
                  pl.BlockSpec((CONV_W, tf), lambda i, j: (0, nj + j)),
                  pl.BlockSpec((tf, D_MODEL), lambda i, j: (j, 0))],
        out_specs=pl.BlockSpec((tm, D_MODEL), lambda i, j: (i, 0)),
        out_shape=jax.ShapeDtypeStruct((T_ALL, D_MODEL), F32),
        compiler_params=_params("parallel", "arbitrary"),
        name="conv_ffn",
    )(h, w_up, w_up, conv, conv, w_down)


def _pad_cols(w, n):
    return jnp.pad(w, [(0, 0)] * (w.ndim - 1) + [(0, n - w.shape[-1])])


def _prep_weights(w_in, mla_w_uq, mla_w_ukv, w_branch_a, w_branch_b, w_branch_c, w_out,
                  ffn_w_up, ffn_conv, ffn_w_down):
    a_end = 5 * H_A * DK_A
    b_end = a_end + Q_LORA + KV_LORA + ROPE_B
    c_end = b_end + (H_C + 2 * KVH_C) * HD_C
    w_za = w_in[:, :, :a_end].astype(BF16)
    w_zb = _pad_cols(w_in[:, :, a_end:b_end], 1024).astype(BF16)
    w_zc = w_in[:, :, b_end:c_end].astype(BF16)
    w_zg = w_in[:, :, c_end:].astype(BF16)
    w_uq = _pad_cols(mla_w_uq.reshape(DEPTH, Q_LORA, H_B, NOPE_B + ROPE_B), 2 * LANE)
    w_uq = w_uq.reshape(DEPTH, Q_LORA, H_B * 2 * LANE).astype(BF16)
    w_up = jnp.concatenate([_pad_cols(ffn_w_up[:, :, :D_FF], D_FF_PAD),
                            _pad_cols(ffn_w_up[:, :, D_FF:], D_FF_PAD)], axis=-1).astype(BF16)
    conv = jnp.concatenate([_pad_cols(ffn_conv[:, :, :D_FF], D_FF_PAD),
                            _pad_cols(ffn_conv[:, :, D_FF:], D_FF_PAD)], axis=-1)
    w_down = jnp.pad(ffn_w_down, ((0, 0), (0, D_FF_PAD - D_FF), (0, 0))).astype(BF16)
    return dict(w_za=w_za, w_zb=w_zb, w_zc=w_zc, w_zg=w_zg, w_uq=w_uq, w_ukv=mla_w_ukv.astype(BF16),
                w_a=w_branch_a.astype(BF16), w_b=w_branch_b.astype(BF16), w_c=w_branch_c.astype(BF16),
                w_o=w_out.astype(BF16), w_up=w_up, conv=conv, w_down=w_down)


def kernel(x_prompt, x_sample, state_hgrn, cache_mla_ckv, cache_mla_krope, cache_swa_k, cache_swa_v,
           c, c_ctx, w_mod, b_mod, norm_pre_attn, norm_post_attn, norm_pre_ffn, norm_post_ffn,
           w_in, hgrn_lb, hgrn_gnorm, mla_gq, mla_w_uq, mla_gkv, mla_w_ukv, swa_sink,
           w_branch_a, w_branch_b, w_branch_c, w_out, ffn_w_up, ffn_conv, ffn_w_down):
    wts = _prep_weights(w_in, mla_w_uq, mla_w_ukv, w_branch_a, w_branch_b, w_branch_c, w_out,
                        ffn_w_up, ffn_conv, ffn_w_down)
    cs = jnp.cumsum(jax.nn.softmax(hgrn_lb.astype(F32), axis=0), axis=0)
    lb_all = cs - cs[0]

    cvec = jnp.concatenate([c_ctx[None, :], c, jnp.zeros((MOD_ROWS - 1 - DEC_BATCH, D_MODEL), F32)], axis=0)
    mod = _modulation(cvec, w_mod, b_mod).reshape(DEPTH, MOD_ROWS, 6, 1, D_MODEL)

    hgrn_tables = _hgrn_tables()
    rope_b = _rope_tables(ROPE_B)
    rope_c = _rope_tables(HD_C)
    sink_b = jnp.broadcast_to(swa_sink[:, :, None], (DEPTH, H_C, LANE))

    x = jnp.concatenate([x_prompt.reshape(T_CTX, D_MODEL), x_sample.reshape(T_LAT, D_MODEL)], axis=0)
    new_hgrn, new_ckv, new_krope, new_k, new_v = [], [], [], [], []
    y = None
    for l in range(DEPTH):
        mod_l = mod[l]
        if l == 0:
            (h,) = _norm(x, npre=norm_pre_attn[l], mod_pre=mod_l, scale_idx=1, shift_idx=0)
        else:
            x, h = _norm(x, y=y, mod_post=mod[l - 1], gate_idx=5, npost=norm_post_ffn[l - 1],
                         npre=norm_pre_attn[l], mod_pre=mod_l, scale_idx=1, shift_idx=0)
        za = _mm(h, wts["w_za"][l], F32, "in_proj_a")
        zb = _mm(h, wts["w_zb"][l], F32, "in_proj_b")
        zc = _mm(h, wts["w_zc"][l], F32, "in_proj_c")
        zg = _mm(h, wts["w_zg"][l], F32, "in_proj_g")

        oa, s_ctx = _hgrn(za, lb_all[l], hgrn_gnorm[l], hgrn_tables, latent=False)
        (oa,) = _hgrn(za, lb_all[l], hgrn_gnorm[l], hgrn_tables, latent=True, s0=state_hgrn[:, l], prev=oa)
        new_hgrn.append(s_ctx)

        (qb,) = _norm_mm(zb, 0, Q_LORA, mla_gq[l], wts["w_uq"][l], F32, emit_normed=False, name="mla_q_proj")
        ckv, kvb = _norm_mm(zb, 2, KV_LORA, mla_gkv[l], wts["w_ukv"][l], BF16, emit_normed=True,
                            name="mla_kv_proj")
        kv_cache = _mm(cache_mla_ckv[:, l].reshape(DEC_BATCH * PAST_LEN, KV_LORA), wts["w_ukv"][l], BF16,
                       "mla_kv_cache")
        kr_cache = _pad_cols(cache_mla_krope[:, l].reshape(DEC_BATCH * PAST_LEN, ROPE_B), LANE)
        ob = _mla_attn(qb, kvb, zb, latent=False)
        ob = _mla_attn(qb, kvb, zb, latent=True, kvc=kv_cache, krc=kr_cache, tables=rope_b, prev=ob)
        new_ckv.append(ckv[:T_CTX].reshape(BATCH, SEQ, KV_LORA))
        new_krope.append(zb[:T_CTX, Q_LORA + KV_LORA:Q_LORA + KV_LORA + ROPE_B].reshape(BATCH, SEQ, ROPE_B))

        oc = _gqa_attn(zc, sink_b[l], latent=False)
        oc = _gqa_attn(zc, sink_b[l], latent=True,
                       kc=cache_swa_k[:, l].reshape(DEC_BATCH * PAST_LEN, KVH_C * HD_C),
                       vc=cache_swa_v[:, l].reshape(DEC_BATCH * PAST_LEN, KVH_C * HD_C), tables=rope_c, prev=oc)
        kq = H_C * HD_C
        new_k.append(zc[:T_CTX, kq:kq + KVH_C * HD_C].reshape(BATCH, SEQ, KVH_C, HD_C))
        new_v.append(zc[:T_CTX, kq + KVH_C * HD_C:].reshape(BATCH, SEQ, KVH_C, HD_C))

        y = _merge(oa, ob, oc, wts["w_a"][l], wts["w_b"][l], wts["w_c"][l], zg, wts["w_o"][l])

        x, h = _norm(x, y=y, mod_post=mod_l, gate_idx=2, npost=norm_post_attn[l],
                     npre=norm_pre_ffn[l], mod_pre=mod_l, scale_idx=4, shift_idx=3)
        y = _ffn(h, wts["w_up"][l], wts["conv"][l], wts["w_down"][l])

    (x,) = _norm(x, y=y, mod_post=mod[DEPTH - 1], gate_idx=5, npost=norm_post_ffn[DEPTH - 1])
    return (x[:T_CTX].reshape(BATCH, SEQ, D_MODEL), x[T_CTX:].reshape(DEC_BATCH, DEC_SEQ, D_MODEL),
            jnp.stack(new_hgrn, axis=1), jnp.stack(new_ckv, axis=1), jnp.stack(new_krope, axis=1),
            jnp.stack(new_k, axis=1), jnp.stack(new_v, axis=1))
```

```python
import functools

import jax
import jax.numpy as jnp
import numpy as np
from jax import lax
from jax.experimental import pallas as pl
from jax.experimental.pallas import tpu as pltpu

F32 = jnp.float32
BF16 = jnp.bfloat16

D_MODEL = 2048
BATCH = 16
SEQ = 256
DEPTH = 4
DEC_BATCH = 4
DEC_SEQ = 1024
PAST_LEN = 256
GRID_W = 64
ROPE_BASE = 10000.0
EPS = 1e-6
NEG_INF = -1e30
H_A, DK_A, DV_A = 8, 128, 128
H_B, Q_LORA, KV_LORA, NOPE_B, ROPE_B, V_B = 8, 512, 256, 128, 64, 128
H_C, KVH_C, HD_C, WINDOW = 8, 2, 128, 128
N_BRANCH = 3
D_FF = 5504
CONV_W = 3

T_CTX = BATCH * SEQ
T_LAT = DEC_BATCH * DEC_SEQ
T_ALL = T_CTX + T_LAT
MOD_ROWS = 8
LANE = 128
D_FF_PAD = 5632
HGRN_CHUNK = 64
Z_A = 0
Z_B = 5 * H_A * DK_A
Z_KV = Z_B + Q_LORA
Z_KR = Z_KV + KV_LORA
Z_B_END = Z_KR + ROPE_B
Z_C = 6144
Z_CK = Z_C + H_C * HD_C
Z_CV = Z_CK + KVH_C * HD_C
Z_G = Z_CV + KVH_C * HD_C
Z_W = Z_G + N_BRANCH * D_MODEL
VMEM_LIMIT = 56 * 1024 * 1024


def _params(*sem):
    return pltpu.CompilerParams(dimension_semantics=sem, vmem_limit_bytes=VMEM_LIMIT)


def _mod_row(i, tm):
    return jnp.where(i * tm < T_CTX, 0, 1 + (i * tm - T_CTX) // DEC_SEQ)


def _rms(x):
    return x * lax.rsqrt(jnp.mean(x * x, axis=-1, keepdims=True) + EPS)


def _mod_kernel(c_ref, w_ref, b_ref, o_ref):
    cv = c_ref[...]
    s = (cv * jax.nn.sigmoid(cv)).astype(BF16)
    o_ref[0] = jnp.dot(s, w_ref[0].astype(BF16), preferred_element_type=F32) + b_ref[0]


def _modulation(cvec, w_mod, b_mod):
    tn = 1024
    n = 6 * D_MODEL
    return pl.pallas_call(
        _mod_kernel,
        grid=(DEPTH, n // tn),
        in_specs=[pl.BlockSpec((MOD_ROWS, D_MODEL), lambda l, j: (0, 0)),
                  pl.BlockSpec((1, D_MODEL, tn), lambda l, j: (l, 0, j)),
                  pl.BlockSpec((1, 1, tn), lambda l, j: (l, 0, j))],
        out_specs=pl.BlockSpec((1, MOD_ROWS, tn), lambda l, j: (l, 0, j)),
        out_shape=jax.ShapeDtypeStruct((DEPTH, MOD_ROWS, n), F32),
        compiler_params=_params("parallel", "parallel"),
        name="modulation",
    )(cvec, w_mod, b_mod.reshape(DEPTH, 1, n))


def _norm_kernel(*refs, has_y, has_h):
    it = iter(refs)
    x_ref = next(it)
    if has_y:
        y_ref, gate_ref, npost_ref = next(it), next(it), next(it)
    if has_h:
        npre_ref, scale_ref, shift_ref = next(it), next(it), next(it)
    x = x_ref[...]
    if has_y:
        xnew_ref = next(it)
        x = x + gate_ref[0, 0] * (_rms(y_ref[...]) * npost_ref[...])
        xnew_ref[...] = x
    if has_h:
        h_ref = next(it)
        h = (_rms(x) * npre_ref[...]) * (1.0 + scale_ref[0, 0]) + shift_ref[0, 0]
        h_ref[...] = h.astype(BF16)


def _norm(x, *, y=None, mod_post=None, gate_idx=None, npost=None,
          npre=None, mod_pre=None, scale_idx=None, shift_idx=None):
    tm = 512
    has_y, has_h = y is not None, npre is not None
    row = pl.BlockSpec((tm, D_MODEL), lambda i: (i, 0))
    vec = pl.BlockSpec((1, D_MODEL), lambda i: (0, 0))

    def modspec(k):
        return pl.BlockSpec((1, 1, 1, D_MODEL), lambda i: (_mod_row(i, tm), k, 0, 0))

    args, specs, out_shape, out_specs = [x], [row], [], []
    if has_y:
        args += [y, mod_post, npost.reshape(1, D_MODEL)]
        specs += [row, modspec(gate_idx), vec]
        out_shape.append(jax.ShapeDtypeStruct((T_ALL, D_MODEL), F32))
        out_specs.append(row)
    if has_h:
        args += [npre.reshape(1, D_MODEL), mod_pre, mod_pre]
        specs += [vec, modspec(scale_idx), modspec(shift_idx)]
        out_shape.append(jax.ShapeDtypeStruct((T_ALL, D_MODEL), BF16))
        out_specs.append(row)
    outs = pl.pallas_call(
        functools.partial(_norm_kernel, has_y=has_y, has_h=has_h),
        grid=(T_ALL // tm,),
        in_specs=specs, out_specs=out_specs, out_shape=out_shape,
        compiler_params=_params("parallel"),
        name="norm_y%d_h%d" % (has_y, has_h),
    )(*args)
    return outs


def _mm_kernel(x_ref, w_ref, o_ref):
    o_ref[...] = jnp.dot(x_ref[...].astype(BF16), w_ref[...],
                         preferred_element_type=F32).astype(o_ref.dtype)


def _mm(x, w, out_dtype, name):
    m, k = x.shape
    n = w.shape[1]
    tm = min(m, 1024)
    tn = min(n, 512)
    return pl.pallas_call(
        _mm_kernel,
        grid=(m // tm, n // tn),
        in_specs=[pl.BlockSpec((tm, k), lambda i, j: (i, 0)),
                  pl.BlockSpec((k, tn), lambda i, j: (0, j))],
        out_specs=pl.BlockSpec((tm, tn), lambda i, j: (i, j)),
        out_shape=jax.ShapeDtypeStruct((m, n), out_dtype),
        compiler_params=_params("parallel", "parallel"),
        name=name,
    )(x, w)


def _norm_mm_kernel(x_ref, g_ref, w_ref, *out_refs, emit_normed):
    xn = _rms(x_ref[...]) * g_ref[...]
    if emit_normed:
        out_refs[0][...] = xn
    out_refs[-1][...] = jnp.dot(xn.astype(BF16), w_ref[...],
                                preferred_element_type=F32).astype(out_refs[-1].dtype)


def _norm_mm(z, col_block, k, gain, w, out_dtype, *, emit_normed, name):
    m = z.shape[0]
    n = w.shape[1]
    tm = 512
    out_shape = [jax.ShapeDtypeStruct((m, n), out_dtype)]
    out_specs = [pl.BlockSpec((tm, n), lambda i: (i, 0))]
    if emit_normed:
        out_shape.insert(0, jax.ShapeDtypeStruct((m, k), F32))
        out_specs.insert(0, pl.BlockSpec((tm, k), lambda i: (i, 0)))
    return pl.pallas_call(
        functools.partial(_norm_mm_kernel, emit_normed=emit_normed),
        grid=(m // tm,),
        in_specs=[pl.BlockSpec((tm, k), lambda i: (i, col_block)),
                  pl.BlockSpec((1, k), lambda i: (0, 0)),
                  pl.BlockSpec((k, n), lambda i: (0, 0))],
        out_specs=out_specs, out_shape=out_shape,
        compiler_params=_params("parallel"),
        name=name,
    )(z, gain.reshape(1, k), w)


def _rope_tables(head_dim):
    n = DEC_SEQ
    half = head_dim // 2
    quarter = half // 2
    row = jnp.repeat(jnp.arange(n // GRID_W), GRID_W).astype(F32)
    col = jnp.tile(jnp.arange(GRID_W), n // GRID_W).astype(F32)
    inv = ROPE_BASE ** (-jnp.arange(0, half, 2, dtype=F32) / half)
    lane = jnp.arange(LANE)
    m = lane % half
    pos = jnp.where((lane // half)[None, :] == 0, row[:, None], col[:, None])
    ang = pos * inv[m % quarter][None, :]
    valid = (lane < head_dim)[None, :]
    cos = jnp.where(valid, jnp.cos(ang), 0.0)
    sin = jnp.where(valid, jnp.where(m < quarter, -1.0, 1.0)[None, :] * jnp.sin(ang), 0.0)
    return cos.astype(F32), sin.astype(F32)


def _rope(x, cos, sin, head_dim):
    quarter = head_dim // 4
    lane = lax.broadcasted_iota(jnp.int32, x.shape, 1)
    first = (lane % (2 * quarter)) < quarter
    partner = jnp.where(first, pltpu.roll(x, LANE - quarter, 1), pltpu.roll(x, quarter, 1))
    return x * cos + partner * sin


def _mla_attn_kernel(*refs, latent):
    if latent:
        q_ref, kv_ref, kr_ref, kvc_ref, krc_ref, cq_ref, sq_ref, ck_ref, sk_ref, _, o_ref = refs
    else:
        q_ref, kv_ref, kr_ref, o_ref = refs
    scale = (NOPE_B + ROPE_B) ** -0.5
    nt = (((1,), (1,)), ((), ()))
    kr = kr_ref[...]
    if latent:
        kr = _rope(kr, ck_ref[...], sk_ref[...], ROPE_B)
        krc = krc_ref[...].astype(BF16)
    kr = kr.astype(BF16)
    for h in range(H_B):
        c0 = h * 2 * LANE
        qn = q_ref[:, c0:c0 + LANE]
        qr = q_ref[:, c0 + LANE:c0 + 2 * LANE]
        if latent:
            qr = _rope(qr, cq_ref[...], sq_ref[...], ROPE_B)
        qh = jnp.concatenate([(qn * scale).astype(BF16), (qr * scale).astype(BF16)], axis=-1)
        kh = jnp.concatenate([kv_ref[:, c0:c0 + LANE], kr], axis=-1)
        vh = kv_ref[:, c0 + LANE:c0 + 2 * LANE]
        s = lax.dot_general(qh, kh, nt, preferred_element_type=F32)
        m = jnp.max(s, axis=-1, keepdims=True)
        if latent:
            khc = jnp.concatenate([kvc_ref[:, c0:c0 + LANE], krc], axis=-1)
            vhc = kvc_ref[:, c0 + LANE:c0 + 2 * LANE]
            sc = lax.dot_general(qh, khc, nt, preferred_element_type=F32)
            m = jnp.maximum(m, jnp.max(sc, axis=-1, keepdims=True))
        p = jnp.exp(s - m)
        l = jnp.sum(p, axis=-1, keepdims=True)
        o = jnp.dot(p.astype(BF16), vh, preferred_element_type=F32)
        if latent:
            pc = jnp.exp(sc - m)
            l = l + jnp.sum(pc, axis=-1, keepdims=True)
            o = o + jnp.dot(pc.astype(BF16), vhc, preferred_element_type=F32)
        o_ref[:, h * LANE:(h + 1) * LANE] = (o / l).astype(BF16)


def _rows_of(specs, args, prev):
    if prev is None:
        return specs, args, {}
    return specs + [pl.BlockSpec(memory_space=pl.ANY)], args + [prev], {len(args): 0}


def _mla_attn(q, kv, z, *, latent, kvc=None, krc=None, tables=None, prev=None):
    nb, n = (DEC_BATCH, DEC_SEQ) if latent else (BATCH, SEQ)
    tq = 256
    nq = n // tq
    off = T_CTX // n if latent else 0
    offq = T_CTX // tq if latent else 0
    w = H_B * 2 * LANE
    specs = [pl.BlockSpec((tq, w), lambda b, i: (offq + b * nq + i, 0)),
             pl.BlockSpec((n, w), lambda b, i: (off + b, 0)),
             pl.BlockSpec((n, LANE), lambda b, i: (off + b, Z_KR // LANE))]
    args = [q, kv, z]
    if latent:
        cos, sin = tables
        specs += [pl.BlockSpec((PAST_LEN, w), lambda b, i: (b, 0)),
                  pl.BlockSpec((PAST_LEN, LANE), lambda b, i: (b, 0)),
                  pl.BlockSpec((tq, LANE), lambda b, i: (i, 0)),
                  pl.BlockSpec((tq, LANE), lambda b, i: (i, 0)),
                  pl.BlockSpec((n, LANE), lambda b, i: (0, 0)),
                  pl.BlockSpec((n, LANE), lambda b, i: (0, 0))]
        args += [kvc, krc, cos, sin, cos, sin]
    specs, args, aliases = _rows_of(specs, args, prev)
    return pl.pallas_call(
        functools.partial(_mla_attn_kernel, latent=latent),
        grid=(nb, nq),
        in_specs=specs,
        out_specs=pl.BlockSpec((tq, H_B * V_B), lambda b, i: (offq + b * nq + i, 0)),
        out_shape=jax.ShapeDtypeStruct((T_ALL, H_B * V_B), BF16),
        input_output_aliases=aliases,
        compiler_params=_params("parallel", "parallel"),
        name="mla_attn_lat" if latent else "mla_attn_ctx",
    )(*args)


def _gqa_attn_kernel(*refs, latent, tq):
    if latent:
        q_ref, k_ref, v_ref, kc_ref, vc_ref, sink_ref, cq_ref, sq_ref, ck_ref, sk_ref, _, o_ref = refs
    else:
        q_ref, k_ref, v_ref, sink_ref, o_ref = refs
    scale = HD_C ** -0.5
    nt = (((1,), (1,)), ((), ()))
    rep = H_C // KVH_C
    n = k_ref.shape[0]
    if latent:
        kw = tq + 2 * WINDOW
        q0 = pl.program_id(1) * tq
        k0 = pl.multiple_of(jnp.clip(q0 - WINDOW, 0, n - kw), WINDOW)
        keys = pl.ds(k0, kw)
        qpos = q0 + lax.broadcasted_iota(jnp.int32, (tq, kw), 0)
        kpos = k0 + lax.broadcasted_iota(jnp.int32, (tq, kw), 1)
        band = jnp.abs(qpos - kpos) <= WINDOW
    else:
        keys = slice(None)
    for g in range(KVH_C):
        kg = k_ref[keys, g * LANE:(g + 1) * LANE]
        if latent:
            kg = _rope(kg, ck_ref[keys, :], sk_ref[keys, :], HD_C)
            kcg = kc_ref[:, g * LANE:(g + 1) * LANE].astype(BF16)
            vcg = vc_ref[:, g * LANE:(g + 1) * LANE].astype(BF16)
        kg = kg.astype(BF16)
        vg = v_ref[keys, g * LANE:(g + 1) * LANE].astype(BF16)
        for r in range(rep):
            h = g * rep + r
            qh = q_ref[:, h * LANE:(h + 1) * LANE]
            if latent:
                qh = _rope(qh, cq_ref[...], sq_ref[...], HD_C)
            qh = (qh * scale).astype(BF16)
            sk = sink_ref[h:h + 1, 0:1]
            s = lax.dot_general(qh, kg, nt, preferred_element_type=F32)
            if latent:
                s = jnp.where(band, s, NEG_INF)
            m = jnp.maximum(jnp.max(s, axis=-1, keepdims=True), sk)
            if latent:
                sc = lax.dot_general(qh, kcg, nt, preferred_element_type=F32)
                m = jnp.maximum(m, jnp.max(sc, axis=-1, keepdims=True))
            p = jnp.exp(s - m)
            l = jnp.sum(p, axis=-1, keepdims=True) + jnp.exp(sk - m)
            o = jnp.dot(p.astype(BF16), vg, preferred_element_type=F32)
            if latent:
                pc = jnp.exp(sc - m)
                l = l + jnp.sum(pc, axis=-1, keepdims=True)
                o = o + jnp.dot(pc.astype(BF16), vcg, preferred_element_type=F32)
            o_ref[:, h * LANE:(h + 1) * LANE] = (o / l).astype(BF16)


def _gqa_attn(z, sink_b, *, latent, kc=None, vc=None, tables=None, prev=None):
    nb, n = (DEC_BATCH, DEC_SEQ) if latent else (BATCH, SEQ)
    tq = 256
    nq = n // tq
    off = T_CTX // n if latent else 0
    offq = T_CTX // tq if latent else 0
    wq, wk = H_C * HD_C, KVH_C * HD_C
    specs = [pl.BlockSpec((tq, wq), lambda b, i: (offq + b * nq + i, Z_C // wq)),
             pl.BlockSpec((n, wk), lambda b, i: (off + b, Z_CK // wk)),
             pl.BlockSpec((n, wk), lambda b, i: (off + b, Z_CV // wk))]
    args = [z, z, z]
    if latent:
        specs += [pl.BlockSpec((PAST_LEN, wk), lambda b, i: (b, 0)),
                  pl.BlockSpec((PAST_LEN, wk), lambda b, i: (b, 0))]
        args += [kc, vc]
    specs.append(pl.BlockSpec((H_C, LANE), lambda b, i: (0, 0)))
    args.append(sink_b)
    if latent:
        cos, sin = tables
        specs += [pl.BlockSpec((tq, LANE), lambda b, i: (i, 0)),
                  pl.BlockSpec((tq, LANE), lambda b, i: (i, 0)),
                  pl.BlockSpec((n, LANE), lambda b, i: (0, 0)),
                  pl.BlockSpec((n, LANE), lambda b, i: (0, 0))]
        args += [cos, sin, cos, sin]
    specs, args, aliases = _rows_of(specs, args, prev)
    return pl.pallas_call(
        functools.partial(_gqa_attn_kernel, latent=latent, tq=tq),
        grid=(nb, nq),
        in_specs=specs,
        out_specs=pl.BlockSpec((tq, wq), lambda b, i: (offq + b * nq + i, 0)),
        out_shape=jax.ShapeDtypeStruct((T_ALL, wq), BF16),
        input_output_aliases=aliases,
        compiler_params=_params("parallel", "parallel"),
        name="gqa_attn_lat" if latent else "gqa_attn_ctx",
    )(*args)


def _hgrn_tables():
    c = HGRN_CHUNK
    halves = [c >> (i + 1) for i in range(c.bit_length() - 1)]
    out = []
    for forward in (True, False):
        sums = np.zeros((len(halves) + 1, c, c), np.float32)
        level = np.full((c, c), -1, np.int32)
        level[np.arange(c), np.arange(c)] = 0
        for li, m in enumerate(halves):
            for r in range(c):
                pos = r % (2 * m)
                mid = r - pos + m
                late = pos >= m
                if forward:
                    lo, hi = (mid, r + 1) if late else (r + 1, mid)
                else:
                    lo, hi = (mid, r) if late else (r, mid)
                sums[li, r, lo:hi] = 1.0
                for s in range(r - pos, r - pos + 2 * m):
                    s_late = (s % (2 * m)) >= m
                    if (late and not s_late) if forward else (not late and s_late):
                        level[r, s] = li + 1
        for r in range(c):
            if forward:
                sums[-1, r, :r + 1] = 1.0
            else:
                sums[-1, r, r:] = 1.0
        sums = sums.reshape(-1, c)
        out.append((jnp.asarray(np.concatenate([sums, sums, sums], axis=1), BF16),
                    jnp.asarray(np.concatenate([level, level], axis=1))))
    return out


def _hgrn_kernel(*refs, n, has_s0, emit_state):
    it = iter(refs)
    q_ref, xf_ref, xb_ref, v_ref, ag_ref, lb_ref, gn_ref = (next(it) for _ in range(7))
    sums_refs = (next(it), next(it))
    level_refs = (next(it), next(it))
    s0_ref = next(it) if has_s0 else None
    if has_s0:
        next(it)
    o_ref = next(it)
    sfin_ref = next(it) if emit_state else None
    o_scr, qe_scr, u_scr, e_scr, st_scr = (next(it) for _ in range(5))

    c = HGRN_CHUNK
    nc = n // c
    nlev = c.bit_length() - 1
    nt = (((1,), (1,)), ((), ()))
    tn = (((0,), (0,)), ((), ()))
    zero = jnp.zeros((c, LANE), BF16)

    def blockdiag(x):
        return jnp.concatenate([jnp.concatenate([x[:, :LANE], zero], axis=1),
                                jnp.concatenate([zero, x[:, LANE:]], axis=1)], axis=0)

    def gates(x, lb):
        e = jnp.exp(-jnp.abs(x))
        big = 1.0 / (1.0 + e)
        small = e * big
        pos = x >= 0.0
        return jnp.log(lb + (1.0 - lb) * jnp.where(pos, big, small)), (1.0 - lb) * jnp.where(pos, small, big)

    for d in range(2):
        for hh in range(2):
            st_scr[d, hh] = s0_ref[0, d, hh].T if has_s0 else jnp.zeros((DV_A, DK_A), F32)

    group = 4

    def intra(t, carry):
        jobs = [(u, d) for u in range(group) for d in range(2)]
        chunk_of = [t * group + u for u in range(group)]
        rows = {u: pl.ds(pl.multiple_of(chunk_of[u] * c, c), c) for u in range(group)}
        q = {ci: q_ref[rows[ci], :] for ci, _ in jobs}
        v = {ci: v_ref[rows[ci], :].astype(BF16) for ci, _ in jobs}
        k = {}

        dall = {}
        for ci, d in jobs:
            g, k[ci, d] = gates((xf_ref, xb_ref)[d][rows[ci], :], lb_ref[d:d + 1, :])
            g_hi = g.astype(BF16)
            rem = g - g_hi.astype(F32)
            g_mid = rem.astype(BF16)
            g_lo = (rem - g_mid.astype(F32)).astype(BF16)
            dall[ci, d] = jnp.dot(sums_refs[d][...], jnp.concatenate([g_hi, g_mid, g_lo], axis=0),
                                  preferred_element_type=F32)

        scores = {}
        for ci, d in jobs:
            rs = [lax.dot_general(q[ci].astype(BF16), blockdiag(k[ci, d].astype(BF16)), nt,
                                  preferred_element_type=F32)]
            for li in range(nlev):
                e = jnp.exp(dall[ci, d][li * c:(li + 1) * c, :])
                rs.append(lax.dot_general((q[ci] * e).astype(BF16), blockdiag((k[ci, d] * e).astype(BF16)), nt,
                                          preferred_element_type=F32))
            scores[ci, d] = rs

        for ci, d in jobs:
            level = level_refs[d][...]
            a = jnp.where(level == 0, scores[ci, d][0], 0.0)
            for li in range(nlev):
                a = jnp.where(level == li + 1, scores[ci, d][li + 1], a)
            o_scr[d, rows[ci], :] = jnp.dot(a.astype(BF16), blockdiag(v[ci]), preferred_element_type=F32)
            gc = dall[ci, d][nlev * c:, :]
            g_end = gc[c - 1:c, :] if d == 0 else gc[0:1, :]
            qe_scr[d, rows[ci], :] = (q[ci] * jnp.exp(gc)).astype(BF16)
            kd = (k[ci, d] * jnp.exp(g_end - gc)).astype(BF16)
            e_scr[d, chunk_of[ci]] = jnp.broadcast_to(jnp.exp(g_end), (8, 2 * LANE))
            for hh in range(2):
                hl = slice(hh * LANE, (hh + 1) * LANE)
                u_scr[d, chunk_of[ci], hh] = lax.dot_general(v[ci][:, hl], kd[:, hl], tn,
                                                             preferred_element_type=F32)
        return carry

    lax.fori_loop(0, nc // group, intra, 0)

    def inter(i, carry):
        for d in range(2):
            ci = i if d == 0 else nc - 1 - i
            rows = pl.ds(pl.multiple_of(ci * c, c), c)
            e = e_scr[d, ci]
            for hh in range(2):
                hl = slice(hh * LANE, (hh + 1) * LANE)
                st = st_scr[d, hh]
                o_scr[d, rows, hl] += lax.dot_general(qe_scr[d, rows, hl], st.astype(BF16), nt,
                                                      preferred_element_type=F32)
                st_scr[d, hh] = st * e[0:1, hl] + u_scr[d, ci, hh]
        return carry

    lax.fori_loop(0, nc, inter, 0, unroll=4)

    def finish(i, carry):
        rows = pl.ds(pl.multiple_of(i * c, c), c)
        o = o_scr[0, rows, :] + o_scr[1, rows, :]
        o = jnp.concatenate([_rms(o[:, :LANE]), _rms(o[:, LANE:])], axis=1)
        ag = ag_ref[rows, :]
        o_ref[rows, :] = (o * gn_ref[...] * (ag * jax.nn.sigmoid(ag))).astype(BF16)
        return carry

    lax.fori_loop(0, nc, finish, 0, unroll=4)
    if emit_state:
        for d in range(2):
            for hh in range(2):
                sfin_ref[0, d, hh] = st_scr[d, hh].T


def _hgrn(z, lb_l, gnorm, tables, *, latent, s0=None, prev=None):
    nb, n = (DEC_BATCH, DEC_SEQ) if latent else (BATCH, SEQ)
    off = T_CTX // n if latent else 0
    emit_state = not latent
    w = 2 * LANE
    pairs = H_A // 2
    c = HGRN_CHUNK
    nc = n // c

    def zspec(k):
        return pl.BlockSpec((n, w), lambda b, p: (off + b, Z_A // w + k * pairs + p))

    def const(x):
        return pl.BlockSpec(x.shape, lambda b, p: (0, 0))

    (sums_f, level_f), (sums_b, level_b) = tables
    specs = [zspec(0), zspec(1), zspec(2), zspec(3), zspec(4),
             pl.BlockSpec((2, w), lambda b, p: (0, p)),
             pl.BlockSpec((1, w), lambda b, p: (0, 0)),
             const(sums_f), const(sums_b), const(level_f), const(level_b)]
    args = [z, z, z, z, z, lb_l, jnp.tile(gnorm.reshape(1, DV_A), (1, 2)),
            sums_f, sums_b, level_f, level_b]
    if latent:
        specs.append(pl.BlockSpec((1, 2, 2, DK_A, DV_A), lambda b, p: (b, 0, p, 0, 0)))
        args.append(s0)
    specs, args, aliases = _rows_of(specs, args, prev)
    out_shape = [jax.ShapeDtypeStruct((T_ALL, H_A * DV_A), BF16)]
    out_specs = [pl.BlockSpec((n, w), lambda b, p: (off + b, p))]
    if emit_state:
        out_shape.append(jax.ShapeDtypeStruct((nb, 2, H_A, DK_A, DV_A), F32))
        out_specs.append(pl.BlockSpec((1, 2, 2, DK_A, DV_A), lambda b, p: (b, 0, p, 0, 0)))
    scratch = [pltpu.VMEM((2, n, w), F32),
               pltpu.VMEM((2, n, w), BF16),
               pltpu.VMEM((2, nc, 2, DV_A, DK_A), F32),
               pltpu.VMEM((2, nc, 8, w), F32),
               pltpu.VMEM((2, 2, DV_A, DK_A), F32)]
    return pl.pallas_call(
        functools.partial(_hgrn_kernel, n=n, has_s0=latent, emit_state=emit_state),
        grid=(nb, pairs),
        in_specs=specs, out_specs=out_specs, out_shape=out_shape,
        scratch_shapes=scratch,
        input_output_aliases=aliases,
        compiler_params=_params("parallel", "parallel"),
        name="hgrn_lat" if latent else "hgrn_ctx",
    )(*args)


def _merge_kernel(oa_ref, ob_ref, oc_ref, wa_ref, wb_ref, wc_ref, g0_ref, g1_ref, g2_ref, wo_ref, y_ref):
    @pl.when(pl.program_id(1) == 0)
    def _():
        y_ref[...] = jnp.zeros_like(y_ref)

    merged = (jax.nn.sigmoid(g0_ref[...]) * jnp.dot(oa_ref[...], wa_ref[...], preferred_element_type=F32)
              + jax.nn.sigmoid(g1_ref[...]) * jnp.dot(ob_ref[...], wb_ref[...], preferred_element_type=F32)
              + jax.nn.sigmoid(g2_ref[...]) * jnp.dot(oc_ref[...], wc_ref[...], preferred_element_type=F32))
    y_ref[...] += jnp.dot(merged.astype(BF16), wo_ref[...], preferred_element_type=F32)


def _merge(oa, ob, oc, wa, wb, wc, z, wo):
    tm, tn = 512, 512
    nj = D_MODEL // tn
    kb = H_A * DV_A
    o_spec = pl.BlockSpec((tm, kb), lambda i, j: (i, 0))
    w_spec = pl.BlockSpec((kb, tn), lambda i, j: (0, j))

    def gspec(k):
        return pl.BlockSpec((tm, tn), lambda i, j: (i, Z_G // tn + k * nj + j))

    return pl.pallas_call(
        _merge_kernel,
        grid=(T_ALL // tm, nj),
        in_specs=[o_spec, o_spec, o_spec, w_spec, w_spec, w_spec, gspec(0), gspec(1), gspec(2),
                  pl.BlockSpec((tn, D_MODEL), lambda i, j: (j, 0))],
        out_specs=pl.BlockSpec((tm, D_MODEL), lambda i, j: (i, 0)),
        out_shape=jax.ShapeDtypeStruct((T_ALL, D_MODEL), F32),
        compiler_params=_params("parallel", "arbitrary"),
        name="merge_out",
    )(oa, ob, oc, wa, wb, wc, z, z, z, wo)


def _ffn_kernel(h_ref, wa_ref, wg_ref, ca_ref, cg_ref, wd_ref, y_ref, *, tm):
    i = pl.program_id(0)

    @pl.when(pl.program_id(1) == 0)
    def _():
        y_ref[...] = jnp.zeros_like(y_ref)

    h = h_ref[...]
    seq_len = jnp.where(i * tm < T_CTX, SEQ, DEC_SEQ)
    pos = lax.broadcasted_iota(jnp.int32, (tm, 1), 0) & (seq_len - 1)
    has_prev = pos != 0
    has_next = pos != seq_len - 1

    def conv(u, c_ref):
        prev = jnp.where(has_prev, pltpu.roll(u, 1, 0), 0.0)
        nxt = jnp.where(has_next, pltpu.roll(u, tm - 1, 0), 0.0)
        return c_ref[0:1, :] * prev + c_ref[1:2, :] * u + c_ref[2:3, :] * nxt

    a = conv(jnp.dot(h, wa_ref[...], preferred_element_type=F32), ca_ref)
    g = conv(jnp.dot(h, wg_ref[...], preferred_element_type=F32), cg_ref)
    act = (a * jax.nn.gelu(g)).astype(BF16)
    y_ref[...] += jnp.dot(act, wd_ref[...], preferred_element_type=F32)


def _ffn(h, w_up, conv, w_down):
    tm, tf = 1024, 512
    nj = D_FF_PAD // tf
    return pl.pallas_call(
        functools.partial(_ffn_kernel, tm=tm),
        grid=(T_ALL // tm, nj),
        in_specs=[pl.BlockSpec((tm, D_MODEL), lambda i, j: (i, 0)),
                  pl.BlockSpec((D_MODEL, tf), lambda i, j: (0, j)),
                  pl.BlockSpec((D_MODEL, tf), lambda i, j: (0, nj + j)),
                  pl.BlockSpec((CONV_W, tf), lambda i, j: (0, j)),
                  pl.BlockSpec((CONV_W, tf), lambda i, j: (0, nj + j)),
                  pl.BlockSpec((tf, D_MODEL), lambda i, j: (j, 0))],
        out_specs=pl.BlockSpec((tm, D_MODEL), lambda i, j: (i, 0)),
        out_shape=jax.ShapeDtypeStruct((T_ALL, D_MODEL), F32),
        compiler_params=_params("parallel", "arbitrary"),
        name="conv_ffn",
    )(h, w_up, w_up, conv, conv, w_down)


def _pad_cols(w, n):
    return jnp.pad(w, [(0, 0)] * (w.ndim - 1) + [(0, n - w.shape[-1])])


def _prep_weights(w_in, mla_w_uq, mla_w_ukv, w_branch_a, w_branch_b, w_branch_c, w_out,
                  ffn_w_up, ffn_conv, ffn_w_down):
    w_z = jnp.concatenate([w_in[:, :, :Z_B_END].astype(BF16),
                           jnp.zeros((DEPTH, D_MODEL, Z_C - Z_B_END), BF16),
                           w_in[:, :, Z_B_END:].astype(BF16)], axis=-1)
    w_uq = _pad_cols(mla_w_uq.reshape(DEPTH, Q_LORA, H_B, NOPE_B + ROPE_B), 2 * LANE)
    w_uq = w_uq.reshape(DEPTH, Q_LORA, H_B * 2 * LANE).astype(BF16)
    w_up = _pad_cols(ffn_w_up.reshape(DEPTH, D_MODEL, 2, D_FF), D_FF_PAD)
    w_up = w_up.reshape(DEPTH, D_MODEL, 2 * D_FF_PAD).astype(BF16)
    conv = _pad_cols(ffn_conv.reshape(DEPTH, CONV_W, 2, D_FF), D_FF_PAD).reshape(DEPTH, CONV_W, 2 * D_FF_PAD)
    w_down = jnp.pad(ffn_w_down, ((0, 0), (0, D_FF_PAD - D_FF), (0, 0))).astype(BF16)
    return dict(w_z=w_z, w_uq=w_uq, w_ukv=mla_w_ukv.astype(BF16),
                w_a=w_branch_a.astype(BF16), w_b=w_branch_b.astype(BF16), w_c=w_branch_c.astype(BF16),
                w_o=w_out.astype(BF16), w_up=w_up, conv=conv, w_down=w_down)


def kernel(x_prompt, x_sample, state_hgrn, cache_mla_ckv, cache_mla_krope, cache_swa_k, cache_swa_v,
           c, c_ctx, w_mod, b_mod, norm_pre_attn, norm_post_attn, norm_pre_ffn, norm_post_ffn,
           w_in, hgrn_lb, hgrn_gnorm, mla_gq, mla_w_uq, mla_gkv, mla_w_ukv, swa_sink,
           w_branch_a, w_branch_b, w_branch_c, w_out, ffn_w_up, ffn_conv, ffn_w_down):
    wts = _prep_weights(w_in, mla_w_uq, mla_w_ukv, w_branch_a, w_branch_b, w_branch_c, w_out,
                        ffn_w_up, ffn_conv, ffn_w_down)
    cs = jnp.cumsum(jax.nn.softmax(hgrn_lb.astype(F32), axis=0), axis=0)
    lb_all = cs - cs[0]

    cvec = jnp.concatenate([c_ctx[None, :], c, jnp.zeros((MOD_ROWS - 1 - DEC_BATCH, D_MODEL), F32)], axis=0)
    mod = _modulation(cvec, w_mod, b_mod).reshape(DEPTH, MOD_ROWS, 6, 1, D_MODEL)

    hgrn_tables = _hgrn_tables()
    rope_b = _rope_tables(ROPE_B)
    rope_c = _rope_tables(HD_C)
    sink_b = jnp.broadcast_to(swa_sink[:, :, None], (DEPTH, H_C, LANE))

    x = jnp.concatenate([x_prompt.reshape(T_CTX, D_MODEL), x_sample.reshape(T_LAT, D_MODEL)], axis=0)
    new_hgrn, new_ckv, new_krope, new_k, new_v = [], [], [], [], []
    y = None
    for l in range(DEPTH):
        mod_l = mod[l]
        if l == 0:
            (h,) = _norm(x, npre=norm_pre_attn[l], mod_pre=mod_l, scale_idx=1, shift_idx=0)
        else:
            x, h = _norm(x, y=y, mod_post=mod[l - 1], gate_idx=5, npost=norm_post_ffn[l - 1],
                         npre=norm_pre_attn[l], mod_pre=mod_l, scale_idx=1, shift_idx=0)
        z = _mm(h, wts["w_z"][l], F32, "in_proj")

        oa, s_ctx = _hgrn(z, lb_all[l], hgrn_gnorm[l], hgrn_tables, latent=False)
        (oa,) = _hgrn(z, lb_all[l], hgrn_gnorm[l], hgrn_tables, latent=True, s0=state_hgrn[:, l], prev=oa)
        new_hgrn.append(s_ctx)

        (qb,) = _norm_mm(z, Z_B // Q_LORA, Q_LORA, mla_gq[l], wts["w_uq"][l], F32, emit_normed=False,
                         name="mla_q_proj")
        ckv, kvb = _norm_mm(z, Z_KV // KV_LORA, KV_LORA, mla_gkv[l], wts["w_ukv"][l], BF16, emit_normed=True,
                            name="mla_kv_proj")
        kv_cache = _mm(cache_mla_ckv[:, l].reshape(DEC_BATCH * PAST_LEN, KV_LORA), wts["w_ukv"][l], BF16,
                       "mla_kv_cache")
        kr_cache = _pad_cols(cache_mla_krope[:, l].reshape(DEC_BATCH * PAST_LEN, ROPE_B), LANE)
        ob = _mla_attn(qb, kvb, z, latent=False)
        ob = _mla_attn(qb, kvb, z, latent=True, kvc=kv_cache, krc=kr_cache, tables=rope_b, prev=ob)
        new_ckv.append(ckv[:T_CTX].reshape(BATCH, SEQ, KV_LORA))
        new_krope.append(z[:T_CTX, Z_KR:Z_B_END].reshape(BATCH, SEQ, ROPE_B))

        oc = _gqa_attn(z, sink_b[l], latent=False)
        oc = _gqa_attn(z, sink_b[l], latent=True,
                       kc=cache_swa_k[:, l].reshape(DEC_BATCH * PAST_LEN, KVH_C * HD_C),
                       vc=cache_swa_v[:, l].reshape(DEC_BATCH * PAST_LEN, KVH_C * HD_C), tables=rope_c, prev=oc)
        new_k.append(z[:T_CTX, Z_CK:Z_CV].reshape(BATCH, SEQ, KVH_C, HD_C))
        new_v.append(z[:T_CTX, Z_CV:Z_G].reshape(BATCH, SEQ, KVH_C, HD_C))

        y = _merge(oa, ob, oc, wts["w_a"][l], wts["w_b"][l], wts["w_c"][l], z, wts["w_o"][l])

        x, h = _norm(x, y=y, mod_post=mod_l, gate_idx=2, npost=norm_post_attn[l],
                     npre=norm_pre_ffn[l], mod_pre=mod_l, scale_idx=4, shift_idx=3)
        y = _ffn(h, wts["w_up"][l], wts["conv"][l], wts["w_down"][l])

    (x,) = _norm(x, y=y, mod_post=mod[DEPTH - 1], gate_idx=5, npost=norm_post_ffn[DEPTH - 1])
    return (x[:T_CTX].reshape(BATCH, SEQ, D_MODEL), x[T_CTX:].reshape(DEC_BATCH, DEC_SEQ, D_MODEL),
            jnp.stack(new_hgrn, axis=1), jnp.stack(new_ckv, axis=1), jnp.stack(new_krope, axis=1),
            jnp.stack(new_k, axis=1), jnp.stack(new_v, axis=1))
```

```python
import functools

import jax
import jax.numpy as jnp
import numpy as np
from jax import lax
from jax.experimental import pallas as pl
from jax.experimental.pallas import tpu as pltpu

F32 = jnp.float32
BF16 = jnp.bfloat16

D_MODEL = 2048
BATCH = 16
SEQ = 256
DEPTH = 4
DEC_BATCH = 4
DEC_SEQ = 1024
PAST_LEN = 256
GRID_W = 64
ROPE_BASE = 10000.0
EPS = 1e-6
NEG_INF = -1e30
H_A, DK_A, DV_A = 8, 128, 128
H_B, Q_LORA, KV_LORA, NOPE_B, ROPE_B, V_B = 8, 512, 256, 128, 64, 128
H_C, KVH_C, HD_C, WINDOW = 8, 2, 128, 128
N_BRANCH = 3
D_FF = 5504
CONV_W = 3

T_CTX = BATCH * SEQ
T_LAT = DEC_BATCH * DEC_SEQ
T_ALL = T_CTX + T_LAT
MOD_ROWS = 8
LANE = 128
D_FF_PAD = 5632
HGRN_CHUNK = 64
Z_A = 0
Z_B = 5 * H_A * DK_A
Z_KV = Z_B + Q_LORA
Z_KR = Z_KV + KV_LORA
Z_B_END = Z_KR + ROPE_B
ZAB_W = 6144
ZC_K = H_C * HD_C
ZC_V = ZC_K + KVH_C * HD_C
ZC_G = ZC_V + KVH_C * HD_C
ZCG_W = ZC_G + N_BRANCH * D_MODEL
VMEM_LIMIT = 56 * 1024 * 1024


def _params(*sem):
    return pltpu.CompilerParams(dimension_semantics=sem, vmem_limit_bytes=VMEM_LIMIT)


def _mod_row(i, tm):
    return jnp.where(i * tm < T_CTX, 0, 1 + (i * tm - T_CTX) // DEC_SEQ)


def _rms(x):
    return x * lax.rsqrt(jnp.mean(x * x, axis=-1, keepdims=True) + EPS)


def _mod_kernel(c_ref, w_ref, b_ref, o_ref):
    cv = c_ref[...]
    s = (cv * jax.nn.sigmoid(cv)).astype(BF16)
    o_ref[0] = jnp.dot(s, w_ref[0].astype(BF16), preferred_element_type=F32) + b_ref[0]


def _modulation(cvec, w_mod, b_mod):
    tn = 1024
    n = 6 * D_MODEL
    return pl.pallas_call(
        _mod_kernel,
        grid=(DEPTH, n // tn),
        in_specs=[pl.BlockSpec((MOD_ROWS, D_MODEL), lambda l, j: (0, 0)),
                  pl.BlockSpec((1, D_MODEL, tn), lambda l, j: (l, 0, j)),
                  pl.BlockSpec((1, 1, tn), lambda l, j: (l, 0, j))],
        out_specs=pl.BlockSpec((1, MOD_ROWS, tn), lambda l, j: (l, 0, j)),
        out_shape=jax.ShapeDtypeStruct((DEPTH, MOD_ROWS, n), F32),
        compiler_params=_params("parallel", "parallel"),
        name="modulation",
    )(cvec, w_mod, b_mod.reshape(DEPTH, 1, n))


def _norm_kernel(*refs, has_y, has_h):
    it = iter(refs)
    x_ref = next(it)
    if has_y:
        y_ref, gate_ref, npost_ref = next(it), next(it), next(it)
    if has_h:
        npre_ref, scale_ref, shift_ref = next(it), next(it), next(it)
    x = x_ref[...]
    if has_y:
        xnew_ref = next(it)
        x = x + gate_ref[0, 0] * (_rms(y_ref[...]) * npost_ref[...])
        xnew_ref[...] = x
    if has_h:
        h_ref = next(it)
        h = (_rms(x) * npre_ref[...]) * (1.0 + scale_ref[0, 0]) + shift_ref[0, 0]
        h_ref[...] = h.astype(BF16)


def _norm(x, *, y=None, mod_post=None, gate_idx=None, npost=None,
          npre=None, mod_pre=None, scale_idx=None, shift_idx=None):
    tm = 512
    has_y, has_h = y is not None, npre is not None
    row = pl.BlockSpec((tm, D_MODEL), lambda i: (i, 0))
    vec = pl.BlockSpec((1, D_MODEL), lambda i: (0, 0))

    def modspec(k):
        return pl.BlockSpec((1, 1, 1, D_MODEL), lambda i: (_mod_row(i, tm), k, 0, 0))

    args, specs, out_shape, out_specs = [x], [row], [], []
    if has_y:
        args += [y, mod_post, npost.reshape(1, D_MODEL)]
        specs += [row, modspec(gate_idx), vec]
        out_shape.append(jax.ShapeDtypeStruct((T_ALL, D_MODEL), F32))
        out_specs.append(row)
    if has_h:
        args += [npre.reshape(1, D_MODEL), mod_pre, mod_pre]
        specs += [vec, modspec(scale_idx), modspec(shift_idx)]
        out_shape.append(jax.ShapeDtypeStruct((T_ALL, D_MODEL), BF16))
        out_specs.append(row)
    outs = pl.pallas_call(
        functools.partial(_norm_kernel, has_y=has_y, has_h=has_h),
        grid=(T_ALL // tm,),
        in_specs=specs, out_specs=out_specs, out_shape=out_shape,
        compiler_params=_params("parallel"),
        name="norm_y%d_h%d" % (has_y, has_h),
    )(*args)
    return outs


def _mm_kernel(x_ref, w_ref, o_ref):
    o_ref[...] = jnp.dot(x_ref[...].astype(BF16), w_ref[...].astype(BF16),
                         preferred_element_type=F32).astype(o_ref.dtype)


def _mm(x, w, l, out_dtype, name, n=None):
    m, k = x.shape
    n = w.shape[2] if n is None else n
    tm = min(m, 1024)
    tn = min(n, 512)
    return pl.pallas_call(
        _mm_kernel,
        grid=(m // tm, n // tn),
        in_specs=[pl.BlockSpec((tm, k), lambda i, j: (i, 0)),
                  pl.BlockSpec((None, k, tn), lambda i, j: (l, 0, j))],
        out_specs=pl.BlockSpec((tm, tn), lambda i, j: (i, j)),
        out_shape=jax.ShapeDtypeStruct((m, n), out_dtype),
        compiler_params=_params("parallel", "parallel"),
        name=name,
    )(x, w)


def _norm_mm_kernel(x_ref, g_ref, w_ref, *out_refs, emit_normed):
    xn = _rms(x_ref[...]) * g_ref[...]
    if emit_normed:
        out_refs[0][...] = xn
    out_refs[-1][...] = jnp.dot(xn.astype(BF16), w_ref[...],
                                preferred_element_type=F32).astype(out_refs[-1].dtype)


def _norm_mm(z, col_block, k, gain, w, l, out_dtype, *, emit_normed, name):
    m = z.shape[0]
    n = w.shape[2]
    tm = 512
    out_shape = [jax.ShapeDtypeStruct((m, n), out_dtype)]
    out_specs = [pl.BlockSpec((tm, n), lambda i: (i, 0))]
    if emit_normed:
        out_shape.insert(0, jax.ShapeDtypeStruct((m, k), F32))
        out_specs.insert(0, pl.BlockSpec((tm, k), lambda i: (i, 0)))
    return pl.pallas_call(
        functools.partial(_norm_mm_kernel, emit_normed=emit_normed),
        grid=(m // tm,),
        in_specs=[pl.BlockSpec((tm, k), lambda i: (i, col_block)),
                  pl.BlockSpec((1, k), lambda i: (0, 0)),
                  pl.BlockSpec((None, k, n), lambda i: (l, 0, 0))],
        out_specs=out_specs, out_shape=out_shape,
        compiler_params=_params("parallel"),
        name=name,
    )(z, gain.reshape(1, k), w)


def _rope_tables(head_dim):
    n = DEC_SEQ
    half = head_dim // 2
    quarter = half // 2
    row = jnp.repeat(jnp.arange(n // GRID_W), GRID_W).astype(F32)
    col = jnp.tile(jnp.arange(GRID_W), n // GRID_W).astype(F32)
    inv = ROPE_BASE ** (-jnp.arange(0, half, 2, dtype=F32) / half)
    lane = jnp.arange(LANE)
    m = lane % half
    pos = jnp.where((lane // half)[None, :] == 0, row[:, None], col[:, None])
    ang = pos * inv[m % quarter][None, :]
    valid = (lane < head_dim)[None, :]
    cos = jnp.where(valid, jnp.cos(ang), 0.0)
    sin = jnp.where(valid, jnp.where(m < quarter, -1.0, 1.0)[None, :] * jnp.sin(ang), 0.0)
    return cos.astype(F32), sin.astype(F32)


def _rope(x, cos, sin, head_dim):
    quarter = head_dim // 4
    lane = lax.broadcasted_iota(jnp.int32, x.shape, 1)
    first = (lane % (2 * quarter)) < quarter
    partner = jnp.where(first, pltpu.roll(x, LANE - quarter, 1), pltpu.roll(x, quarter, 1))
    return x * cos + partner * sin


def _mla_attn_kernel(*refs, latent):
    if latent:
        q_ref, kv_ref, kr_ref, kvc_ref, krc_ref, cq_ref, sq_ref, ck_ref, sk_ref, _, o_ref = refs
    else:
        q_ref, kv_ref, kr_ref, o_ref = refs
    scale = (NOPE_B + ROPE_B) ** -0.5
    nt = (((1,), (1,)), ((), ()))
    kr = kr_ref[...]
    if latent:
        kr = _rope(kr, ck_ref[...], sk_ref[...], ROPE_B)
        krc = krc_ref[...].astype(BF16)
    kr = kr.astype(BF16)
    for h in range(H_B):
        c0 = h * 2 * LANE
        qn = q_ref[:, c0:c0 + LANE]
        qr = q_ref[:, c0 + LANE:c0 + 2 * LANE]
        if latent:
            qr = _rope(qr, cq_ref[...], sq_ref[...], ROPE_B)
        qh = jnp.concatenate([(qn * scale).astype(BF16), (qr * scale).astype(BF16)], axis=-1)
        kh = jnp.concatenate([kv_ref[:, c0:c0 + LANE], kr], axis=-1)
        vh = kv_ref[:, c0 + LANE:c0 + 2 * LANE]
        s = lax.dot_general(qh, kh, nt, preferred_element_type=F32)
        m = jnp.max(s, axis=-1, keepdims=True)
        if latent:
            khc = jnp.concatenate([kvc_ref[:, c0:c0 + LANE], krc], axis=-1)
            vhc = kvc_ref[:, c0 + LANE:c0 + 2 * LANE]
            sc = lax.dot_general(qh, khc, nt, preferred_element_type=F32)
            m = jnp.maximum(m, jnp.max(sc, axis=-1, keepdims=True))
        p = jnp.exp(s - m)
        l = jnp.sum(p, axis=-1, keepdims=True)
        o = jnp.dot(p.astype(BF16), vh, preferred_element_type=F32)
        if latent:
            pc = jnp.exp(sc - m)
            l = l + jnp.sum(pc, axis=-1, keepdims=True)
            o = o + jnp.dot(pc.astype(BF16), vhc, preferred_element_type=F32)
        o_ref[:, h * LANE:(h + 1) * LANE] = (o / l).astype(BF16)


def _rows_of(specs, args, prev):
    if prev is None:
        return specs, args, {}
    return specs + [pl.BlockSpec(memory_space=pl.ANY)], args + [prev], {len(args): 0}


def _mla_attn(q, kv, z, *, latent, kvc=None, krc=None, tables=None, prev=None):
    nb, n = (DEC_BATCH, DEC_SEQ) if latent else (BATCH, SEQ)
    tq = 256
    nq = n // tq
    off = T_CTX // n if latent else 0
    offq = T_CTX // tq if latent else 0
    w = H_B * 2 * LANE
    specs = [pl.BlockSpec((tq, w), lambda b, i: (offq + b * nq + i, 0)),
             pl.BlockSpec((n, w), lambda b, i: (off + b, 0)),
             pl.BlockSpec((n, LANE), lambda b, i: (off + b, Z_KR // LANE))]
    args = [q, kv, z]
    if latent:
        cos, sin = tables
        specs += [pl.BlockSpec((PAST_LEN, w), lambda b, i: (b, 0)),
                  pl.BlockSpec((PAST_LEN, LANE), lambda b, i: (b, 0)),
                  pl.BlockSpec((tq, LANE), lambda b, i: (i, 0)),
                  pl.BlockSpec((tq, LANE), lambda b, i: (i, 0)),
                  pl.BlockSpec((n, LANE), lambda b, i: (0, 0)),
                  pl.BlockSpec((n, LANE), lambda b, i: (0, 0))]
        args += [kvc, krc, cos, sin, cos, sin]
    specs, args, aliases = _rows_of(specs, args, prev)
    return pl.pallas_call(
        functools.partial(_mla_attn_kernel, latent=latent),
        grid=(nb, nq),
        in_specs=specs,
        out_specs=pl.BlockSpec((tq, H_B * V_B), lambda b, i: (offq + b * nq + i, 0)),
        out_shape=jax.ShapeDtypeStruct((T_ALL, H_B * V_B), BF16),
        input_output_aliases=aliases,
        compiler_params=_params("parallel", "parallel"),
        name="mla_attn_lat" if latent else "mla_attn_ctx",
    )(*args)


def _gqa_attn_kernel(*refs, latent, tq):
    if latent:
        q_ref, k_ref, v_ref, kc_ref, vc_ref, sink_ref, cq_ref, sq_ref, ck_ref, sk_ref, _, o_ref = refs
    else:
        q_ref, k_ref, v_ref, sink_ref, o_ref = refs
    scale = HD_C ** -0.5
    nt = (((1,), (1,)), ((), ()))
    rep = H_C // KVH_C
    n = k_ref.shape[0]
    if latent:
        kw = tq + 2 * WINDOW
        q0 = pl.program_id(1) * tq
        k0 = pl.multiple_of(jnp.clip(q0 - WINDOW, 0, n - kw), WINDOW)
        keys = pl.ds(k0, kw)
        qpos = q0 + lax.broadcasted_iota(jnp.int32, (tq, kw), 0)
        kpos = k0 + lax.broadcasted_iota(jnp.int32, (tq, kw), 1)
        band = jnp.abs(qpos - kpos) <= WINDOW
    else:
        keys = slice(None)
    for g in range(KVH_C):
        kg = k_ref[keys, g * LANE:(g + 1) * LANE]
        if latent:
            kg = _rope(kg, ck_ref[keys, :], sk_ref[keys, :], HD_C)
            kcg = kc_ref[:, g * LANE:(g + 1) * LANE].astype(BF16)
            vcg = vc_ref[:, g * LANE:(g + 1) * LANE].astype(BF16)
        kg = kg.astype(BF16)
        vg = v_ref[keys, g * LANE:(g + 1) * LANE].astype(BF16)
        for r in range(rep):
            h = g * rep + r
            qh = q_ref[:, h * LANE:(h + 1) * LANE]
            if latent:
                qh = _rope(qh, cq_ref[...], sq_ref[...], HD_C)
            qh = (qh * scale).astype(BF16)
            sk = sink_ref[h:h + 1, 0:1]
            s = lax.dot_general(qh, kg, nt, preferred_element_type=F32)
            if latent:
                s = jnp.where(band, s, NEG_INF)
            m = jnp.maximum(jnp.max(s, axis=-1, keepdims=True), sk)
            if latent:
                sc = lax.dot_general(qh, kcg, nt, preferred_element_type=F32)
                m = jnp.maximum(m, jnp.max(sc, axis=-1, keepdims=True))
            p = jnp.exp(s - m)
            l = jnp.sum(p, axis=-1, keepdims=True) + jnp.exp(sk - m)
            o = jnp.dot(p.astype(BF16), vg, preferred_element_type=F32)
            if latent:
                pc = jnp.exp(sc - m)
                l = l + jnp.sum(pc, axis=-1, keepdims=True)
                o = o + jnp.dot(pc.astype(BF16), vcg, preferred_element_type=F32)
            o_ref[:, h * LANE:(h + 1) * LANE] = (o / l).astype(BF16)


def _gqa_attn(z, sink_b, *, latent, kc=None, vc=None, tables=None, prev=None):
    nb, n = (DEC_BATCH, DEC_SEQ) if latent else (BATCH, SEQ)
    tq = 256
    nq = n // tq
    off = T_CTX // n if latent else 0
    offq = T_CTX // tq if latent else 0
    wq, wk = H_C * HD_C, KVH_C * HD_C
    specs = [pl.BlockSpec((tq, wq), lambda b, i: (offq + b * nq + i, 0)),
             pl.BlockSpec((n, wk), lambda b, i: (off + b, ZC_K // wk)),
             pl.BlockSpec((n, wk), lambda b, i: (off + b, ZC_V // wk))]
    args = [z, z, z]
    if latent:
        specs += [pl.BlockSpec((PAST_LEN, wk), lambda b, i: (b, 0)),
                  pl.BlockSpec((PAST_LEN, wk), lambda b, i: (b, 0))]
        args += [kc, vc]
    specs.append(pl.BlockSpec((H_C, LANE), lambda b, i: (0, 0)))
    args.append(sink_b)
    if latent:
        cos, sin = tables
        specs += [pl.BlockSpec((tq, LANE), lambda b, i: (i, 0)),
                  pl.BlockSpec((tq, LANE), lambda b, i: (i, 0)),
                  pl.BlockSpec((n, LANE), lambda b, i: (0, 0)),
                  pl.BlockSpec((n, LANE), lambda b, i: (0, 0))]
        args += [cos, sin, cos, sin]
    specs, args, aliases = _rows_of(specs, args, prev)
    return pl.pallas_call(
        functools.partial(_gqa_attn_kernel, latent=latent, tq=tq),
        grid=(nb, nq),
        in_specs=specs,
        out_specs=pl.BlockSpec((tq, wq), lambda b, i: (offq + b * nq + i, 0)),
        out_shape=jax.ShapeDtypeStruct((T_ALL, wq), BF16),
        input_output_aliases=aliases,
        compiler_params=_params("parallel", "parallel"),
        name="gqa_attn_lat" if latent else "gqa_attn_ctx",
    )(*args)


def _hgrn_tables():
    c = HGRN_CHUNK
    halves = [c >> (i + 1) for i in range(c.bit_length() - 1)]
    out = []
    for forward in (True, False):
        sums = np.zeros((len(halves) + 1, c, c), np.float32)
        level = np.full((c, c), -1, np.int32)
        level[np.arange(c), np.arange(c)] = 0
        for li, m in enumerate(halves):
            for r in range(c):
                pos = r % (2 * m)
                mid = r - pos + m
                late = pos >= m
                if forward:
                    lo, hi = (mid, r + 1) if late else (r + 1, mid)
                else:
                    lo, hi = (mid, r) if late else (r, mid)
                sums[li, r, lo:hi] = 1.0
                for s in range(r - pos, r - pos + 2 * m):
                    s_late = (s % (2 * m)) >= m
                    if (late and not s_late) if forward else (not late and s_late):
                        level[r, s] = li + 1
        for r in range(c):
            if forward:
                sums[-1, r, :r + 1] = 1.0
            else:
                sums[-1, r, r:] = 1.0
        sums = sums.reshape(-1, c)
        out.append((jnp.asarray(np.concatenate([sums, sums, sums], axis=1), BF16),
                    jnp.asarray(np.concatenate([level, level], axis=1))))
    return out


def _hgrn_kernel(*refs, n, has_s0, emit_state):
    it = iter(refs)
    q_ref, xf_ref, xb_ref, v_ref, ag_ref, lb_ref, gn_ref = (next(it) for _ in range(7))
    sums_refs = (next(it), next(it))
    level_refs = (next(it), next(it))
    s0_ref = next(it) if has_s0 else None
    if has_s0:
        next(it)
    o_ref = next(it)
    sfin_ref = next(it) if emit_state else None
    o_scr, qe_scr, u_scr, e_scr, st_scr = (next(it) for _ in range(5))

    c = HGRN_CHUNK
    nc = n // c
    nlev = c.bit_length() - 1
    nt = (((1,), (1,)), ((), ()))
    tn = (((0,), (0,)), ((), ()))
    zero = jnp.zeros((c, LANE), BF16)

    def blockdiag(x):
        return jnp.concatenate([jnp.concatenate([x[:, :LANE], zero], axis=1),
                                jnp.concatenate([zero, x[:, LANE:]], axis=1)], axis=0)

    def gates(x, lb):
        e = jnp.exp(-jnp.abs(x))
        big = 1.0 / (1.0 + e)
        small = e * big
        pos = x >= 0.0
        return jnp.log(lb + (1.0 - lb) * jnp.where(pos, big, small)), (1.0 - lb) * jnp.where(pos, small, big)

    for d in range(2):
        for hh in range(2):
            st_scr[d, hh] = s0_ref[0, d, hh].T if has_s0 else jnp.zeros((DV_A, DK_A), F32)

    group = 4

    def intra(t, carry):
        jobs = [(u, d) for u in range(group) for d in range(2)]
        chunk_of = [t * group + u for u in range(group)]
        rows = {u: pl.ds(pl.multiple_of(chunk_of[u] * c, c), c) for u in range(group)}
        q = {ci: q_ref[rows[ci], :] for ci, _ in jobs}
        v = {ci: v_ref[rows[ci], :].astype(BF16) for ci, _ in jobs}
        k = {}

        dall = {}
        for ci, d in jobs:
            g, k[ci, d] = gates((xf_ref, xb_ref)[d][rows[ci], :], lb_ref[d:d + 1, :])
            g_hi = g.astype(BF16)
            rem = g - g_hi.astype(F32)
            g_mid = rem.astype(BF16)
            g_lo = (rem - g_mid.astype(F32)).astype(BF16)
            dall[ci, d] = jnp.dot(sums_refs[d][...], jnp.concatenate([g_hi, g_mid, g_lo], axis=0),
                                  preferred_element_type=F32)

        scores = {}
        for ci, d in jobs:
            rs = [lax.dot_general(q[ci].astype(BF16), blockdiag(k[ci, d].astype(BF16)), nt,
                                  preferred_element_type=F32)]
            for li in range(nlev):
                e = jnp.exp(dall[ci, d][li * c:(li + 1) * c, :])
                rs.append(lax.dot_general((q[ci] * e).astype(BF16), blockdiag((k[ci, d] * e).astype(BF16)), nt,
                                          preferred_element_type=F32))
            scores[ci, d] = rs

        for ci, d in jobs:
            level = level_refs[d][...]
            a = jnp.where(level == 0, scores[ci, d][0], 0.0)
            for li in range(nlev):
                a = jnp.where(level == li + 1, scores[ci, d][li + 1], a)
            o_scr[d, rows[ci], :] = jnp.dot(a.astype(BF16), blockdiag(v[ci]), preferred_element_type=F32)
            gc = dall[ci, d][nlev * c:, :]
            g_end = gc[c - 1:c, :] if d == 0 else gc[0:1, :]
            qe_scr[d, rows[ci], :] = (q[ci] * jnp.exp(gc)).astype(BF16)
            kd = (k[ci, d] * jnp.exp(g_end - gc)).astype(BF16)
            e_scr[d, chunk_of[ci]] = jnp.broadcast_to(jnp.exp(g_end), (8, 2 * LANE))
            for hh in range(2):
                hl = slice(hh * LANE, (hh + 1) * LANE)
                u_scr[d, chunk_of[ci], hh] = lax.dot_general(v[ci][:, hl], kd[:, hl], tn,
                                                             preferred_element_type=F32)
        return carry

    lax.fori_loop(0, nc // group, intra, 0)

    def inter(i, carry):
        for d in range(2):
            ci = i if d == 0 else nc - 1 - i
            rows = pl.ds(pl.multiple_of(ci * c, c), c)
            e = e_scr[d, ci]
            for hh in range(2):
                hl = slice(hh * LANE, (hh + 1) * LANE)
                st = st_scr[d, hh]
                o_scr[d, rows, hl] += lax.dot_general(qe_scr[d, rows, hl], st.astype(BF16), nt,
                                                      preferred_element_type=F32)
                st_scr[d, hh] = st * e[0:1, hl] + u_scr[d, ci, hh]
        return carry

    lax.fori_loop(0, nc, inter, 0, unroll=4)

    def finish(i, carry):
        rows = pl.ds(pl.multiple_of(i * c, c), c)
        o = o_scr[0, rows, :] + o_scr[1, rows, :]
        o = jnp.concatenate([_rms(o[:, :LANE]), _rms(o[:, LANE:])], axis=1)
        ag = ag_ref[rows, :]
        o_ref[rows, :] = (o * gn_ref[...] * (ag * jax.nn.sigmoid(ag))).astype(BF16)
        return carry

    lax.fori_loop(0, nc, finish, 0, unroll=4)
    if emit_state:
        for d in range(2):
            for hh in range(2):
                sfin_ref[0, d, hh] = st_scr[d, hh].T


def _hgrn(z, lb_l, gnorm, tables, *, latent, s0=None, prev=None):
    nb, n = (DEC_BATCH, DEC_SEQ) if latent else (BATCH, SEQ)
    off = T_CTX // n if latent else 0
    emit_state = not latent
    w = 2 * LANE
    pairs = H_A // 2
    c = HGRN_CHUNK
    nc = n // c

    def zspec(k):
        return pl.BlockSpec((n, w), lambda b, p: (off + b, Z_A // w + k * pairs + p))

    def const(x):
        return pl.BlockSpec(x.shape, lambda b, p: (0, 0))

    (sums_f, level_f), (sums_b, level_b) = tables
    specs = [zspec(0), zspec(1), zspec(2), zspec(3), zspec(4),
             pl.BlockSpec((2, w), lambda b, p: (0, p)),
             pl.BlockSpec((1, w), lambda b, p: (0, 0)),
             const(sums_f), const(sums_b), const(level_f), const(level_b)]
    args = [z, z, z, z, z, lb_l, jnp.tile(gnorm.reshape(1, DV_A), (1, 2)),
            sums_f, sums_b, level_f, level_b]
    if latent:
        specs.append(pl.BlockSpec((1, 2, 2, DK_A, DV_A), lambda b, p: (b, 0, p, 0, 0)))
        args.append(s0)
    specs, args, aliases = _rows_of(specs, args, prev)
    out_shape = [jax.ShapeDtypeStruct((T_ALL, H_A * DV_A), BF16)]
    out_specs = [pl.BlockSpec((n, w), lambda b, p: (off + b, p))]
    if emit_state:
        out_shape.append(jax.ShapeDtypeStruct((nb, 2, H_A, DK_A, DV_A), F32))
        out_specs.append(pl.BlockSpec((1, 2, 2, DK_A, DV_A), lambda b, p: (b, 0, p, 0, 0)))
    scratch = [pltpu.VMEM((2, n, w), F32),
               pltpu.VMEM((2, n, w), BF16),
               pltpu.VMEM((2, nc, 2, DV_A, DK_A), F32),
               pltpu.VMEM((2, nc, 8, w), F32),
               pltpu.VMEM((2, 2, DV_A, DK_A), F32)]
    return pl.pallas_call(
        functools.partial(_hgrn_kernel, n=n, has_s0=latent, emit_state=emit_state),
        grid=(nb, pairs),
        in_specs=specs, out_specs=out_specs, out_shape=out_shape,
        scratch_shapes=scratch,
        input_output_aliases=aliases,
        compiler_params=_params("parallel", "parallel"),
        name="hgrn_lat" if latent else "hgrn_ctx",
    )(*args)


def _merge_kernel(oa_ref, ob_ref, oc_ref, wa_ref, wb_ref, wc_ref, g0_ref, g1_ref, g2_ref, wo_ref, y_ref):
    @pl.when(pl.program_id(1) == 0)
    def _():
        y_ref[...] = jnp.zeros_like(y_ref)

    merged = (jax.nn.sigmoid(g0_ref[...]) * jnp.dot(oa_ref[...], wa_ref[...], preferred_element_type=F32)
              + jax.nn.sigmoid(g1_ref[...]) * jnp.dot(ob_ref[...], wb_ref[...], preferred_element_type=F32)
              + jax.nn.sigmoid(g2_ref[...]) * jnp.dot(oc_ref[...], wc_ref[...], preferred_element_type=F32))
    y_ref[...] += jnp.dot(merged.astype(BF16), wo_ref[...], preferred_element_type=F32)


def _merge(oa, ob, oc, wa, wb, wc, z, wo, l):
    tm, tn = 512, 512
    nj = D_MODEL // tn
    kb = H_A * DV_A
    o_spec = pl.BlockSpec((tm, kb), lambda i, j: (i, 0))
    w_spec = pl.BlockSpec((None, kb, tn), lambda i, j: (l, 0, j))

    def gspec(k):
        return pl.BlockSpec((tm, tn), lambda i, j: (i, ZC_G // tn + k * nj + j))

    return pl.pallas_call(
        _merge_kernel,
        grid=(T_ALL // tm, nj),
        in_specs=[o_spec, o_spec, o_spec, w_spec, w_spec, w_spec, gspec(0), gspec(1), gspec(2),
                  pl.BlockSpec((None, tn, D_MODEL), lambda i, j: (l, j, 0))],
        out_specs=pl.BlockSpec((tm, D_MODEL), lambda i, j: (i, 0)),
        out_shape=jax.ShapeDtypeStruct((T_ALL, D_MODEL), F32),
        compiler_params=_params("parallel", "arbitrary"),
        name="merge_out",
    )(oa, ob, oc, wa, wb, wc, z, z, z, wo)


def _ffn_kernel(h_ref, wa_ref, wg_ref, ca_ref, cg_ref, wd_ref, y_ref, *, tm):
    i = pl.program_id(0)

    @pl.when(pl.program_id(1) == 0)
    def _():
        y_ref[...] = jnp.zeros_like(y_ref)

    h = h_ref[...]
    seq_len = jnp.where(i * tm < T_CTX, SEQ, DEC_SEQ)
    pos = lax.broadcasted_iota(jnp.int32, (tm, 1), 0) & (seq_len - 1)
    has_prev = pos != 0
    has_next = pos != seq_len - 1

    def conv(u, c_ref):
        prev = jnp.where(has_prev, pltpu.roll(u, 1, 0), 0.0)
        nxt = jnp.where(has_next, pltpu.roll(u, tm - 1, 0), 0.0)
        return c_ref[0:1, :] * prev + c_ref[1:2, :] * u + c_ref[2:3, :] * nxt

    a = conv(jnp.dot(h, wa_ref[...], preferred_element_type=F32), ca_ref)
    g = conv(jnp.dot(h, wg_ref[...], preferred_element_type=F32), cg_ref)
    act = (a * jax.nn.gelu(g)).astype(BF16)
    y_ref[...] += jnp.dot(act, wd_ref[...], preferred_element_type=F32)


def _ffn(h, w_up, conv, w_down, l):
    tm, tf = 1024, 512
    nj = D_FF_PAD // tf
    return pl.pallas_call(
        functools.partial(_ffn_kernel, tm=tm),
        grid=(T_ALL // tm, nj),
        in_specs=[pl.BlockSpec((tm, D_MODEL), lambda i, j: (i, 0)),
                  pl.BlockSpec((None, D_MODEL, tf), lambda i, j: (l, 0, j)),
                  pl.BlockSpec((None, D_MODEL, tf), lambda i, j: (l, 0, nj + j)),
                  pl.BlockSpec((None, CONV_W, tf), lambda i, j: (l, 0, j)),
                  pl.BlockSpec((None, CONV_W, tf), lambda i, j: (l, 0, nj + j)),
                  pl.BlockSpec((None, tf, D_MODEL), lambda i, j: (l, j, 0))],
        out_specs=pl.BlockSpec((tm, D_MODEL), lambda i, j: (i, 0)),
        out_shape=jax.ShapeDtypeStruct((T_ALL, D_MODEL), F32),
        compiler_params=_params("parallel", "arbitrary"),
        name="conv_ffn",
    )(h, w_up, w_up, conv, conv, w_down)


def _pad_cols(w, n):
    return jnp.pad(w, [(0, 0)] * (w.ndim - 1) + [(0, n - w.shape[-1])])


def _prep_weights(w_in, mla_w_uq, mla_w_ukv, w_branch_a, w_branch_b, w_branch_c, w_out,
                  ffn_w_up, ffn_conv, ffn_w_down):
    w_zcg = w_in[:, :, Z_B_END:].astype(BF16)
    w_uq = _pad_cols(mla_w_uq.reshape(DEPTH, Q_LORA, H_B, NOPE_B + ROPE_B), 2 * LANE)
    w_uq = w_uq.reshape(DEPTH, Q_LORA, H_B * 2 * LANE).astype(BF16)
    w_up = jnp.concatenate([_pad_cols(ffn_w_up[:, :, :D_FF], D_FF_PAD),
                            _pad_cols(ffn_w_up[:, :, D_FF:], D_FF_PAD)], axis=-1).astype(BF16)
    conv = jnp.concatenate([_pad_cols(ffn_conv[:, :, :D_FF], D_FF_PAD),
                            _pad_cols(ffn_conv[:, :, D_FF:], D_FF_PAD)], axis=-1)
    w_down = jnp.pad(ffn_w_down, ((0, 0), (0, D_FF_PAD - D_FF), (0, 0))).astype(BF16)
    return dict(w_zcg=w_zcg, w_uq=w_uq, w_ukv=mla_w_ukv.astype(BF16),
                w_a=w_branch_a.astype(BF16), w_b=w_branch_b.astype(BF16), w_c=w_branch_c.astype(BF16),
                w_o=w_out.astype(BF16), w_up=w_up, conv=conv, w_down=w_down)


def kernel(x_prompt, x_sample, state_hgrn, cache_mla_ckv, cache_mla_krope, cache_swa_k, cache_swa_v,
           c, c_ctx, w_mod, b_mod, norm_pre_attn, norm_post_attn, norm_pre_ffn, norm_post_ffn,
           w_in, hgrn_lb, hgrn_gnorm, mla_gq, mla_w_uq, mla_gkv, mla_w_ukv, swa_sink,
           w_branch_a, w_branch_b, w_branch_c, w_out, ffn_w_up, ffn_conv, ffn_w_down):
    wts = _prep_weights(w_in, mla_w_uq, mla_w_ukv, w_branch_a, w_branch_b, w_branch_c, w_out,
                        ffn_w_up, ffn_conv, ffn_w_down)
    cs = jnp.cumsum(jax.nn.softmax(hgrn_lb.astype(F32), axis=0), axis=0)
    lb_all = cs - cs[0]

    cvec = jnp.concatenate([c_ctx[None, :], c, jnp.zeros((MOD_ROWS - 1 - DEC_BATCH, D_MODEL), F32)], axis=0)
    mod = _modulation(cvec, w_mod, b_mod).reshape(DEPTH, MOD_ROWS, 6, 1, D_MODEL)

    hgrn_tables = _hgrn_tables()
    rope_b = _rope_tables(ROPE_B)
    rope_c = _rope_tables(HD_C)
    sink_b = jnp.broadcast_to(swa_sink[:, :, None], (DEPTH, H_C, LANE))

    x = jnp.concatenate([x_prompt.reshape(T_CTX, D_MODEL), x_sample.reshape(T_LAT, D_MODEL)], axis=0)
    new_hgrn, new_ckv, new_krope, new_k, new_v = [], [], [], [], []
    y = None
    for l in range(DEPTH):
        mod_l = mod[l]
        if l == 0:
            (h,) = _norm(x, npre=norm_pre_attn[l], mod_pre=mod_l, scale_idx=1, shift_idx=0)
        else:
            x, h = _norm(x, y=y, mod_post=mod[l - 1], gate_idx=5, npost=norm_post_ffn[l - 1],
                         npre=norm_pre_attn[l], mod_pre=mod_l, scale_idx=1, shift_idx=0)
        zab = _mm(h, w_in, l, F32, "in_proj_ab", n=ZAB_W)
        zcg = _mm(h, wts["w_zcg"], l, F32, "in_proj_cg")

        oa, s_ctx = _hgrn(zab, lb_all[l], hgrn_gnorm[l], hgrn_tables, latent=False)
        (oa,) = _hgrn(zab, lb_all[l], hgrn_gnorm[l], hgrn_tables, latent=True, s0=state_hgrn[:, l], prev=oa)
        new_hgrn.append(s_ctx)

        (qb,) = _norm_mm(zab, Z_B // Q_LORA, Q_LORA, mla_gq[l], wts["w_uq"], l, F32, emit_normed=False,
                         name="mla_q_proj")
        ckv, kvb = _norm_mm(zab, Z_KV // KV_LORA, KV_LORA, mla_gkv[l], wts["w_ukv"], l, BF16, emit_normed=True,
                            name="mla_kv_proj")
        kv_cache = _mm(cache_mla_ckv[:, l].reshape(DEC_BATCH * PAST_LEN, KV_LORA), wts["w_ukv"], l, BF16,
                       "mla_kv_cache")
        kr_cache = _pad_cols(cache_mla_krope[:, l].reshape(DEC_BATCH * PAST_LEN, ROPE_B), LANE)
        ob = _mla_attn(qb, kvb, zab, latent=False)
        ob = _mla_attn(qb, kvb, zab, latent=True, kvc=kv_cache, krc=kr_cache, tables=rope_b, prev=ob)
        new_ckv.append(ckv[:T_CTX].reshape(BATCH, SEQ, KV_LORA))
        new_krope.append(zab[:T_CTX, Z_KR:Z_B_END].reshape(BATCH, SEQ, ROPE_B))

        oc = _gqa_attn(zcg, sink_b[l], latent=False)
        oc = _gqa_attn(zcg, sink_b[l], latent=True,
                       kc=cache_swa_k[:, l].reshape(DEC_BATCH * PAST_LEN, KVH_C * HD_C),
                       vc=cache_swa_v[:, l].reshape(DEC_BATCH * PAST_LEN, KVH_C * HD_C), tables=rope_c, prev=oc)
        new_k.append(zcg[:T_CTX, ZC_K:ZC_V].reshape(BATCH, SEQ, KVH_C, HD_C))
        new_v.append(zcg[:T_CTX, ZC_V:ZC_G].reshape(BATCH, SEQ, KVH_C, HD_C))

        y = _merge(oa, ob, oc, wts["w_a"], wts["w_b"], wts["w_c"], zcg, wts["w_o"], l)

        x, h = _norm(x, y=y, mod_post=mod_l, gate_idx=2, npost=norm_post_attn[l],
                     npre=norm_pre_ffn[l], mod_pre=mod_l, scale_idx=4, shift_idx=3)
        y = _ffn(h, wts["w_up"], wts["conv"], wts["w_down"], l)

    (x,) = _norm(x, y=y, mod_post=mod[DEPTH - 1], gate_idx=5, npost=norm_post_ffn[DEPTH - 1])
    return (x[:T_CTX].reshape(BATCH, SEQ, D_MODEL), x[T_CTX:].reshape(DEC_BATCH, DEC_SEQ, D_MODEL),
            jnp.stack(new_hgrn, axis=1), jnp.stack(new_ckv, axis=1), jnp.stack(new_krope, axis=1),
            jnp.stack(new_k, axis=1), jnp.stack(new_v, axis=1))
```

```python
import functools

import jax
import jax.numpy as jnp
import numpy as np
from jax import lax
from jax.experimental import pallas as pl
from jax.experimental.pallas import tpu as pltpu

F32 = jnp.float32
BF16 = jnp.bfloat16

D_MODEL = 2048
BATCH = 16
SEQ = 256
DEPTH = 4
DEC_BATCH = 4
DEC_SEQ = 1024
PAST_LEN = 256
GRID_W = 64
ROPE_BASE = 10000.0
EPS = 1e-6
NEG_INF = -1e30
H_A, DK_A, DV_A = 8, 128, 128
H_B, Q_LORA, KV_LORA, NOPE_B, ROPE_B, V_B = 8, 512, 256, 128, 64, 128
H_C, KVH_C, HD_C, WINDOW = 8, 2, 128, 128
N_BRANCH = 3
D_FF = 5504
CONV_W = 3

T_CTX = BATCH * SEQ
T_LAT = DEC_BATCH * DEC_SEQ
T_ALL = T_CTX + T_LAT
MOD_ROWS = 8
LANE = 128
D_FF_PAD = 5632
HGRN_CHUNK = 64
Z_A = 0
Z_B = 5 * H_A * DK_A
Z_KV = Z_B + Q_LORA
Z_KR = Z_KV + KV_LORA
Z_B_END = Z_KR + ROPE_B
ZAB_W = 6144
ZC_K = H_C * HD_C
ZC_V = ZC_K + KVH_C * HD_C
ZC_G = ZC_V + KVH_C * HD_C
ZCG_W = ZC_G + N_BRANCH * D_MODEL
VMEM_LIMIT = 56 * 1024 * 1024


def _params(*sem):
    return pltpu.CompilerParams(dimension_semantics=sem, vmem_limit_bytes=VMEM_LIMIT)


def _mod_row(i, tm):
    return jnp.where(i * tm < T_CTX, 0, 1 + (i * tm - T_CTX) // DEC_SEQ)


def _rms(x):
    return x * lax.rsqrt(jnp.mean(x * x, axis=-1, keepdims=True) + EPS)


def _mod_kernel(c_ref, w_ref, b_ref, o_ref):
    cv = c_ref[...]
    s = (cv * jax.nn.sigmoid(cv)).astype(BF16)
    o_ref[0] = jnp.dot(s, w_ref[0].astype(BF16), preferred_element_type=F32) + b_ref[0]


def _modulation(cvec, w_mod, b_mod):
    tn = 1024
    n = 6 * D_MODEL
    return pl.pallas_call(
        _mod_kernel,
        grid=(DEPTH, n // tn),
        in_specs=[pl.BlockSpec((MOD_ROWS, D_MODEL), lambda l, j: (0, 0)),
                  pl.BlockSpec((1, D_MODEL, tn), lambda l, j: (l, 0, j)),
                  pl.BlockSpec((1, 1, tn), lambda l, j: (l, 0, j))],
        out_specs=pl.BlockSpec((1, MOD_ROWS, tn), lambda l, j: (l, 0, j)),
        out_shape=jax.ShapeDtypeStruct((DEPTH, MOD_ROWS, n), F32),
        compiler_params=_params("parallel", "parallel"),
        name="modulation",
    )(cvec, w_mod, b_mod.reshape(DEPTH, 1, n))


def _norm_kernel(*refs, has_y, has_h):
    it = iter(refs)
    x_ref = next(it)
    if has_y:
        y_ref, gate_ref, npost_ref = next(it), next(it), next(it)
    if has_h:
        npre_ref, scale_ref, shift_ref = next(it), next(it), next(it)
    x = x_ref[...]
    if has_y:
        xnew_ref = next(it)
        x = x + gate_ref[0, 0] * (_rms(y_ref[...]) * npost_ref[...])
        xnew_ref[...] = x
    if has_h:
        h_ref = next(it)
        h = (_rms(x) * npre_ref[...]) * (1.0 + scale_ref[0, 0]) + shift_ref[0, 0]
        h_ref[...] = h.astype(BF16)


def _norm(x, *, y=None, mod_post=None, gate_idx=None, npost=None,
          npre=None, mod_pre=None, scale_idx=None, shift_idx=None):
    tm = 512
    has_y, has_h = y is not None, npre is not None
    row = pl.BlockSpec((tm, D_MODEL), lambda i: (i, 0))
    vec = pl.BlockSpec((1, D_MODEL), lambda i: (0, 0))

    def modspec(k):
        return pl.BlockSpec((1, 1, 1, D_MODEL), lambda i: (_mod_row(i, tm), k, 0, 0))

    args, specs, out_shape, out_specs = [x], [row], [], []
    if has_y:
        args += [y, mod_post, npost.reshape(1, D_MODEL)]
        specs += [row, modspec(gate_idx), vec]
        out_shape.append(jax.ShapeDtypeStruct((T_ALL, D_MODEL), F32))
        out_specs.append(row)
    if has_h:
        args += [npre.reshape(1, D_MODEL), mod_pre, mod_pre]
        specs += [vec, modspec(scale_idx), modspec(shift_idx)]
        out_shape.append(jax.ShapeDtypeStruct((T_ALL, D_MODEL), BF16))
        out_specs.append(row)
    outs = pl.pallas_call(
        functools.partial(_norm_kernel, has_y=has_y, has_h=has_h),
        grid=(T_ALL // tm,),
        in_specs=specs, out_specs=out_specs, out_shape=out_shape,
        compiler_params=_params("parallel"),
        name="norm_y%d_h%d" % (has_y, has_h),
    )(*args)
    return outs


def _mm_kernel(x_ref, w_ref, o_ref):
    o_ref[...] = jnp.dot(x_ref[...].astype(BF16), w_ref[...].astype(BF16),
                         preferred_element_type=F32).astype(o_ref.dtype)


def _mm(x, w, l, out_dtype, name, n=None):
    m, k = x.shape
    n = w.shape[2] if n is None else n
    tm = min(m, 1024)
    tn = min(n, 512)
    return pl.pallas_call(
        _mm_kernel,
        grid=(m // tm, n // tn),
        in_specs=[pl.BlockSpec((tm, k), lambda i, j: (i, 0)),
                  pl.BlockSpec((None, k, tn), lambda i, j: (l, 0, j))],
        out_specs=pl.BlockSpec((tm, tn), lambda i, j: (i, j)),
        out_shape=jax.ShapeDtypeStruct((m, n), out_dtype),
        compiler_params=_params("parallel", "parallel"),
        name=name,
    )(x, w)


def _mm_nt_kernel(x_ref, wt_ref, o_ref):
    o_ref[...] = lax.dot_general(x_ref[...], wt_ref[...].astype(BF16), (((1,), (1,)), ((), ())),
                                 preferred_element_type=F32)


def _in_proj(h, w_t, l, col0, n, name):
    m, k = h.shape
    tm, tn = 1024, 512
    row0 = l * w_t.shape[1] + col0
    return pl.pallas_call(
        _mm_nt_kernel,
        grid=(m // tm, n // tn),
        in_specs=[pl.BlockSpec((tm, k), lambda i, j: (i, 0)),
                  pl.BlockSpec((pl.Element(tn), pl.Element(k)),
                               lambda i, j: ((row0 // 8 + j * (tn // 8)) * 8, 0))],
        out_specs=pl.BlockSpec((tm, tn), lambda i, j: (i, j)),
        out_shape=jax.ShapeDtypeStruct((m, n), F32),
        compiler_params=_params("parallel", "parallel"),
        name=name,
    )(h, w_t.reshape(-1, k))


def _norm_mm_kernel(x_ref, g_ref, w_ref, *out_refs, emit_normed):
    xn = _rms(x_ref[...]) * g_ref[...]
    if emit_normed:
        out_refs[0][...] = xn
    out_refs[-1][...] = jnp.dot(xn.astype(BF16), w_ref[...],
                                preferred_element_type=F32).astype(out_refs[-1].dtype)


def _norm_mm(z, col_block, k, gain, w, l, out_dtype, *, emit_normed, name):
    m = z.shape[0]
    n = w.shape[2]
    tm = 512
    out_shape = [jax.ShapeDtypeStruct((m, n), out_dtype)]
    out_specs = [pl.BlockSpec((tm, n), lambda i: (i, 0))]
    if emit_normed:
        out_shape.insert(0, jax.ShapeDtypeStruct((m, k), F32))
        out_specs.insert(0, pl.BlockSpec((tm, k), lambda i: (i, 0)))
    return pl.pallas_call(
        functools.partial(_norm_mm_kernel, emit_normed=emit_normed),
        grid=(m // tm,),
        in_specs=[pl.BlockSpec((tm, k), lambda i: (i, col_block)),
                  pl.BlockSpec((1, k), lambda i: (0, 0)),
                  pl.BlockSpec((None, k, n), lambda i: (l, 0, 0))],
        out_specs=out_specs, out_shape=out_shape,
        compiler_params=_params("parallel"),
        name=name,
    )(z, gain.reshape(1, k), w)


def _rope_tables(head_dim):
    n = DEC_SEQ
    half = head_dim // 2
    quarter = half // 2
    row = jnp.repeat(jnp.arange(n // GRID_W), GRID_W).astype(F32)
    col = jnp.tile(jnp.arange(GRID_W), n // GRID_W).astype(F32)
    inv = ROPE_BASE ** (-jnp.arange(0, half, 2, dtype=F32) / half)
    lane = jnp.arange(LANE)
    m = lane % half
    pos = jnp.where((lane // half)[None, :] == 0, row[:, None], col[:, None])
    ang = pos * inv[m % quarter][None, :]
    valid = (lane < head_dim)[None, :]
    cos = jnp.where(valid, jnp.cos(ang), 0.0)
    sin = jnp.where(valid, jnp.where(m < quarter, -1.0, 1.0)[None, :] * jnp.sin(ang), 0.0)
    return cos.astype(F32), sin.astype(F32)


def _rope(x, cos, sin, head_dim):
    quarter = head_dim // 4
    lane = lax.broadcasted_iota(jnp.int32, x.shape, 1)
    first = (lane % (2 * quarter)) < quarter
    partner = jnp.where(first, pltpu.roll(x, LANE - quarter, 1), pltpu.roll(x, quarter, 1))
    return x * cos + partner * sin


def _mla_attn_kernel(*refs, latent):
    if latent:
        q_ref, kv_ref, kr_ref, kvc_ref, krc_ref, cq_ref, sq_ref, ck_ref, sk_ref, _, o_ref = refs
    else:
        q_ref, kv_ref, kr_ref, o_ref = refs
    scale = (NOPE_B + ROPE_B) ** -0.5
    nt = (((1,), (1,)), ((), ()))
    kr = kr_ref[...]
    if latent:
        kr = _rope(kr, ck_ref[...], sk_ref[...], ROPE_B)
        krc = krc_ref[...].astype(BF16)
    kr = kr.astype(BF16)
    for h in range(H_B):
        c0 = h * 2 * LANE
        qn = q_ref[:, c0:c0 + LANE]
        qr = q_ref[:, c0 + LANE:c0 + 2 * LANE]
        if latent:
            qr = _rope(qr, cq_ref[...], sq_ref[...], ROPE_B)
        qh = jnp.concatenate([(qn * scale).astype(BF16), (qr * scale).astype(BF16)], axis=-1)
        kh = jnp.concatenate([kv_ref[:, c0:c0 + LANE], kr], axis=-1)
        vh = kv_ref[:, c0 + LANE:c0 + 2 * LANE]
        s = lax.dot_general(qh, kh, nt, preferred_element_type=F32)
        m = jnp.max(s, axis=-1, keepdims=True)
        if latent:
            khc = jnp.concatenate([kvc_ref[:, c0:c0 + LANE], krc], axis=-1)
            vhc = kvc_ref[:, c0 + LANE:c0 + 2 * LANE]
            sc = lax.dot_general(qh, khc, nt, preferred_element_type=F32)
            m = jnp.maximum(m, jnp.max(sc, axis=-1, keepdims=True))
        p = jnp.exp(s - m)
        l = jnp.sum(p, axis=-1, keepdims=True)
        o = jnp.dot(p.astype(BF16), vh, preferred_element_type=F32)
        if latent:
            pc = jnp.exp(sc - m)
            l = l + jnp.sum(pc, axis=-1, keepdims=True)
            o = o + jnp.dot(pc.astype(BF16), vhc, preferred_element_type=F32)
        o_ref[:, h * LANE:(h + 1) * LANE] = (o / l).astype(BF16)


def _rows_of(specs, args, prev):
    if prev is None:
        return specs, args, {}
    return specs + [pl.BlockSpec(memory_space=pl.ANY)], args + [prev], {len(args): 0}


def _mla_attn(q, kv, z, *, latent, kvc=None, krc=None, tables=None, prev=None):
    nb, n = (DEC_BATCH, DEC_SEQ) if latent else (BATCH, SEQ)
    tq = 256
    nq = n // tq
    off = T_CTX // n if latent else 0
    offq = T_CTX // tq if latent else 0
    w = H_B * 2 * LANE
    specs = [pl.BlockSpec((tq, w), lambda b, i: (offq + b * nq + i, 0)),
             pl.BlockSpec((n, w), lambda b, i: (off + b, 0)),
             pl.BlockSpec((n, LANE), lambda b, i: (off + b, Z_KR // LANE))]
    args = [q, kv, z]
    if latent:
        cos, sin = tables
        specs += [pl.BlockSpec((PAST_LEN, w), lambda b, i: (b, 0)),
                  pl.BlockSpec((PAST_LEN, LANE), lambda b, i: (b, 0)),
                  pl.BlockSpec((tq, LANE), lambda b, i: (i, 0)),
                  pl.BlockSpec((tq, LANE), lambda b, i: (i, 0)),
                  pl.BlockSpec((n, LANE), lambda b, i: (0, 0)),
                  pl.BlockSpec((n, LANE), lambda b, i: (0, 0))]
        args += [kvc, krc, cos, sin, cos, sin]
    specs, args, aliases = _rows_of(specs, args, prev)
    return pl.pallas_call(
        functools.partial(_mla_attn_kernel, latent=latent),
        grid=(nb, nq),
        in_specs=specs,
        out_specs=pl.BlockSpec((tq, H_B * V_B), lambda b, i: (offq + b * nq + i, 0)),
        out_shape=jax.ShapeDtypeStruct((T_ALL, H_B * V_B), BF16),
        input_output_aliases=aliases,
        compiler_params=_params("parallel", "parallel"),
        name="mla_attn_lat" if latent else "mla_attn_ctx",
    )(*args)


def _gqa_attn_kernel(*refs, latent, tq):
    if latent:
        q_ref, k_ref, v_ref, kc_ref, vc_ref, sink_ref, cq_ref, sq_ref, ck_ref, sk_ref, _, o_ref = refs
    else:
        q_ref, k_ref, v_ref, sink_ref, o_ref = refs
    scale = HD_C ** -0.5
    nt = (((1,), (1,)), ((), ()))
    rep = H_C // KVH_C
    n = k_ref.shape[0]
    if latent:
        kw = tq + 2 * WINDOW
        q0 = pl.program_id(1) * tq
        k0 = pl.multiple_of(jnp.clip(q0 - WINDOW, 0, n - kw), WINDOW)
        keys = pl.ds(k0, kw)
        qpos = q0 + lax.broadcasted_iota(jnp.int32, (tq, kw), 0)
        kpos = k0 + lax.broadcasted_iota(jnp.int32, (tq, kw), 1)
        band = jnp.abs(qpos - kpos) <= WINDOW
    else:
        keys = slice(None)
    for g in range(KVH_C):
        kg = k_ref[keys, g * LANE:(g + 1) * LANE]
        if latent:
            kg = _rope(kg, ck_ref[keys, :], sk_ref[keys, :], HD_C)
            kcg = kc_ref[:, g * LANE:(g + 1) * LANE].astype(BF16)
            vcg = vc_ref[:, g * LANE:(g + 1) * LANE].astype(BF16)
        kg = kg.astype(BF16)
        vg = v_ref[keys, g * LANE:(g + 1) * LANE].astype(BF16)
        for r in range(rep):
            h = g * rep + r
            qh = q_ref[:, h * LANE:(h + 1) * LANE]
            if latent:
                qh = _rope(qh, cq_ref[...], sq_ref[...], HD_C)
            qh = (qh * scale).astype(BF16)
            sk = sink_ref[h:h + 1, 0:1]
            s = lax.dot_general(qh, kg, nt, preferred_element_type=F32)
            if latent:
                s = jnp.where(band, s, NEG_INF)
            m = jnp.maximum(jnp.max(s, axis=-1, keepdims=True), sk)
            if latent:
                sc = lax.dot_general(qh, kcg, nt, preferred_element_type=F32)
                m = jnp.maximum(m, jnp.max(sc, axis=-1, keepdims=True))
            p = jnp.exp(s - m)
            l = jnp.sum(p, axis=-1, keepdims=True) + jnp.exp(sk - m)
            o = jnp.dot(p.astype(BF16), vg, preferred_element_type=F32)
            if latent:
                pc = jnp.exp(sc - m)
                l = l + jnp.sum(pc, axis=-1, keepdims=True)
                o = o + jnp.dot(pc.astype(BF16), vcg, preferred_element_type=F32)
            o_ref[:, h * LANE:(h + 1) * LANE] = (o / l).astype(BF16)


def _gqa_attn(z, sink_b, *, latent, kc=None, vc=None, tables=None, prev=None):
    nb, n = (DEC_BATCH, DEC_SEQ) if latent else (BATCH, SEQ)
    tq = 256
    nq = n // tq
    off = T_CTX // n if latent else 0
    offq = T_CTX // tq if latent else 0
    wq, wk = H_C * HD_C, KVH_C * HD_C
    specs = [pl.BlockSpec((tq, wq), lambda b, i: (offq + b * nq + i, 0)),
             pl.BlockSpec((n, wk), lambda b, i: (off + b, ZC_K // wk)),
             pl.BlockSpec((n, wk), lambda b, i: (off + b, ZC_V // wk))]
    args = [z, z, z]
    if latent:
        specs += [pl.BlockSpec((PAST_LEN, wk), lambda b, i: (b, 0)),
                  pl.BlockSpec((PAST_LEN, wk), lambda b, i: (b, 0))]
        args += [kc, vc]
    specs.append(pl.BlockSpec((H_C, LANE), lambda b, i: (0, 0)))
    args.append(sink_b)
    if latent:
        cos, sin = tables
        specs += [pl.BlockSpec((tq, LANE), lambda b, i: (i, 0)),
                  pl.BlockSpec((tq, LANE), lambda b, i: (i, 0)),
                  pl.BlockSpec((n, LANE), lambda b, i: (0, 0)),
                  pl.BlockSpec((n, LANE), lambda b, i: (0, 0))]
        args += [cos, sin, cos, sin]
    specs, args, aliases = _rows_of(specs, args, prev)
    return pl.pallas_call(
        functools.partial(_gqa_attn_kernel, latent=latent, tq=tq),
        grid=(nb, nq),
        in_specs=specs,
        out_specs=pl.BlockSpec((tq, wq), lambda b, i: (offq + b * nq + i, 0)),
        out_shape=jax.ShapeDtypeStruct((T_ALL, wq), BF16),
        input_output_aliases=aliases,
        compiler_params=_params("parallel", "parallel"),
        name="gqa_attn_lat" if latent else "gqa_attn_ctx",
    )(*args)


def _hgrn_tables():
    c = HGRN_CHUNK
    halves = [c >> (i + 1) for i in range(c.bit_length() - 1)]
    out = []
    for forward in (True, False):
        sums = np.zeros((len(halves) + 1, c, c), np.float32)
        level = np.full((c, c), -1, np.int32)
        level[np.arange(c), np.arange(c)] = 0
        for li, m in enumerate(halves):
            for r in range(c):
                pos = r % (2 * m)
                mid = r - pos + m
                late = pos >= m
                if forward:
                    lo, hi = (mid, r + 1) if late else (r + 1, mid)
                else:
                    lo, hi = (mid, r) if late else (r, mid)
                sums[li, r, lo:hi] = 1.0
                for s in range(r - pos, r - pos + 2 * m):
                    s_late = (s % (2 * m)) >= m
                    if (late and not s_late) if forward else (not late and s_late):
                        level[r, s] = li + 1
        for r in range(c):
            if forward:
                sums[-1, r, :r + 1] = 1.0
            else:
                sums[-1, r, r:] = 1.0
        sums = sums.reshape(-1, c)
        out.append((jnp.asarray(np.concatenate([sums, sums, sums], axis=1), BF16),
                    jnp.asarray(np.concatenate([level, level], axis=1))))
    return out


def _hgrn_kernel(*refs, n, has_s0, emit_state):
    it = iter(refs)
    q_ref, xf_ref, xb_ref, v_ref, ag_ref, lb_ref, gn_ref = (next(it) for _ in range(7))
    sums_refs = (next(it), next(it))
    level_refs = (next(it), next(it))
    s0_ref = next(it) if has_s0 else None
    if has_s0:
        next(it)
    o_ref = next(it)
    sfin_ref = next(it) if emit_state else None
    o_scr, qe_scr, u_scr, e_scr, st_scr = (next(it) for _ in range(5))

    c = HGRN_CHUNK
    nc = n // c
    nlev = c.bit_length() - 1
    nt = (((1,), (1,)), ((), ()))
    tn = (((0,), (0,)), ((), ()))
    zero = jnp.zeros((c, LANE), BF16)

    def blockdiag(x):
        return jnp.concatenate([jnp.concatenate([x[:, :LANE], zero], axis=1),
                                jnp.concatenate([zero, x[:, LANE:]], axis=1)], axis=0)

    def gates(x, lb):
        e = jnp.exp(-jnp.abs(x))
        big = 1.0 / (1.0 + e)
        small = e * big
        pos = x >= 0.0
        return jnp.log(lb + (1.0 - lb) * jnp.where(pos, big, small)), (1.0 - lb) * jnp.where(pos, small, big)

    for d in range(2):
        for hh in range(2):
            st_scr[d, hh] = s0_ref[0, d, hh].T if has_s0 else jnp.zeros((DV_A, DK_A), F32)

    group = 4

    def intra(t, carry):
        jobs = [(u, d) for u in range(group) for d in range(2)]
        chunk_of = [t * group + u for u in range(group)]
        rows = {u: pl.ds(pl.multiple_of(chunk_of[u] * c, c), c) for u in range(group)}
        q = {ci: q_ref[rows[ci], :] for ci, _ in jobs}
        v = {ci: v_ref[rows[ci], :].astype(BF16) for ci, _ in jobs}
        k = {}

        dall = {}
        for ci, d in jobs:
            g, k[ci, d] = gates((xf_ref, xb_ref)[d][rows[ci], :], lb_ref[d:d + 1, :])
            g_hi = g.astype(BF16)
            rem = g - g_hi.astype(F32)
            g_mid = rem.astype(BF16)
            g_lo = (rem - g_mid.astype(F32)).astype(BF16)
            dall[ci, d] = jnp.dot(sums_refs[d][...], jnp.concatenate([g_hi, g_mid, g_lo], axis=0),
                                  preferred_element_type=F32)

        scores = {}
        for ci, d in jobs:
            rs = [lax.dot_general(q[ci].astype(BF16), blockdiag(k[ci, d].astype(BF16)), nt,
                                  preferred_element_type=F32)]
            for li in range(nlev):
                e = jnp.exp(dall[ci, d][li * c:(li + 1) * c, :])
                rs.append(lax.dot_general((q[ci] * e).astype(BF16), blockdiag((k[ci, d] * e).astype(BF16)), nt,
                                          preferred_element_type=F32))
            scores[ci, d] = rs

        for ci, d in jobs:
            level = level_refs[d][...]
            a = jnp.where(level == 0, scores[ci, d][0], 0.0)
            for li in range(nlev):
                a = jnp.where(level == li + 1, scores[ci, d][li + 1], a)
            o_scr[d, rows[ci], :] = jnp.dot(a.astype(BF16), blockdiag(v[ci]), preferred_element_type=F32)
            gc = dall[ci, d][nlev * c:, :]
            g_end = gc[c - 1:c, :] if d == 0 else gc[0:1, :]
            qe_scr[d, rows[ci], :] = (q[ci] * jnp.exp(gc)).astype(BF16)
            kd = (k[ci, d] * jnp.exp(g_end - gc)).astype(BF16)
            e_scr[d, chunk_of[ci]] = jnp.broadcast_to(jnp.exp(g_end), (8, 2 * LANE))
            for hh in range(2):
                hl = slice(hh * LANE, (hh + 1) * LANE)
                u_scr[d, chunk_of[ci], hh] = lax.dot_general(v[ci][:, hl], kd[:, hl], tn,
                                                             preferred_element_type=F32)
        return carry

    lax.fori_loop(0, nc // group, intra, 0)

    def inter(i, carry):
        for d in range(2):
            ci = i if d == 0 else nc - 1 - i
            rows = pl.ds(pl.multiple_of(ci * c, c), c)
            e = e_scr[d, ci]
            for hh in range(2):
                hl = slice(hh * LANE, (hh + 1) * LANE)
                st = st_scr[d, hh]
                o_scr[d, rows, hl] += lax.dot_general(qe_scr[d, rows, hl], st.astype(BF16), nt,
                                                      preferred_element_type=F32)
                st_scr[d, hh] = st * e[0:1, hl] + u_scr[d, ci, hh]
        return carry

    lax.fori_loop(0, nc, inter, 0, unroll=4)

    def finish(i, carry):
        rows = pl.ds(pl.multiple_of(i * c, c), c)
        o = o_scr[0, rows, :] + o_scr[1, rows, :]
        o = jnp.concatenate([_rms(o[:, :LANE]), _rms(o[:, LANE:])], axis=1)
        ag = ag_ref[rows, :]
        o_ref[rows, :] = (o * gn_ref[...] * (ag * jax.nn.sigmoid(ag))).astype(BF16)
        return carry

    lax.fori_loop(0, nc, finish, 0, unroll=4)
    if emit_state:
        for d in range(2):
            for hh in range(2):
                sfin_ref[0, d, hh] = st_scr[d, hh].T


def _hgrn(z, lb_l, gnorm, tables, *, latent, s0=None, prev=None):
    nb, n = (DEC_BATCH, DEC_SEQ) if latent else (BATCH, SEQ)
    off = T_CTX // n if latent else 0
    emit_state = not latent
    w = 2 * LANE
    pairs = H_A // 2
    c = HGRN_CHUNK
    nc = n // c

    def zspec(k):
        return pl.BlockSpec((n, w), lambda b, p: (off + b, Z_A // w + k * pairs + p))

    def const(x):
        return pl.BlockSpec(x.shape, lambda b, p: (0, 0))

    (sums_f, level_f), (sums_b, level_b) = tables
    specs = [zspec(0), zspec(1), zspec(2), zspec(3), zspec(4),
             pl.BlockSpec((2, w), lambda b, p: (0, p)),
             pl.BlockSpec((1, w), lambda b, p: (0, 0)),
             const(sums_f), const(sums_b), const(level_f), const(level_b)]
    args = [z, z, z, z, z, lb_l, jnp.tile(gnorm.reshape(1, DV_A), (1, 2)),
            sums_f, sums_b, level_f, level_b]
    if latent:
        specs.append(pl.BlockSpec((1, 2, 2, DK_A, DV_A), lambda b, p: (b, 0, p, 0, 0)))
        args.append(s0)
    specs, args, aliases = _rows_of(specs, args, prev)
    out_shape = [jax.ShapeDtypeStruct((T_ALL, H_A * DV_A), BF16)]
    out_specs = [pl.BlockSpec((n, w), lambda b, p: (off + b, p))]
    if emit_state:
        out_shape.append(jax.ShapeDtypeStruct((nb, 2, H_A, DK_A, DV_A), F32))
        out_specs.append(pl.BlockSpec((1, 2, 2, DK_A, DV_A), lambda b, p: (b, 0, p, 0, 0)))
    scratch = [pltpu.VMEM((2, n, w), F32),
               pltpu.VMEM((2, n, w), BF16),
               pltpu.VMEM((2, nc, 2, DV_A, DK_A), F32),
               pltpu.VMEM((2, nc, 8, w), F32),
               pltpu.VMEM((2, 2, DV_A, DK_A), F32)]
    return pl.pallas_call(
        functools.partial(_hgrn_kernel, n=n, has_s0=latent, emit_state=emit_state),
        grid=(nb, pairs),
        in_specs=specs, out_specs=out_specs, out_shape=out_shape,
        scratch_shapes=scratch,
        input_output_aliases=aliases,
        compiler_params=_params("parallel", "parallel"),
        name="hgrn_lat" if latent else "hgrn_ctx",
    )(*args)


def _merge_kernel(oa_ref, ob_ref, oc_ref, wa_ref, wb_ref, wc_ref, g0_ref, g1_ref, g2_ref, wo_ref, y_ref):
    @pl.when(pl.program_id(1) == 0)
    def _():
        y_ref[...] = jnp.zeros_like(y_ref)

    merged = (jax.nn.sigmoid(g0_ref[...]) * jnp.dot(oa_ref[...], wa_ref[...], preferred_element_type=F32)
              + jax.nn.sigmoid(g1_ref[...]) * jnp.dot(ob_ref[...], wb_ref[...], preferred_element_type=F32)
              + jax.nn.sigmoid(g2_ref[...]) * jnp.dot(oc_ref[...], wc_ref[...], preferred_element_type=F32))
    y_ref[...] += jnp.dot(merged.astype(BF16), wo_ref[...], preferred_element_type=F32)


def _merge(oa, ob, oc, wa, wb, wc, z, wo, l):
    tm, tn = 512, 512
    nj = D_MODEL // tn
    kb = H_A * DV_A
    o_spec = pl.BlockSpec((tm, kb), lambda i, j: (i, 0))
    w_spec = pl.BlockSpec((None, kb, tn), lambda i, j: (l, 0, j))

    def gspec(k):
        return pl.BlockSpec((tm, tn), lambda i, j: (i, ZC_G // tn + k * nj + j))

    return pl.pallas_call(
        _merge_kernel,
        grid=(T_ALL // tm, nj),
        in_specs=[o_spec, o_spec, o_spec, w_spec, w_spec, w_spec, gspec(0), gspec(1), gspec(2),
                  pl.BlockSpec((None, tn, D_MODEL), lambda i, j: (l, j, 0))],
        out_specs=pl.BlockSpec((tm, D_MODEL), lambda i, j: (i, 0)),
        out_shape=jax.ShapeDtypeStruct((T_ALL, D_MODEL), F32),
        compiler_params=_params("parallel", "arbitrary"),
        name="merge_out",
    )(oa, ob, oc, wa, wb, wc, z, z, z, wo)


def _ffn_kernel(h_ref, wa_ref, wg_ref, ca_ref, cg_ref, wd_ref, y_ref, *, tm):
    i = pl.program_id(0)

    @pl.when(pl.program_id(1) == 0)
    def _():
        y_ref[...] = jnp.zeros_like(y_ref)

    h = h_ref[...]
    seq_len = jnp.where(i * tm < T_CTX, SEQ, DEC_SEQ)
    pos = lax.broadcasted_iota(jnp.int32, (tm, 1), 0) & (seq_len - 1)
    has_prev = pos != 0
    has_next = pos != seq_len - 1

    def conv(u, c_ref):
        prev = jnp.where(has_prev, pltpu.roll(u, 1, 0), 0.0)
        nxt = jnp.where(has_next, pltpu.roll(u, tm - 1, 0), 0.0)
        return c_ref[0:1, :] * prev + c_ref[1:2, :] * u + c_ref[2:3, :] * nxt

    a = conv(jnp.dot(h, wa_ref[...], preferred_element_type=F32), ca_ref)
    g = conv(jnp.dot(h, wg_ref[...], preferred_element_type=F32), cg_ref)
    act = (a * jax.nn.gelu(g)).astype(BF16)
    y_ref[...] += jnp.dot(act, wd_ref[...], preferred_element_type=F32)


def _ffn(h, w_up, conv, w_down, l):
    tm, tf = 1024, 512
    nj = D_FF_PAD // tf
    return pl.pallas_call(
        functools.partial(_ffn_kernel, tm=tm),
        grid=(T_ALL // tm, nj),
        in_specs=[pl.BlockSpec((tm, D_MODEL), lambda i, j: (i, 0)),
                  pl.BlockSpec((None, D_MODEL, tf), lambda i, j: (l, 0, j)),
                  pl.BlockSpec((None, D_MODEL, tf), lambda i, j: (l, 0, nj + j)),
                  pl.BlockSpec((None, CONV_W, tf), lambda i, j: (l, 0, j)),
                  pl.BlockSpec((None, CONV_W, tf), lambda i, j: (l, 0, nj + j)),
                  pl.BlockSpec((None, tf, D_MODEL), lambda i, j: (l, j, 0))],
        out_specs=pl.BlockSpec((tm, D_MODEL), lambda i, j: (i, 0)),
        out_shape=jax.ShapeDtypeStruct((T_ALL, D_MODEL), F32),
        compiler_params=_params("parallel", "arbitrary"),
        name="conv_ffn",
    )(h, w_up, w_up, conv, conv, w_down)


def _pad_cols(w, n):
    return jnp.pad(w, [(0, 0)] * (w.ndim - 1) + [(0, n - w.shape[-1])])


def _prep_weights(w_in, mla_w_uq, mla_w_ukv, w_branch_a, w_branch_b, w_branch_c, w_out,
                  ffn_w_up, ffn_conv, ffn_w_down):
    w_uq = _pad_cols(mla_w_uq.reshape(DEPTH, Q_LORA, H_B, NOPE_B + ROPE_B), 2 * LANE)
    w_uq = w_uq.reshape(DEPTH, Q_LORA, H_B * 2 * LANE).astype(BF16)
    w_up = jnp.concatenate([_pad_cols(ffn_w_up[:, :, :D_FF], D_FF_PAD),
                            _pad_cols(ffn_w_up[:, :, D_FF:], D_FF_PAD)], axis=-1).astype(BF16)
    conv = jnp.concatenate([_pad_cols(ffn_conv[:, :, :D_FF], D_FF_PAD),
                            _pad_cols(ffn_conv[:, :, D_FF:], D_FF_PAD)], axis=-1)
    w_down = jnp.pad(ffn_w_down, ((0, 0), (0, D_FF_PAD - D_FF), (0, 0))).astype(BF16)
    return dict(w_in_t=jnp.swapaxes(w_in, 1, 2), w_uq=w_uq, w_ukv=mla_w_ukv.astype(BF16),
                w_a=w_branch_a.astype(BF16), w_b=w_branch_b.astype(BF16), w_c=w_branch_c.astype(BF16),
                w_o=w_out.astype(BF16), w_up=w_up, conv=conv, w_down=w_down)


def kernel(x_prompt, x_sample, state_hgrn, cache_mla_ckv, cache_mla_krope, cache_swa_k, cache_swa_v,
           c, c_ctx, w_mod, b_mod, norm_pre_attn, norm_post_attn, norm_pre_ffn, norm_post_ffn,
           w_in, hgrn_lb, hgrn_gnorm, mla_gq, mla_w_uq, mla_gkv, mla_w_ukv, swa_sink,
           w_branch_a, w_branch_b, w_branch_c, w_out, ffn_w_up, ffn_conv, ffn_w_down):
    wts = _prep_weights(w_in, mla_w_uq, mla_w_ukv, w_branch_a, w_branch_b, w_branch_c, w_out,
                        ffn_w_up, ffn_conv, ffn_w_down)
    cs = jnp.cumsum(jax.nn.softmax(hgrn_lb.astype(F32), axis=0), axis=0)
    lb_all = cs - cs[0]

    cvec = jnp.concatenate([c_ctx[None, :], c, jnp.zeros((MOD_ROWS - 1 - DEC_BATCH, D_MODEL), F32)], axis=0)
    mod = _modulation(cvec, w_mod, b_mod).reshape(DEPTH, MOD_ROWS, 6, 1, D_MODEL)

    hgrn_tables = _hgrn_tables()
    rope_b = _rope_tables(ROPE_B)
    rope_c = _rope_tables(HD_C)
    sink_b = jnp.broadcast_to(swa_sink[:, :, None], (DEPTH, H_C, LANE))

    x = jnp.concatenate([x_prompt.reshape(T_CTX, D_MODEL), x_sample.reshape(T_LAT, D_MODEL)], axis=0)
    new_hgrn, new_ckv, new_krope, new_k, new_v = [], [], [], [], []
    y = None
    for l in range(DEPTH):
        mod_l = mod[l]
        if l == 0:
            (h,) = _norm(x, npre=norm_pre_attn[l], mod_pre=mod_l, scale_idx=1, shift_idx=0)
        else:
            x, h = _norm(x, y=y, mod_post=mod[l - 1], gate_idx=5, npost=norm_post_ffn[l - 1],
                         npre=norm_pre_attn[l], mod_pre=mod_l, scale_idx=1, shift_idx=0)
        zab = _in_proj(h, wts["w_in_t"], l, 0, ZAB_W, "in_proj_ab")
        zcg = _in_proj(h, wts["w_in_t"], l, Z_B_END, ZCG_W, "in_proj_cg")

        oa, s_ctx = _hgrn(zab, lb_all[l], hgrn_gnorm[l], hgrn_tables, latent=False)
        (oa,) = _hgrn(zab, lb_all[l], hgrn_gnorm[l], hgrn_tables, latent=True, s0=state_hgrn[:, l], prev=oa)
        new_hgrn.append(s_ctx)

        (qb,) = _norm_mm(zab, Z_B // Q_LORA, Q_LORA, mla_gq[l], wts["w_uq"], l, F32, emit_normed=False,
                         name="mla_q_proj")
        ckv, kvb = _norm_mm(zab, Z_KV // KV_LORA, KV_LORA, mla_gkv[l], wts["w_ukv"], l, BF16, emit_normed=True,
                            name="mla_kv_proj")
        kv_cache = _mm(cache_mla_ckv[:, l].reshape(DEC_BATCH * PAST_LEN, KV_LORA), wts["w_ukv"], l, BF16,
                       "mla_kv_cache")
        kr_cache = _pad_cols(cache_mla_krope[:, l].reshape(DEC_BATCH * PAST_LEN, ROPE_B), LANE)
        ob = _mla_attn(qb, kvb, zab, latent=False)
        ob = _mla_attn(qb, kvb, zab, latent=True, kvc=kv_cache, krc=kr_cache, tables=rope_b, prev=ob)
        new_ckv.append(ckv[:T_CTX].reshape(BATCH, SEQ, KV_LORA))
        new_krope.append(zab[:T_CTX, Z_KR:Z_B_END].reshape(BATCH, SEQ, ROPE_B))

        oc = _gqa_attn(zcg, sink_b[l], latent=False)
        oc = _gqa_attn(zcg, sink_b[l], latent=True,
                       kc=cache_swa_k[:, l].reshape(DEC_BATCH * PAST_LEN, KVH_C * HD_C),
                       vc=cache_swa_v[:, l].reshape(DEC_BATCH * PAST_LEN, KVH_C * HD_C), tables=rope_c, prev=oc)
        new_k.append(zcg[:T_CTX, ZC_K:ZC_V].reshape(BATCH, SEQ, KVH_C, HD_C))
        new_v.append(zcg[:T_CTX, ZC_V:ZC_G].reshape(BATCH, SEQ, KVH_C, HD_C))

        y = _merge(oa, ob, oc, wts["w_a"], wts["w_b"], wts["w_c"], zcg, wts["w_o"], l)

        x, h = _norm(x, y=y, mod_post=mod_l, gate_idx=2, npost=norm_post_attn[l],
                     npre=norm_pre_ffn[l], mod_pre=mod_l, scale_idx=4, shift_idx=3)
        y = _ffn(h, wts["w_up"], wts["conv"], wts["w_down"], l)

    (x,) = _norm(x, y=y, mod_post=mod[DEPTH - 1], gate_idx=5, npost=norm_post_ffn[DEPTH - 1])
    return (x[:T_CTX].reshape(BATCH, SEQ, D_MODEL), x[T_CTX:].reshape(DEC_BATCH, DEC_SEQ, D_MODEL),
            jnp.stack(new_hgrn, axis=1), jnp.stack(new_ckv, axis=1), jnp.stack(new_krope, axis=1),
            jnp.stack(new_k, axis=1), jnp.stack(new_v, axis=1))
```

```python
import functools

import jax
import jax.numpy as jnp
import numpy as np
from jax import lax
from jax.experimental import pallas as pl
from jax.experimental.pallas import tpu as pltpu

F32 = jnp.float32
BF16 = jnp.bfloat16

D_MODEL = 2048
BATCH = 16
SEQ = 256
DEPTH = 4
DEC_BATCH = 4
DEC_SEQ = 1024
PAST_LEN = 256
GRID_W = 64
ROPE_BASE = 10000.0
EPS = 1e-6
NEG_INF = -1e30
H_A, DK_A, DV_A = 8, 128, 128
H_B, Q_LORA, KV_LORA, NOPE_B, ROPE_B, V_B = 8, 512, 256, 128, 64, 128
H_C, KVH_C, HD_C, WINDOW = 8, 2, 128, 128
N_BRANCH = 3
D_FF = 5504
CONV_W = 3

T_CTX = BATCH * SEQ
T_LAT = DEC_BATCH * DEC_SEQ
T_ALL = T_CTX + T_LAT
MOD_ROWS = 8
LANE = 128
D_FF_PAD = 5632
HGRN_CHUNK = 64
Z_A = 0
Z_B = 5 * H_A * DK_A
Z_KV = Z_B + Q_LORA
Z_KR = Z_KV + KV_LORA
Z_B_END = Z_KR + ROPE_B
ZAB_W = 6144
ZC_K = H_C * HD_C
ZC_V = ZC_K + KVH_C * HD_C
ZC_G = ZC_V + KVH_C * HD_C
ZCG_W = ZC_G + N_BRANCH * D_MODEL
VMEM_LIMIT = 56 * 1024 * 1024


def _params(*sem, flags=None):
    return pltpu.CompilerParams(dimension_semantics=sem, vmem_limit_bytes=VMEM_LIMIT, flags=flags)


def _mod_row(i, tm):
    return jnp.where(i * tm < T_CTX, 0, 1 + (i * tm - T_CTX) // DEC_SEQ)


def _rms(x):
    return x * lax.rsqrt(jnp.mean(x * x, axis=-1, keepdims=True) + EPS)


def _mod_kernel(c_ref, w_ref, b_ref, o_ref):
    cv = c_ref[...]
    s = (cv * jax.nn.sigmoid(cv)).astype(BF16)
    o_ref[0] = jnp.dot(s, w_ref[0].astype(BF16), preferred_element_type=F32) + b_ref[0]


def _modulation(cvec, w_mod, b_mod):
    tn = 1024
    n = 6 * D_MODEL
    return pl.pallas_call(
        _mod_kernel,
        grid=(DEPTH, n // tn),
        in_specs=[pl.BlockSpec((MOD_ROWS, D_MODEL), lambda l, j: (0, 0)),
                  pl.BlockSpec((1, D_MODEL, tn), lambda l, j: (l, 0, j)),
                  pl.BlockSpec((1, 1, tn), lambda l, j: (l, 0, j))],
        out_specs=pl.BlockSpec((1, MOD_ROWS, tn), lambda l, j: (l, 0, j)),
        out_shape=jax.ShapeDtypeStruct((DEPTH, MOD_ROWS, n), F32),
        compiler_params=_params("parallel", "parallel"),
        name="modulation",
    )(cvec, w_mod, b_mod.reshape(DEPTH, 1, n))


def _norm_kernel(*refs, has_y, has_h):
    it = iter(refs)
    x_ref = next(it)
    if has_y:
        y_ref, gate_ref, npost_ref = next(it), next(it), next(it)
    if has_h:
        npre_ref, scale_ref, shift_ref = next(it), next(it), next(it)
    x = x_ref[...]
    if has_y:
        xnew_ref = next(it)
        x = x + gate_ref[0, 0] * (_rms(y_ref[...]) * npost_ref[...])
        xnew_ref[...] = x
    if has_h:
        h_ref = next(it)
        h = (_rms(x) * npre_ref[...]) * (1.0 + scale_ref[0, 0]) + shift_ref[0, 0]
        h_ref[...] = h.astype(BF16)


def _norm(x, *, y=None, mod_post=None, gate_idx=None, npost=None,
          npre=None, mod_pre=None, scale_idx=None, shift_idx=None):
    tm = 512
    has_y, has_h = y is not None, npre is not None
    row = pl.BlockSpec((tm, D_MODEL), lambda i: (i, 0))
    vec = pl.BlockSpec((1, D_MODEL), lambda i: (0, 0))

    def modspec(k):
        return pl.BlockSpec((1, 1, 1, D_MODEL), lambda i: (_mod_row(i, tm), k, 0, 0))

    args, specs, out_shape, out_specs = [x], [row], [], []
    if has_y:
        args += [y, mod_post, npost.reshape(1, D_MODEL)]
        specs += [row, modspec(gate_idx), vec]
        out_shape.append(jax.ShapeDtypeStruct((T_ALL, D_MODEL), F32))
        out_specs.append(row)
    if has_h:
        args += [npre.reshape(1, D_MODEL), mod_pre, mod_pre]
        specs += [vec, modspec(scale_idx), modspec(shift_idx)]
        out_shape.append(jax.ShapeDtypeStruct((T_ALL, D_MODEL), BF16))
        out_specs.append(row)
    outs = pl.pallas_call(
        functools.partial(_norm_kernel, has_y=has_y, has_h=has_h),
        grid=(T_ALL // tm,),
        in_specs=specs, out_specs=out_specs, out_shape=out_shape,
        compiler_params=_params("parallel"),
        name="norm_y%d_h%d" % (has_y, has_h),
    )(*args)
    return outs


def _mm_kernel(x_ref, w_ref, o_ref):
    o_ref[...] = jnp.dot(x_ref[...].astype(BF16), w_ref[...].astype(BF16),
                         preferred_element_type=F32).astype(o_ref.dtype)


def _mm(x, w, l, out_dtype, name, n=None):
    m, k = x.shape
    n = w.shape[2] if n is None else n
    tm = min(m, 1024)
    tn = min(n, 512)
    return pl.pallas_call(
        _mm_kernel,
        grid=(m // tm, n // tn),
        in_specs=[pl.BlockSpec((tm, k), lambda i, j: (i, 0)),
                  pl.BlockSpec((None, k, tn), lambda i, j: (l, 0, j))],
        out_specs=pl.BlockSpec((tm, tn), lambda i, j: (i, j)),
        out_shape=jax.ShapeDtypeStruct((m, n), out_dtype),
        compiler_params=_params("parallel", "parallel"),
        name=name,
    )(x, w)


def _mm_nt_kernel(x_ref, wt_ref, o_ref):
    o_ref[...] = lax.dot_general(x_ref[...], wt_ref[...].astype(BF16), (((1,), (1,)), ((), ())),
                                 preferred_element_type=F32)


def _in_proj(h, w_t, l, col0, n, name):
    m, k = h.shape
    tm, tn = 2048, 512
    row0 = l * w_t.shape[1] + col0
    return pl.pallas_call(
        _mm_nt_kernel,
        grid=(m // tm, n // tn),
        in_specs=[pl.BlockSpec((tm, k), lambda i, j: (i, 0)),
                  pl.BlockSpec((pl.Element(tn), pl.Element(k)),
                               lambda i, j: ((row0 // 8 + j * (tn // 8)) * 8, 0))],
        out_specs=pl.BlockSpec((tm, tn), lambda i, j: (i, j)),
        out_shape=jax.ShapeDtypeStruct((m, n), F32),
        compiler_params=_params("parallel", "parallel"),
        name=name,
    )(h, w_t.reshape(-1, k))


def _norm_mm_kernel(x_ref, g_ref, w_ref, *out_refs, emit_normed):
    xn = _rms(x_ref[...]) * g_ref[...]
    if emit_normed:
        out_refs[0][...] = xn
    out_refs[-1][...] = jnp.dot(xn.astype(BF16), w_ref[...],
                                preferred_element_type=F32).astype(out_refs[-1].dtype)


def _norm_mm(z, col_block, k, gain, w, l, out_dtype, *, emit_normed, name):
    m = z.shape[0]
    n = w.shape[2]
    tm = 512
    out_shape = [jax.ShapeDtypeStruct((m, n), out_dtype)]
    out_specs = [pl.BlockSpec((tm, n), lambda i: (i, 0))]
    if emit_normed:
        out_shape.insert(0, jax.ShapeDtypeStruct((m, k), F32))
        out_specs.insert(0, pl.BlockSpec((tm, k), lambda i: (i, 0)))
    return pl.pallas_call(
        functools.partial(_norm_mm_kernel, emit_normed=emit_normed),
        grid=(m // tm,),
        in_specs=[pl.BlockSpec((tm, k), lambda i: (i, col_block)),
                  pl.BlockSpec((1, k), lambda i: (0, 0)),
                  pl.BlockSpec((None, k, n), lambda i: (l, 0, 0))],
        out_specs=out_specs, out_shape=out_shape,
        compiler_params=_params("parallel"),
        name=name,
    )(z, gain.reshape(1, k), w)


def _rope_tables(head_dim):
    n = DEC_SEQ
    half = head_dim // 2
    quarter = half // 2
    row = jnp.repeat(jnp.arange(n // GRID_W), GRID_W).astype(F32)
    col = jnp.tile(jnp.arange(GRID_W), n // GRID_W).astype(F32)
    inv = ROPE_BASE ** (-jnp.arange(0, half, 2, dtype=F32) / half)
    lane = jnp.arange(LANE)
    m = lane % half
    pos = jnp.where((lane // half)[None, :] == 0, row[:, None], col[:, None])
    ang = pos * inv[m % quarter][None, :]
    valid = (lane < head_dim)[None, :]
    cos = jnp.where(valid, jnp.cos(ang), 0.0)
    sin = jnp.where(valid, jnp.where(m < quarter, -1.0, 1.0)[None, :] * jnp.sin(ang), 0.0)
    return cos.astype(F32), sin.astype(F32)


def _rope(x, cos, sin, head_dim):
    quarter = head_dim // 4
    lane = lax.broadcasted_iota(jnp.int32, x.shape, 1)
    first = (lane % (2 * quarter)) < quarter
    partner = jnp.where(first, pltpu.roll(x, LANE - quarter, 1), pltpu.roll(x, quarter, 1))
    return x * cos + partner * sin


def _mla_attn_kernel(*refs, latent):
    if latent:
        q_ref, kv_ref, kr_ref, kvc_ref, krc_ref, cq_ref, sq_ref, ck_ref, sk_ref, _, o_ref = refs
    else:
        q_ref, kv_ref, kr_ref, o_ref = refs
    scale = (NOPE_B + ROPE_B) ** -0.5
    nt = (((1,), (1,)), ((), ()))
    kr = kr_ref[...]
    if latent:
        kr = _rope(kr, ck_ref[...], sk_ref[...], ROPE_B)
        krc = krc_ref[...].astype(BF16)
    kr = kr.astype(BF16)
    for h in range(H_B):
        c0 = h * 2 * LANE
        qn = q_ref[:, c0:c0 + LANE]
        qr = q_ref[:, c0 + LANE:c0 + 2 * LANE]
        if latent:
            qr = _rope(qr, cq_ref[...], sq_ref[...], ROPE_B)
        qh = jnp.concatenate([(qn * scale).astype(BF16), (qr * scale).astype(BF16)], axis=-1)
        kh = jnp.concatenate([kv_ref[:, c0:c0 + LANE], kr], axis=-1)
        vh = kv_ref[:, c0 + LANE:c0 + 2 * LANE]
        s = lax.dot_general(qh, kh, nt, preferred_element_type=F32)
        m = jnp.max(s, axis=-1, keepdims=True)
        if latent:
            khc = jnp.concatenate([kvc_ref[:, c0:c0 + LANE], krc], axis=-1)
            vhc = kvc_ref[:, c0 + LANE:c0 + 2 * LANE]
            sc = lax.dot_general(qh, khc, nt, preferred_element_type=F32)
            m = jnp.maximum(m, jnp.max(sc, axis=-1, keepdims=True))
        p = jnp.exp(s - m)
        l = jnp.sum(p, axis=-1, keepdims=True)
        o = jnp.dot(p.astype(BF16), vh, preferred_element_type=F32)
        if latent:
            pc = jnp.exp(sc - m)
            l = l + jnp.sum(pc, axis=-1, keepdims=True)
            o = o + jnp.dot(pc.astype(BF16), vhc, preferred_element_type=F32)
        o_ref[:, h * LANE:(h + 1) * LANE] = (o / l).astype(BF16)


def _rows_of(specs, args, prev):
    if prev is None:
        return specs, args, {}
    return specs + [pl.BlockSpec(memory_space=pl.ANY)], args + [prev], {len(args): 0}


def _mla_attn(q, kv, z, *, latent, kvc=None, krc=None, tables=None, prev=None):
    nb, n = (DEC_BATCH, DEC_SEQ) if latent else (BATCH, SEQ)
    tq = 256
    nq = n // tq
    off = T_CTX // n if latent else 0
    offq = T_CTX // tq if latent else 0
    w = H_B * 2 * LANE
    specs = [pl.BlockSpec((tq, w), lambda b, i: (offq + b * nq + i, 0)),
             pl.BlockSpec((n, w), lambda b, i: (off + b, 0)),
             pl.BlockSpec((n, LANE), lambda b, i: (off + b, Z_KR // LANE))]
    args = [q, kv, z]
    if latent:
        cos, sin = tables
        specs += [pl.BlockSpec((PAST_LEN, w), lambda b, i: (b, 0)),
                  pl.BlockSpec((PAST_LEN, LANE), lambda b, i: (b, 0)),
                  pl.BlockSpec((tq, LANE), lambda b, i: (i, 0)),
                  pl.BlockSpec((tq, LANE), lambda b, i: (i, 0)),
                  pl.BlockSpec((n, LANE), lambda b, i: (0, 0)),
                  pl.BlockSpec((n, LANE), lambda b, i: (0, 0))]
        args += [kvc, krc, cos, sin, cos, sin]
    specs, args, aliases = _rows_of(specs, args, prev)
    return pl.pallas_call(
        functools.partial(_mla_attn_kernel, latent=latent),
        grid=(nb, nq),
        in_specs=specs,
        out_specs=pl.BlockSpec((tq, H_B * V_B), lambda b, i: (offq + b * nq + i, 0)),
        out_shape=jax.ShapeDtypeStruct((T_ALL, H_B * V_B), BF16),
        input_output_aliases=aliases,
        compiler_params=_params("parallel", "parallel"),
        name="mla_attn_lat" if latent else "mla_attn_ctx",
    )(*args)


def _gqa_attn_kernel(*refs, latent, tq):
    if latent:
        q_ref, k_ref, v_ref, kc_ref, vc_ref, sink_ref, cq_ref, sq_ref, ck_ref, sk_ref, _, o_ref = refs
    else:
        q_ref, k_ref, v_ref, sink_ref, o_ref = refs
    scale = HD_C ** -0.5
    nt = (((1,), (1,)), ((), ()))
    rep = H_C // KVH_C
    n = k_ref.shape[0]
    if latent:
        kw = tq + 2 * WINDOW
        q0 = pl.program_id(1) * tq
        k0 = pl.multiple_of(jnp.clip(q0 - WINDOW, 0, n - kw), WINDOW)
        keys = pl.ds(k0, kw)
        qpos = q0 + lax.broadcasted_iota(jnp.int32, (tq, kw), 0)
        kpos = k0 + lax.broadcasted_iota(jnp.int32, (tq, kw), 1)
        band = jnp.abs(qpos - kpos) <= WINDOW
    else:
        keys = slice(None)
    for g in range(KVH_C):
        kg = k_ref[keys, g * LANE:(g + 1) * LANE]
        if latent:
            kg = _rope(kg, ck_ref[keys, :], sk_ref[keys, :], HD_C)
            kcg = kc_ref[:, g * LANE:(g + 1) * LANE].astype(BF16)
            vcg = vc_ref[:, g * LANE:(g + 1) * LANE].astype(BF16)
        kg = kg.astype(BF16)
        vg = v_ref[keys, g * LANE:(g + 1) * LANE].astype(BF16)
        for r in range(rep):
            h = g * rep + r
            qh = q_ref[:, h * LANE:(h + 1) * LANE]
            if latent:
                qh = _rope(qh, cq_ref[...], sq_ref[...], HD_C)
            qh = (qh * scale).astype(BF16)
            sk = sink_ref[h:h + 1, 0:1]
            s = lax.dot_general(qh, kg, nt, preferred_element_type=F32)
            if latent:
                s = jnp.where(band, s, NEG_INF)
            m = jnp.maximum(jnp.max(s, axis=-1, keepdims=True), sk)
            if latent:
                sc = lax.dot_general(qh, kcg, nt, preferred_element_type=F32)
                m = jnp.maximum(m, jnp.max(sc, axis=-1, keepdims=True))
            p = jnp.exp(s - m)
            l = jnp.sum(p, axis=-1, keepdims=True) + jnp.exp(sk - m)
            o = jnp.dot(p.astype(BF16), vg, preferred_element_type=F32)
            if latent:
                pc = jnp.exp(sc - m)
                l = l + jnp.sum(pc, axis=-1, keepdims=True)
                o = o + jnp.dot(pc.astype(BF16), vcg, preferred_element_type=F32)
            o_ref[:, h * LANE:(h + 1) * LANE] = (o / l).astype(BF16)


def _gqa_attn(z, sink_b, *, latent, kc=None, vc=None, tables=None, prev=None):
    nb, n = (DEC_BATCH, DEC_SEQ) if latent else (BATCH, SEQ)
    tq = 256
    nq = n // tq
    off = T_CTX // n if latent else 0
    offq = T_CTX // tq if latent else 0
    wq, wk = H_C * HD_C, KVH_C * HD_C
    specs = [pl.BlockSpec((tq, wq), lambda b, i: (offq + b * nq + i, 0)),
             pl.BlockSpec((n, wk), lambda b, i: (off + b, ZC_K // wk)),
             pl.BlockSpec((n, wk), lambda b, i: (off + b, ZC_V // wk))]
    args = [z, z, z]
    if latent:
        specs += [pl.BlockSpec((PAST_LEN, wk), lambda b, i: (b, 0)),
                  pl.BlockSpec((PAST_LEN, wk), lambda b, i: (b, 0))]
        args += [kc, vc]
    specs.append(pl.BlockSpec((H_C, LANE), lambda b, i: (0, 0)))
    args.append(sink_b)
    if latent:
        cos, sin = tables
        specs += [pl.BlockSpec((tq, LANE), lambda b, i: (i, 0)),
                  pl.BlockSpec((tq, LANE), lambda b, i: (i, 0)),
                  pl.BlockSpec((n, LANE), lambda b, i: (0, 0)),
                  pl.BlockSpec((n, LANE), lambda b, i: (0, 0))]
        args += [cos, sin, cos, sin]
    specs, args, aliases = _rows_of(specs, args, prev)
    return pl.pallas_call(
        functools.partial(_gqa_attn_kernel, latent=latent, tq=tq),
        grid=(nb, nq),
        in_specs=specs,
        out_specs=pl.BlockSpec((tq, wq), lambda b, i: (offq + b * nq + i, 0)),
        out_shape=jax.ShapeDtypeStruct((T_ALL, wq), BF16),
        input_output_aliases=aliases,
        compiler_params=_params("parallel", "parallel"),
        name="gqa_attn_lat" if latent else "gqa_attn_ctx",
    )(*args)


def _hgrn_tables():
    c = HGRN_CHUNK
    halves = [c >> (i + 1) for i in range(c.bit_length() - 1)]
    out = []
    for forward in (True, False):
        sums = np.zeros((len(halves) + 1, c, c), np.float32)
        level = np.full((c, c), -1, np.int32)
        level[np.arange(c), np.arange(c)] = 0
        for li, m in enumerate(halves):
            for r in range(c):
                pos = r % (2 * m)
                mid = r - pos + m
                late = pos >= m
                if forward:
                    lo, hi = (mid, r + 1) if late else (r + 1, mid)
                else:
                    lo, hi = (mid, r) if late else (r, mid)
                sums[li, r, lo:hi] = 1.0
                for s in range(r - pos, r - pos + 2 * m):
                    s_late = (s % (2 * m)) >= m
                    if (late and not s_late) if forward else (not late and s_late):
                        level[r, s] = li + 1
        for r in range(c):
            if forward:
                sums[-1, r, :r + 1] = 1.0
            else:
                sums[-1, r, r:] = 1.0
        sums = sums.reshape(-1, c)
        out.append((jnp.asarray(np.concatenate([sums, sums, sums], axis=1), BF16),
                    jnp.asarray(np.concatenate([level, level], axis=1))))
    return out


def _hgrn_kernel(*refs, n, has_s0, emit_state):
    it = iter(refs)
    q_ref, xf_ref, xb_ref, v_ref, ag_ref, lb_ref, gn_ref = (next(it) for _ in range(7))
    sums_refs = (next(it), next(it))
    level_refs = (next(it), next(it))
    s0_ref = next(it) if has_s0 else None
    if has_s0:
        next(it)
    o_ref = next(it)
    sfin_ref = next(it) if emit_state else None
    o_scr, qe_scr, u_scr, e_scr, st_scr = (next(it) for _ in range(5))

    c = HGRN_CHUNK
    nc = n // c
    nlev = c.bit_length() - 1
    nt = (((1,), (1,)), ((), ()))
    tn = (((0,), (0,)), ((), ()))
    zero = jnp.zeros((c, LANE), BF16)

    def blockdiag(x):
        return jnp.concatenate([jnp.concatenate([x[:, :LANE], zero], axis=1),
                                jnp.concatenate([zero, x[:, LANE:]], axis=1)], axis=0)

    def gates(x, lb):
        e = jnp.exp(-jnp.abs(x))
        big = 1.0 / (1.0 + e)
        small = e * big
        pos = x >= 0.0
        return jnp.log(lb + (1.0 - lb) * jnp.where(pos, big, small)), (1.0 - lb) * jnp.where(pos, small, big)

    for d in range(2):
        for hh in range(2):
            st_scr[d, hh] = s0_ref[0, d, hh].T if has_s0 else jnp.zeros((DV_A, DK_A), F32)

    group = 4

    def intra(t, carry):
        jobs = [(u, d) for u in range(group) for d in range(2)]
        chunk_of = [t * group + u for u in range(group)]
        rows = {u: pl.ds(pl.multiple_of(chunk_of[u] * c, c), c) for u in range(group)}
        q = {ci: q_ref[rows[ci], :] for ci, _ in jobs}
        v = {ci: v_ref[rows[ci], :].astype(BF16) for ci, _ in jobs}
        k = {}

        dall = {}
        for ci, d in jobs:
            g, k[ci, d] = gates((xf_ref, xb_ref)[d][rows[ci], :], lb_ref[d:d + 1, :])
            g_hi = g.astype(BF16)
            rem = g - g_hi.astype(F32)
            g_mid = rem.astype(BF16)
            g_lo = (rem - g_mid.astype(F32)).astype(BF16)
            dall[ci, d] = jnp.dot(sums_refs[d][...], jnp.concatenate([g_hi, g_mid, g_lo], axis=0),
                                  preferred_element_type=F32)

        scores = {}
        for ci, d in jobs:
            rs = [lax.dot_general(q[ci].astype(BF16), blockdiag(k[ci, d].astype(BF16)), nt,
                                  preferred_element_type=F32)]
            for li in range(nlev):
                e = jnp.exp(dall[ci, d][li * c:(li + 1) * c, :])
                rs.append(lax.dot_general((q[ci] * e).astype(BF16), blockdiag((k[ci, d] * e).astype(BF16)), nt,
                                          preferred_element_type=F32))
            scores[ci, d] = rs

        for ci, d in jobs:
            level = level_refs[d][...]
            a = jnp.where(level == 0, scores[ci, d][0], 0.0)
            for li in range(nlev):
                a = jnp.where(level == li + 1, scores[ci, d][li + 1], a)
            o_scr[d, rows[ci], :] = jnp.dot(a.astype(BF16), blockdiag(v[ci]), preferred_element_type=F32)
            gc = dall[ci, d][nlev * c:, :]
            g_end = gc[c - 1:c, :] if d == 0 else gc[0:1, :]
            qe_scr[d, rows[ci], :] = (q[ci] * jnp.exp(gc)).astype(BF16)
            kd = (k[ci, d] * jnp.exp(g_end - gc)).astype(BF16)
            e_scr[d, chunk_of[ci]] = jnp.broadcast_to(jnp.exp(g_end), (8, 2 * LANE))
            for hh in range(2):
                hl = slice(hh * LANE, (hh + 1) * LANE)
                u_scr[d, chunk_of[ci], hh] = lax.dot_general(v[ci][:, hl], kd[:, hl], tn,
                                                             preferred_element_type=F32)
        return carry

    lax.fori_loop(0, nc // group, intra, 0)

    def inter(i, carry):
        for d in range(2):
            ci = i if d == 0 else nc - 1 - i
            rows = pl.ds(pl.multiple_of(ci * c, c), c)
            e = e_scr[d, ci]
            for hh in range(2):
                hl = slice(hh * LANE, (hh + 1) * LANE)
                st = st_scr[d, hh]
                o_scr[d, rows, hl] += lax.dot_general(qe_scr[d, rows, hl], st.astype(BF16), nt,
                                                      preferred_element_type=F32)
                st_scr[d, hh] = st * e[0:1, hl] + u_scr[d, ci, hh]
        return carry

    lax.fori_loop(0, nc, inter, 0, unroll=4)

    def finish(i, carry):
        rows = pl.ds(pl.multiple_of(i * c, c), c)
        o = o_scr[0, rows, :] + o_scr[1, rows, :]
        o = jnp.concatenate([_rms(o[:, :LANE]), _rms(o[:, LANE:])], axis=1)
        ag = ag_ref[rows, :]
        o_ref[rows, :] = (o * gn_ref[...] * (ag * jax.nn.sigmoid(ag))).astype(BF16)
        return carry

    lax.fori_loop(0, nc, finish, 0, unroll=4)
    if emit_state:
        for d in range(2):
            for hh in range(2):
                sfin_ref[0, d, hh] = st_scr[d, hh].T


def _hgrn(z, lb_l, gnorm, tables, *, latent, s0=None, prev=None):
    nb, n = (DEC_BATCH, DEC_SEQ) if latent else (BATCH, SEQ)
    off = T_CTX // n if latent else 0
    emit_state = not latent
    w = 2 * LANE
    pairs = H_A // 2
    c = HGRN_CHUNK
    nc = n // c

    def zspec(k):
        return pl.BlockSpec((n, w), lambda b, p: (off + b, Z_A // w + k * pairs + p))

    def const(x):
        return pl.BlockSpec(x.shape, lambda b, p: (0, 0))

    (sums_f, level_f), (sums_b, level_b) = tables
    specs = [zspec(0), zspec(1), zspec(2), zspec(3), zspec(4),
             pl.BlockSpec((2, w), lambda b, p: (0, p)),
             pl.BlockSpec((1, w), lambda b, p: (0, 0)),
             const(sums_f), const(sums_b), const(level_f), const(level_b)]
    args = [z, z, z, z, z, lb_l, jnp.tile(gnorm.reshape(1, DV_A), (1, 2)),
            sums_f, sums_b, level_f, level_b]
    if latent:
        specs.append(pl.BlockSpec((1, 2, 2, DK_A, DV_A), lambda b, p: (b, 0, p, 0, 0)))
        args.append(s0)
    specs, args, aliases = _rows_of(specs, args, prev)
    out_shape = [jax.ShapeDtypeStruct((T_ALL, H_A * DV_A), BF16)]
    out_specs = [pl.BlockSpec((n, w), lambda b, p: (off + b, p))]
    if emit_state:
        out_shape.append(jax.ShapeDtypeStruct((nb, 2, H_A, DK_A, DV_A), F32))
        out_specs.append(pl.BlockSpec((1, 2, 2, DK_A, DV_A), lambda b, p: (b, 0, p, 0, 0)))
    scratch = [pltpu.VMEM((2, n, w), F32),
               pltpu.VMEM((2, n, w), BF16),
               pltpu.VMEM((2, nc, 2, DV_A, DK_A), F32),
               pltpu.VMEM((2, nc, 8, w), F32),
               pltpu.VMEM((2, 2, DV_A, DK_A), F32)]
    return pl.pallas_call(
        functools.partial(_hgrn_kernel, n=n, has_s0=latent, emit_state=emit_state),
        grid=(nb, pairs),
        in_specs=specs, out_specs=out_specs, out_shape=out_shape,
        scratch_shapes=scratch,
        input_output_aliases=aliases,
        compiler_params=_params("parallel", "parallel"),
        name="hgrn_lat" if latent else "hgrn_ctx",
    )(*args)


def _merge_kernel(oa_ref, ob_ref, oc_ref, wa_ref, wb_ref, wc_ref, g0_ref, g1_ref, g2_ref, wo_ref, y_ref):
    @pl.when(pl.program_id(1) == 0)
    def _():
        y_ref[...] = jnp.zeros_like(y_ref)

    merged = (jax.nn.sigmoid(g0_ref[...]) * jnp.dot(oa_ref[...], wa_ref[...], preferred_element_type=F32)
              + jax.nn.sigmoid(g1_ref[...]) * jnp.dot(ob_ref[...], wb_ref[...], preferred_element_type=F32)
              + jax.nn.sigmoid(g2_ref[...]) * jnp.dot(oc_ref[...], wc_ref[...], preferred_element_type=F32))
    y_ref[...] += jnp.dot(merged.astype(BF16), wo_ref[...], preferred_element_type=F32)


def _merge(oa, ob, oc, wa, wb, wc, z, wo, l):
    tm, tn = 512, 512
    nj = D_MODEL // tn
    kb = H_A * DV_A
    o_spec = pl.BlockSpec((tm, kb), lambda i, j: (i, 0))
    w_spec = pl.BlockSpec((None, kb, tn), lambda i, j: (l, 0, j))

    def gspec(k):
        return pl.BlockSpec((tm, tn), lambda i, j: (i, ZC_G // tn + k * nj + j))

    return pl.pallas_call(
        _merge_kernel,
        grid=(T_ALL // tm, nj),
        in_specs=[o_spec, o_spec, o_spec, w_spec, w_spec, w_spec, gspec(0), gspec(1), gspec(2),
                  pl.BlockSpec((None, tn, D_MODEL), lambda i, j: (l, j, 0))],
        out_specs=pl.BlockSpec((tm, D_MODEL), lambda i, j: (i, 0)),
        out_shape=jax.ShapeDtypeStruct((T_ALL, D_MODEL), F32),
        compiler_params=_params("parallel", "arbitrary"),
        name="merge_out",
    )(oa, ob, oc, wa, wb, wc, z, z, z, wo)


def _ffn_kernel(h_ref, wa_ref, wg_ref, ca_ref, cg_ref, wd_ref, y_ref, *, tm):
    i = pl.program_id(0)

    @pl.when(pl.program_id(1) == 0)
    def _():
        y_ref[...] = jnp.zeros_like(y_ref)

    h = h_ref[...]
    seq_len = jnp.where(i * tm < T_CTX, SEQ, DEC_SEQ)
    pos = lax.broadcasted_iota(jnp.int32, (tm, 1), 0) & (seq_len - 1)
    has_prev = pos != 0
    has_next = pos != seq_len - 1

    def conv(u, c):
        prev = jnp.where(has_prev, pltpu.roll(u, 1, 0), 0.0)
        nxt = jnp.where(has_next, pltpu.roll(u, tm - 1, 0), 0.0)
        return c[0:1, :] * prev + c[1:2, :] * u + c[2:3, :] * nxt

    tf = wa_ref.shape[1]
    subs = (slice(0, tf // 2), slice(tf // 2, tf))
    ups = [(jnp.dot(h, wa_ref[:, cols], preferred_element_type=F32),
            jnp.dot(h, wg_ref[:, cols], preferred_element_type=F32)) for cols in subs]
    for cols, (ua, ug) in zip(subs, ups):
        act = (conv(ua, ca_ref[:, cols]) * jax.nn.gelu(conv(ug, cg_ref[:, cols]))).astype(BF16)
        y_ref[...] += jnp.dot(act, wd_ref[cols, :], preferred_element_type=F32)


def _ffn(h, w_up, conv, w_down, l):
    tm, tf = 1024, 512
    nj = D_FF_PAD // tf
    return pl.pallas_call(
        functools.partial(_ffn_kernel, tm=tm),
        grid=(T_ALL // tm, nj),
        in_specs=[pl.BlockSpec((tm, D_MODEL), lambda i, j: (i, 0)),
                  pl.BlockSpec((None, D_MODEL, tf), lambda i, j: (l, 0, j)),
                  pl.BlockSpec((None, D_MODEL, tf), lambda i, j: (l, 0, nj + j)),
                  pl.BlockSpec((None, CONV_W, tf), lambda i, j: (l, 0, j)),
                  pl.BlockSpec((None, CONV_W, tf), lambda i, j: (l, 0, nj + j)),
                  pl.BlockSpec((None, tf, D_MODEL), lambda i, j: (l, j, 0))],
        out_specs=pl.BlockSpec((tm, D_MODEL), lambda i, j: (i, 0)),
        out_shape=jax.ShapeDtypeStruct((T_ALL, D_MODEL), F32),
        compiler_params=_params("parallel", "arbitrary"),
        name="conv_ffn",
    )(h, w_up, w_up, conv, conv, w_down)


def _pad_cols(w, n):
    return jnp.pad(w, [(0, 0)] * (w.ndim - 1) + [(0, n - w.shape[-1])])


def _prep_weights(w_in, mla_w_uq, mla_w_ukv, w_branch_a, w_branch_b, w_branch_c, w_out,
                  ffn_w_up, ffn_conv, ffn_w_down):
    w_uq = _pad_cols(mla_w_uq.reshape(DEPTH, Q_LORA, H_B, NOPE_B + ROPE_B), 2 * LANE)
    w_uq = w_uq.reshape(DEPTH, Q_LORA, H_B * 2 * LANE).astype(BF16)
    w_up = jnp.concatenate([_pad_cols(ffn_w_up[:, :, :D_FF], D_FF_PAD),
                            _pad_cols(ffn_w_up[:, :, D_FF:], D_FF_PAD)], axis=-1).astype(BF16)
    conv = jnp.concatenate([_pad_cols(ffn_conv[:, :, :D_FF], D_FF_PAD),
                            _pad_cols(ffn_conv[:, :, D_FF:], D_FF_PAD)], axis=-1)
    w_down = jnp.pad(ffn_w_down, ((0, 0), (0, D_FF_PAD - D_FF), (0, 0))).astype(BF16)
    return dict(w_in_t=jnp.swapaxes(w_in, 1, 2), w_uq=w_uq, w_ukv=mla_w_ukv.astype(BF16),
                w_a=w_branch_a.astype(BF16), w_b=w_branch_b.astype(BF16), w_c=w_branch_c.astype(BF16),
                w_o=w_out.astype(BF16), w_up=w_up, conv=conv, w_down=w_down)


def kernel(x_prompt, x_sample, state_hgrn, cache_mla_ckv, cache_mla_krope, cache_swa_k, cache_swa_v,
           c, c_ctx, w_mod, b_mod, norm_pre_attn, norm_post_attn, norm_pre_ffn, norm_post_ffn,
           w_in, hgrn_lb, hgrn_gnorm, mla_gq, mla_w_uq, mla_gkv, mla_w_ukv, swa_sink,
           w_branch_a, w_branch_b, w_branch_c, w_out, ffn_w_up, ffn_conv, ffn_w_down):
    wts = _prep_weights(w_in, mla_w_uq, mla_w_ukv, w_branch_a, w_branch_b, w_branch_c, w_out,
                        ffn_w_up, ffn_conv, ffn_w_down)
    cs = jnp.cumsum(jax.nn.softmax(hgrn_lb.astype(F32), axis=0), axis=0)
    lb_all = cs - cs[0]

    cvec = jnp.concatenate([c_ctx[None, :], c, jnp.zeros((MOD_ROWS - 1 - DEC_BATCH, D_MODEL), F32)], axis=0)
    mod = _modulation(cvec, w_mod, b_mod).reshape(DEPTH, MOD_ROWS, 6, 1, D_MODEL)

    hgrn_tables = _hgrn_tables()
    rope_b = _rope_tables(ROPE_B)
    rope_c = _rope_tables(HD_C)
    sink_b = jnp.broadcast_to(swa_sink[:, :, None], (DEPTH, H_C, LANE))

    x = jnp.concatenate([x_prompt.reshape(T_CTX, D_MODEL), x_sample.reshape(T_LAT, D_MODEL)], axis=0)
    new_hgrn, new_ckv, new_krope, new_k, new_v = [], [], [], [], []
    y = None
    for l in range(DEPTH):
        mod_l = mod[l]
        if l == 0:
            (h,) = _norm(x, npre=norm_pre_attn[l], mod_pre=mod_l, scale_idx=1, shift_idx=0)
        else:
            x, h = _norm(x, y=y, mod_post=mod[l - 1], gate_idx=5, npost=norm_post_ffn[l - 1],
                         npre=norm_pre_attn[l], mod_pre=mod_l, scale_idx=1, shift_idx=0)
        zab = _in_proj(h, wts["w_in_t"], l, 0, ZAB_W, "in_proj_ab")
        zcg = _in_proj(h, wts["w_in_t"], l, Z_B_END, ZCG_W, "in_proj_cg")

        oa, s_ctx = _hgrn(zab, lb_all[l], hgrn_gnorm[l], hgrn_tables, latent=False)
        (oa,) = _hgrn(zab, lb_all[l], hgrn_gnorm[l], hgrn_tables, latent=True, s0=state_hgrn[:, l], prev=oa)
        new_hgrn.append(s_ctx)

        (qb,) = _norm_mm(zab, Z_B // Q_LORA, Q_LORA, mla_gq[l], wts["w_uq"], l, F32, emit_normed=False,
                         name="mla_q_proj")
        ckv, kvb = _norm_mm(zab, Z_KV // KV_LORA, KV_LORA, mla_gkv[l], wts["w_ukv"], l, BF16, emit_normed=True,
                            name="mla_kv_proj")
        kv_cache = _mm(cache_mla_ckv[:, l].reshape(DEC_BATCH * PAST_LEN, KV_LORA), wts["w_ukv"], l, BF16,
                       "mla_kv_cache")
        kr_cache = _pad_cols(cache_mla_krope[:, l].reshape(DEC_BATCH * PAST_LEN, ROPE_B), LANE)
        ob = _mla_attn(qb, kvb, zab, latent=False)
        ob = _mla_attn(qb, kvb, zab, latent=True, kvc=kv_cache, krc=kr_cache, tables=rope_b, prev=ob)
        new_ckv.append(ckv[:T_CTX].reshape(BATCH, SEQ, KV_LORA))
        new_krope.append(zab[:T_CTX, Z_KR:Z_B_END].reshape(BATCH, SEQ, ROPE_B))

        oc = _gqa_attn(zcg, sink_b[l], latent=False)
        oc = _gqa_attn(zcg, sink_b[l], latent=True,
                       kc=cache_swa_k[:, l].reshape(DEC_BATCH * PAST_LEN, KVH_C * HD_C),
                       vc=cache_swa_v[:, l].reshape(DEC_BATCH * PAST_LEN, KVH_C * HD_C), tables=rope_c, prev=oc)
        new_k.append(zcg[:T_CTX, ZC_K:ZC_V].reshape(BATCH, SEQ, KVH_C, HD_C))
        new_v.append(zcg[:T_CTX, ZC_V:ZC_G].reshape(BATCH, SEQ, KVH_C, HD_C))

        y = _merge(oa, ob, oc, wts["w_a"], wts["w_b"], wts["w_c"], zcg, wts["w_o"], l)

        x, h = _norm(x, y=y, mod_post=mod_l, gate_idx=2, npost=norm_post_attn[l],
                     npre=norm_pre_ffn[l], mod_pre=mod_l, scale_idx=4, shift_idx=3)
        y = _ffn(h, wts["w_up"], wts["conv"], wts["w_down"], l)

    (x,) = _norm(x, y=y, mod_post=mod[DEPTH - 1], gate_idx=5, npost=norm_post_ffn[DEPTH - 1])
    return (x[:T_CTX].reshape(BATCH, SEQ, D_MODEL), x[T_CTX:].reshape(DEC_BATCH, DEC_SEQ, D_MODEL),
            jnp.stack(new_hgrn, axis=1), jnp.stack(new_ckv, axis=1), jnp.stack(new_krope, axis=1),
            jnp.stack(new_k, axis=1), jnp.stack(new_v, axis=1))
```

```python
import functools

import jax
import jax.numpy as jnp
import numpy as np
from jax import lax
from jax.experimental import pallas as pl
from jax.experimental.pallas import tpu as pltpu

F32 = jnp.float32
BF16 = jnp.bfloat16

D_MODEL = 2048
BATCH = 16
SEQ = 256
DEPTH = 4
DEC_BATCH = 4
DEC_SEQ = 1024
PAST_LEN = 256
GRID_W = 64
ROPE_BASE = 10000.0
EPS = 1e-6
NEG_INF = -1e30
H_A, DK_A, DV_A = 8, 128, 128
H_B, Q_LORA, KV_LORA, NOPE_B, ROPE_B, V_B = 8, 512, 256, 128, 64, 128
H_C, KVH_C, HD_C, WINDOW = 8, 2, 128, 128
N_BRANCH = 3
D_FF = 5504
CONV_W = 3

T_CTX = BATCH * SEQ
T_LAT = DEC_BATCH * DEC_SEQ
T_ALL = T_CTX + T_LAT
MOD_ROWS = 8
LANE = 128
D_FF_PAD = 5632
HGRN_CHUNK = 64
Z_A = 0
Z_B = 5 * H_A * DK_A
Z_KV = Z_B + Q_LORA
Z_KR = Z_KV + KV_LORA
Z_B_END = Z_KR + ROPE_B
ZAB_W = 6144
ZC_K = H_C * HD_C
ZC_V = ZC_K + KVH_C * HD_C
ZC_G = ZC_V + KVH_C * HD_C
ZCG_W = ZC_G + N_BRANCH * D_MODEL
VMEM_LIMIT = 56 * 1024 * 1024


def _params(*sem, flags=None):
    return pltpu.CompilerParams(dimension_semantics=sem, vmem_limit_bytes=VMEM_LIMIT, flags=flags)


def _mod_row(i, tm):
    return jnp.where(i * tm < T_CTX, 0, 1 + (i * tm - T_CTX) // DEC_SEQ)


def _rms(x):
    return x * lax.rsqrt(jnp.mean(x * x, axis=-1, keepdims=True) + EPS)


def _mod_kernel(c_ref, w_ref, b_ref, o_ref):
    cv = c_ref[...]
    s = (cv * jax.nn.sigmoid(cv)).astype(BF16)
    o_ref[0] = jnp.dot(s, w_ref[0].astype(BF16), preferred_element_type=F32) + b_ref[0]


def _modulation(cvec, w_mod, b_mod):
    tn = 1024
    n = 6 * D_MODEL
    return pl.pallas_call(
        _mod_kernel,
        grid=(DEPTH, n // tn),
        in_specs=[pl.BlockSpec((MOD_ROWS, D_MODEL), lambda l, j: (0, 0)),
                  pl.BlockSpec((1, D_MODEL, tn), lambda l, j: (l, 0, j)),
                  pl.BlockSpec((1, 1, tn), lambda l, j: (l, 0, j))],
        out_specs=pl.BlockSpec((1, MOD_ROWS, tn), lambda l, j: (l, 0, j)),
        out_shape=jax.ShapeDtypeStruct((DEPTH, MOD_ROWS, n), F32),
        compiler_params=_params("parallel", "parallel"),
        name="modulation",
    )(cvec, w_mod, b_mod.reshape(DEPTH, 1, n))


def _norm_kernel(*refs, has_y, has_h):
    it = iter(refs)
    x_ref = next(it)
    if has_y:
        y_ref, gate_ref, npost_ref = next(it), next(it), next(it)
    if has_h:
        npre_ref, scale_ref, shift_ref = next(it), next(it), next(it)
    x = x_ref[...]
    if has_y:
        xnew_ref = next(it)
        x = x + gate_ref[0, 0] * (_rms(y_ref[...]) * npost_ref[...])
        xnew_ref[...] = x
    if has_h:
        h_ref = next(it)
        h = (_rms(x) * npre_ref[...]) * (1.0 + scale_ref[0, 0]) + shift_ref[0, 0]
        h_ref[...] = h.astype(BF16)


def _norm(x, *, y=None, mod_post=None, gate_idx=None, npost=None,
          npre=None, mod_pre=None, scale_idx=None, shift_idx=None):
    tm = 512
    has_y, has_h = y is not None, npre is not None
    row = pl.BlockSpec((tm, D_MODEL), lambda i: (i, 0))
    vec = pl.BlockSpec((1, D_MODEL), lambda i: (0, 0))

    def modspec(k):
        return pl.BlockSpec((1, 1, 1, D_MODEL), lambda i: (_mod_row(i, tm), k, 0, 0))

    args, specs, out_shape, out_specs = [x], [row], [], []
    if has_y:
        args += [y, mod_post, npost.reshape(1, D_MODEL)]
        specs += [row, modspec(gate_idx), vec]
        out_shape.append(jax.ShapeDtypeStruct((T_ALL, D_MODEL), F32))
        out_specs.append(row)
    if has_h:
        args += [npre.reshape(1, D_MODEL), mod_pre, mod_pre]
        specs += [vec, modspec(scale_idx), modspec(shift_idx)]
        out_shape.append(jax.ShapeDtypeStruct((T_ALL, D_MODEL), BF16))
        out_specs.append(row)
    outs = pl.pallas_call(
        functools.partial(_norm_kernel, has_y=has_y, has_h=has_h),
        grid=(T_ALL // tm,),
        in_specs=specs, out_specs=out_specs, out_shape=out_shape,
        compiler_params=_params("parallel"),
        name="norm_y%d_h%d" % (has_y, has_h),
    )(*args)
    return outs


def _mm_kernel(x_ref, w_ref, o_ref):
    o_ref[...] = jnp.dot(x_ref[...].astype(BF16), w_ref[...].astype(BF16),
                         preferred_element_type=F32).astype(o_ref.dtype)


def _mm(x, w, l, out_dtype, name, n=None):
    m, k = x.shape
    n = w.shape[2] if n is None else n
    tm = min(m, 1024)
    tn = min(n, 512)
    return pl.pallas_call(
        _mm_kernel,
        grid=(m // tm, n // tn),
        in_specs=[pl.BlockSpec((tm, k), lambda i, j: (i, 0)),
                  pl.BlockSpec((None, k, tn), lambda i, j: (l, 0, j))],
        out_specs=pl.BlockSpec((tm, tn), lambda i, j: (i, j)),
        out_shape=jax.ShapeDtypeStruct((m, n), out_dtype),
        compiler_params=_params("parallel", "parallel"),
        name=name,
    )(x, w)


def _mm_nt_kernel(x_ref, wt_ref, o_ref):
    o_ref[...] = lax.dot_general(x_ref[...], wt_ref[...].astype(BF16), (((1,), (1,)), ((), ())),
                                 preferred_element_type=F32).astype(o_ref.dtype)


def _in_proj(h, w_t, l, col0, n, name, out_dtype=F32):
    m, k = h.shape
    tm, tn = 2048, 512
    row0 = l * w_t.shape[1] + col0
    return pl.pallas_call(
        _mm_nt_kernel,
        grid=(m // tm, n // tn),
        in_specs=[pl.BlockSpec((tm, k), lambda i, j: (i, 0)),
                  pl.BlockSpec((pl.Element(tn), pl.Element(k)),
                               lambda i, j: ((row0 // 8 + j * (tn // 8)) * 8, 0))],
        out_specs=pl.BlockSpec((tm, tn), lambda i, j: (i, j)),
        out_shape=jax.ShapeDtypeStruct((m, n), out_dtype),
        compiler_params=_params("parallel", "parallel"),
        name=name,
    )(h, w_t.reshape(-1, k))


def _norm_mm_kernel(x_ref, g_ref, w_ref, *out_refs, emit_normed):
    xn = _rms(x_ref[...]) * g_ref[...]
    if emit_normed:
        out_refs[0][...] = xn
    out_refs[-1][...] = jnp.dot(xn.astype(BF16), w_ref[...],
                                preferred_element_type=F32).astype(out_refs[-1].dtype)


def _norm_mm(z, col_block, k, gain, w, l, out_dtype, *, emit_normed, name):
    m = z.shape[0]
    n = w.shape[2]
    tm = 512
    out_shape = [jax.ShapeDtypeStruct((m, n), out_dtype)]
    out_specs = [pl.BlockSpec((tm, n), lambda i: (i, 0))]
    if emit_normed:
        out_shape.insert(0, jax.ShapeDtypeStruct((m, k), F32))
        out_specs.insert(0, pl.BlockSpec((tm, k), lambda i: (i, 0)))
    return pl.pallas_call(
        functools.partial(_norm_mm_kernel, emit_normed=emit_normed),
        grid=(m // tm,),
        in_specs=[pl.BlockSpec((tm, k), lambda i: (i, col_block)),
                  pl.BlockSpec((1, k), lambda i: (0, 0)),
                  pl.BlockSpec((None, k, n), lambda i: (l, 0, 0))],
        out_specs=out_specs, out_shape=out_shape,
        compiler_params=_params("parallel"),
        name=name,
    )(z, gain.reshape(1, k), w)


def _rope_tables(head_dim):
    n = DEC_SEQ
    half = head_dim // 2
    quarter = half // 2
    row = jnp.repeat(jnp.arange(n // GRID_W), GRID_W).astype(F32)
    col = jnp.tile(jnp.arange(GRID_W), n // GRID_W).astype(F32)
    inv = ROPE_BASE ** (-jnp.arange(0, half, 2, dtype=F32) / half)
    lane = jnp.arange(LANE)
    m = lane % half
    pos = jnp.where((lane // half)[None, :] == 0, row[:, None], col[:, None])
    ang = pos * inv[m % quarter][None, :]
    valid = (lane < head_dim)[None, :]
    cos = jnp.where(valid, jnp.cos(ang), 0.0)
    sin = jnp.where(valid, jnp.where(m < quarter, -1.0, 1.0)[None, :] * jnp.sin(ang), 0.0)
    return cos.astype(F32), sin.astype(F32)


def _rope(x, cos, sin, head_dim):
    quarter = head_dim // 4
    lane = lax.broadcasted_iota(jnp.int32, x.shape, 1)
    first = (lane % (2 * quarter)) < quarter
    partner = jnp.where(first, pltpu.roll(x, LANE - quarter, 1), pltpu.roll(x, quarter, 1))
    return x * cos + partner * sin


def _mla_attn_kernel(*refs, latent):
    if latent:
        q_ref, kv_ref, kr_ref, kvc_ref, krc_ref, cq_ref, sq_ref, ck_ref, sk_ref, _, o_ref = refs
    else:
        q_ref, kv_ref, kr_ref, o_ref = refs
    scale = (NOPE_B + ROPE_B) ** -0.5
    nt = (((1,), (1,)), ((), ()))
    kr = kr_ref[...]
    if latent:
        kr = _rope(kr, ck_ref[...], sk_ref[...], ROPE_B)
        krc = krc_ref[...].astype(BF16)
    kr = kr.astype(BF16)
    for h in range(H_B):
        c0 = h * 2 * LANE
        qn = q_ref[:, c0:c0 + LANE]
        qr = q_ref[:, c0 + LANE:c0 + 2 * LANE]
        if latent:
            qr = _rope(qr, cq_ref[...], sq_ref[...], ROPE_B)
        qh = jnp.concatenate([(qn * scale).astype(BF16), (qr * scale).astype(BF16)], axis=-1)
        kh = jnp.concatenate([kv_ref[:, c0:c0 + LANE], kr], axis=-1)
        vh = kv_ref[:, c0 + LANE:c0 + 2 * LANE]
        s = lax.dot_general(qh, kh, nt, preferred_element_type=F32)
        m = jnp.max(s, axis=-1, keepdims=True)
        if latent:
            khc = jnp.concatenate([kvc_ref[:, c0:c0 + LANE], krc], axis=-1)
            vhc = kvc_ref[:, c0 + LANE:c0 + 2 * LANE]
            sc = lax.dot_general(qh, khc, nt, preferred_element_type=F32)
            m = jnp.maximum(m, jnp.max(sc, axis=-1, keepdims=True))
        p = jnp.exp(s - m)
        l = jnp.sum(p, axis=-1, keepdims=True)
        o = jnp.dot(p.astype(BF16), vh, preferred_element_type=F32)
        if latent:
            pc = jnp.exp(sc - m)
            l = l + jnp.sum(pc, axis=-1, keepdims=True)
            o = o + jnp.dot(pc.astype(BF16), vhc, preferred_element_type=F32)
        o_ref[:, h * LANE:(h + 1) * LANE] = (o / l).astype(BF16)


def _rows_of(specs, args, prev):
    if prev is None:
        return specs, args, {}
    return specs + [pl.BlockSpec(memory_space=pl.ANY)], args + [prev], {len(args): 0}


def _mla_attn(q, kv, z, *, latent, kvc=None, krc=None, tables=None, prev=None):
    nb, n = (DEC_BATCH, DEC_SEQ) if latent else (BATCH, SEQ)
    tq = 256
    nq = n // tq
    off = T_CTX // n if latent else 0
    offq = T_CTX // tq if latent else 0
    w = H_B * 2 * LANE
    specs = [pl.BlockSpec((tq, w), lambda b, i: (offq + b * nq + i, 0)),
             pl.BlockSpec((n, w), lambda b, i: (off + b, 0)),
             pl.BlockSpec((n, LANE), lambda b, i: (off + b, Z_KR // LANE))]
    args = [q, kv, z]
    if latent:
        cos, sin = tables
        specs += [pl.BlockSpec((PAST_LEN, w), lambda b, i: (b, 0)),
                  pl.BlockSpec((PAST_LEN, LANE), lambda b, i: (b, 0)),
                  pl.BlockSpec((tq, LANE), lambda b, i: (i, 0)),
                  pl.BlockSpec((tq, LANE), lambda b, i: (i, 0)),
                  pl.BlockSpec((n, LANE), lambda b, i: (0, 0)),
                  pl.BlockSpec((n, LANE), lambda b, i: (0, 0))]
        args += [kvc, krc, cos, sin, cos, sin]
    specs, args, aliases = _rows_of(specs, args, prev)
    return pl.pallas_call(
        functools.partial(_mla_attn_kernel, latent=latent),
        grid=(nb, nq),
        in_specs=specs,
        out_specs=pl.BlockSpec((tq, H_B * V_B), lambda b, i: (offq + b * nq + i, 0)),
        out_shape=jax.ShapeDtypeStruct((T_ALL, H_B * V_B), BF16),
        input_output_aliases=aliases,
        compiler_params=_params("parallel", "parallel"),
        name="mla_attn_lat" if latent else "mla_attn_ctx",
    )(*args)


def _gqa_attn_kernel(*refs, latent, tq):
    if latent:
        q_ref, k_ref, v_ref, kc_ref, vc_ref, sink_ref, cq_ref, sq_ref, ck_ref, sk_ref, _, o_ref = refs
    else:
        q_ref, k_ref, v_ref, sink_ref, o_ref = refs
    scale = HD_C ** -0.5
    nt = (((1,), (1,)), ((), ()))
    rep = H_C // KVH_C
    n = k_ref.shape[0]
    if latent:
        kw = tq + 2 * WINDOW
        q0 = pl.program_id(1) * tq
        k0 = pl.multiple_of(jnp.clip(q0 - WINDOW, 0, n - kw), WINDOW)
        keys = pl.ds(k0, kw)
        qpos = q0 + lax.broadcasted_iota(jnp.int32, (tq, kw), 0)
        kpos = k0 + lax.broadcasted_iota(jnp.int32, (tq, kw), 1)
        band = jnp.abs(qpos - kpos) <= WINDOW
    else:
        keys = slice(None)
    for g in range(KVH_C):
        kg = k_ref[keys, g * LANE:(g + 1) * LANE]
        if latent:
            kg = _rope(kg, ck_ref[keys, :], sk_ref[keys, :], HD_C)
            kcg = kc_ref[:, g * LANE:(g + 1) * LANE].astype(BF16)
            vcg = vc_ref[:, g * LANE:(g + 1) * LANE].astype(BF16)
        kg = kg.astype(BF16)
        vg = v_ref[keys, g * LANE:(g + 1) * LANE].astype(BF16)
        for r in range(rep):
            h = g * rep + r
            qh = q_ref[:, h * LANE:(h + 1) * LANE]
            if latent:
                qh = _rope(qh, cq_ref[...], sq_ref[...], HD_C)
            qh = (qh * scale).astype(BF16)
            sk = sink_ref[h:h + 1, 0:1]
            s = lax.dot_general(qh, kg, nt, preferred_element_type=F32)
            if latent:
                s = jnp.where(band, s, NEG_INF)
            m = jnp.maximum(jnp.max(s, axis=-1, keepdims=True), sk)
            if latent:
                sc = lax.dot_general(qh, kcg, nt, preferred_element_type=F32)
                m = jnp.maximum(m, jnp.max(sc, axis=-1, keepdims=True))
            p = jnp.exp(s - m)
            l = jnp.sum(p, axis=-1, keepdims=True) + jnp.exp(sk - m)
            o = jnp.dot(p.astype(BF16), vg, preferred_element_type=F32)
            if latent:
                pc = jnp.exp(sc - m)
                l = l + jnp.sum(pc, axis=-1, keepdims=True)
                o = o + jnp.dot(pc.astype(BF16), vcg, preferred_element_type=F32)
            o_ref[:, h * LANE:(h + 1) * LANE] = (o / l).astype(BF16)


def _gqa_attn(z, sink_b, *, latent, kc=None, vc=None, tables=None, prev=None):
    nb, n = (DEC_BATCH, DEC_SEQ) if latent else (BATCH, SEQ)
    tq = 256
    nq = n // tq
    off = T_CTX // n if latent else 0
    offq = T_CTX // tq if latent else 0
    wq, wk = H_C * HD_C, KVH_C * HD_C
    specs = [pl.BlockSpec((tq, wq), lambda b, i: (offq + b * nq + i, 0)),
             pl.BlockSpec((n, wk), lambda b, i: (off + b, ZC_K // wk)),
             pl.BlockSpec((n, wk), lambda b, i: (off + b, ZC_V // wk))]
    args = [z, z, z]
    if latent:
        specs += [pl.BlockSpec((PAST_LEN, wk), lambda b, i: (b, 0)),
                  pl.BlockSpec((PAST_LEN, wk), lambda b, i: (b, 0))]
        args += [kc, vc]
    specs.append(pl.BlockSpec((H_C, LANE), lambda b, i: (0, 0)))
    args.append(sink_b)
    if latent:
        cos, sin = tables
        specs += [pl.BlockSpec((tq, LANE), lambda b, i: (i, 0)),
                  pl.BlockSpec((tq, LANE), lambda b, i: (i, 0)),
                  pl.BlockSpec((n, LANE), lambda b, i: (0, 0)),
                  pl.BlockSpec((n, LANE), lambda b, i: (0, 0))]
        args += [cos, sin, cos, sin]
    specs, args, aliases = _rows_of(specs, args, prev)
    return pl.pallas_call(
        functools.partial(_gqa_attn_kernel, latent=latent, tq=tq),
        grid=(nb, nq),
        in_specs=specs,
        out_specs=pl.BlockSpec((tq, wq), lambda b, i: (offq + b * nq + i, 0)),
        out_shape=jax.ShapeDtypeStruct((T_ALL, wq), BF16),
        input_output_aliases=aliases,
        compiler_params=_params("parallel", "parallel"),
        name="gqa_attn_lat" if latent else "gqa_attn_ctx",
    )(*args)


def _hgrn_tables():
    c = HGRN_CHUNK
    halves = [c >> (i + 1) for i in range(c.bit_length() - 1)]
    out = []
    for forward in (True, False):
        sums = np.zeros((len(halves) + 1, c, c), np.float32)
        level = np.full((c, c), -1, np.int32)
        level[np.arange(c), np.arange(c)] = 0
        for li, m in enumerate(halves):
            for r in range(c):
                pos = r % (2 * m)
                mid = r - pos + m
                late = pos >= m
                if forward:
                    lo, hi = (mid, r + 1) if late else (r + 1, mid)
                else:
                    lo, hi = (mid, r) if late else (r, mid)
                sums[li, r, lo:hi] = 1.0
                for s in range(r - pos, r - pos + 2 * m):
                    s_late = (s % (2 * m)) >= m
                    if (late and not s_late) if forward else (not late and s_late):
                        level[r, s] = li + 1
        for r in range(c):
            if forward:
                sums[-1, r, :r + 1] = 1.0
            else:
                sums[-1, r, r:] = 1.0
        sums = sums.reshape(-1, c)
        out.append((jnp.asarray(np.concatenate([sums, sums, sums], axis=1), BF16),
                    jnp.asarray(np.concatenate([level, level], axis=1))))
    return out


def _hgrn_kernel(*refs, n, has_s0, emit_state):
    it = iter(refs)
    q_ref, xf_ref, xb_ref, v_ref, ag_ref, lb_ref, gn_ref = (next(it) for _ in range(7))
    sums_refs = (next(it), next(it))
    level_refs = (next(it), next(it))
    s0_ref = next(it) if has_s0 else None
    if has_s0:
        next(it)
    o_ref = next(it)
    sfin_ref = next(it) if emit_state else None
    o_scr, qe_scr, u_scr, e_scr, st_scr = (next(it) for _ in range(5))

    c = HGRN_CHUNK
    nc = n // c
    nlev = c.bit_length() - 1
    nt = (((1,), (1,)), ((), ()))
    tn = (((0,), (0,)), ((), ()))
    zero = jnp.zeros((c, LANE), BF16)

    def blockdiag(x):
        return jnp.concatenate([jnp.concatenate([x[:, :LANE], zero], axis=1),
                                jnp.concatenate([zero, x[:, LANE:]], axis=1)], axis=0)

    def gates(x, lb):
        e = jnp.exp(-jnp.abs(x))
        big = 1.0 / (1.0 + e)
        small = e * big
        pos = x >= 0.0
        return jnp.log(lb + (1.0 - lb) * jnp.where(pos, big, small)), (1.0 - lb) * jnp.where(pos, small, big)

    for d in range(2):
        for hh in range(2):
            st_scr[d, hh] = s0_ref[0, d, hh].T if has_s0 else jnp.zeros((DV_A, DK_A), F32)

    group = 4

    def intra(t, carry):
        jobs = [(u, d) for u in range(group) for d in range(2)]
        chunk_of = [t * group + u for u in range(group)]
        rows = {u: pl.ds(pl.multiple_of(chunk_of[u] * c, c), c) for u in range(group)}
        q = {ci: q_ref[rows[ci], :] for ci, _ in jobs}
        v = {ci: v_ref[rows[ci], :].astype(BF16) for ci, _ in jobs}
        k = {}

        dall = {}
        for ci, d in jobs:
            g, k[ci, d] = gates((xf_ref, xb_ref)[d][rows[ci], :], lb_ref[d:d + 1, :])
            g_hi = g.astype(BF16)
            rem = g - g_hi.astype(F32)
            g_mid = rem.astype(BF16)
            g_lo = (rem - g_mid.astype(F32)).astype(BF16)
            dall[ci, d] = jnp.dot(sums_refs[d][...], jnp.concatenate([g_hi, g_mid, g_lo], axis=0),
                                  preferred_element_type=F32)

        scores = {}
        for ci, d in jobs:
            rs = [lax.dot_general(q[ci].astype(BF16), blockdiag(k[ci, d].astype(BF16)), nt,
                                  preferred_element_type=F32)]
            for li in range(nlev):
                e = jnp.exp(dall[ci, d][li * c:(li + 1) * c, :])
                rs.append(lax.dot_general((q[ci] * e).astype(BF16), blockdiag((k[ci, d] * e).astype(BF16)), nt,
                                          preferred_element_type=F32))
            scores[ci, d] = rs

        for ci, d in jobs:
            level = level_refs[d][...]
            a = jnp.where(level == 0, scores[ci, d][0], 0.0)
            for li in range(nlev):
                a = jnp.where(level == li + 1, scores[ci, d][li + 1], a)
            o_scr[d, rows[ci], :] = jnp.dot(a.astype(BF16), blockdiag(v[ci]), preferred_element_type=F32)
            gc = dall[ci, d][nlev * c:, :]
            g_end = gc[c - 1:c, :] if d == 0 else gc[0:1, :]
            qe_scr[d, rows[ci], :] = (q[ci] * jnp.exp(gc)).astype(BF16)
            kd = (k[ci, d] * jnp.exp(g_end - gc)).astype(BF16)
            e_scr[d, chunk_of[ci]] = jnp.broadcast_to(jnp.exp(g_end), (8, 2 * LANE))
            for hh in range(2):
                hl = slice(hh * LANE, (hh + 1) * LANE)
                u_scr[d, chunk_of[ci], hh] = lax.dot_general(v[ci][:, hl], kd[:, hl], tn,
                                                             preferred_element_type=F32)
        return carry

    lax.fori_loop(0, nc // group, intra, 0)

    def inter(i, carry):
        for d in range(2):
            ci = i if d == 0 else nc - 1 - i
            rows = pl.ds(pl.multiple_of(ci * c, c), c)
            e = e_scr[d, ci]
            for hh in range(2):
                hl = slice(hh * LANE, (hh + 1) * LANE)
                st = st_scr[d, hh]
                o_scr[d, rows, hl] += lax.dot_general(qe_scr[d, rows, hl], st.astype(BF16), nt,
                                                      preferred_element_type=F32)
                st_scr[d, hh] = st * e[0:1, hl] + u_scr[d, ci, hh]
        return carry

    lax.fori_loop(0, nc, inter, 0, unroll=4)

    def finish(i, carry):
        rows = pl.ds(pl.multiple_of(i * c, c), c)
        o = o_scr[0, rows, :] + o_scr[1, rows, :]
        o = jnp.concatenate([_rms(o[:, :LANE]), _rms(o[:, LANE:])], axis=1)
        ag = ag_ref[rows, :]
        o_ref[rows, :] = (o * gn_ref[...] * (ag * jax.nn.sigmoid(ag))).astype(BF16)
        return carry

    lax.fori_loop(0, nc, finish, 0, unroll=4)
    if emit_state:
        for d in range(2):
            for hh in range(2):
                sfin_ref[0, d, hh] = st_scr[d, hh].T


def _hgrn(z, lb_l, gnorm, tables, *, latent, s0=None, prev=None):
    nb, n = (DEC_BATCH, DEC_SEQ) if latent else (BATCH, SEQ)
    off = T_CTX // n if latent else 0
    emit_state = not latent
    w = 2 * LANE
    pairs = H_A // 2
    c = HGRN_CHUNK
    nc = n // c

    def zspec(k):
        return pl.BlockSpec((n, w), lambda b, p: (off + b, Z_A // w + k * pairs + p))

    def const(x):
        return pl.BlockSpec(x.shape, lambda b, p: (0, 0))

    (sums_f, level_f), (sums_b, level_b) = tables
    specs = [zspec(0), zspec(1), zspec(2), zspec(3), zspec(4),
             pl.BlockSpec((2, w), lambda b, p: (0, p)),
             pl.BlockSpec((1, w), lambda b, p: (0, 0)),
             const(sums_f), const(sums_b), const(level_f), const(level_b)]
    args = [z, z, z, z, z, lb_l, jnp.tile(gnorm.reshape(1, DV_A), (1, 2)),
            sums_f, sums_b, level_f, level_b]
    if latent:
        specs.append(pl.BlockSpec((1, 2, 2, DK_A, DV_A), lambda b, p: (b, 0, p, 0, 0)))
        args.append(s0)
    specs, args, aliases = _rows_of(specs, args, prev)
    out_shape = [jax.ShapeDtypeStruct((T_ALL, H_A * DV_A), BF16)]
    out_specs = [pl.BlockSpec((n, w), lambda b, p: (off + b, p))]
    if emit_state:
        out_shape.append(jax.ShapeDtypeStruct((nb, 2, H_A, DK_A, DV_A), F32))
        out_specs.append(pl.BlockSpec((1, 2, 2, DK_A, DV_A), lambda b, p: (b, 0, p, 0, 0)))
    scratch = [pltpu.VMEM((2, n, w), F32),
               pltpu.VMEM((2, n, w), BF16),
               pltpu.VMEM((2, nc, 2, DV_A, DK_A), F32),
               pltpu.VMEM((2, nc, 8, w), F32),
               pltpu.VMEM((2, 2, DV_A, DK_A), F32)]
    return pl.pallas_call(
        functools.partial(_hgrn_kernel, n=n, has_s0=latent, emit_state=emit_state),
        grid=(nb, pairs),
        in_specs=specs, out_specs=out_specs, out_shape=out_shape,
        scratch_shapes=scratch,
        input_output_aliases=aliases,
        compiler_params=_params("parallel", "parallel"),
        name="hgrn_lat" if latent else "hgrn_ctx",
    )(*args)


def _merge_kernel(oa_ref, ob_ref, oc_ref, wa_ref, wb_ref, wc_ref, g0_ref, g1_ref, g2_ref, wo_ref,
                  x_ref, gate_ref, npost_ref, npre_ref, scale_ref, shift_ref, xnew_ref, h_ref, y_scr):
    j = pl.program_id(1)

    @pl.when(j == 0)
    def _():
        y_scr[...] = jnp.zeros_like(y_scr)

    def gate(g_ref):
        return jax.nn.sigmoid(g_ref[...].astype(F32))

    merged = (gate(g0_ref) * jnp.dot(oa_ref[...], wa_ref[...], preferred_element_type=F32)
              + gate(g1_ref) * jnp.dot(ob_ref[...], wb_ref[...], preferred_element_type=F32)
              + gate(g2_ref) * jnp.dot(oc_ref[...], wc_ref[...], preferred_element_type=F32))
    y_scr[...] += jnp.dot(merged.astype(BF16), wo_ref[...], preferred_element_type=F32)

    @pl.when(j == pl.num_programs(1) - 1)
    def _():
        x = x_ref[...] + gate_ref[0, 0] * (_rms(y_scr[...]) * npost_ref[...])
        xnew_ref[...] = x
        h_ref[...] = ((_rms(x) * npre_ref[...]) * (1.0 + scale_ref[0, 0]) + shift_ref[0, 0]).astype(BF16)


def _merge(oa, ob, oc, wa, wb, wc, zg, wo, l, x, mod_l, npost, npre):
    tm, tn = 512, 512
    nj = D_MODEL // tn
    kb = H_A * DV_A
    o_spec = pl.BlockSpec((tm, kb), lambda i, j: (i, 0))
    w_spec = pl.BlockSpec((None, kb, tn), lambda i, j: (l, 0, j))
    row = pl.BlockSpec((tm, D_MODEL), lambda i, j: (i, 0))
    vec = pl.BlockSpec((1, D_MODEL), lambda i, j: (0, 0))

    def gspec(k):
        return pl.BlockSpec((tm, tn), lambda i, j: (i, k * nj + j))

    def modspec(k):
        return pl.BlockSpec((1, 1, 1, D_MODEL), lambda i, j: (_mod_row(i, tm), k, 0, 0))

    return pl.pallas_call(
        _merge_kernel,
        grid=(T_ALL // tm, nj),
        in_specs=[o_spec, o_spec, o_spec, w_spec, w_spec, w_spec, gspec(0), gspec(1), gspec(2),
                  pl.BlockSpec((None, tn, D_MODEL), lambda i, j: (l, j, 0)),
                  row, modspec(2), vec, vec, modspec(4), modspec(3)],
        out_specs=[row, row],
        out_shape=[jax.ShapeDtypeStruct((T_ALL, D_MODEL), F32), jax.ShapeDtypeStruct((T_ALL, D_MODEL), BF16)],
        scratch_shapes=[pltpu.VMEM((tm, D_MODEL), F32)],
        compiler_params=_params("parallel", "arbitrary"),
        name="merge_out",
    )(oa, ob, oc, wa, wb, wc, zg, zg, zg, wo, x, mod_l, npost.reshape(1, D_MODEL), npre.reshape(1, D_MODEL),
      mod_l, mod_l)


def _ffn_kernel(h_ref, wa_ref, wg_ref, ca_ref, cg_ref, wd_ref, y_ref, *, tm):
    i = pl.program_id(0)

    @pl.when(pl.program_id(1) == 0)
    def _():
        y_ref[...] = jnp.zeros_like(y_ref)

    h = h_ref[...]
    seq_len = jnp.where(i * tm < T_CTX, SEQ, DEC_SEQ)
    pos = lax.broadcasted_iota(jnp.int32, (tm, 1), 0) & (seq_len - 1)
    has_prev = pos != 0
    has_next = pos != seq_len - 1

    def conv(u, c):
        prev = jnp.where(has_prev, pltpu.roll(u, 1, 0), 0.0)
        nxt = jnp.where(has_next, pltpu.roll(u, tm - 1, 0), 0.0)
        return c[0:1, :] * prev + c[1:2, :] * u + c[2:3, :] * nxt

    tf = wa_ref.shape[1]
    subs = (slice(0, tf // 2), slice(tf // 2, tf))
    ups = [(jnp.dot(h, wa_ref[:, cols], preferred_element_type=F32),
            jnp.dot(h, wg_ref[:, cols], preferred_element_type=F32)) for cols in subs]
    for cols, (ua, ug) in zip(subs, ups):
        act = (conv(ua, ca_ref[:, cols]) * jax.nn.gelu(conv(ug, cg_ref[:, cols]))).astype(BF16)
        y_ref[...] += jnp.dot(act, wd_ref[cols, :], preferred_element_type=F32)


def _ffn(h, w_up, conv, w_down, l):
    tm, tf = 1024, 512
    nj = D_FF_PAD // tf
    return pl.pallas_call(
        functools.partial(_ffn_kernel, tm=tm),
        grid=(T_ALL // tm, nj),
        in_specs=[pl.BlockSpec((tm, D_MODEL), lambda i, j: (i, 0)),
                  pl.BlockSpec((None, D_MODEL, tf), lambda i, j: (l, 0, j)),
                  pl.BlockSpec((None, D_MODEL, tf), lambda i, j: (l, 0, nj + j)),
                  pl.BlockSpec((None, CONV_W, tf), lambda i, j: (l, 0, j)),
                  pl.BlockSpec((None, CONV_W, tf), lambda i, j: (l, 0, nj + j)),
                  pl.BlockSpec((None, tf, D_MODEL), lambda i, j: (l, j, 0))],
        out_specs=pl.BlockSpec((tm, D_MODEL), lambda i, j: (i, 0)),
        out_shape=jax.ShapeDtypeStruct((T_ALL, D_MODEL), F32),
        compiler_params=_params("parallel", "arbitrary"),
        name="conv_ffn",
    )(h, w_up, w_up, conv, conv, w_down)


def _pad_cols(w, n):
    return jnp.pad(w, [(0, 0)] * (w.ndim - 1) + [(0, n - w.shape[-1])])


def _prep_weights(w_in, mla_w_uq, mla_w_ukv, w_branch_a, w_branch_b, w_branch_c, w_out,
                  ffn_w_up, ffn_conv, ffn_w_down):
    w_uq = _pad_cols(mla_w_uq.reshape(DEPTH, Q_LORA, H_B, NOPE_B + ROPE_B), 2 * LANE)
    w_uq = w_uq.reshape(DEPTH, Q_LORA, H_B * 2 * LANE).astype(BF16)
    w_up = jnp.concatenate([_pad_cols(ffn_w_up[:, :, :D_FF], D_FF_PAD),
                            _pad_cols(ffn_w_up[:, :, D_FF:], D_FF_PAD)], axis=-1).astype(BF16)
    conv = jnp.concatenate([_pad_cols(ffn_conv[:, :, :D_FF], D_FF_PAD),
                            _pad_cols(ffn_conv[:, :, D_FF:], D_FF_PAD)], axis=-1)
    w_down = jnp.pad(ffn_w_down, ((0, 0), (0, D_FF_PAD - D_FF), (0, 0))).astype(BF16)
    return dict(w_in_t=jnp.swapaxes(w_in, 1, 2), w_uq=w_uq, w_ukv=mla_w_ukv.astype(BF16),
                w_a=w_branch_a.astype(BF16), w_b=w_branch_b.astype(BF16), w_c=w_branch_c.astype(BF16),
                w_o=w_out.astype(BF16), w_up=w_up, conv=conv, w_down=w_down)


def kernel(x_prompt, x_sample, state_hgrn, cache_mla_ckv, cache_mla_krope, cache_swa_k, cache_swa_v,
           c, c_ctx, w_mod, b_mod, norm_pre_attn, norm_post_attn, norm_pre_ffn, norm_post_ffn,
           w_in, hgrn_lb, hgrn_gnorm, mla_gq, mla_w_uq, mla_gkv, mla_w_ukv, swa_sink,
           w_branch_a, w_branch_b, w_branch_c, w_out, ffn_w_up, ffn_conv, ffn_w_down):
    wts = _prep_weights(w_in, mla_w_uq, mla_w_ukv, w_branch_a, w_branch_b, w_branch_c, w_out,
                        ffn_w_up, ffn_conv, ffn_w_down)
    cs = jnp.cumsum(jax.nn.softmax(hgrn_lb.astype(F32), axis=0), axis=0)
    lb_all = cs - cs[0]

    cvec = jnp.concatenate([c_ctx[None, :], c, jnp.zeros((MOD_ROWS - 1 - DEC_BATCH, D_MODEL), F32)], axis=0)
    mod = _modulation(cvec, w_mod, b_mod).reshape(DEPTH, MOD_ROWS, 6, 1, D_MODEL)

    hgrn_tables = _hgrn_tables()
    rope_b = _rope_tables(ROPE_B)
    rope_c = _rope_tables(HD_C)
    sink_b = jnp.broadcast_to(swa_sink[:, :, None], (DEPTH, H_C, LANE))

    x = jnp.concatenate([x_prompt.reshape(T_CTX, D_MODEL), x_sample.reshape(T_LAT, D_MODEL)], axis=0)
    new_hgrn, new_ckv, new_krope, new_k, new_v = [], [], [], [], []
    y = None
    for l in range(DEPTH):
        mod_l = mod[l]
        if l == 0:
            (h,) = _norm(x, npre=norm_pre_attn[l], mod_pre=mod_l, scale_idx=1, shift_idx=0)
        else:
            x, h = _norm(x, y=y, mod_post=mod[l - 1], gate_idx=5, npost=norm_post_ffn[l - 1],
                         npre=norm_pre_attn[l], mod_pre=mod_l, scale_idx=1, shift_idx=0)
        zab = _in_proj(h, wts["w_in_t"], l, 0, ZAB_W, "in_proj_ab")
        zcg = _in_proj(h, wts["w_in_t"], l, Z_B_END, ZC_G, "in_proj_c")
        zg = _in_proj(h, wts["w_in_t"], l, Z_B_END + ZC_G, N_BRANCH * D_MODEL, "in_proj_g", BF16)

        oa, s_ctx = _hgrn(zab, lb_all[l], hgrn_gnorm[l], hgrn_tables, latent=False)
        (oa,) = _hgrn(zab, lb_all[l], hgrn_gnorm[l], hgrn_tables, latent=True, s0=state_hgrn[:, l], prev=oa)
        new_hgrn.append(s_ctx)

        (qb,) = _norm_mm(zab, Z_B // Q_LORA, Q_LORA, mla_gq[l], wts["w_uq"], l, F32, emit_normed=False,
                         name="mla_q_proj")
        ckv, kvb = _norm_mm(zab, Z_KV // KV_LORA, KV_LORA, mla_gkv[l], wts["w_ukv"], l, BF16, emit_normed=True,
                            name="mla_kv_proj")
        kv_cache = _mm(cache_mla_ckv[:, l].reshape(DEC_BATCH * PAST_LEN, KV_LORA), wts["w_ukv"], l, BF16,
                       "mla_kv_cache")
        kr_cache = _pad_cols(cache_mla_krope[:, l].reshape(DEC_BATCH * PAST_LEN, ROPE_B), LANE)
        ob = _mla_attn(qb, kvb, zab, latent=False)
        ob = _mla_attn(qb, kvb, zab, latent=True, kvc=kv_cache, krc=kr_cache, tables=rope_b, prev=ob)
        new_ckv.append(ckv[:T_CTX].reshape(BATCH, SEQ, KV_LORA))
        new_krope.append(zab[:T_CTX, Z_KR:Z_B_END].reshape(BATCH, SEQ, ROPE_B))

        oc = _gqa_attn(zcg, sink_b[l], latent=False)
        oc = _gqa_attn(zcg, sink_b[l], latent=True,
                       kc=cache_swa_k[:, l].reshape(DEC_BATCH * PAST_LEN, KVH_C * HD_C),
                       vc=cache_swa_v[:, l].reshape(DEC_BATCH * PAST_LEN, KVH_C * HD_C), tables=rope_c, prev=oc)
        new_k.append(zcg[:T_CTX, ZC_K:ZC_V].reshape(BATCH, SEQ, KVH_C, HD_C))
        new_v.append(zcg[:T_CTX, ZC_V:ZC_G].reshape(BATCH, SEQ, KVH_C, HD_C))

        x, h = _merge(oa, ob, oc, wts["w_a"], wts["w_b"], wts["w_c"], zg, wts["w_o"], l,
                      x, mod_l, norm_post_attn[l], norm_pre_ffn[l])
        y = _ffn(h, wts["w_up"], wts["conv"], wts["w_down"], l)

    (x,) = _norm(x, y=y, mod_post=mod[DEPTH - 1], gate_idx=5, npost=norm_post_ffn[DEPTH - 1])
    return (x[:T_CTX].reshape(BATCH, SEQ, D_MODEL), x[T_CTX:].reshape(DEC_BATCH, DEC_SEQ, D_MODEL),
            jnp.stack(new_hgrn, axis=1), jnp.stack(new_ckv, axis=1), jnp.stack(new_krope, axis=1),
            jnp.stack(new_k, axis=1), jnp.stack(new_v, axis=1))
```

```python
import functools

import jax
import jax.numpy as jnp
import numpy as np
from jax import lax
from jax.experimental import pallas as pl
from jax.experimental.pallas import tpu as pltpu

F32 = jnp.float32
BF16 = jnp.bfloat16

D_MODEL = 2048
BATCH = 16
SEQ = 256
DEPTH = 4
DEC_BATCH = 4
DEC_SEQ = 1024
PAST_LEN = 256
GRID_W = 64
ROPE_BASE = 10000.0
EPS = 1e-6
NEG_INF = -1e30
H_A, DK_A, DV_A = 8, 128, 128
H_B, Q_LORA, KV_LORA, NOPE_B, ROPE_B, V_B = 8, 512, 256, 128, 64, 128
H_C, KVH_C, HD_C, WINDOW = 8, 2, 128, 128
N_BRANCH = 3
D_FF = 5504
CONV_W = 3

T_CTX = BATCH * SEQ
T_LAT = DEC_BATCH * DEC_SEQ
T_ALL = T_CTX + T_LAT
MOD_ROWS = 8
LANE = 128
FFN_TF = 512
FFN_TILES = -(-D_FF // FFN_TF)
HGRN_CHUNK = 64
Z_A = 0
Z_B = 5 * H_A * DK_A
Z_KV = Z_B + Q_LORA
Z_KR = Z_KV + KV_LORA
Z_B_END = Z_KR + ROPE_B
ZAB_W = 6144
ZC_K = H_C * HD_C
ZC_V = ZC_K + KVH_C * HD_C
ZC_G = ZC_V + KVH_C * HD_C
ZCG_W = ZC_G + N_BRANCH * D_MODEL
VMEM_LIMIT = 56 * 1024 * 1024


def _params(*sem, flags=None):
    return pltpu.CompilerParams(dimension_semantics=sem, vmem_limit_bytes=VMEM_LIMIT, flags=flags)


def _mod_row(i, tm):
    return jnp.where(i * tm < T_CTX, 0, 1 + (i * tm - T_CTX) // DEC_SEQ)


def _rms(x):
    return x * lax.rsqrt(jnp.mean(x * x, axis=-1, keepdims=True) + EPS)


def _mod_kernel(c_ref, w_ref, b_ref, o_ref):
    cv = c_ref[...]
    s = (cv * jax.nn.sigmoid(cv)).astype(BF16)
    o_ref[0] = jnp.dot(s, w_ref[0].astype(BF16), preferred_element_type=F32) + b_ref[0]


def _modulation(cvec, w_mod, b_mod):
    tn = 1024
    n = 6 * D_MODEL
    return pl.pallas_call(
        _mod_kernel,
        grid=(DEPTH, n // tn),
        in_specs=[pl.BlockSpec((MOD_ROWS, D_MODEL), lambda l, j: (0, 0)),
                  pl.BlockSpec((1, D_MODEL, tn), lambda l, j: (l, 0, j)),
                  pl.BlockSpec((1, 1, tn), lambda l, j: (l, 0, j))],
        out_specs=pl.BlockSpec((1, MOD_ROWS, tn), lambda l, j: (l, 0, j)),
        out_shape=jax.ShapeDtypeStruct((DEPTH, MOD_ROWS, n), F32),
        compiler_params=_params("parallel", "parallel"),
        name="modulation",
    )(cvec, w_mod, b_mod.reshape(DEPTH, 1, n))


def _norm_kernel(*refs, has_y, has_h):
    it = iter(refs)
    x_ref = next(it)
    if has_y:
        y_ref, gate_ref, npost_ref = next(it), next(it), next(it)
    if has_h:
        npre_ref, scale_ref, shift_ref = next(it), next(it), next(it)
    x = x_ref[...]
    if has_y:
        xnew_ref = next(it)
        x = x + gate_ref[0, 0] * (_rms(y_ref[...]) * npost_ref[...])
        xnew_ref[...] = x
    if has_h:
        h_ref = next(it)
        h = (_rms(x) * npre_ref[...]) * (1.0 + scale_ref[0, 0]) + shift_ref[0, 0]
        h_ref[...] = h.astype(BF16)


def _norm(x, *, y=None, mod_post=None, gate_idx=None, npost=None,
          npre=None, mod_pre=None, scale_idx=None, shift_idx=None):
    tm = 512
    has_y, has_h = y is not None, npre is not None
    row = pl.BlockSpec((tm, D_MODEL), lambda i: (i, 0))
    vec = pl.BlockSpec((1, D_MODEL), lambda i: (0, 0))

    def modspec(k):
        return pl.BlockSpec((1, 1, 1, D_MODEL), lambda i: (_mod_row(i, tm), k, 0, 0))

    args, specs, out_shape, out_specs = [x], [row], [], []
    if has_y:
        args += [y, mod_post, npost.reshape(1, D_MODEL)]
        specs += [row, modspec(gate_idx), vec]
        out_shape.append(jax.ShapeDtypeStruct((T_ALL, D_MODEL), F32))
        out_specs.append(row)
    if has_h:
        args += [npre.reshape(1, D_MODEL), mod_pre, mod_pre]
        specs += [vec, modspec(scale_idx), modspec(shift_idx)]
        out_shape.append(jax.ShapeDtypeStruct((T_ALL, D_MODEL), BF16))
        out_specs.append(row)
    outs = pl.pallas_call(
        functools.partial(_norm_kernel, has_y=has_y, has_h=has_h),
        grid=(T_ALL // tm,),
        in_specs=specs, out_specs=out_specs, out_shape=out_shape,
        compiler_params=_params("parallel"),
        name="norm_y%d_h%d" % (has_y, has_h),
    )(*args)
    return outs


def _mm_kernel(x_ref, w_ref, o_ref):
    o_ref[...] = jnp.dot(x_ref[...].astype(BF16), w_ref[...].astype(BF16),
                         preferred_element_type=F32).astype(o_ref.dtype)


def _mm(x, w, l, out_dtype, name, n=None):
    m, k = x.shape
    n = w.shape[2] if n is None else n
    tm = min(m, 1024)
    tn = min(n, 512)
    return pl.pallas_call(
        _mm_kernel,
        grid=(m // tm, n // tn),
        in_specs=[pl.BlockSpec((tm, k), lambda i, j: (i, 0)),
                  pl.BlockSpec((None, k, tn), lambda i, j: (l, 0, j))],
        out_specs=pl.BlockSpec((tm, tn), lambda i, j: (i, j)),
        out_shape=jax.ShapeDtypeStruct((m, n), out_dtype),
        compiler_params=_params("parallel", "parallel"),
        name=name,
    )(x, w)


def _mm_nt_kernel(x_ref, wt_ref, o_ref):
    o_ref[...] = lax.dot_general(x_ref[...], wt_ref[...].astype(BF16), (((1,), (1,)), ((), ())),
                                 preferred_element_type=F32).astype(o_ref.dtype)


def _in_proj(h, w_t, l, col0, n, name, out_dtype=F32):
    m, k = h.shape
    tm, tn = 2048, 512
    row0 = l * w_t.shape[1] + col0
    return pl.pallas_call(
        _mm_nt_kernel,
        grid=(m // tm, n // tn),
        in_specs=[pl.BlockSpec((tm, k), lambda i, j: (i, 0)),
                  pl.BlockSpec((pl.Element(tn), pl.Element(k)),
                               lambda i, j: ((row0 // 8 + j * (tn // 8)) * 8, 0))],
        out_specs=pl.BlockSpec((tm, tn), lambda i, j: (i, j)),
        out_shape=jax.ShapeDtypeStruct((m, n), out_dtype),
        compiler_params=_params("parallel", "parallel"),
        name=name,
    )(h, w_t.reshape(-1, k))


def _norm_mm_kernel(x_ref, g_ref, w_ref, *out_refs, emit_normed):
    xn = _rms(x_ref[...]) * g_ref[...]
    if emit_normed:
        out_refs[0][...] = xn
    out_refs[-1][...] = jnp.dot(xn.astype(BF16), w_ref[...],
                                preferred_element_type=F32).astype(out_refs[-1].dtype)


def _norm_mm(z, col_block, k, gain, w, l, out_dtype, *, emit_normed, name):
    m = z.shape[0]
    n = w.shape[2]
    tm = 512
    out_shape = [jax.ShapeDtypeStruct((m, n), out_dtype)]
    out_specs = [pl.BlockSpec((tm, n), lambda i: (i, 0))]
    if emit_normed:
        out_shape.insert(0, jax.ShapeDtypeStruct((m, k), F32))
        out_specs.insert(0, pl.BlockSpec((tm, k), lambda i: (i, 0)))
    return pl.pallas_call(
        functools.partial(_norm_mm_kernel, emit_normed=emit_normed),
        grid=(m // tm,),
        in_specs=[pl.BlockSpec((tm, k), lambda i: (i, col_block)),
                  pl.BlockSpec((1, k), lambda i: (0, 0)),
                  pl.BlockSpec((None, k, n), lambda i: (l, 0, 0))],
        out_specs=out_specs, out_shape=out_shape,
        compiler_params=_params("parallel"),
        name=name,
    )(z, gain.reshape(1, k), w)


def _rope_tables(head_dim):
    n = DEC_SEQ
    half = head_dim // 2
    quarter = half // 2
    row = jnp.repeat(jnp.arange(n // GRID_W), GRID_W).astype(F32)
    col = jnp.tile(jnp.arange(GRID_W), n // GRID_W).astype(F32)
    inv = ROPE_BASE ** (-jnp.arange(0, half, 2, dtype=F32) / half)
    lane = jnp.arange(LANE)
    m = lane % half
    pos = jnp.where((lane // half)[None, :] == 0, row[:, None], col[:, None])
    ang = pos * inv[m % quarter][None, :]
    valid = (lane < head_dim)[None, :]
    cos = jnp.where(valid, jnp.cos(ang), 0.0)
    sin = jnp.where(valid, jnp.where(m < quarter, -1.0, 1.0)[None, :] * jnp.sin(ang), 0.0)
    return cos.astype(F32), sin.astype(F32)


def _rope(x, cos, sin, head_dim):
    quarter = head_dim // 4
    lane = lax.broadcasted_iota(jnp.int32, x.shape, 1)
    first = (lane % (2 * quarter)) < quarter
    partner = jnp.where(first, pltpu.roll(x, LANE - quarter, 1), pltpu.roll(x, quarter, 1))
    return x * cos + partner * sin


def _mla_attn_kernel(*refs, latent):
    if latent:
        q_ref, kv_ref, kr_ref, kvc_ref, krc_ref, cq_ref, sq_ref, ck_ref, sk_ref, _, o_ref = refs
    else:
        q_ref, kv_ref, kr_ref, o_ref = refs
    scale = (NOPE_B + ROPE_B) ** -0.5
    nt = (((1,), (1,)), ((), ()))
    kr = kr_ref[...]
    if latent:
        kr = _rope(kr, ck_ref[...], sk_ref[...], ROPE_B)
        krc = krc_ref[...].astype(BF16)
    kr = kr.astype(BF16)
    for h in range(H_B):
        c0 = h * 2 * LANE
        qn = q_ref[:, c0:c0 + LANE]
        qr = q_ref[:, c0 + LANE:c0 + 2 * LANE]
        if latent:
            qr = _rope(qr, cq_ref[...], sq_ref[...], ROPE_B)
        qh = jnp.concatenate([(qn * scale).astype(BF16), (qr * scale).astype(BF16)], axis=-1)
        kh = jnp.concatenate([kv_ref[:, c0:c0 + LANE], kr], axis=-1)
        vh = kv_ref[:, c0 + LANE:c0 + 2 * LANE]
        s = lax.dot_general(qh, kh, nt, preferred_element_type=F32)
        m = jnp.max(s, axis=-1, keepdims=True)
        if latent:
            khc = jnp.concatenate([kvc_ref[:, c0:c0 + LANE], krc], axis=-1)
            vhc = kvc_ref[:, c0 + LANE:c0 + 2 * LANE]
            sc = lax.dot_general(qh, khc, nt, preferred_element_type=F32)
            m = jnp.maximum(m, jnp.max(sc, axis=-1, keepdims=True))
        p = jnp.exp(s - m)
        l = jnp.sum(p, axis=-1, keepdims=True)
        o = jnp.dot(p.astype(BF16), vh, preferred_element_type=F32)
        if latent:
            pc = jnp.exp(sc - m)
            l = l + jnp.sum(pc, axis=-1, keepdims=True)
            o = o + jnp.dot(pc.astype(BF16), vhc, preferred_element_type=F32)
        o_ref[:, h * LANE:(h + 1) * LANE] = (o / l).astype(BF16)


def _rows_of(specs, args, prev):
    if prev is None:
        return specs, args, {}
    return specs + [pl.BlockSpec(memory_space=pl.ANY)], args + [prev], {len(args): 0}


def _mla_attn(q, kv, z, *, latent, kvc=None, krc=None, tables=None, prev=None):
    nb, n = (DEC_BATCH, DEC_SEQ) if latent else (BATCH, SEQ)
    tq = 256
    nq = n // tq
    off = T_CTX // n if latent else 0
    offq = T_CTX // tq if latent else 0
    w = H_B * 2 * LANE
    specs = [pl.BlockSpec((tq, w), lambda b, i: (offq + b * nq + i, 0)),
             pl.BlockSpec((n, w), lambda b, i: (off + b, 0)),
             pl.BlockSpec((n, LANE), lambda b, i: (off + b, Z_KR // LANE))]
    args = [q, kv, z]
    if latent:
        cos, sin = tables
        specs += [pl.BlockSpec((PAST_LEN, w), lambda b, i: (b, 0)),
                  pl.BlockSpec((PAST_LEN, LANE), lambda b, i: (b, 0)),
                  pl.BlockSpec((tq, LANE), lambda b, i: (i, 0)),
                  pl.BlockSpec((tq, LANE), lambda b, i: (i, 0)),
                  pl.BlockSpec((n, LANE), lambda b, i: (0, 0)),
                  pl.BlockSpec((n, LANE), lambda b, i: (0, 0))]
        args += [kvc, krc, cos, sin, cos, sin]
    specs, args, aliases = _rows_of(specs, args, prev)
    return pl.pallas_call(
        functools.partial(_mla_attn_kernel, latent=latent),
        grid=(nb, nq),
        in_specs=specs,
        out_specs=pl.BlockSpec((tq, H_B * V_B), lambda b, i: (offq + b * nq + i, 0)),
        out_shape=jax.ShapeDtypeStruct((T_ALL, H_B * V_B), BF16),
        input_output_aliases=aliases,
        compiler_params=_params("parallel", "parallel"),
        name="mla_attn_lat" if latent else "mla_attn_ctx",
    )(*args)


def _gqa_attn_kernel(*refs, latent, tq):
    if latent:
        q_ref, k_ref, v_ref, kc_ref, vc_ref, sink_ref, cq_ref, sq_ref, ck_ref, sk_ref, _, o_ref = refs
    else:
        q_ref, k_ref, v_ref, sink_ref, o_ref = refs
    scale = HD_C ** -0.5
    nt = (((1,), (1,)), ((), ()))
    rep = H_C // KVH_C
    n = k_ref.shape[0]
    if latent:
        kw = tq + 2 * WINDOW
        q0 = pl.program_id(1) * tq
        k0 = pl.multiple_of(jnp.clip(q0 - WINDOW, 0, n - kw), WINDOW)
        keys = pl.ds(k0, kw)
        qpos = q0 + lax.broadcasted_iota(jnp.int32, (tq, kw), 0)
        kpos = k0 + lax.broadcasted_iota(jnp.int32, (tq, kw), 1)
        band = jnp.abs(qpos - kpos) <= WINDOW
    else:
        keys = slice(None)
    for g in range(KVH_C):
        kg = k_ref[keys, g * LANE:(g + 1) * LANE]
        if latent:
            kg = _rope(kg, ck_ref[keys, :], sk_ref[keys, :], HD_C)
            kcg = kc_ref[:, g * LANE:(g + 1) * LANE].astype(BF16)
            vcg = vc_ref[:, g * LANE:(g + 1) * LANE].astype(BF16)
        kg = kg.astype(BF16)
        vg = v_ref[keys, g * LANE:(g + 1) * LANE].astype(BF16)
        for r in range(rep):
            h = g * rep + r
            qh = q_ref[:, h * LANE:(h + 1) * LANE]
            if latent:
                qh = _rope(qh, cq_ref[...], sq_ref[...], HD_C)
            qh = (qh * scale).astype(BF16)
            sk = sink_ref[h:h + 1, 0:1]
            s = lax.dot_general(qh, kg, nt, preferred_element_type=F32)
            if latent:
                s = jnp.where(band, s, NEG_INF)
            m = jnp.maximum(jnp.max(s, axis=-1, keepdims=True), sk)
            if latent:
                sc = lax.dot_general(qh, kcg, nt, preferred_element_type=F32)
                m = jnp.maximum(m, jnp.max(sc, axis=-1, keepdims=True))
            p = jnp.exp(s - m)
            l = jnp.sum(p, axis=-1, keepdims=True) + jnp.exp(sk - m)
            o = jnp.dot(p.astype(BF16), vg, preferred_element_type=F32)
            if latent:
                pc = jnp.exp(sc - m)
                l = l + jnp.sum(pc, axis=-1, keepdims=True)
                o = o + jnp.dot(pc.astype(BF16), vcg, preferred_element_type=F32)
            o_ref[:, h * LANE:(h + 1) * LANE] = (o / l).astype(BF16)


def _gqa_attn(z, sink_b, *, latent, kc=None, vc=None, tables=None, prev=None):
    nb, n = (DEC_BATCH, DEC_SEQ) if latent else (BATCH, SEQ)
    tq = 256
    nq = n // tq
    off = T_CTX // n if latent else 0
    offq = T_CTX // tq if latent else 0
    wq, wk = H_C * HD_C, KVH_C * HD_C
    specs = [pl.BlockSpec((tq, wq), lambda b, i: (offq + b * nq + i, 0)),
             pl.BlockSpec((n, wk), lambda b, i: (off + b, ZC_K // wk)),
             pl.BlockSpec((n, wk), lambda b, i: (off + b, ZC_V // wk))]
    args = [z, z, z]
    if latent:
        specs += [pl.BlockSpec((PAST_LEN, wk), lambda b, i: (b, 0)),
                  pl.BlockSpec((PAST_LEN, wk), lambda b, i: (b, 0))]
        args += [kc, vc]
    specs.append(pl.BlockSpec((H_C, LANE), lambda b, i: (0, 0)))
    args.append(sink_b)
    if latent:
        cos, sin = tables
        specs += [pl.BlockSpec((tq, LANE), lambda b, i: (i, 0)),
                  pl.BlockSpec((tq, LANE), lambda b, i: (i, 0)),
                  pl.BlockSpec((n, LANE), lambda b, i: (0, 0)),
                  pl.BlockSpec((n, LANE), lambda b, i: (0, 0))]
        args += [cos, sin, cos, sin]
    specs, args, aliases = _rows_of(specs, args, prev)
    return pl.pallas_call(
        functools.partial(_gqa_attn_kernel, latent=latent, tq=tq),
        grid=(nb, nq),
        in_specs=specs,
        out_specs=pl.BlockSpec((tq, wq), lambda b, i: (offq + b * nq + i, 0)),
        out_shape=jax.ShapeDtypeStruct((T_ALL, wq), BF16),
        input_output_aliases=aliases,
        compiler_params=_params("parallel", "parallel"),
        name="gqa_attn_lat" if latent else "gqa_attn_ctx",
    )(*args)


def _hgrn_tables():
    c = HGRN_CHUNK
    halves = [c >> (i + 1) for i in range(c.bit_length() - 1)]
    out = []
    for forward in (True, False):
        sums = np.zeros((len(halves) + 1, c, c), np.float32)
        level = np.full((c, c), -1, np.int32)
        level[np.arange(c), np.arange(c)] = 0
        for li, m in enumerate(halves):
            for r in range(c):
                pos = r % (2 * m)
                mid = r - pos + m
                late = pos >= m
                if forward:
                    lo, hi = (mid, r + 1) if late else (r + 1, mid)
                else:
                    lo, hi = (mid, r) if late else (r, mid)
                sums[li, r, lo:hi] = 1.0
                for s in range(r - pos, r - pos + 2 * m):
                    s_late = (s % (2 * m)) >= m
                    if (late and not s_late) if forward else (not late and s_late):
                        level[r, s] = li + 1
        for r in range(c):
            if forward:
                sums[-1, r, :r + 1] = 1.0
            else:
                sums[-1, r, r:] = 1.0
        sums = sums.reshape(-1, c)
        out.append((jnp.asarray(np.concatenate([sums, sums, sums], axis=1), BF16),
                    jnp.asarray(np.concatenate([level, level], axis=1))))
    return out


def _hgrn_kernel(*refs, n, has_s0, emit_state):
    it = iter(refs)
    q_ref, xf_ref, xb_ref, v_ref, ag_ref, lb_ref, gn_ref = (next(it) for _ in range(7))
    sums_refs = (next(it), next(it))
    level_refs = (next(it), next(it))
    s0_ref = next(it) if has_s0 else None
    if has_s0:
        next(it)
    o_ref = next(it)
    sfin_ref = next(it) if emit_state else None
    o_scr, qe_scr, u_scr, e_scr, st_scr = (next(it) for _ in range(5))

    c = HGRN_CHUNK
    nc = n // c
    nlev = c.bit_length() - 1
    nt = (((1,), (1,)), ((), ()))
    tn = (((0,), (0,)), ((), ()))
    zero = jnp.zeros((c, LANE), BF16)

    def blockdiag(x):
        return jnp.concatenate([jnp.concatenate([x[:, :LANE], zero], axis=1),
                                jnp.concatenate([zero, x[:, LANE:]], axis=1)], axis=0)

    def gates(x, lb):
        e = jnp.exp(-jnp.abs(x))
        big = 1.0 / (1.0 + e)
        small = e * big
        pos = x >= 0.0
        return jnp.log(lb + (1.0 - lb) * jnp.where(pos, big, small)), (1.0 - lb) * jnp.where(pos, small, big)

    for d in range(2):
        for hh in range(2):
            st_scr[d, hh] = s0_ref[0, d, hh].T if has_s0 else jnp.zeros((DV_A, DK_A), F32)

    group = 4

    def intra(t, carry):
        jobs = [(u, d) for u in range(group) for d in range(2)]
        chunk_of = [t * group + u for u in range(group)]
        rows = {u: pl.ds(pl.multiple_of(chunk_of[u] * c, c), c) for u in range(group)}
        q = {ci: q_ref[rows[ci], :] for ci, _ in jobs}
        v = {ci: v_ref[rows[ci], :].astype(BF16) for ci, _ in jobs}
        k = {}

        dall = {}
        for ci, d in jobs:
            g, k[ci, d] = gates((xf_ref, xb_ref)[d][rows[ci], :], lb_ref[d:d + 1, :])
            g_hi = g.astype(BF16)
            rem = g - g_hi.astype(F32)
            g_mid = rem.astype(BF16)
            g_lo = (rem - g_mid.astype(F32)).astype(BF16)
            dall[ci, d] = jnp.dot(sums_refs[d][...], jnp.concatenate([g_hi, g_mid, g_lo], axis=0),
                                  preferred_element_type=F32)

        scores = {}
        for ci, d in jobs:
            rs = [lax.dot_general(q[ci].astype(BF16), blockdiag(k[ci, d].astype(BF16)), nt,
                                  preferred_element_type=F32)]
            for li in range(nlev):
                e = jnp.exp(dall[ci, d][li * c:(li + 1) * c, :])
                rs.append(lax.dot_general((q[ci] * e).astype(BF16), blockdiag((k[ci, d] * e).astype(BF16)), nt,
                                          preferred_element_type=F32))
            scores[ci, d] = rs

        for ci, d in jobs:
            level = level_refs[d][...]
            a = jnp.where(level == 0, scores[ci, d][0], 0.0)
            for li in range(nlev):
                a = jnp.where(level == li + 1, scores[ci, d][li + 1], a)
            o_scr[d, rows[ci], :] = jnp.dot(a.astype(BF16), blockdiag(v[ci]), preferred_element_type=F32)
            gc = dall[ci, d][nlev * c:, :]
            g_end = gc[c - 1:c, :] if d == 0 else gc[0:1, :]
            qe_scr[d, rows[ci], :] = (q[ci] * jnp.exp(gc)).astype(BF16)
            kd = (k[ci, d] * jnp.exp(g_end - gc)).astype(BF16)
            e_scr[d, chunk_of[ci]] = jnp.broadcast_to(jnp.exp(g_end), (8, 2 * LANE))
            for hh in range(2):
                hl = slice(hh * LANE, (hh + 1) * LANE)
                u_scr[d, chunk_of[ci], hh] = lax.dot_general(v[ci][:, hl], kd[:, hl], tn,
                                                             preferred_element_type=F32)
        return carry

    lax.fori_loop(0, nc // group, intra, 0)

    def inter(i, carry):
        for d in range(2):
            ci = i if d == 0 else nc - 1 - i
            rows = pl.ds(pl.multiple_of(ci * c, c), c)
            e = e_scr[d, ci]
            for hh in range(2):
                hl = slice(hh * LANE, (hh + 1) * LANE)
                st = st_scr[d, hh]
                o_scr[d, rows, hl] += lax.dot_general(qe_scr[d, rows, hl], st.astype(BF16), nt,
                                                      preferred_element_type=F32)
                st_scr[d, hh] = st * e[0:1, hl] + u_scr[d, ci, hh]
        return carry

    lax.fori_loop(0, nc, inter, 0, unroll=4)

    def finish(i, carry):
        rows = pl.ds(pl.multiple_of(i * c, c), c)
        o = o_scr[0, rows, :] + o_scr[1, rows, :]
        o = jnp.concatenate([_rms(o[:, :LANE]), _rms(o[:, LANE:])], axis=1)
        ag = ag_ref[rows, :]
        o_ref[rows, :] = (o * gn_ref[...] * (ag * jax.nn.sigmoid(ag))).astype(BF16)
        return carry

    lax.fori_loop(0, nc, finish, 0, unroll=4)
    if emit_state:
        for d in range(2):
            for hh in range(2):
                sfin_ref[0, d, hh] = st_scr[d, hh].T


def _hgrn(z, lb_l, gnorm, tables, *, latent, s0=None, prev=None):
    nb, n = (DEC_BATCH, DEC_SEQ) if latent else (BATCH, SEQ)
    off = T_CTX // n if latent else 0
    emit_state = not latent
    w = 2 * LANE
    pairs = H_A // 2
    c = HGRN_CHUNK
    nc = n // c

    def zspec(k):
        return pl.BlockSpec((n, w), lambda b, p: (off + b, Z_A // w + k * pairs + p))

    def const(x):
        return pl.BlockSpec(x.shape, lambda b, p: (0, 0))

    (sums_f, level_f), (sums_b, level_b) = tables
    specs = [zspec(0), zspec(1), zspec(2), zspec(3), zspec(4),
             pl.BlockSpec((2, w), lambda b, p: (0, p)),
             pl.BlockSpec((1, w), lambda b, p: (0, 0)),
             const(sums_f), const(sums_b), const(level_f), const(level_b)]
    args = [z, z, z, z, z, lb_l, jnp.tile(gnorm.reshape(1, DV_A), (1, 2)),
            sums_f, sums_b, level_f, level_b]
    if latent:
        specs.append(pl.BlockSpec((1, 2, 2, DK_A, DV_A), lambda b, p: (b, 0, p, 0, 0)))
        args.append(s0)
    specs, args, aliases = _rows_of(specs, args, prev)
    out_shape = [jax.ShapeDtypeStruct((T_ALL, H_A * DV_A), BF16)]
    out_specs = [pl.BlockSpec((n, w), lambda b, p: (off + b, p))]
    if emit_state:
        out_shape.append(jax.ShapeDtypeStruct((nb, 2, H_A, DK_A, DV_A), F32))
        out_specs.append(pl.BlockSpec((1, 2, 2, DK_A, DV_A), lambda b, p: (b, 0, p, 0, 0)))
    scratch = [pltpu.VMEM((2, n, w), F32),
               pltpu.VMEM((2, n, w), BF16),
               pltpu.VMEM((2, nc, 2, DV_A, DK_A), F32),
               pltpu.VMEM((2, nc, 8, w), F32),
               pltpu.VMEM((2, 2, DV_A, DK_A), F32)]
    return pl.pallas_call(
        functools.partial(_hgrn_kernel, n=n, has_s0=latent, emit_state=emit_state),
        grid=(nb, pairs),
        in_specs=specs, out_specs=out_specs, out_shape=out_shape,
        scratch_shapes=scratch,
        input_output_aliases=aliases,
        compiler_params=_params("parallel", "parallel"),
        name="hgrn_lat" if latent else "hgrn_ctx",
    )(*args)


def _merge_kernel(oa_ref, ob_ref, oc_ref, wa_ref, wb_ref, wc_ref, g0_ref, g1_ref, g2_ref, wo_ref,
                  x_ref, gate_ref, npost_ref, npre_ref, scale_ref, shift_ref, xnew_ref, h_ref, y_scr):
    j = pl.program_id(1)

    @pl.when(j == 0)
    def _():
        y_scr[...] = jnp.zeros_like(y_scr)

    def gate(g_ref):
        return jax.nn.sigmoid(g_ref[...].astype(F32))

    merged = (gate(g0_ref) * jnp.dot(oa_ref[...], wa_ref[...], preferred_element_type=F32)
              + gate(g1_ref) * jnp.dot(ob_ref[...], wb_ref[...], preferred_element_type=F32)
              + gate(g2_ref) * jnp.dot(oc_ref[...], wc_ref[...], preferred_element_type=F32))
    y_scr[...] += jnp.dot(merged.astype(BF16), wo_ref[...], preferred_element_type=F32)

    @pl.when(j == pl.num_programs(1) - 1)
    def _():
        x = x_ref[...] + gate_ref[0, 0] * (_rms(y_scr[...]) * npost_ref[...])
        xnew_ref[...] = x
        h_ref[...] = ((_rms(x) * npre_ref[...]) * (1.0 + scale_ref[0, 0]) + shift_ref[0, 0]).astype(BF16)


def _merge(oa, ob, oc, wa, wb, wc, zg, wo, l, x, mod_l, npost, npre):
    tm, tn = 512, 512
    nj = D_MODEL // tn
    kb = H_A * DV_A
    o_spec = pl.BlockSpec((tm, kb), lambda i, j: (i, 0))
    w_spec = pl.BlockSpec((None, kb, tn), lambda i, j: (l, 0, j))
    row = pl.BlockSpec((tm, D_MODEL), lambda i, j: (i, 0))
    vec = pl.BlockSpec((1, D_MODEL), lambda i, j: (0, 0))

    def gspec(k):
        return pl.BlockSpec((tm, tn), lambda i, j: (i, k * nj + j))

    def modspec(k):
        return pl.BlockSpec((1, 1, 1, D_MODEL), lambda i, j: (_mod_row(i, tm), k, 0, 0))

    return pl.pallas_call(
        _merge_kernel,
        grid=(T_ALL // tm, nj),
        in_specs=[o_spec, o_spec, o_spec, w_spec, w_spec, w_spec, gspec(0), gspec(1), gspec(2),
                  pl.BlockSpec((None, tn, D_MODEL), lambda i, j: (l, j, 0)),
                  row, modspec(2), vec, vec, modspec(4), modspec(3)],
        out_specs=[row, row],
        out_shape=[jax.ShapeDtypeStruct((T_ALL, D_MODEL), F32), jax.ShapeDtypeStruct((T_ALL, D_MODEL), BF16)],
        scratch_shapes=[pltpu.VMEM((tm, D_MODEL), F32)],
        compiler_params=_params("parallel", "arbitrary"),
        name="merge_out",
    )(oa, ob, oc, wa, wb, wc, zg, zg, zg, wo, x, mod_l, npost.reshape(1, D_MODEL), npre.reshape(1, D_MODEL),
      mod_l, mod_l)


def _ffn_kernel(h_ref, wa_ref, wg_ref, ca_ref, cg_ref, wd_ref, y_ref, *, tm):
    i = pl.program_id(0)

    @pl.when(pl.program_id(1) == 0)
    def _():
        y_ref[...] = jnp.zeros_like(y_ref)

    h = h_ref[...]
    seq_len = jnp.where(i * tm < T_CTX, SEQ, DEC_SEQ)
    pos = lax.broadcasted_iota(jnp.int32, (tm, 1), 0) & (seq_len - 1)
    has_prev = pos != 0
    has_next = pos != seq_len - 1

    def conv(u, c):
        prev = jnp.where(has_prev, pltpu.roll(u, 1, 0), 0.0)
        nxt = jnp.where(has_next, pltpu.roll(u, tm - 1, 0), 0.0)
        return c[0:1, :] * prev + c[1:2, :] * u + c[2:3, :] * nxt

    tf = wa_ref.shape[2]
    col = lax.broadcasted_iota(jnp.int32, (1, tf), 1)
    fresh = (pl.program_id(1) < pl.num_programs(1) - 1) | (col >= FFN_TILES * tf - D_FF)
    subs = (slice(0, tf // 2), slice(tf // 2, tf))
    ups = [(jnp.dot(h, wa_ref[0, :, cols], preferred_element_type=F32),
            jnp.dot(h, wg_ref[0, :, cols], preferred_element_type=F32)) for cols in subs]
    for cols, (ua, ug) in zip(subs, ups):
        act = conv(ua, ca_ref[:, cols]) * jax.nn.gelu(conv(ug, cg_ref[:, cols]))
        act = jnp.where(fresh[:, cols], act, 0.0).astype(BF16)
        y_ref[...] += jnp.dot(act, wd_ref[0, cols, :], preferred_element_type=F32)


def _ffn_tile_start(j, base=0):
    return LANE * (base // LANE + jnp.minimum(j * (FFN_TF // LANE), (D_FF - FFN_TF) // LANE))


def _ffn(h, w_up, conv_t, w_down, l):
    tm, tf, nj = 1024, FFN_TF, FFN_TILES
    one = pl.Element(1)
    return pl.pallas_call(
        functools.partial(_ffn_kernel, tm=tm),
        grid=(T_ALL // tm, nj),
        in_specs=[pl.BlockSpec((tm, D_MODEL), lambda i, j: (i, 0)),
                  pl.BlockSpec((one, pl.Element(D_MODEL), pl.Element(tf)), lambda i, j: (l, 0, _ffn_tile_start(j))),
                  pl.BlockSpec((one, pl.Element(D_MODEL), pl.Element(tf)),
                               lambda i, j: (l, 0, _ffn_tile_start(j, D_FF))),
                  pl.BlockSpec((None, CONV_W, tf), lambda i, j: (l, 0, j)),
                  pl.BlockSpec((None, CONV_W, tf), lambda i, j: (l, 0, nj + j)),
                  pl.BlockSpec((one, pl.Element(tf), pl.Element(D_MODEL)), lambda i, j: (l, _ffn_tile_start(j), 0))],
        out_specs=pl.BlockSpec((tm, D_MODEL), lambda i, j: (i, 0)),
        out_shape=jax.ShapeDtypeStruct((T_ALL, D_MODEL), F32),
        compiler_params=_params("parallel", "arbitrary"),
        name="conv_ffn",
    )(h, w_up, w_up, conv_t, conv_t, w_down)


def _pad_cols(w, n):
    return jnp.pad(w, [(0, 0)] * (w.ndim - 1) + [(0, n - w.shape[-1])])


def _prep_weights(w_in, mla_w_uq, mla_w_ukv, w_branch_a, w_branch_b, w_branch_c, w_out,
                  ffn_w_up, ffn_conv, ffn_w_down):
    w_uq = _pad_cols(mla_w_uq.reshape(DEPTH, Q_LORA, H_B, NOPE_B + ROPE_B), 2 * LANE)
    w_uq = w_uq.reshape(DEPTH, Q_LORA, H_B * 2 * LANE).astype(BF16)
    w_up = ffn_w_up.astype(BF16)
    w_down = ffn_w_down.astype(BF16)
    starts = [min(j * FFN_TF, D_FF - FFN_TF) for j in range(FFN_TILES)]
    conv = jnp.concatenate([ffn_conv[:, :, base + s:base + s + FFN_TF] for base in (0, D_FF) for s in starts],
                           axis=-1)
    return dict(w_in_t=jnp.swapaxes(w_in, 1, 2), w_uq=w_uq, w_ukv=mla_w_ukv.astype(BF16),
                w_a=w_branch_a.astype(BF16), w_b=w_branch_b.astype(BF16), w_c=w_branch_c.astype(BF16),
                w_o=w_out.astype(BF16), w_up=w_up, conv=conv, w_down=w_down)


def kernel(x_prompt, x_sample, state_hgrn, cache_mla_ckv, cache_mla_krope, cache_swa_k, cache_swa_v,
           c, c_ctx, w_mod, b_mod, norm_pre_attn, norm_post_attn, norm_pre_ffn, norm_post_ffn,
           w_in, hgrn_lb, hgrn_gnorm, mla_gq, mla_w_uq, mla_gkv, mla_w_ukv, swa_sink,
           w_branch_a, w_branch_b, w_branch_c, w_out, ffn_w_up, ffn_conv, ffn_w_down):
    wts = _prep_weights(w_in, mla_w_uq, mla_w_ukv, w_branch_a, w_branch_b, w_branch_c, w_out,
                        ffn_w_up, ffn_conv, ffn_w_down)
    cs = jnp.cumsum(jax.nn.softmax(hgrn_lb.astype(F32), axis=0), axis=0)
    lb_all = cs - cs[0]

    cvec = jnp.concatenate([c_ctx[None, :], c, jnp.zeros((MOD_ROWS - 1 - DEC_BATCH, D_MODEL), F32)], axis=0)
    mod = _modulation(cvec, w_mod, b_mod).reshape(DEPTH, MOD_ROWS, 6, 1, D_MODEL)

    hgrn_tables = _hgrn_tables()
    rope_b = _rope_tables(ROPE_B)
    rope_c = _rope_tables(HD_C)
    sink_b = jnp.broadcast_to(swa_sink[:, :, None], (DEPTH, H_C, LANE))

    x = jnp.concatenate([x_prompt.reshape(T_CTX, D_MODEL), x_sample.reshape(T_LAT, D_MODEL)], axis=0)
    new_hgrn, new_ckv, new_krope, new_k, new_v = [], [], [], [], []
    y = None
    for l in range(DEPTH):
        mod_l = mod[l]
        if l == 0:
            (h,) = _norm(x, npre=norm_pre_attn[l], mod_pre=mod_l, scale_idx=1, shift_idx=0)
        else:
            x, h = _norm(x, y=y, mod_post=mod[l - 1], gate_idx=5, npost=norm_post_ffn[l - 1],
                         npre=norm_pre_attn[l], mod_pre=mod_l, scale_idx=1, shift_idx=0)
        zab = _in_proj(h, wts["w_in_t"], l, 0, ZAB_W, "in_proj_ab")
        zcg = _in_proj(h, wts["w_in_t"], l, Z_B_END, ZC_G, "in_proj_c")
        zg = _in_proj(h, wts["w_in_t"], l, Z_B_END + ZC_G, N_BRANCH * D_MODEL, "in_proj_g", BF16)

        oa, s_ctx = _hgrn(zab, lb_all[l], hgrn_gnorm[l], hgrn_tables, latent=False)
        (oa,) = _hgrn(zab, lb_all[l], hgrn_gnorm[l], hgrn_tables, latent=True, s0=state_hgrn[:, l], prev=oa)
        new_hgrn.append(s_ctx)

        (qb,) = _norm_mm(zab, Z_B // Q_LORA, Q_LORA, mla_gq[l], wts["w_uq"], l, F32, emit_normed=False,
                         name="mla_q_proj")
        ckv, kvb = _norm_mm(zab, Z_KV // KV_LORA, KV_LORA, mla_gkv[l], wts["w_ukv"], l, BF16, emit_normed=True,
                            name="mla_kv_proj")
        kv_cache = _mm(cache_mla_ckv[:, l].reshape(DEC_BATCH * PAST_LEN, KV_LORA), wts["w_ukv"], l, BF16,
                       "mla_kv_cache")
        kr_cache = _pad_cols(cache_mla_krope[:, l].reshape(DEC_BATCH * PAST_LEN, ROPE_B), LANE)
        ob = _mla_attn(qb, kvb, zab, latent=False)
        ob = _mla_attn(qb, kvb, zab, latent=True, kvc=kv_cache, krc=kr_cache, tables=rope_b, prev=ob)
        new_ckv.append(ckv[:T_CTX].reshape(BATCH, SEQ, KV_LORA))
        new_krope.append(zab[:T_CTX, Z_KR:Z_B_END].reshape(BATCH, SEQ, ROPE_B))

        oc = _gqa_attn(zcg, sink_b[l], latent=False)
        oc = _gqa_attn(zcg, sink_b[l], latent=True,
                       kc=cache_swa_k[:, l].reshape(DEC_BATCH * PAST_LEN, KVH_C * HD_C),
                       vc=cache_swa_v[:, l].reshape(DEC_BATCH * PAST_LEN, KVH_C * HD_C), tables=rope_c, prev=oc)
        new_k.append(zcg[:T_CTX, ZC_K:ZC_V].reshape(BATCH, SEQ, KVH_C, HD_C))
        new_v.append(zcg[:T_CTX, ZC_V:ZC_G].reshape(BATCH, SEQ, KVH_C, HD_C))

        x, h = _merge(oa, ob, oc, wts["w_a"], wts["w_b"], wts["w_c"], zg, wts["w_o"], l,
                      x, mod_l, norm_post_attn[l], norm_pre_ffn[l])
        y = _ffn(h, wts["w_up"], wts["conv"], wts["w_down"], l)

    (x,) = _norm(x, y=y, mod_post=mod[DEPTH - 1], gate_idx=5, npost=norm_post_ffn[DEPTH - 1])
    return (x[:T_CTX].reshape(BATCH, SEQ, D_MODEL), x[T_CTX:].reshape(DEC_BATCH, DEC_SEQ, D_MODEL),
            jnp.stack(new_hgrn, axis=1), jnp.stack(new_ckv, axis=1), jnp.stack(new_krope, axis=1),
            jnp.stack(new_k, axis=1), jnp.stack(new_v, axis=1))
```

```python
import functools

import jax
import jax.numpy as jnp
import numpy as np
from jax import lax
from jax.experimental import pallas as pl
from jax.experimental.pallas import tpu as pltpu

F32 = jnp.float32
BF16 = jnp.bfloat16

D_MODEL = 2048
BATCH = 16
SEQ = 256
DEPTH = 4
DEC_BATCH = 4
DEC_SEQ = 1024
PAST_LEN = 256
GRID_W = 64
ROPE_BASE = 10000.0
EPS = 1e-6
NEG_INF = -1e30
H_A, DK_A, DV_A = 8, 128, 128
H_B, Q_LORA, KV_LORA, NOPE_B, ROPE_B, V_B = 8, 512, 256, 128, 64, 128
H_C, KVH_C, HD_C, WINDOW = 8, 2, 128, 128
N_BRANCH = 3
D_FF = 5504
CONV_W = 3

T_CTX = BATCH * SEQ
T_LAT = DEC_BATCH * DEC_SEQ
T_ALL = T_CTX + T_LAT
MOD_ROWS = 8
LANE = 128
FFN_TF = 512
FFN_TILES = -(-D_FF // FFN_TF)
HGRN_CHUNK = 64
Z_A = 0
Z_B = 5 * H_A * DK_A
Z_KV = Z_B + Q_LORA
Z_KR = Z_KV + KV_LORA
Z_B_END = Z_KR + ROPE_B
ZAB_W = 6144
ZC_K = H_C * HD_C
ZC_V = ZC_K + KVH_C * HD_C
ZC_G = ZC_V + KVH_C * HD_C
ZCG_W = ZC_G + N_BRANCH * D_MODEL
VMEM_LIMIT = 56 * 1024 * 1024


def _params(*sem, flags=None):
    return pltpu.CompilerParams(dimension_semantics=sem, vmem_limit_bytes=VMEM_LIMIT, flags=flags)


def _mod_row(i, tm):
    return jnp.where(i * tm < T_CTX, 0, 1 + (i * tm - T_CTX) // DEC_SEQ)


def _rms(x):
    return x * lax.rsqrt(jnp.mean(x * x, axis=-1, keepdims=True) + EPS)


def _mod_kernel(c_ref, w_ref, b_ref, o_ref):
    cv = c_ref[...]
    s = (cv * jax.nn.sigmoid(cv)).astype(BF16)
    o_ref[0] = jnp.dot(s, w_ref[0].astype(BF16), preferred_element_type=F32) + b_ref[0]


def _modulation(cvec, w_mod, b_mod):
    tn = 1024
    n = 6 * D_MODEL
    return pl.pallas_call(
        _mod_kernel,
        grid=(DEPTH, n // tn),
        in_specs=[pl.BlockSpec((MOD_ROWS, D_MODEL), lambda l, j: (0, 0)),
                  pl.BlockSpec((1, D_MODEL, tn), lambda l, j: (l, 0, j)),
                  pl.BlockSpec((1, 1, tn), lambda l, j: (l, 0, j))],
        out_specs=pl.BlockSpec((1, MOD_ROWS, tn), lambda l, j: (l, 0, j)),
        out_shape=jax.ShapeDtypeStruct((DEPTH, MOD_ROWS, n), F32),
        compiler_params=_params("parallel", "parallel"),
        name="modulation",
    )(cvec, w_mod, b_mod.reshape(DEPTH, 1, n))


def _norm_kernel(*refs, has_y, has_h):
    it = iter(refs)
    x_ref = next(it)
    if has_y:
        y_ref, gate_ref, npost_ref = next(it), next(it), next(it)
    if has_h:
        npre_ref, scale_ref, shift_ref = next(it), next(it), next(it)
    x = x_ref[...]
    if has_y:
        xnew_ref = next(it)
        x = x + gate_ref[0, 0] * (_rms(y_ref[...]) * npost_ref[...])
        xnew_ref[...] = x
    if has_h:
        h_ref = next(it)
        h = (_rms(x) * npre_ref[...]) * (1.0 + scale_ref[0, 0]) + shift_ref[0, 0]
        h_ref[...] = h.astype(BF16)


def _norm(x, *, y=None, mod_post=None, gate_idx=None, npost=None,
          npre=None, mod_pre=None, scale_idx=None, shift_idx=None):
    tm = 512
    has_y, has_h = y is not None, npre is not None
    row = pl.BlockSpec((tm, D_MODEL), lambda i: (i, 0))
    vec = pl.BlockSpec((1, D_MODEL), lambda i: (0, 0))

    def modspec(k):
        return pl.BlockSpec((1, 1, 1, D_MODEL), lambda i: (_mod_row(i, tm), k, 0, 0))

    args, specs, out_shape, out_specs = [x], [row], [], []
    if has_y:
        args += [y, mod_post, npost.reshape(1, D_MODEL)]
        specs += [row, modspec(gate_idx), vec]
        out_shape.append(jax.ShapeDtypeStruct((T_ALL, D_MODEL), F32))
        out_specs.append(row)
    if has_h:
        args += [npre.reshape(1, D_MODEL), mod_pre, mod_pre]
        specs += [vec, modspec(scale_idx), modspec(shift_idx)]
        out_shape.append(jax.ShapeDtypeStruct((T_ALL, D_MODEL), BF16))
        out_specs.append(row)
    outs = pl.pallas_call(
        functools.partial(_norm_kernel, has_y=has_y, has_h=has_h),
        grid=(T_ALL // tm,),
        in_specs=specs, out_specs=out_specs, out_shape=out_shape,
        compiler_params=_params("parallel"),
        name="norm_y%d_h%d" % (has_y, has_h),
    )(*args)
    return outs


def _mm_kernel(x_ref, w_ref, o_ref):
    o_ref[...] = jnp.dot(x_ref[...].astype(BF16), w_ref[...].astype(BF16),
                         preferred_element_type=F32).astype(o_ref.dtype)


def _mm(x, w, l, out_dtype, name, n=None):
    m, k = x.shape
    n = w.shape[2] if n is None else n
    tm = min(m, 1024)
    tn = min(n, 512)
    return pl.pallas_call(
        _mm_kernel,
        grid=(m // tm, n // tn),
        in_specs=[pl.BlockSpec((tm, k), lambda i, j: (i, 0)),
                  pl.BlockSpec((None, k, tn), lambda i, j: (l, 0, j))],
        out_specs=pl.BlockSpec((tm, tn), lambda i, j: (i, j)),
        out_shape=jax.ShapeDtypeStruct((m, n), out_dtype),
        compiler_params=_params("parallel", "parallel"),
        name=name,
    )(x, w)


def _mm_nt_kernel(x_ref, wt_ref, o_ref):
    o_ref[...] = lax.dot_general(x_ref[...], wt_ref[...].astype(BF16), (((1,), (1,)), ((), ())),
                                 preferred_element_type=F32).astype(o_ref.dtype)


def _in_proj(h, w_t, l, col0, n, name, out_dtype=F32):
    m, k = h.shape
    tm, tn = 2048, 512
    row0 = l * w_t.shape[1] + col0
    return pl.pallas_call(
        _mm_nt_kernel,
        grid=(m // tm, n // tn),
        in_specs=[pl.BlockSpec((tm, k), lambda i, j: (i, 0)),
                  pl.BlockSpec((pl.Element(tn), pl.Element(k)),
                               lambda i, j: ((row0 // 8 + j * (tn // 8)) * 8, 0))],
        out_specs=pl.BlockSpec((tm, tn), lambda i, j: (i, j)),
        out_shape=jax.ShapeDtypeStruct((m, n), out_dtype),
        compiler_params=_params("parallel", "parallel"),
        name=name,
    )(h, w_t.reshape(-1, k))


def _norm_mm_kernel(x_ref, g_ref, w_ref, *out_refs, emit_normed):
    xn = _rms(x_ref[...]) * g_ref[...]
    if emit_normed:
        out_refs[0][...] = xn
    out_refs[-1][...] = jnp.dot(xn.astype(BF16), w_ref[...],
                                preferred_element_type=F32).astype(out_refs[-1].dtype)


def _norm_mm(z, col_block, k, gain, w, l, out_dtype, *, emit_normed, name):
    m = z.shape[0]
    n = w.shape[2]
    tm = 512
    out_shape = [jax.ShapeDtypeStruct((m, n), out_dtype)]
    out_specs = [pl.BlockSpec((tm, n), lambda i: (i, 0))]
    if emit_normed:
        out_shape.insert(0, jax.ShapeDtypeStruct((m, k), F32))
        out_specs.insert(0, pl.BlockSpec((tm, k), lambda i: (i, 0)))
    return pl.pallas_call(
        functools.partial(_norm_mm_kernel, emit_normed=emit_normed),
        grid=(m // tm,),
        in_specs=[pl.BlockSpec((tm, k), lambda i: (i, col_block)),
                  pl.BlockSpec((1, k), lambda i: (0, 0)),
                  pl.BlockSpec((None, k, n), lambda i: (l, 0, 0))],
        out_specs=out_specs, out_shape=out_shape,
        compiler_params=_params("parallel"),
        name=name,
    )(z, gain.reshape(1, k), w)


def _rope_tables(head_dim):
    n = DEC_SEQ
    half = head_dim // 2
    quarter = half // 2
    row = jnp.repeat(jnp.arange(n // GRID_W), GRID_W).astype(F32)
    col = jnp.tile(jnp.arange(GRID_W), n // GRID_W).astype(F32)
    inv = ROPE_BASE ** (-jnp.arange(0, half, 2, dtype=F32) / half)
    lane = jnp.arange(LANE)
    m = lane % half
    pos = jnp.where((lane // half)[None, :] == 0, row[:, None], col[:, None])
    ang = pos * inv[m % quarter][None, :]
    valid = (lane < head_dim)[None, :]
    cos = jnp.where(valid, jnp.cos(ang), 0.0)
    sin = jnp.where(valid, jnp.where(m < quarter, -1.0, 1.0)[None, :] * jnp.sin(ang), 0.0)
    return cos.astype(F32), sin.astype(F32)


def _rope(x, cos, sin, head_dim):
    quarter = head_dim // 4
    lane = lax.broadcasted_iota(jnp.int32, x.shape, 1)
    first = (lane % (2 * quarter)) < quarter
    partner = jnp.where(first, pltpu.roll(x, LANE - quarter, 1), pltpu.roll(x, quarter, 1))
    return x * cos + partner * sin


def _mla_attn_kernel(*refs, latent):
    if latent:
        q_ref, kv_ref, kr_ref, kvc_ref, krc_ref, cq_ref, sq_ref, ck_ref, sk_ref, _, o_ref = refs
    else:
        q_ref, kv_ref, kr_ref, o_ref = refs
    scale = (NOPE_B + ROPE_B) ** -0.5
    nt = (((1,), (1,)), ((), ()))
    kr = kr_ref[...]
    if latent:
        kr = _rope(kr, ck_ref[...], sk_ref[...], ROPE_B)
        krc = krc_ref[...].astype(BF16)
    kr = kr.astype(BF16)
    for h in range(H_B):
        c0 = h * 2 * LANE
        qn = q_ref[:, c0:c0 + LANE]
        qr = q_ref[:, c0 + LANE:c0 + 2 * LANE]
        if latent:
            qr = _rope(qr, cq_ref[...], sq_ref[...], ROPE_B)
        qh = jnp.concatenate([(qn * scale).astype(BF16), (qr * scale).astype(BF16)], axis=-1)
        kh = jnp.concatenate([kv_ref[:, c0:c0 + LANE], kr], axis=-1)
        vh = kv_ref[:, c0 + LANE:c0 + 2 * LANE]
        s = lax.dot_general(qh, kh, nt, preferred_element_type=F32)
        m = jnp.max(s, axis=-1, keepdims=True)
        if latent:
            khc = jnp.concatenate([kvc_ref[:, c0:c0 + LANE], krc], axis=-1)
            vhc = kvc_ref[:, c0 + LANE:c0 + 2 * LANE]
            sc = lax.dot_general(qh, khc, nt, preferred_element_type=F32)
            m = jnp.maximum(m, jnp.max(sc, axis=-1, keepdims=True))
        p = jnp.exp(s - m)
        l = jnp.sum(p, axis=-1, keepdims=True)
        o = jnp.dot(p.astype(BF16), vh, preferred_element_type=F32)
        if latent:
            pc = jnp.exp(sc - m)
            l = l + jnp.sum(pc, axis=-1, keepdims=True)
            o = o + jnp.dot(pc.astype(BF16), vhc, preferred_element_type=F32)
        o_ref[:, h * LANE:(h + 1) * LANE] = (o / l).astype(BF16)


def _rows_of(specs, args, prev):
    if prev is None:
        return specs, args, {}
    return specs + [pl.BlockSpec(memory_space=pl.ANY)], args + [prev], {len(args): 0}


def _mla_attn(q, kv, z, *, latent, kvc=None, krc=None, tables=None, prev=None):
    nb, n = (DEC_BATCH, DEC_SEQ) if latent else (BATCH, SEQ)
    tq = 256
    nq = n // tq
    off = T_CTX // n if latent else 0
    offq = T_CTX // tq if latent else 0
    w = H_B * 2 * LANE
    specs = [pl.BlockSpec((tq, w), lambda b, i: (offq + b * nq + i, 0)),
             pl.BlockSpec((n, w), lambda b, i: (off + b, 0)),
             pl.BlockSpec((n, LANE), lambda b, i: (off + b, Z_KR // LANE))]
    args = [q, kv, z]
    if latent:
        cos, sin = tables
        specs += [pl.BlockSpec((PAST_LEN, w), lambda b, i: (b, 0)),
                  pl.BlockSpec((PAST_LEN, LANE), lambda b, i: (b, 0)),
                  pl.BlockSpec((tq, LANE), lambda b, i: (i, 0)),
                  pl.BlockSpec((tq, LANE), lambda b, i: (i, 0)),
                  pl.BlockSpec((n, LANE), lambda b, i: (0, 0)),
                  pl.BlockSpec((n, LANE), lambda b, i: (0, 0))]
        args += [kvc, krc, cos, sin, cos, sin]
    specs, args, aliases = _rows_of(specs, args, prev)
    return pl.pallas_call(
        functools.partial(_mla_attn_kernel, latent=latent),
        grid=(nb, nq),
        in_specs=specs,
        out_specs=pl.BlockSpec((tq, H_B * V_B), lambda b, i: (offq + b * nq + i, 0)),
        out_shape=jax.ShapeDtypeStruct((T_ALL, H_B * V_B), BF16),
        input_output_aliases=aliases,
        compiler_params=_params("parallel", "parallel"),
        name="mla_attn_lat" if latent else "mla_attn_ctx",
    )(*args)


def _gqa_attn_kernel(*refs, latent, tq):
    if latent:
        q_ref, k_ref, v_ref, kc_ref, vc_ref, sink_ref, cq_ref, sq_ref, ck_ref, sk_ref, _, o_ref = refs
    else:
        q_ref, k_ref, v_ref, sink_ref, o_ref = refs
    scale = HD_C ** -0.5
    nt = (((1,), (1,)), ((), ()))
    rep = H_C // KVH_C
    n = k_ref.shape[0]
    if latent:
        kw = tq + 2 * WINDOW
        q0 = pl.program_id(1) * tq
        k0 = pl.multiple_of(jnp.clip(q0 - WINDOW, 0, n - kw), WINDOW)
        keys = pl.ds(k0, kw)
        qpos = q0 + lax.broadcasted_iota(jnp.int32, (tq, kw), 0)
        kpos = k0 + lax.broadcasted_iota(jnp.int32, (tq, kw), 1)
        band = jnp.abs(qpos - kpos) <= WINDOW
    else:
        keys = slice(None)
    for g in range(KVH_C):
        kg = k_ref[keys, g * LANE:(g + 1) * LANE]
        if latent:
            kg = _rope(kg, ck_ref[keys, :], sk_ref[keys, :], HD_C)
            kcg = kc_ref[:, g * LANE:(g + 1) * LANE].astype(BF16)
            vcg = vc_ref[:, g * LANE:(g + 1) * LANE].astype(BF16)
        kg = kg.astype(BF16)
        vg = v_ref[keys, g * LANE:(g + 1) * LANE].astype(BF16)
        for r in range(rep):
            h = g * rep + r
            qh = q_ref[:, h * LANE:(h + 1) * LANE]
            if latent:
                qh = _rope(qh, cq_ref[...], sq_ref[...], HD_C)
            qh = (qh * scale).astype(BF16)
            sk = sink_ref[h:h + 1, 0:1]
            s = lax.dot_general(qh, kg, nt, preferred_element_type=F32)
            if latent:
                s = jnp.where(band, s, NEG_INF)
            m = jnp.maximum(jnp.max(s, axis=-1, keepdims=True), sk)
            if latent:
                sc = lax.dot_general(qh, kcg, nt, preferred_element_type=F32)
                m = jnp.maximum(m, jnp.max(sc, axis=-1, keepdims=True))
            p = jnp.exp(s - m)
            l = jnp.sum(p, axis=-1, keepdims=True) + jnp.exp(sk - m)
            o = jnp.dot(p.astype(BF16), vg, preferred_element_type=F32)
            if latent:
                pc = jnp.exp(sc - m)
                l = l + jnp.sum(pc, axis=-1, keepdims=True)
                o = o + jnp.dot(pc.astype(BF16), vcg, preferred_element_type=F32)
            o_ref[:, h * LANE:(h + 1) * LANE] = (o / l).astype(BF16)


def _gqa_attn(z, sink_b, *, latent, kc=None, vc=None, tables=None, prev=None):
    nb, n = (DEC_BATCH, DEC_SEQ) if latent else (BATCH, SEQ)
    tq = 256
    nq = n // tq
    off = T_CTX // n if latent else 0
    offq = T_CTX // tq if latent else 0
    wq, wk = H_C * HD_C, KVH_C * HD_C
    specs = [pl.BlockSpec((tq, wq), lambda b, i: (offq + b * nq + i, 0)),
             pl.BlockSpec((n, wk), lambda b, i: (off + b, ZC_K // wk)),
             pl.BlockSpec((n, wk), lambda b, i: (off + b, ZC_V // wk))]
    args = [z, z, z]
    if latent:
        specs += [pl.BlockSpec((PAST_LEN, wk), lambda b, i: (b, 0)),
                  pl.BlockSpec((PAST_LEN, wk), lambda b, i: (b, 0))]
        args += [kc, vc]
    specs.append(pl.BlockSpec((H_C, LANE), lambda b, i: (0, 0)))
    args.append(sink_b)
    if latent:
        cos, sin = tables
        specs += [pl.BlockSpec((tq, LANE), lambda b, i: (i, 0)),
                  pl.BlockSpec((tq, LANE), lambda b, i: (i, 0)),
                  pl.BlockSpec((n, LANE), lambda b, i: (0, 0)),
                  pl.BlockSpec((n, LANE), lambda b, i: (0, 0))]
        args += [cos, sin, cos, sin]
    specs, args, aliases = _rows_of(specs, args, prev)
    return pl.pallas_call(
        functools.partial(_gqa_attn_kernel, latent=latent, tq=tq),
        grid=(nb, nq),
        in_specs=specs,
        out_specs=pl.BlockSpec((tq, wq), lambda b, i: (offq + b * nq + i, 0)),
        out_shape=jax.ShapeDtypeStruct((T_ALL, wq), BF16),
        input_output_aliases=aliases,
        compiler_params=_params("parallel", "parallel"),
        name="gqa_attn_lat" if latent else "gqa_attn_ctx",
    )(*args)


def _hgrn_tables():
    c = HGRN_CHUNK
    halves = [c >> (i + 1) for i in range(c.bit_length() - 1)]
    out = []
    for forward in (True, False):
        sums = np.zeros((len(halves) + 1, c, c), np.float32)
        level = np.full((c, c), -1, np.int32)
        level[np.arange(c), np.arange(c)] = 0
        for li, m in enumerate(halves):
            for r in range(c):
                pos = r % (2 * m)
                mid = r - pos + m
                late = pos >= m
                if forward:
                    lo, hi = (mid, r + 1) if late else (r + 1, mid)
                else:
                    lo, hi = (mid, r) if late else (r, mid)
                sums[li, r, lo:hi] = 1.0
                for s in range(r - pos, r - pos + 2 * m):
                    s_late = (s % (2 * m)) >= m
                    if (late and not s_late) if forward else (not late and s_late):
                        level[r, s] = li + 1
        for r in range(c):
            if forward:
                sums[-1, r, :r + 1] = 1.0
            else:
                sums[-1, r, r:] = 1.0
        sums = sums.reshape(-1, c)
        out.append((jnp.asarray(np.concatenate([sums, sums, sums], axis=1), BF16),
                    jnp.asarray(np.concatenate([level, level], axis=1))))
    return out


def _hgrn_kernel(*refs, n, has_s0, emit_state):
    it = iter(refs)
    q_ref, xf_ref, xb_ref, v_ref, ag_ref, lb_ref, gn_ref = (next(it) for _ in range(7))
    sums_refs = (next(it), next(it))
    level_refs = (next(it), next(it))
    s0_ref = next(it) if has_s0 else None
    if has_s0:
        next(it)
    o_ref = next(it)
    sfin_ref = next(it) if emit_state else None
    o_scr, qe_scr, u_scr, e_scr, st_scr = (next(it) for _ in range(5))

    c = HGRN_CHUNK
    nc = n // c
    nlev = c.bit_length() - 1
    nt = (((1,), (1,)), ((), ()))
    tn = (((0,), (0,)), ((), ()))
    zero = jnp.zeros((c, LANE), BF16)

    def blockdiag(x):
        return jnp.concatenate([jnp.concatenate([x[:, :LANE], zero], axis=1),
                                jnp.concatenate([zero, x[:, LANE:]], axis=1)], axis=0)

    def gates(x, lb):
        e = jnp.exp(-jnp.abs(x))
        big = 1.0 / (1.0 + e)
        small = e * big
        pos = x >= 0.0
        return jnp.log(lb + (1.0 - lb) * jnp.where(pos, big, small)), (1.0 - lb) * jnp.where(pos, small, big)

    for d in range(2):
        for hh in range(2):
            st_scr[d, hh] = s0_ref[0, d, hh].T if has_s0 else jnp.zeros((DV_A, DK_A), F32)

    group = 4

    def intra(t, carry):
        jobs = [(u, d) for u in range(group) for d in range(2)]
        chunk_of = [t * group + u for u in range(group)]
        rows = {u: pl.ds(pl.multiple_of(chunk_of[u] * c, c), c) for u in range(group)}
        q = {ci: q_ref[rows[ci], :] for ci, _ in jobs}
        v = {ci: v_ref[rows[ci], :].astype(BF16) for ci, _ in jobs}
        k = {}

        dall = {}
        for ci, d in jobs:
            g, k[ci, d] = gates((xf_ref, xb_ref)[d][rows[ci], :], lb_ref[d:d + 1, :])
            g_hi = g.astype(BF16)
            rem = g - g_hi.astype(F32)
            g_mid = rem.astype(BF16)
            g_lo = (rem - g_mid.astype(F32)).astype(BF16)
            dall[ci, d] = jnp.dot(sums_refs[d][...], jnp.concatenate([g_hi, g_mid, g_lo], axis=0),
                                  preferred_element_type=F32)

        scores = {}
        for ci, d in jobs:
            rs = [lax.dot_general(q[ci].astype(BF16), blockdiag(k[ci, d].astype(BF16)), nt,
                                  preferred_element_type=F32)]
            for li in range(nlev):
                e = jnp.exp(dall[ci, d][li * c:(li + 1) * c, :])
                rs.append(lax.dot_general((q[ci] * e).astype(BF16), blockdiag((k[ci, d] * e).astype(BF16)), nt,
                                          preferred_element_type=F32))
            scores[ci, d] = rs

        for ci, d in jobs:
            level = level_refs[d][...]
            a = jnp.where(level == 0, scores[ci, d][0], 0.0)
            for li in range(nlev):
                a = jnp.where(level == li + 1, scores[ci, d][li + 1], a)
            o_scr[d, rows[ci], :] = jnp.dot(a.astype(BF16), blockdiag(v[ci]), preferred_element_type=F32)
            gc = dall[ci, d][nlev * c:, :]
            g_end = gc[c - 1:c, :] if d == 0 else gc[0:1, :]
            qe_scr[d, rows[ci], :] = (q[ci] * jnp.exp(gc)).astype(BF16)
            kd = (k[ci, d] * jnp.exp(g_end - gc)).astype(BF16)
            e_scr[d, chunk_of[ci]] = jnp.broadcast_to(jnp.exp(g_end), (8, 2 * LANE))
            for hh in range(2):
                hl = slice(hh * LANE, (hh + 1) * LANE)
                u_scr[d, chunk_of[ci], hh] = lax.dot_general(v[ci][:, hl], kd[:, hl], tn,
                                                             preferred_element_type=F32)
        return carry

    lax.fori_loop(0, nc // group, intra, 0)

    def inter(i, carry):
        for d in range(2):
            ci = i if d == 0 else nc - 1 - i
            rows = pl.ds(pl.multiple_of(ci * c, c), c)
            e = e_scr[d, ci]
            for hh in range(2):
                hl = slice(hh * LANE, (hh + 1) * LANE)
                st = st_scr[d, hh]
                o_scr[d, rows, hl] += lax.dot_general(qe_scr[d, rows, hl], st.astype(BF16), nt,
                                                      preferred_element_type=F32)
                st_scr[d, hh] = st * e[0:1, hl] + u_scr[d, ci, hh]
        return carry

    lax.fori_loop(0, nc, inter, 0, unroll=4)

    def finish(i, carry):
        rows = pl.ds(pl.multiple_of(i * c, c), c)
        o = o_scr[0, rows, :] + o_scr[1, rows, :]
        o = jnp.concatenate([_rms(o[:, :LANE]), _rms(o[:, LANE:])], axis=1)
        ag = ag_ref[rows, :]
        o_ref[rows, :] = (o * gn_ref[...] * (ag * jax.nn.sigmoid(ag))).astype(BF16)
        return carry

    lax.fori_loop(0, nc, finish, 0, unroll=4)
    if emit_state:
        for d in range(2):
            for hh in range(2):
                sfin_ref[0, d, hh] = st_scr[d, hh].T


def _hgrn(z, lb_l, gnorm, tables, *, latent, s0=None, prev=None):
    nb, n = (DEC_BATCH, DEC_SEQ) if latent else (BATCH, SEQ)
    off = T_CTX // n if latent else 0
    emit_state = not latent
    w = 2 * LANE
    pairs = H_A // 2
    c = HGRN_CHUNK
    nc = n // c

    def zspec(k):
        return pl.BlockSpec((n, w), lambda b, p: (off + b, Z_A // w + k * pairs + p))

    def const(x):
        return pl.BlockSpec(x.shape, lambda b, p: (0, 0))

    (sums_f, level_f), (sums_b, level_b) = tables
    specs = [zspec(0), zspec(1), zspec(2), zspec(3), zspec(4),
             pl.BlockSpec((2, w), lambda b, p: (0, p)),
             pl.BlockSpec((1, w), lambda b, p: (0, 0)),
             const(sums_f), const(sums_b), const(level_f), const(level_b)]
    args = [z, z, z, z, z, lb_l, jnp.tile(gnorm.reshape(1, DV_A), (1, 2)),
            sums_f, sums_b, level_f, level_b]
    if latent:
        specs.append(pl.BlockSpec((1, 2, 2, DK_A, DV_A), lambda b, p: (b, 0, p, 0, 0)))
        args.append(s0)
    specs, args, aliases = _rows_of(specs, args, prev)
    out_shape = [jax.ShapeDtypeStruct((T_ALL, H_A * DV_A), BF16)]
    out_specs = [pl.BlockSpec((n, w), lambda b, p: (off + b, p))]
    if emit_state:
        out_shape.append(jax.ShapeDtypeStruct((nb, 2, H_A, DK_A, DV_A), F32))
        out_specs.append(pl.BlockSpec((1, 2, 2, DK_A, DV_A), lambda b, p: (b, 0, p, 0, 0)))
    scratch = [pltpu.VMEM((2, n, w), F32),
               pltpu.VMEM((2, n, w), BF16),
               pltpu.VMEM((2, nc, 2, DV_A, DK_A), F32),
               pltpu.VMEM((2, nc, 8, w), F32),
               pltpu.VMEM((2, 2, DV_A, DK_A), F32)]
    return pl.pallas_call(
        functools.partial(_hgrn_kernel, n=n, has_s0=latent, emit_state=emit_state),
        grid=(nb, pairs),
        in_specs=specs, out_specs=out_specs, out_shape=out_shape,
        scratch_shapes=scratch,
        input_output_aliases=aliases,
        compiler_params=_params("parallel", "parallel"),
        name="hgrn_lat" if latent else "hgrn_ctx",
    )(*args)


def _merge_kernel(oa_ref, ob_ref, oc_ref, wa_ref, wb_ref, wc_ref, g0_ref, g1_ref, g2_ref, wo_ref,
                  x_ref, gate_ref, npost_ref, npre_ref, scale_ref, shift_ref, xnew_ref, h_ref, y_scr):
    j = pl.program_id(1)

    @pl.when(j == 0)
    def _():
        y_scr[...] = jnp.zeros_like(y_scr)

    def gate(g_ref):
        return jax.nn.sigmoid(g_ref[...].astype(F32))

    merged = (gate(g0_ref) * jnp.dot(oa_ref[...], wa_ref[...], preferred_element_type=F32)
              + gate(g1_ref) * jnp.dot(ob_ref[...], wb_ref[...], preferred_element_type=F32)
              + gate(g2_ref) * jnp.dot(oc_ref[...], wc_ref[...], preferred_element_type=F32))
    y_scr[...] += jnp.dot(merged.astype(BF16), wo_ref[...], preferred_element_type=F32)

    @pl.when(j == pl.num_programs(1) - 1)
    def _():
        x = x_ref[...] + gate_ref[0, 0] * (_rms(y_scr[...]) * npost_ref[...])
        xnew_ref[...] = x
        h_ref[...] = ((_rms(x) * npre_ref[...]) * (1.0 + scale_ref[0, 0]) + shift_ref[0, 0]).astype(BF16)


def _merge(oa, ob, oc, wa, wb, wc, zg, wo, l, x, mod_l, npost, npre):
    tm, tn = 512, 512
    nj = D_MODEL // tn
    kb = H_A * DV_A
    o_spec = pl.BlockSpec((tm, kb), lambda i, j: (i, 0))
    w_spec = pl.BlockSpec((None, kb, tn), lambda i, j: (l, 0, j))
    row = pl.BlockSpec((tm, D_MODEL), lambda i, j: (i, 0))
    vec = pl.BlockSpec((1, D_MODEL), lambda i, j: (0, 0))

    def gspec(k):
        return pl.BlockSpec((tm, tn), lambda i, j: (i, k * nj + j))

    def modspec(k):
        return pl.BlockSpec((1, 1, 1, D_MODEL), lambda i, j: (_mod_row(i, tm), k, 0, 0))

    return pl.pallas_call(
        _merge_kernel,
        grid=(T_ALL // tm, nj),
        in_specs=[o_spec, o_spec, o_spec, w_spec, w_spec, w_spec, gspec(0), gspec(1), gspec(2),
                  pl.BlockSpec((None, tn, D_MODEL), lambda i, j: (l, j, 0)),
                  row, modspec(2), vec, vec, modspec(4), modspec(3)],
        out_specs=[row, row],
        out_shape=[jax.ShapeDtypeStruct((T_ALL, D_MODEL), F32), jax.ShapeDtypeStruct((T_ALL, D_MODEL), BF16)],
        scratch_shapes=[pltpu.VMEM((tm, D_MODEL), F32)],
        compiler_params=_params("parallel", "arbitrary"),
        name="merge_out",
    )(oa, ob, oc, wa, wb, wc, zg, zg, zg, wo, x, mod_l, npost.reshape(1, D_MODEL), npre.reshape(1, D_MODEL),
      mod_l, mod_l)


def _ffn_kernel(h_ref, wa_ref, wg_ref, ca_ref, cg_ref, wd_ref, y_ref, act_scr, *, tm):
    i = pl.program_id(0)
    j = pl.program_id(1)
    tf = wa_ref.shape[2]

    def activation():
        h = h_ref[...]
        seq_len = jnp.where(i * tm < T_CTX, SEQ, DEC_SEQ)
        pos = lax.broadcasted_iota(jnp.int32, (tm, 1), 0) & (seq_len - 1)
        has_prev = pos != 0
        has_next = pos != seq_len - 1
        col = lax.broadcasted_iota(jnp.int32, (1, tf), 1)
        fresh = (j < FFN_TILES - 1) | (col >= FFN_TILES * tf - D_FF)

        def conv(u, c):
            prev = jnp.where(has_prev, pltpu.roll(u, 1, 0), 0.0)
            nxt = jnp.where(has_next, pltpu.roll(u, tm - 1, 0), 0.0)
            return c[0:1, :] * prev + c[1:2, :] * u + c[2:3, :] * nxt

        subs = (slice(0, tf // 2), slice(tf // 2, tf))
        return subs, fresh, conv, [(jnp.dot(h, wa_ref[0, :, cols], preferred_element_type=F32),
                                    jnp.dot(h, wg_ref[0, :, cols], preferred_element_type=F32)) for cols in subs]

    mine = j % 2
    theirs = 1 - mine

    def finish_activation(subs, fresh, conv, ups):
        for cols, (ua, ug) in zip(subs, ups):
            act = conv(ua, ca_ref[:, cols]) * jax.nn.gelu(conv(ug, cg_ref[:, cols]))
            act_scr[mine, :, cols] = jnp.where(fresh[:, cols], act, 0.0).astype(BF16)

    @pl.when(j == 0)
    def _():
        y_ref[...] = jnp.zeros_like(y_ref)
        finish_activation(*activation())

    @pl.when((j > 0) & (j < FFN_TILES))
    def _():
        pending = activation()
        y_ref[...] += jnp.dot(act_scr[theirs], wd_ref[0], preferred_element_type=F32)
        finish_activation(*pending)

    @pl.when(j == FFN_TILES)
    def _():
        y_ref[...] += jnp.dot(act_scr[theirs], wd_ref[0], preferred_element_type=F32)


def _ffn_tile_start(j, base=0):
    return LANE * (base // LANE + jnp.minimum(j * (FFN_TF // LANE), (D_FF - FFN_TF) // LANE))


def _ffn(h, w_up, conv_t, w_down, l):
    tm, tf, nj = 1024, FFN_TF, FFN_TILES
    one = pl.Element(1)

    def up_tile(j):
        return jnp.minimum(j, nj - 1)

    def down_tile(j):
        return jnp.maximum(j - 1, 0)

    return pl.pallas_call(
        functools.partial(_ffn_kernel, tm=tm),
        grid=(T_ALL // tm, nj + 1),
        in_specs=[pl.BlockSpec((tm, D_MODEL), lambda i, j: (i, 0)),
                  pl.BlockSpec((one, pl.Element(D_MODEL), pl.Element(tf)),
                               lambda i, j: (l, 0, _ffn_tile_start(up_tile(j)))),
                  pl.BlockSpec((one, pl.Element(D_MODEL), pl.Element(tf)),
                               lambda i, j: (l, 0, _ffn_tile_start(up_tile(j), D_FF))),
                  pl.BlockSpec((None, CONV_W, tf), lambda i, j: (l, 0, up_tile(j))),
                  pl.BlockSpec((None, CONV_W, tf), lambda i, j: (l, 0, nj + up_tile(j))),
                  pl.BlockSpec((one, pl.Element(tf), pl.Element(D_MODEL)),
                               lambda i, j: (l, _ffn_tile_start(down_tile(j)), 0))],
        out_specs=pl.BlockSpec((tm, D_MODEL), lambda i, j: (i, 0)),
        out_shape=jax.ShapeDtypeStruct((T_ALL, D_MODEL), F32),
        scratch_shapes=[pltpu.VMEM((2, tm, tf), BF16)],
        compiler_params=_params("parallel", "arbitrary"),
        name="conv_ffn",
    )(h, w_up, w_up, conv_t, conv_t, w_down)


def _pad_cols(w, n):
    return jnp.pad(w, [(0, 0)] * (w.ndim - 1) + [(0, n - w.shape[-1])])


def _prep_weights(w_in, mla_w_uq, mla_w_ukv, w_branch_a, w_branch_b, w_branch_c, w_out,
                  ffn_w_up, ffn_conv, ffn_w_down):
    w_uq = _pad_cols(mla_w_uq.reshape(DEPTH, Q_LORA, H_B, NOPE_B + ROPE_B), 2 * LANE)
    w_uq = w_uq.reshape(DEPTH, Q_LORA, H_B * 2 * LANE).astype(BF16)
    w_up = ffn_w_up.astype(BF16)
    w_down = ffn_w_down.astype(BF16)
    starts = [min(j * FFN_TF, D_FF - FFN_TF) for j in range(FFN_TILES)]
    conv = jnp.concatenate([ffn_conv[:, :, base + s:base + s + FFN_TF] for base in (0, D_FF) for s in starts],
                           axis=-1)
    return dict(w_in_t=jnp.swapaxes(w_in, 1, 2), w_uq=w_uq, w_ukv=mla_w_ukv.astype(BF16),
                w_a=w_branch_a.astype(BF16), w_b=w_branch_b.astype(BF16), w_c=w_branch_c.astype(BF16),
                w_o=w_out.astype(BF16), w_up=w_up, conv=conv, w_down=w_down)


def kernel(x_prompt, x_sample, state_hgrn, cache_mla_ckv, cache_mla_krope, cache_swa_k, cache_swa_v,
           c, c_ctx, w_mod, b_mod, norm_pre_attn, norm_post_attn, norm_pre_ffn, norm_post_ffn,
           w_in, hgrn_lb, hgrn_gnorm, mla_gq, mla_w_uq, mla_gkv, mla_w_ukv, swa_sink,
           w_branch_a, w_branch_b, w_branch_c, w_out, ffn_w_up, ffn_conv, ffn_w_down):
    wts = _prep_weights(w_in, mla_w_uq, mla_w_ukv, w_branch_a, w_branch_b, w_branch_c, w_out,
                        ffn_w_up, ffn_conv, ffn_w_down)
    cs = jnp.cumsum(jax.nn.softmax(hgrn_lb.astype(F32), axis=0), axis=0)
    lb_all = cs - cs[0]

    cvec = jnp.concatenate([c_ctx[None, :], c, jnp.zeros((MOD_ROWS - 1 - DEC_BATCH, D_MODEL), F32)], axis=0)
    mod = _modulation(cvec, w_mod, b_mod).reshape(DEPTH, MOD_ROWS, 6, 1, D_MODEL)

    hgrn_tables = _hgrn_tables()
    rope_b = _rope_tables(ROPE_B)
    rope_c = _rope_tables(HD_C)
    sink_b = jnp.broadcast_to(swa_sink[:, :, None], (DEPTH, H_C, LANE))

    x = jnp.concatenate([x_prompt.reshape(T_CTX, D_MODEL), x_sample.reshape(T_LAT, D_MODEL)], axis=0)
    new_hgrn, new_ckv, new_krope, new_k, new_v = [], [], [], [], []
    y = None
    for l in range(DEPTH):
        mod_l = mod[l]
        if l == 0:
            (h,) = _norm(x, npre=norm_pre_attn[l], mod_pre=mod_l, scale_idx=1, shift_idx=0)
        else:
            x, h = _norm(x, y=y, mod_post=mod[l - 1], gate_idx=5, npost=norm_post_ffn[l - 1],
                         npre=norm_pre_attn[l], mod_pre=mod_l, scale_idx=1, shift_idx=0)
        zab = _in_proj(h, wts["w_in_t"], l, 0, ZAB_W, "in_proj_ab")
        zcg = _in_proj(h, wts["w_in_t"], l, Z_B_END, ZC_G, "in_proj_c")
        zg = _in_proj(h, wts["w_in_t"], l, Z_B_END + ZC_G, N_BRANCH * D_MODEL, "in_proj_g", BF16)

        oa, s_ctx = _hgrn(zab, lb_all[l], hgrn_gnorm[l], hgrn_tables, latent=False)
        (oa,) = _hgrn(zab, lb_all[l], hgrn_gnorm[l], hgrn_tables, latent=True, s0=state_hgrn[:, l], prev=oa)
        new_hgrn.append(s_ctx)

        (qb,) = _norm_mm(zab, Z_B // Q_LORA, Q_LORA, mla_gq[l], wts["w_uq"], l, F32, emit_normed=False,
                         name="mla_q_proj")
        ckv, kvb = _norm_mm(zab, Z_KV // KV_LORA, KV_LORA, mla_gkv[l], wts["w_ukv"], l, BF16, emit_normed=True,
                            name="mla_kv_proj")
        kv_cache = _mm(cache_mla_ckv[:, l].reshape(DEC_BATCH * PAST_LEN, KV_LORA), wts["w_ukv"], l, BF16,
                       "mla_kv_cache")
        kr_cache = _pad_cols(cache_mla_krope[:, l].reshape(DEC_BATCH * PAST_LEN, ROPE_B), LANE)
        ob = _mla_attn(qb, kvb, zab, latent=False)
        ob = _mla_attn(qb, kvb, zab, latent=True, kvc=kv_cache, krc=kr_cache, tables=rope_b, prev=ob)
        new_ckv.append(ckv[:T_CTX].reshape(BATCH, SEQ, KV_LORA))
        new_krope.append(zab[:T_CTX, Z_KR:Z_B_END].reshape(BATCH, SEQ, ROPE_B))

        oc = _gqa_attn(zcg, sink_b[l], latent=False)
        oc = _gqa_attn(zcg, sink_b[l], latent=True,
                       kc=cache_swa_k[:, l].reshape(DEC_BATCH * PAST_LEN, KVH_C * HD_C),
                       vc=cache_swa_v[:, l].reshape(DEC_BATCH * PAST_LEN, KVH_C * HD_C), tables=rope_c, prev=oc)
        new_k.append(zcg[:T_CTX, ZC_K:ZC_V].reshape(BATCH, SEQ, KVH_C, HD_C))
        new_v.append(zcg[:T_CTX, ZC_V:ZC_G].reshape(BATCH, SEQ, KVH_C, HD_C))

        x, h = _merge(oa, ob, oc, wts["w_a"], wts["w_b"], wts["w_c"], zg, wts["w_o"], l,
                      x, mod_l, norm_post_attn[l], norm_pre_ffn[l])
        y = _ffn(h, wts["w_up"], wts["conv"], wts["w_down"], l)

    (x,) = _norm(x, y=y, mod_post=mod[DEPTH - 1], gate_idx=5, npost=norm_post_ffn[DEPTH - 1])
    return (x[:T_CTX].reshape(BATCH, SEQ, D_MODEL), x[T_CTX:].reshape(DEC_BATCH, DEC_SEQ, D_MODEL),
            jnp.stack(new_hgrn, axis=1), jnp.stack(new_ckv, axis=1), jnp.stack(new_krope, axis=1),
            jnp.stack(new_k, axis=1), jnp.stack(new_v, axis=1))
```

```python
import functools

import jax
import jax.numpy as jnp
import numpy as np
from jax import lax
from jax.experimental import pallas as pl
from jax.experimental.pallas import tpu as pltpu

F32 = jnp.float32
BF16 = jnp.bfloat16

D_MODEL = 2048
BATCH = 16
SEQ = 256
DEPTH = 4
DEC_BATCH = 4
DEC_SEQ = 1024
PAST_LEN = 256
GRID_W = 64
ROPE_BASE = 10000.0
EPS = 1e-6
NEG_INF = -1e30
H_A, DK_A, DV_A = 8, 128, 128
H_B, Q_LORA, KV_LORA, NOPE_B, ROPE_B, V_B = 8, 512, 256, 128, 64, 128
H_C, KVH_C, HD_C, WINDOW = 8, 2, 128, 128
N_BRANCH = 3
D_FF = 5504
CONV_W = 3

T_CTX = BATCH * SEQ
T_LAT = DEC_BATCH * DEC_SEQ
T_ALL = T_CTX + T_LAT
MOD_ROWS = 8
LANE = 128
CTX_SEQS = 4
FFN_TF = 512
FFN_TILES = -(-D_FF // FFN_TF)
HGRN_CHUNK = 64
Z_A = 0
Z_B = 5 * H_A * DK_A
Z_KV = Z_B + Q_LORA
Z_KR = Z_KV + KV_LORA
Z_B_END = Z_KR + ROPE_B
ZAB_W = 6144
ZC_K = H_C * HD_C
ZC_V = ZC_K + KVH_C * HD_C
ZC_G = ZC_V + KVH_C * HD_C
ZCG_W = ZC_G + N_BRANCH * D_MODEL
VMEM_LIMIT = 56 * 1024 * 1024


def _params(*sem, flags=None):
    return pltpu.CompilerParams(dimension_semantics=sem, vmem_limit_bytes=VMEM_LIMIT, flags=flags)


def _mod_row(i, tm):
    return jnp.where(i * tm < T_CTX, 0, 1 + (i * tm - T_CTX) // DEC_SEQ)


def _rms(x):
    return x * lax.rsqrt(jnp.mean(x * x, axis=-1, keepdims=True) + EPS)


def _mod_kernel(c_ref, w_ref, b_ref, o_ref):
    cv = c_ref[...]
    s = (cv * jax.nn.sigmoid(cv)).astype(BF16)
    o_ref[0] = jnp.dot(s, w_ref[0].astype(BF16), preferred_element_type=F32) + b_ref[0]


def _modulation(cvec, w_mod, b_mod):
    tn = 1024
    n = 6 * D_MODEL
    return pl.pallas_call(
        _mod_kernel,
        grid=(DEPTH, n // tn),
        in_specs=[pl.BlockSpec((MOD_ROWS, D_MODEL), lambda l, j: (0, 0)),
                  pl.BlockSpec((1, D_MODEL, tn), lambda l, j: (l, 0, j)),
                  pl.BlockSpec((1, 1, tn), lambda l, j: (l, 0, j))],
        out_specs=pl.BlockSpec((1, MOD_ROWS, tn), lambda l, j: (l, 0, j)),
        out_shape=jax.ShapeDtypeStruct((DEPTH, MOD_ROWS, n), F32),
        compiler_params=_params("parallel", "parallel"),
        name="modulation",
    )(cvec, w_mod, b_mod.reshape(DEPTH, 1, n))


def _norm_kernel(*refs, has_y, has_h):
    it = iter(refs)
    x_ref = next(it)
    if has_y:
        y_ref, gate_ref, npost_ref = next(it), next(it), next(it)
    if has_h:
        npre_ref, scale_ref, shift_ref = next(it), next(it), next(it)
    x = x_ref[...]
    if has_y:
        xnew_ref = next(it)
        x = x + gate_ref[0, 0] * (_rms(y_ref[...]) * npost_ref[...])
        xnew_ref[...] = x
    if has_h:
        h_ref = next(it)
        h = (_rms(x) * npre_ref[...]) * (1.0 + scale_ref[0, 0]) + shift_ref[0, 0]
        h_ref[...] = h.astype(BF16)


def _norm(x, *, y=None, mod_post=None, gate_idx=None, npost=None,
          npre=None, mod_pre=None, scale_idx=None, shift_idx=None):
    tm = 512
    has_y, has_h = y is not None, npre is not None
    row = pl.BlockSpec((tm, D_MODEL), lambda i: (i, 0))
    vec = pl.BlockSpec((1, D_MODEL), lambda i: (0, 0))

    def modspec(k):
        return pl.BlockSpec((1, 1, 1, D_MODEL), lambda i: (_mod_row(i, tm), k, 0, 0))

    args, specs, out_shape, out_specs = [x], [row], [], []
    if has_y:
        args += [y, mod_post, npost.reshape(1, D_MODEL)]
        specs += [row, modspec(gate_idx), vec]
        out_shape.append(jax.ShapeDtypeStruct((T_ALL, D_MODEL), F32))
        out_specs.append(row)
    if has_h:
        args += [npre.reshape(1, D_MODEL), mod_pre, mod_pre]
        specs += [vec, modspec(scale_idx), modspec(shift_idx)]
        out_shape.append(jax.ShapeDtypeStruct((T_ALL, D_MODEL), BF16))
        out_specs.append(row)
    outs = pl.pallas_call(
        functools.partial(_norm_kernel, has_y=has_y, has_h=has_h),
        grid=(T_ALL // tm,),
        in_specs=specs, out_specs=out_specs, out_shape=out_shape,
        compiler_params=_params("parallel"),
        name="norm_y%d_h%d" % (has_y, has_h),
    )(*args)
    return outs


def _mm_kernel(x_ref, w_ref, o_ref):
    o_ref[...] = jnp.dot(x_ref[...].astype(BF16), w_ref[...].astype(BF16),
                         preferred_element_type=F32).astype(o_ref.dtype)


def _mm(x, w, l, out_dtype, name, n=None):
    m, k = x.shape
    n = w.shape[2] if n is None else n
    tm = min(m, 1024)
    tn = min(n, 512)
    return pl.pallas_call(
        _mm_kernel,
        grid=(m // tm, n // tn),
        in_specs=[pl.BlockSpec((tm, k), lambda i, j: (i, 0)),
                  pl.BlockSpec((None, k, tn), lambda i, j: (l, 0, j))],
        out_specs=pl.BlockSpec((tm, tn), lambda i, j: (i, j)),
        out_shape=jax.ShapeDtypeStruct((m, n), out_dtype),
        compiler_params=_params("parallel", "parallel"),
        name=name,
    )(x, w)


def _mm_nt_kernel(x_ref, wt_ref, o_ref):
    o_ref[...] = lax.dot_general(x_ref[...], wt_ref[...].astype(BF16), (((1,), (1,)), ((), ())),
                                 preferred_element_type=F32).astype(o_ref.dtype)


def _in_proj(h, w_t, l, col0, n, name, out_dtype=F32):
    m, k = h.shape
    tm, tn = 2048, 512
    row0 = l * w_t.shape[1] + col0
    return pl.pallas_call(
        _mm_nt_kernel,
        grid=(m // tm, n // tn),
        in_specs=[pl.BlockSpec((tm, k), lambda i, j: (i, 0)),
                  pl.BlockSpec((pl.Element(tn), pl.Element(k)),
                               lambda i, j: ((row0 // 8 + j * (tn // 8)) * 8, 0))],
        out_specs=pl.BlockSpec((tm, tn), lambda i, j: (i, j)),
        out_shape=jax.ShapeDtypeStruct((m, n), out_dtype),
        compiler_params=_params("parallel", "parallel"),
        name=name,
    )(h, w_t.reshape(-1, k))


def _norm_mm_kernel(x_ref, g_ref, w_ref, *out_refs, emit_normed):
    xn = _rms(x_ref[...]) * g_ref[...]
    if emit_normed:
        out_refs[0][...] = xn
    out_refs[-1][...] = jnp.dot(xn.astype(BF16), w_ref[...],
                                preferred_element_type=F32).astype(out_refs[-1].dtype)


def _norm_mm(z, col_block, k, gain, w, l, out_dtype, *, emit_normed, name):
    m = z.shape[0]
    n = w.shape[2]
    tm = 512
    out_shape = [jax.ShapeDtypeStruct((m, n), out_dtype)]
    out_specs = [pl.BlockSpec((tm, n), lambda i: (i, 0))]
    if emit_normed:
        out_shape.insert(0, jax.ShapeDtypeStruct((m, k), F32))
        out_specs.insert(0, pl.BlockSpec((tm, k), lambda i: (i, 0)))
    return pl.pallas_call(
        functools.partial(_norm_mm_kernel, emit_normed=emit_normed),
        grid=(m // tm,),
        in_specs=[pl.BlockSpec((tm, k), lambda i: (i, col_block)),
                  pl.BlockSpec((1, k), lambda i: (0, 0)),
                  pl.BlockSpec((None, k, n), lambda i: (l, 0, 0))],
        out_specs=out_specs, out_shape=out_shape,
        compiler_params=_params("parallel"),
        name=name,
    )(z, gain.reshape(1, k), w)


def _rope_tables(head_dim):
    n = DEC_SEQ
    half = head_dim // 2
    quarter = half // 2
    row = jnp.repeat(jnp.arange(n // GRID_W), GRID_W).astype(F32)
    col = jnp.tile(jnp.arange(GRID_W), n // GRID_W).astype(F32)
    inv = ROPE_BASE ** (-jnp.arange(0, half, 2, dtype=F32) / half)
    lane = jnp.arange(LANE)
    m = lane % half
    pos = jnp.where((lane // half)[None, :] == 0, row[:, None], col[:, None])
    ang = pos * inv[m % quarter][None, :]
    valid = (lane < head_dim)[None, :]
    cos = jnp.where(valid, jnp.cos(ang), 0.0)
    sin = jnp.where(valid, jnp.where(m < quarter, -1.0, 1.0)[None, :] * jnp.sin(ang), 0.0)
    return cos.astype(F32), sin.astype(F32)


def _rope(x, cos, sin, head_dim):
    quarter = head_dim // 4
    lane = lax.broadcasted_iota(jnp.int32, x.shape, 1)
    first = (lane % (2 * quarter)) < quarter
    partner = jnp.where(first, pltpu.roll(x, LANE - quarter, 1), pltpu.roll(x, quarter, 1))
    return x * cos + partner * sin


def _mla_attn_kernel(*refs, latent, n):
    kv_ref = refs[1]
    scale = (NOPE_B + ROPE_B) ** -0.5
    nt = (((1,), (1,)), ((), ()))
    for r0 in range(0, kv_ref.shape[0], n):
        rows = slice(r0, r0 + n)
        _mla_attn_seq(refs, latent, rows, rows if not latent else slice(None), scale, nt)


def _mla_attn_seq(refs, latent, krows, qrows, scale, nt):
    if latent:
        q_ref, kv_ref, kr_ref, kvc_ref, krc_ref, cq_ref, sq_ref, ck_ref, sk_ref, _, o_ref = refs
    else:
        q_ref, kv_ref, kr_ref, o_ref = refs
    kr = kr_ref[krows, :]
    if latent:
        kr = _rope(kr, ck_ref[...], sk_ref[...], ROPE_B)
        krc = krc_ref[...].astype(BF16)
    kr = kr.astype(BF16)
    for h in range(H_B):
        c0 = h * 2 * LANE
        qn = q_ref[qrows, c0:c0 + LANE]
        qr = q_ref[qrows, c0 + LANE:c0 + 2 * LANE]
        if latent:
            qr = _rope(qr, cq_ref[...], sq_ref[...], ROPE_B)
        qh = jnp.concatenate([(qn * scale).astype(BF16), (qr * scale).astype(BF16)], axis=-1)
        kh = jnp.concatenate([kv_ref[krows, c0:c0 + LANE], kr], axis=-1)
        vh = kv_ref[krows, c0 + LANE:c0 + 2 * LANE]
        s = lax.dot_general(qh, kh, nt, preferred_element_type=F32)
        m = jnp.max(s, axis=-1, keepdims=True)
        if latent:
            khc = jnp.concatenate([kvc_ref[:, c0:c0 + LANE], krc], axis=-1)
            vhc = kvc_ref[:, c0 + LANE:c0 + 2 * LANE]
            sc = lax.dot_general(qh, khc, nt, preferred_element_type=F32)
            m = jnp.maximum(m, jnp.max(sc, axis=-1, keepdims=True))
        p = jnp.exp(s - m)
        l = jnp.sum(p, axis=-1, keepdims=True)
        o = jnp.dot(p.astype(BF16), vh, preferred_element_type=F32)
        if latent:
            pc = jnp.exp(sc - m)
            l = l + jnp.sum(pc, axis=-1, keepdims=True)
            o = o + jnp.dot(pc.astype(BF16), vhc, preferred_element_type=F32)
        o_ref[qrows, h * LANE:(h + 1) * LANE] = (o / l).astype(BF16)


def _rows_of(specs, args, prev):
    if prev is None:
        return specs, args, {}
    return specs + [pl.BlockSpec(memory_space=pl.ANY)], args + [prev], {len(args): 0}


def _mla_attn(q, kv, z, *, latent, kvc=None, krc=None, tables=None, prev=None):
    nb, n = (DEC_BATCH, DEC_SEQ) if latent else (BATCH // CTX_SEQS, SEQ)
    seqs = 1 if latent else CTX_SEQS
    tq = 256 * seqs
    nq = n * seqs // tq
    off = T_CTX // n if latent else 0
    offq = T_CTX // tq if latent else 0
    w = H_B * 2 * LANE
    specs = [pl.BlockSpec((tq, w), lambda b, i: (offq + b * nq + i, 0)),
             pl.BlockSpec((n * seqs, w), lambda b, i: (off + b, 0)),
             pl.BlockSpec((n * seqs, LANE), lambda b, i: (off + b, Z_KR // LANE))]
    args = [q, kv, z]
    if latent:
        cos, sin = tables
        specs += [pl.BlockSpec((PAST_LEN, w), lambda b, i: (b, 0)),
                  pl.BlockSpec((PAST_LEN, LANE), lambda b, i: (b, 0)),
                  pl.BlockSpec((tq, LANE), lambda b, i: (i, 0)),
                  pl.BlockSpec((tq, LANE), lambda b, i: (i, 0)),
                  pl.BlockSpec((n, LANE), lambda b, i: (0, 0)),
                  pl.BlockSpec((n, LANE), lambda b, i: (0, 0))]
        args += [kvc, krc, cos, sin, cos, sin]
    specs, args, aliases = _rows_of(specs, args, prev)
    return pl.pallas_call(
        functools.partial(_mla_attn_kernel, latent=latent, n=n),
        grid=(nb, nq),
        in_specs=specs,
        out_specs=pl.BlockSpec((tq, H_B * V_B), lambda b, i: (offq + b * nq + i, 0)),
        out_shape=jax.ShapeDtypeStruct((T_ALL, H_B * V_B), BF16),
        input_output_aliases=aliases,
        compiler_params=_params("parallel", "parallel"),
        name="mla_attn_lat" if latent else "mla_attn_ctx",
    )(*args)


def _gqa_attn_kernel(*refs, latent, tq):
    if latent:
        q_ref, k_ref, v_ref, kc_ref, vc_ref, sink_ref, cq_ref, sq_ref, ck_ref, sk_ref, _, o_ref = refs
    else:
        q_ref, k_ref, v_ref, sink_ref, o_ref = refs
    scale = HD_C ** -0.5
    nt = (((1,), (1,)), ((), ()))
    rep = H_C // KVH_C
    n = k_ref.shape[0]
    if latent:
        kw = tq + 2 * WINDOW
        q0 = pl.program_id(1) * tq
        k0 = pl.multiple_of(jnp.clip(q0 - WINDOW, 0, n - kw), WINDOW)
        keys = pl.ds(k0, kw)
        qpos = q0 + lax.broadcasted_iota(jnp.int32, (tq, kw), 0)
        kpos = k0 + lax.broadcasted_iota(jnp.int32, (tq, kw), 1)
        band = jnp.abs(qpos - kpos) <= WINDOW
        row_sets = [(keys, slice(None))]
    else:
        row_sets = [(slice(r0, r0 + SEQ),) * 2 for r0 in range(0, n, SEQ)]
    for keys, qrows in row_sets:
        _gqa_attn_seq(refs, latent, keys, qrows, band if latent else None, scale, nt, rep)


def _gqa_attn_seq(refs, latent, keys, qrows, band, scale, nt, rep):
    if latent:
        q_ref, k_ref, v_ref, kc_ref, vc_ref, sink_ref, cq_ref, sq_ref, ck_ref, sk_ref, _, o_ref = refs
    else:
        q_ref, k_ref, v_ref, sink_ref, o_ref = refs
    for g in range(KVH_C):
        kg = k_ref[keys, g * LANE:(g + 1) * LANE]
        if latent:
            kg = _rope(kg, ck_ref[keys, :], sk_ref[keys, :], HD_C)
            kcg = kc_ref[:, g * LANE:(g + 1) * LANE].astype(BF16)
            vcg = vc_ref[:, g * LANE:(g + 1) * LANE].astype(BF16)
        kg = kg.astype(BF16)
        vg = v_ref[keys, g * LANE:(g + 1) * LANE].astype(BF16)
        for r in range(rep):
            h = g * rep + r
            qh = q_ref[qrows, h * LANE:(h + 1) * LANE]
            if latent:
                qh = _rope(qh, cq_ref[...], sq_ref[...], HD_C)
            qh = (qh * scale).astype(BF16)
            sk = sink_ref[h:h + 1, 0:1]
            s = lax.dot_general(qh, kg, nt, preferred_element_type=F32)
            if latent:
                s = jnp.where(band, s, NEG_INF)
            m = jnp.maximum(jnp.max(s, axis=-1, keepdims=True), sk)
            if latent:
                sc = lax.dot_general(qh, kcg, nt, preferred_element_type=F32)
                m = jnp.maximum(m, jnp.max(sc, axis=-1, keepdims=True))
            p = jnp.exp(s - m)
            l = jnp.sum(p, axis=-1, keepdims=True) + jnp.exp(sk - m)
            o = jnp.dot(p.astype(BF16), vg, preferred_element_type=F32)
            if latent:
                pc = jnp.exp(sc - m)
                l = l + jnp.sum(pc, axis=-1, keepdims=True)
                o = o + jnp.dot(pc.astype(BF16), vcg, preferred_element_type=F32)
            o_ref[qrows, h * LANE:(h + 1) * LANE] = (o / l).astype(BF16)


def _gqa_attn(z, sink_b, *, latent, kc=None, vc=None, tables=None, prev=None):
    nb, n = (DEC_BATCH, DEC_SEQ) if latent else (BATCH // CTX_SEQS, SEQ * CTX_SEQS)
    tq = 256 if latent else n
    nq = n // tq
    off = T_CTX // n if latent else 0
    offq = T_CTX // tq if latent else 0
    wq, wk = H_C * HD_C, KVH_C * HD_C
    specs = [pl.BlockSpec((tq, wq), lambda b, i: (offq + b * nq + i, 0)),
             pl.BlockSpec((n, wk), lambda b, i: (off + b, ZC_K // wk)),
             pl.BlockSpec((n, wk), lambda b, i: (off + b, ZC_V // wk))]
    args = [z, z, z]
    if latent:
        specs += [pl.BlockSpec((PAST_LEN, wk), lambda b, i: (b, 0)),
                  pl.BlockSpec((PAST_LEN, wk), lambda b, i: (b, 0))]
        args += [kc, vc]
    specs.append(pl.BlockSpec((H_C, LANE), lambda b, i: (0, 0)))
    args.append(sink_b)
    if latent:
        cos, sin = tables
        specs += [pl.BlockSpec((tq, LANE), lambda b, i: (i, 0)),
                  pl.BlockSpec((tq, LANE), lambda b, i: (i, 0)),
                  pl.BlockSpec((n, LANE), lambda b, i: (0, 0)),
                  pl.BlockSpec((n, LANE), lambda b, i: (0, 0))]
        args += [cos, sin, cos, sin]
    specs, args, aliases = _rows_of(specs, args, prev)
    return pl.pallas_call(
        functools.partial(_gqa_attn_kernel, latent=latent, tq=tq),
        grid=(nb, nq),
        in_specs=specs,
        out_specs=pl.BlockSpec((tq, wq), lambda b, i: (offq + b * nq + i, 0)),
        out_shape=jax.ShapeDtypeStruct((T_ALL, wq), BF16),
        input_output_aliases=aliases,
        compiler_params=_params("parallel", "parallel"),
        name="gqa_attn_lat" if latent else "gqa_attn_ctx",
    )(*args)


def _hgrn_tables():
    c = HGRN_CHUNK
    halves = [c >> (i + 1) for i in range(c.bit_length() - 1)]
    out = []
    for forward in (True, False):
        sums = np.zeros((len(halves) + 1, c, c), np.float32)
        level = np.full((c, c), -1, np.int32)
        level[np.arange(c), np.arange(c)] = 0
        for li, m in enumerate(halves):
            for r in range(c):
                pos = r % (2 * m)
                mid = r - pos + m
                late = pos >= m
                if forward:
                    lo, hi = (mid, r + 1) if late else (r + 1, mid)
                else:
                    lo, hi = (mid, r) if late else (r, mid)
                sums[li, r, lo:hi] = 1.0
                for s in range(r - pos, r - pos + 2 * m):
                    s_late = (s % (2 * m)) >= m
                    if (late and not s_late) if forward else (not late and s_late):
                        level[r, s] = li + 1
        for r in range(c):
            if forward:
                sums[-1, r, :r + 1] = 1.0
            else:
                sums[-1, r, r:] = 1.0
        sums = sums.reshape(-1, c)
        out.append((jnp.asarray(np.concatenate([sums, sums, sums], axis=1), BF16),
                    jnp.asarray(np.concatenate([level, level], axis=1))))
    return out


def _hgrn_kernel(*refs, n, has_s0, emit_state):
    it = iter(refs)
    q_ref, xf_ref, xb_ref, v_ref, ag_ref, lb_ref, gn_ref = (next(it) for _ in range(7))
    sums_refs = (next(it), next(it))
    level_refs = (next(it), next(it))
    s0_ref = next(it) if has_s0 else None
    if has_s0:
        next(it)
    o_ref = next(it)
    sfin_ref = next(it) if emit_state else None
    o_scr, qe_scr, u_scr, e_scr, st_scr = (next(it) for _ in range(5))

    c = HGRN_CHUNK
    nc = n // c
    nlev = c.bit_length() - 1
    nt = (((1,), (1,)), ((), ()))
    tn = (((0,), (0,)), ((), ()))
    zero = jnp.zeros((c, LANE), BF16)

    def blockdiag(x):
        return jnp.concatenate([jnp.concatenate([x[:, :LANE], zero], axis=1),
                                jnp.concatenate([zero, x[:, LANE:]], axis=1)], axis=0)

    def gates(x, lb):
        e = jnp.exp(-jnp.abs(x))
        big = 1.0 / (1.0 + e)
        small = e * big
        pos = x >= 0.0
        return jnp.log(lb + (1.0 - lb) * jnp.where(pos, big, small)), (1.0 - lb) * jnp.where(pos, small, big)

    for d in range(2):
        for hh in range(2):
            st_scr[d, hh] = s0_ref[0, d, hh].T if has_s0 else jnp.zeros((DV_A, DK_A), F32)

    group = 4

    def intra(t, carry):
        jobs = [(u, d) for u in range(group) for d in range(2)]
        chunk_of = [t * group + u for u in range(group)]
        rows = {u: pl.ds(pl.multiple_of(chunk_of[u] * c, c), c) for u in range(group)}
        q = {ci: q_ref[rows[ci], :] for ci, _ in jobs}
        v = {ci: v_ref[rows[ci], :].astype(BF16) for ci, _ in jobs}
        k = {}

        dall = {}
        for ci, d in jobs:
            g, k[ci, d] = gates((xf_ref, xb_ref)[d][rows[ci], :], lb_ref[d:d + 1, :])
            g_hi = g.astype(BF16)
            rem = g - g_hi.astype(F32)
            g_mid = rem.astype(BF16)
            g_lo = (rem - g_mid.astype(F32)).astype(BF16)
            dall[ci, d] = jnp.dot(sums_refs[d][...], jnp.concatenate([g_hi, g_mid, g_lo], axis=0),
                                  preferred_element_type=F32)

        scores = {}
        for ci, d in jobs:
            rs = [lax.dot_general(q[ci].astype(BF16), blockdiag(k[ci, d].astype(BF16)), nt,
                                  preferred_element_type=F32)]
            for li in range(nlev):
                e = jnp.exp(dall[ci, d][li * c:(li + 1) * c, :])
                rs.append(lax.dot_general((q[ci] * e).astype(BF16), blockdiag((k[ci, d] * e).astype(BF16)), nt,
                                          preferred_element_type=F32))
            scores[ci, d] = rs

        for ci, d in jobs:
            level = level_refs[d][...]
            a = jnp.where(level == 0, scores[ci, d][0], 0.0)
            for li in range(nlev):
                a = jnp.where(level == li + 1, scores[ci, d][li + 1], a)
            o_scr[d, rows[ci], :] = jnp.dot(a.astype(BF16), blockdiag(v[ci]), preferred_element_type=F32)
            gc = dall[ci, d][nlev * c:, :]
            g_end = gc[c - 1:c, :] if d == 0 else gc[0:1, :]
            qe_scr[d, rows[ci], :] = (q[ci] * jnp.exp(gc)).astype(BF16)
            kd = (k[ci, d] * jnp.exp(g_end - gc)).astype(BF16)
            e_scr[d, chunk_of[ci]] = jnp.broadcast_to(jnp.exp(g_end), (8, 2 * LANE))
            for hh in range(2):
                hl = slice(hh * LANE, (hh + 1) * LANE)
                u_scr[d, chunk_of[ci], hh] = lax.dot_general(v[ci][:, hl], kd[:, hl], tn,
                                                             preferred_element_type=F32)
        return carry

    lax.fori_loop(0, nc // group, intra, 0)

    def inter(i, carry):
        for d in range(2):
            ci = i if d == 0 else nc - 1 - i
            rows = pl.ds(pl.multiple_of(ci * c, c), c)
            e = e_scr[d, ci]
            for hh in range(2):
                hl = slice(hh * LANE, (hh + 1) * LANE)
                st = st_scr[d, hh]
                o_scr[d, rows, hl] += lax.dot_general(qe_scr[d, rows, hl], st.astype(BF16), nt,
                                                      preferred_element_type=F32)
                st_scr[d, hh] = st * e[0:1, hl] + u_scr[d, ci, hh]
        return carry

    lax.fori_loop(0, nc, inter, 0, unroll=4)

    def finish(i, carry):
        rows = pl.ds(pl.multiple_of(i * c, c), c)
        o = o_scr[0, rows, :] + o_scr[1, rows, :]
        o = jnp.concatenate([_rms(o[:, :LANE]), _rms(o[:, LANE:])], axis=1)
        ag = ag_ref[rows, :]
        o_ref[rows, :] = (o * gn_ref[...] * (ag * jax.nn.sigmoid(ag))).astype(BF16)
        return carry

    lax.fori_loop(0, nc, finish, 0, unroll=4)
    if emit_state:
        for d in range(2):
            for hh in range(2):
                sfin_ref[0, d, hh] = st_scr[d, hh].T


def _hgrn(z, lb_l, gnorm, tables, *, latent, s0=None, prev=None):
    nb, n = (DEC_BATCH, DEC_SEQ) if latent else (BATCH, SEQ)
    off = T_CTX // n if latent else 0
    emit_state = not latent
    w = 2 * LANE
    pairs = H_A // 2
    c = HGRN_CHUNK
    nc = n // c

    def zspec(k):
        return pl.BlockSpec((n, w), lambda b, p: (off + b, Z_A // w + k * pairs + p))

    def const(x):
        return pl.BlockSpec(x.shape, lambda b, p: (0, 0))

    (sums_f, level_f), (sums_b, level_b) = tables
    specs = [zspec(0), zspec(1), zspec(2), zspec(3), zspec(4),
             pl.BlockSpec((2, w), lambda b, p: (0, p)),
             pl.BlockSpec((1, w), lambda b, p: (0, 0)),
             const(sums_f), const(sums_b), const(level_f), const(level_b)]
    args = [z, z, z, z, z, lb_l, jnp.tile(gnorm.reshape(1, DV_A), (1, 2)),
            sums_f, sums_b, level_f, level_b]
    if latent:
        specs.append(pl.BlockSpec((1, 2, 2, DK_A, DV_A), lambda b, p: (b, 0, p, 0, 0)))
        args.append(s0)
    specs, args, aliases = _rows_of(specs, args, prev)
    out_shape = [jax.ShapeDtypeStruct((T_ALL, H_A * DV_A), BF16)]
    out_specs = [pl.BlockSpec((n, w), lambda b, p: (off + b, p))]
    if emit_state:
        out_shape.append(jax.ShapeDtypeStruct((nb, 2, H_A, DK_A, DV_A), F32))
        out_specs.append(pl.BlockSpec((1, 2, 2, DK_A, DV_A), lambda b, p: (b, 0, p, 0, 0)))
    scratch = [pltpu.VMEM((2, n, w), F32),
               pltpu.VMEM((2, n, w), BF16),
               pltpu.VMEM((2, nc, 2, DV_A, DK_A), F32),
               pltpu.VMEM((2, nc, 8, w), F32),
               pltpu.VMEM((2, 2, DV_A, DK_A), F32)]
    return pl.pallas_call(
        functools.partial(_hgrn_kernel, n=n, has_s0=latent, emit_state=emit_state),
        grid=(nb, pairs),
        in_specs=specs, out_specs=out_specs, out_shape=out_shape,
        scratch_shapes=scratch,
        input_output_aliases=aliases,
        compiler_params=_params("parallel", "parallel"),
        name="hgrn_lat" if latent else "hgrn_ctx",
    )(*args)


def _merge_kernel(oa_ref, ob_ref, oc_ref, wa_ref, wb_ref, wc_ref, g0_ref, g1_ref, g2_ref, wo_ref,
                  x_ref, gate_ref, npost_ref, npre_ref, scale_ref, shift_ref, xnew_ref, h_ref, y_scr):
    j = pl.program_id(1)

    @pl.when(j == 0)
    def _():
        y_scr[...] = jnp.zeros_like(y_scr)

    def gate(g_ref):
        return jax.nn.sigmoid(g_ref[...].astype(F32))

    merged = (gate(g0_ref) * jnp.dot(oa_ref[...], wa_ref[...], preferred_element_type=F32)
              + gate(g1_ref) * jnp.dot(ob_ref[...], wb_ref[...], preferred_element_type=F32)
              + gate(g2_ref) * jnp.dot(oc_ref[...], wc_ref[...], preferred_element_type=F32))
    y_scr[...] += jnp.dot(merged.astype(BF16), wo_ref[...], preferred_element_type=F32)

    @pl.when(j == pl.num_programs(1) - 1)
    def _():
        x = x_ref[...] + gate_ref[0, 0] * (_rms(y_scr[...]) * npost_ref[...])
        xnew_ref[...] = x
        h_ref[...] = ((_rms(x) * npre_ref[...]) * (1.0 + scale_ref[0, 0]) + shift_ref[0, 0]).astype(BF16)


def _merge(oa, ob, oc, wa, wb, wc, zg, wo, l, x, mod_l, npost, npre):
    tm, tn = 512, 512
    nj = D_MODEL // tn
    kb = H_A * DV_A
    o_spec = pl.BlockSpec((tm, kb), lambda i, j: (i, 0))
    w_spec = pl.BlockSpec((None, kb, tn), lambda i, j: (l, 0, j))
    row = pl.BlockSpec((tm, D_MODEL), lambda i, j: (i, 0))
    vec = pl.BlockSpec((1, D_MODEL), lambda i, j: (0, 0))

    def gspec(k):
        return pl.BlockSpec((tm, tn), lambda i, j: (i, k * nj + j))

    def modspec(k):
        return pl.BlockSpec((1, 1, 1, D_MODEL), lambda i, j: (_mod_row(i, tm), k, 0, 0))

    return pl.pallas_call(
        _merge_kernel,
        grid=(T_ALL // tm, nj),
        in_specs=[o_spec, o_spec, o_spec, w_spec, w_spec, w_spec, gspec(0), gspec(1), gspec(2),
                  pl.BlockSpec((None, tn, D_MODEL), lambda i, j: (l, j, 0)),
                  row, modspec(2), vec, vec, modspec(4), modspec(3)],
        out_specs=[row, row],
        out_shape=[jax.ShapeDtypeStruct((T_ALL, D_MODEL), F32), jax.ShapeDtypeStruct((T_ALL, D_MODEL), BF16)],
        scratch_shapes=[pltpu.VMEM((tm, D_MODEL), F32)],
        compiler_params=_params("parallel", "arbitrary"),
        name="merge_out",
    )(oa, ob, oc, wa, wb, wc, zg, zg, zg, wo, x, mod_l, npost.reshape(1, D_MODEL), npre.reshape(1, D_MODEL),
      mod_l, mod_l)


def _ffn_kernel(h_ref, wa_ref, wg_ref, ca_ref, cg_ref, wd_ref, y_ref, *, tm):
    i = pl.program_id(0)

    @pl.when(pl.program_id(1) == 0)
    def _():
        y_ref[...] = jnp.zeros_like(y_ref)

    h = h_ref[...]
    seq_len = jnp.where(i * tm < T_CTX, SEQ, DEC_SEQ)
    pos = lax.broadcasted_iota(jnp.int32, (tm, 1), 0) & (seq_len - 1)
    has_prev = pos != 0
    has_next = pos != seq_len - 1

    def conv(u, c):
        prev = jnp.where(has_prev, pltpu.roll(u, 1, 0), 0.0)
        nxt = jnp.where(has_next, pltpu.roll(u, tm - 1, 0), 0.0)
        return c[0:1, :] * prev + c[1:2, :] * u + c[2:3, :] * nxt

    tf = wa_ref.shape[2]
    col = lax.broadcasted_iota(jnp.int32, (1, tf), 1)
    fresh = (pl.program_id(1) < pl.num_programs(1) - 1) | (col >= FFN_TILES * tf - D_FF)
    subs = (slice(0, tf // 2), slice(tf // 2, tf))
    ups = [(jnp.dot(h, wa_ref[0, :, cols], preferred_element_type=F32),
            jnp.dot(h, wg_ref[0, :, cols], preferred_element_type=F32)) for cols in subs]
    for cols, (ua, ug) in zip(subs, ups):
        act = conv(ua, ca_ref[:, cols]) * jax.nn.gelu(conv(ug, cg_ref[:, cols]))
        act = jnp.where(fresh[:, cols], act, 0.0).astype(BF16)
        y_ref[...] += jnp.dot(act, wd_ref[0, cols, :], preferred_element_type=F32)


def _ffn_tile_start(j, base=0):
    return LANE * (base // LANE + jnp.minimum(j * (FFN_TF // LANE), (D_FF - FFN_TF) // LANE))


def _ffn(h, w_up, conv_t, w_down, l):
    tm, tf, nj = 1024, FFN_TF, FFN_TILES
    one = pl.Element(1)
    return pl.pallas_call(
        functools.partial(_ffn_kernel, tm=tm),
        grid=(T_ALL // tm, nj),
        in_specs=[pl.BlockSpec((tm, D_MODEL), lambda i, j: (i, 0)),
                  pl.BlockSpec((one, pl.Element(D_MODEL), pl.Element(tf)), lambda i, j: (l, 0, _ffn_tile_start(j))),
                  pl.BlockSpec((one, pl.Element(D_MODEL), pl.Element(tf)),
                               lambda i, j: (l, 0, _ffn_tile_start(j, D_FF))),
                  pl.BlockSpec((None, CONV_W, tf), lambda i, j: (l, 0, j)),
                  pl.BlockSpec((None, CONV_W, tf), lambda i, j: (l, 0, nj + j)),
                  pl.BlockSpec((one, pl.Element(tf), pl.Element(D_MODEL)), lambda i, j: (l, _ffn_tile_start(j), 0))],
        out_specs=pl.BlockSpec((tm, D_MODEL), lambda i, j: (i, 0)),
        out_shape=jax.ShapeDtypeStruct((T_ALL, D_MODEL), F32),
        compiler_params=_params("parallel", "arbitrary"),
        name="conv_ffn",
    )(h, w_up, w_up, conv_t, conv_t, w_down)


def _pad_cols(w, n):
    return jnp.pad(w, [(0, 0)] * (w.ndim - 1) + [(0, n - w.shape[-1])])


def _prep_weights(w_in, mla_w_uq, mla_w_ukv, w_branch_a, w_branch_b, w_branch_c, w_out,
                  ffn_w_up, ffn_conv, ffn_w_down):
    w_uq = _pad_cols(mla_w_uq.reshape(DEPTH, Q_LORA, H_B, NOPE_B + ROPE_B), 2 * LANE)
    w_uq = w_uq.reshape(DEPTH, Q_LORA, H_B * 2 * LANE).astype(BF16)
    w_up = ffn_w_up.astype(BF16)
    w_down = ffn_w_down.astype(BF16)
    starts = [min(j * FFN_TF, D_FF - FFN_TF) for j in range(FFN_TILES)]
    conv = jnp.concatenate([ffn_conv[:, :, base + s:base + s + FFN_TF] for base in (0, D_FF) for s in starts],
                           axis=-1)
    return dict(w_in_t=jnp.swapaxes(w_in, 1, 2), w_uq=w_uq, w_ukv=mla_w_ukv.astype(BF16),
                w_a=w_branch_a.astype(BF16), w_b=w_branch_b.astype(BF16), w_c=w_branch_c.astype(BF16),
                w_o=w_out.astype(BF16), w_up=w_up, conv=conv, w_down=w_down)


def kernel(x_prompt, x_sample, state_hgrn, cache_mla_ckv, cache_mla_krope, cache_swa_k, cache_swa_v,
           c, c_ctx, w_mod, b_mod, norm_pre_attn, norm_post_attn, norm_pre_ffn, norm_post_ffn,
           w_in, hgrn_lb, hgrn_gnorm, mla_gq, mla_w_uq, mla_gkv, mla_w_ukv, swa_sink,
           w_branch_a, w_branch_b, w_branch_c, w_out, ffn_w_up, ffn_conv, ffn_w_down):
    wts = _prep_weights(w_in, mla_w_uq, mla_w_ukv, w_branch_a, w_branch_b, w_branch_c, w_out,
                        ffn_w_up, ffn_conv, ffn_w_down)
    cs = jnp.cumsum(jax.nn.softmax(hgrn_lb.astype(F32), axis=0), axis=0)
    lb_all = cs - cs[0]

    cvec = jnp.concatenate([c_ctx[None, :], c, jnp.zeros((MOD_ROWS - 1 - DEC_BATCH, D_MODEL), F32)], axis=0)
    mod = _modulation(cvec, w_mod, b_mod).reshape(DEPTH, MOD_ROWS, 6, 1, D_MODEL)

    hgrn_tables = _hgrn_tables()
    rope_b = _rope_tables(ROPE_B)
    rope_c = _rope_tables(HD_C)
    sink_b = jnp.broadcast_to(swa_sink[:, :, None], (DEPTH, H_C, LANE))

    x = jnp.concatenate([x_prompt.reshape(T_CTX, D_MODEL), x_sample.reshape(T_LAT, D_MODEL)], axis=0)
    new_hgrn, new_ckv, new_krope, new_k, new_v = [], [], [], [], []
    y = None
    for l in range(DEPTH):
        mod_l = mod[l]
        if l == 0:
            (h,) = _norm(x, npre=norm_pre_attn[l], mod_pre=mod_l, scale_idx=1, shift_idx=0)
        else:
            x, h = _norm(x, y=y, mod_post=mod[l - 1], gate_idx=5, npost=norm_post_ffn[l - 1],
                         npre=norm_pre_attn[l], mod_pre=mod_l, scale_idx=1, shift_idx=0)
        zab = _in_proj(h, wts["w_in_t"], l, 0, ZAB_W, "in_proj_ab")
        zcg = _in_proj(h, wts["w_in_t"], l, Z_B_END, ZC_G, "in_proj_c")
        zg = _in_proj(h, wts["w_in_t"], l, Z_B_END + ZC_G, N_BRANCH * D_MODEL, "in_proj_g", BF16)

        oa, s_ctx = _hgrn(zab, lb_all[l], hgrn_gnorm[l], hgrn_tables, latent=False)
        (oa,) = _hgrn(zab, lb_all[l], hgrn_gnorm[l], hgrn_tables, latent=True, s0=state_hgrn[:, l], prev=oa)
        new_hgrn.append(s_ctx)

        (qb,) = _norm_mm(zab, Z_B // Q_LORA, Q_LORA, mla_gq[l], wts["w_uq"], l, F32, emit_normed=False,
                         name="mla_q_proj")
        ckv, kvb = _norm_mm(zab, Z_KV // KV_LORA, KV_LORA, mla_gkv[l], wts["w_ukv"], l, BF16, emit_normed=True,
                            name="mla_kv_proj")
        kv_cache = _mm(cache_mla_ckv[:, l].reshape(DEC_BATCH * PAST_LEN, KV_LORA), wts["w_ukv"], l, BF16,
                       "mla_kv_cache")
        kr_cache = _pad_cols(cache_mla_krope[:, l].reshape(DEC_BATCH * PAST_LEN, ROPE_B), LANE)
        ob = _mla_attn(qb, kvb, zab, latent=False)
        ob = _mla_attn(qb, kvb, zab, latent=True, kvc=kv_cache, krc=kr_cache, tables=rope_b, prev=ob)
        new_ckv.append(ckv[:T_CTX].reshape(BATCH, SEQ, KV_LORA))
        new_krope.append(zab[:T_CTX, Z_KR:Z_B_END].reshape(BATCH, SEQ, ROPE_B))

        oc = _gqa_attn(zcg, sink_b[l], latent=False)
        oc = _gqa_attn(zcg, sink_b[l], latent=True,
                       kc=cache_swa_k[:, l].reshape(DEC_BATCH * PAST_LEN, KVH_C * HD_C),
                       vc=cache_swa_v[:, l].reshape(DEC_BATCH * PAST_LEN, KVH_C * HD_C), tables=rope_c, prev=oc)
        new_k.append(zcg[:T_CTX, ZC_K:ZC_V].reshape(BATCH, SEQ, KVH_C, HD_C))
        new_v.append(zcg[:T_CTX, ZC_V:ZC_G].reshape(BATCH, SEQ, KVH_C, HD_C))

        x, h = _merge(oa, ob, oc, wts["w_a"], wts["w_b"], wts["w_c"], zg, wts["w_o"], l,
                      x, mod_l, norm_post_attn[l], norm_pre_ffn[l])
        y = _ffn(h, wts["w_up"], wts["conv"], wts["w_down"], l)

    (x,) = _norm(x, y=y, mod_post=mod[DEPTH - 1], gate_idx=5, npost=norm_post_ffn[DEPTH - 1])
    return (x[:T_CTX].reshape(BATCH, SEQ, D_MODEL), x[T_CTX:].reshape(DEC_BATCH, DEC_SEQ, D_MODEL),
            jnp.stack(new_hgrn, axis=1), jnp.stack(new_ckv, axis=1), jnp.stack(new_krope, axis=1),
            jnp.stack(new_k, axis=1), jnp.stack(new_v, axis=1))
```

```python
import functools

import jax
import jax.numpy as jnp
import numpy as np
from jax import lax
from jax.experimental import pallas as pl
from jax.experimental.pallas import tpu as pltpu

F32 = jnp.float32
BF16 = jnp.bfloat16

D_MODEL = 2048
BATCH = 16
SEQ = 256
DEPTH = 4
DEC_BATCH = 4
DEC_SEQ = 1024
PAST_LEN = 256
GRID_W = 64
ROPE_BASE = 10000.0
EPS = 1e-6
NEG_INF = -1e30
H_A, DK_A, DV_A = 8, 128, 128
H_B, Q_LORA, KV_LORA, NOPE_B, ROPE_B, V_B = 8, 512, 256, 128, 64, 128
H_C, KVH_C, HD_C, WINDOW = 8, 2, 128, 128
N_BRANCH = 3
D_FF = 5504
CONV_W = 3

T_CTX = BATCH * SEQ
T_LAT = DEC_BATCH * DEC_SEQ
T_ALL = T_CTX + T_LAT
MOD_ROWS = 8
LANE = 128
CTX_SEQS = 4
FFN_TF = 512
FFN_TILES = -(-D_FF // FFN_TF)
HGRN_CHUNK = 64
Z_A = 0
Z_B = 5 * H_A * DK_A
Z_KV = Z_B + Q_LORA
Z_KR = Z_KV + KV_LORA
Z_B_END = Z_KR + ROPE_B
ZAB_W = 6144
ZC_K = H_C * HD_C
ZC_V = ZC_K + KVH_C * HD_C
ZC_G = ZC_V + KVH_C * HD_C
ZCG_W = ZC_G + N_BRANCH * D_MODEL
VMEM_LIMIT = 56 * 1024 * 1024


def _params(*sem, flags=None):
    return pltpu.CompilerParams(dimension_semantics=sem, vmem_limit_bytes=VMEM_LIMIT, flags=flags)


def _mod_row(i, tm):
    return jnp.where(i * tm < T_CTX, 0, 1 + (i * tm - T_CTX) // DEC_SEQ)


def _rms(x):
    return x * lax.rsqrt(jnp.mean(x * x, axis=-1, keepdims=True) + EPS)


def _mod_kernel(c_ref, w_ref, b_ref, o_ref):
    cv = c_ref[...]
    s = (cv * jax.nn.sigmoid(cv)).astype(BF16)
    o_ref[0] = jnp.dot(s, w_ref[0].astype(BF16), preferred_element_type=F32) + b_ref[0]


def _modulation(cvec, w_mod, b_mod):
    tn = 1024
    n = 6 * D_MODEL
    return pl.pallas_call(
        _mod_kernel,
        grid=(DEPTH, n // tn),
        in_specs=[pl.BlockSpec((MOD_ROWS, D_MODEL), lambda l, j: (0, 0)),
                  pl.BlockSpec((1, D_MODEL, tn), lambda l, j: (l, 0, j)),
                  pl.BlockSpec((1, 1, tn), lambda l, j: (l, 0, j))],
        out_specs=pl.BlockSpec((1, MOD_ROWS, tn), lambda l, j: (l, 0, j)),
        out_shape=jax.ShapeDtypeStruct((DEPTH, MOD_ROWS, n), F32),
        compiler_params=_params("parallel", "parallel"),
        name="modulation",
    )(cvec, w_mod, b_mod.reshape(DEPTH, 1, n))


def _norm_kernel(*refs, has_y, has_h):
    it = iter(refs)
    x_ref = next(it)
    if has_y:
        y_ref, gate_ref, npost_ref = next(it), next(it), next(it)
    if has_h:
        npre_ref, scale_ref, shift_ref = next(it), next(it), next(it)
    x = x_ref[...]
    if has_y:
        xnew_ref = next(it)
        x = x + gate_ref[0, 0] * (_rms(y_ref[...]) * npost_ref[...])
        xnew_ref[...] = x
    if has_h:
        h_ref = next(it)
        h = (_rms(x) * npre_ref[...]) * (1.0 + scale_ref[0, 0]) + shift_ref[0, 0]
        h_ref[...] = h.astype(BF16)


def _norm(x, *, y=None, mod_post=None, gate_idx=None, npost=None,
          npre=None, mod_pre=None, scale_idx=None, shift_idx=None):
    tm = 512
    has_y, has_h = y is not None, npre is not None
    row = pl.BlockSpec((tm, D_MODEL), lambda i: (i, 0))
    vec = pl.BlockSpec((1, D_MODEL), lambda i: (0, 0))

    def modspec(k):
        return pl.BlockSpec((1, 1, 1, D_MODEL), lambda i: (_mod_row(i, tm), k, 0, 0))

    args, specs, out_shape, out_specs = [x], [row], [], []
    if has_y:
        args += [y, mod_post, npost.reshape(1, D_MODEL)]
        specs += [row, modspec(gate_idx), vec]
        out_shape.append(jax.ShapeDtypeStruct((T_ALL, D_MODEL), F32))
        out_specs.append(row)
    if has_h:
        args += [npre.reshape(1, D_MODEL), mod_pre, mod_pre]
        specs += [vec, modspec(scale_idx), modspec(shift_idx)]
        out_shape.append(jax.ShapeDtypeStruct((T_ALL, D_MODEL), BF16))
        out_specs.append(row)
    outs = pl.pallas_call(
        functools.partial(_norm_kernel, has_y=has_y, has_h=has_h),
        grid=(T_ALL // tm,),
        in_specs=specs, out_specs=out_specs, out_shape=out_shape,
        compiler_params=_params("parallel"),
        name="norm_y%d_h%d" % (has_y, has_h),
    )(*args)
    return outs


def _mm_kernel(x_ref, w_ref, o_ref):
    o_ref[...] = jnp.dot(x_ref[...].astype(BF16), w_ref[...].astype(BF16),
                         preferred_element_type=F32).astype(o_ref.dtype)


def _mm(x, w, l, out_dtype, name, n=None):
    m, k = x.shape
    n = w.shape[2] if n is None else n
    tm = min(m, 1024)
    tn = min(n, 512)
    return pl.pallas_call(
        _mm_kernel,
        grid=(m // tm, n // tn),
        in_specs=[pl.BlockSpec((tm, k), lambda i, j: (i, 0)),
                  pl.BlockSpec((None, k, tn), lambda i, j: (l, 0, j))],
        out_specs=pl.BlockSpec((tm, tn), lambda i, j: (i, j)),
        out_shape=jax.ShapeDtypeStruct((m, n), out_dtype),
        compiler_params=_params("parallel", "parallel"),
        name=name,
    )(x, w)


def _mm_nt_kernel(x_ref, wt_ref, o_ref):
    o_ref[...] = lax.dot_general(x_ref[...], wt_ref[...].astype(BF16), (((1,), (1,)), ((), ())),
                                 preferred_element_type=F32).astype(o_ref.dtype)


def _in_proj(h, w_t, l, col0, n, name, out_dtype=F32):
    m, k = h.shape
    tm, tn = 2048, 512
    row0 = l * w_t.shape[1] + col0
    return pl.pallas_call(
        _mm_nt_kernel,
        grid=(m // tm, n // tn),
        in_specs=[pl.BlockSpec((tm, k), lambda i, j: (i, 0)),
                  pl.BlockSpec((pl.Element(tn), pl.Element(k)),
                               lambda i, j: ((row0 // 8 + j * (tn // 8)) * 8, 0))],
        out_specs=pl.BlockSpec((tm, tn), lambda i, j: (i, j)),
        out_shape=jax.ShapeDtypeStruct((m, n), out_dtype),
        compiler_params=_params("parallel", "parallel"),
        name=name,
    )(h, w_t.reshape(-1, k))


def _norm_mm_kernel(x_ref, g_ref, w_ref, *out_refs, emit_normed):
    xn = _rms(x_ref[...]) * g_ref[...]
    if emit_normed:
        out_refs[0][...] = xn
    out_refs[-1][...] = jnp.dot(xn.astype(BF16), w_ref[...],
                                preferred_element_type=F32).astype(out_refs[-1].dtype)


def _norm_mm(z, col_block, k, gain, w, l, out_dtype, *, emit_normed, name):
    m = z.shape[0]
    n = w.shape[2]
    tm = 512
    out_shape = [jax.ShapeDtypeStruct((m, n), out_dtype)]
    out_specs = [pl.BlockSpec((tm, n), lambda i: (i, 0))]
    if emit_normed:
        out_shape.insert(0, jax.ShapeDtypeStruct((m, k), F32))
        out_specs.insert(0, pl.BlockSpec((tm, k), lambda i: (i, 0)))
    return pl.pallas_call(
        functools.partial(_norm_mm_kernel, emit_normed=emit_normed),
        grid=(m // tm,),
        in_specs=[pl.BlockSpec((tm, k), lambda i: (i, col_block)),
                  pl.BlockSpec((1, k), lambda i: (0, 0)),
                  pl.BlockSpec((None, k, n), lambda i: (l, 0, 0))],
        out_specs=out_specs, out_shape=out_shape,
        compiler_params=_params("parallel"),
        name=name,
    )(z, gain.reshape(1, k), w)


def _rope_tables(head_dim):
    n = DEC_SEQ
    half = head_dim // 2
    quarter = half // 2
    row = jnp.repeat(jnp.arange(n // GRID_W), GRID_W).astype(F32)
    col = jnp.tile(jnp.arange(GRID_W), n // GRID_W).astype(F32)
    inv = ROPE_BASE ** (-jnp.arange(0, half, 2, dtype=F32) / half)
    lane = jnp.arange(LANE)
    m = lane % half
    pos = jnp.where((lane // half)[None, :] == 0, row[:, None], col[:, None])
    ang = pos * inv[m % quarter][None, :]
    valid = (lane < head_dim)[None, :]
    cos = jnp.where(valid, jnp.cos(ang), 0.0)
    sin = jnp.where(valid, jnp.where(m < quarter, -1.0, 1.0)[None, :] * jnp.sin(ang), 0.0)
    return cos.astype(F32), sin.astype(F32)


def _rope(x, cos, sin, head_dim):
    quarter = head_dim // 4
    lane = lax.broadcasted_iota(jnp.int32, x.shape, 1)
    first = (lane % (2 * quarter)) < quarter
    partner = jnp.where(first, pltpu.roll(x, LANE - quarter, 1), pltpu.roll(x, quarter, 1))
    return x * cos + partner * sin


def _mla_attn_kernel(*refs, latent, n):
    kv_ref = refs[1]
    scale = (NOPE_B + ROPE_B) ** -0.5
    nt = (((1,), (1,)), ((), ()))
    for r0 in range(0, kv_ref.shape[0], n):
        rows = slice(r0, r0 + n)
        _mla_attn_seq(refs, latent, rows, rows if not latent else slice(None), scale, nt)


def _mla_attn_seq(refs, latent, krows, qrows, scale, nt):
    if latent:
        q_ref, kv_ref, kr_ref, kvc_ref, krc_ref, cq_ref, sq_ref, ck_ref, sk_ref, _, o_ref = refs
    else:
        q_ref, kv_ref, kr_ref, _, o_ref = refs
    kr = kr_ref[krows, :]
    if latent:
        kr = _rope(kr, ck_ref[...], sk_ref[...], ROPE_B)
        krc = krc_ref[...].astype(BF16)
    kr = kr.astype(BF16)
    for h in range(H_B):
        c0 = h * 2 * LANE
        qn = q_ref[qrows, c0:c0 + LANE]
        qr = q_ref[qrows, c0 + LANE:c0 + 2 * LANE]
        if latent:
            qr = _rope(qr, cq_ref[...], sq_ref[...], ROPE_B)
        qh = jnp.concatenate([(qn * scale).astype(BF16), (qr * scale).astype(BF16)], axis=-1)
        kh = jnp.concatenate([kv_ref[krows, c0:c0 + LANE], kr], axis=-1)
        vh = kv_ref[krows, c0 + LANE:c0 + 2 * LANE]
        s = lax.dot_general(qh, kh, nt, preferred_element_type=F32)
        m = jnp.max(s, axis=-1, keepdims=True)
        if latent:
            khc = jnp.concatenate([kvc_ref[:, c0:c0 + LANE], krc], axis=-1)
            vhc = kvc_ref[:, c0 + LANE:c0 + 2 * LANE]
            sc = lax.dot_general(qh, khc, nt, preferred_element_type=F32)
            m = jnp.maximum(m, jnp.max(sc, axis=-1, keepdims=True))
        p = jnp.exp(s - m)
        l = jnp.sum(p, axis=-1, keepdims=True)
        o = jnp.dot(p.astype(BF16), vh, preferred_element_type=F32)
        if latent:
            pc = jnp.exp(sc - m)
            l = l + jnp.sum(pc, axis=-1, keepdims=True)
            o = o + jnp.dot(pc.astype(BF16), vhc, preferred_element_type=F32)
        o_ref[qrows, h * LANE:(h + 1) * LANE] = (o / l).astype(BF16)


def _rows_of(specs, args, prev, width):
    if prev is None:
        prev = jnp.zeros((T_ALL, width), BF16)
    return specs + [pl.BlockSpec(memory_space=pl.ANY)], args + [prev], {len(args): 0}


def _mla_attn(q, kv, z, *, latent, kvc=None, krc=None, tables=None, prev=None):
    nb, n = (DEC_BATCH, DEC_SEQ) if latent else (BATCH // CTX_SEQS, SEQ)
    seqs = 1 if latent else CTX_SEQS
    tq = 256 * seqs
    nq = n * seqs // tq
    off = T_CTX // n if latent else 0
    offq = T_CTX // tq if latent else 0
    w = H_B * 2 * LANE
    specs = [pl.BlockSpec((tq, w), lambda b, i: (offq + b * nq + i, 0)),
             pl.BlockSpec((n * seqs, w), lambda b, i: (off + b, 0)),
             pl.BlockSpec((n * seqs, LANE), lambda b, i: (off + b, Z_KR // LANE))]
    args = [q, kv, z]
    if latent:
        cos, sin = tables
        specs += [pl.BlockSpec((PAST_LEN, w), lambda b, i: (b, 0)),
                  pl.BlockSpec((PAST_LEN, LANE), lambda b, i: (b, 0)),
                  pl.BlockSpec((tq, LANE), lambda b, i: (i, 0)),
                  pl.BlockSpec((tq, LANE), lambda b, i: (i, 0)),
                  pl.BlockSpec((n, LANE), lambda b, i: (0, 0)),
                  pl.BlockSpec((n, LANE), lambda b, i: (0, 0))]
        args += [kvc, krc, cos, sin, cos, sin]
    specs, args, aliases = _rows_of(specs, args, prev, H_B * V_B)
    return pl.pallas_call(
        functools.partial(_mla_attn_kernel, latent=latent, n=n),
        grid=(nb, nq),
        in_specs=specs,
        out_specs=pl.BlockSpec((tq, H_B * V_B), lambda b, i: (offq + b * nq + i, 0)),
        out_shape=jax.ShapeDtypeStruct((T_ALL, H_B * V_B), BF16),
        input_output_aliases=aliases,
        compiler_params=_params("parallel", "parallel"),
        name="mla_attn_lat" if latent else "mla_attn_ctx",
    )(*args)


def _gqa_attn_kernel(*refs, latent, tq):
    if latent:
        q_ref, k_ref, v_ref, kc_ref, vc_ref, sink_ref, cq_ref, sq_ref, ck_ref, sk_ref, _, o_ref = refs
    else:
        q_ref, k_ref, v_ref, sink_ref, _, o_ref = refs
    scale = HD_C ** -0.5
    nt = (((1,), (1,)), ((), ()))
    rep = H_C // KVH_C
    n = k_ref.shape[0]
    if latent:
        kw = tq + 2 * WINDOW
        q0 = pl.program_id(1) * tq
        k0 = pl.multiple_of(jnp.clip(q0 - WINDOW, 0, n - kw), WINDOW)
        keys = pl.ds(k0, kw)
        qpos = q0 + lax.broadcasted_iota(jnp.int32, (tq, kw), 0)
        kpos = k0 + lax.broadcasted_iota(jnp.int32, (tq, kw), 1)
        band = jnp.abs(qpos - kpos) <= WINDOW
        row_sets = [(keys, slice(None))]
    else:
        row_sets = [(slice(r0, r0 + SEQ),) * 2 for r0 in range(0, n, SEQ)]
    for keys, qrows in row_sets:
        _gqa_attn_seq(refs, latent, keys, qrows, band if latent else None, scale, nt, rep)


def _gqa_attn_seq(refs, latent, keys, qrows, band, scale, nt, rep):
    if latent:
        q_ref, k_ref, v_ref, kc_ref, vc_ref, sink_ref, cq_ref, sq_ref, ck_ref, sk_ref, _, o_ref = refs
    else:
        q_ref, k_ref, v_ref, sink_ref, _, o_ref = refs
    for g in range(KVH_C):
        kg = k_ref[keys, g * LANE:(g + 1) * LANE]
        if latent:
            kg = _rope(kg, ck_ref[keys, :], sk_ref[keys, :], HD_C)
            kcg = kc_ref[:, g * LANE:(g + 1) * LANE].astype(BF16)
            vcg = vc_ref[:, g * LANE:(g + 1) * LANE].astype(BF16)
        kg = kg.astype(BF16)
        vg = v_ref[keys, g * LANE:(g + 1) * LANE].astype(BF16)
        for r in range(rep):
            h = g * rep + r
            qh = q_ref[qrows, h * LANE:(h + 1) * LANE]
            if latent:
                qh = _rope(qh, cq_ref[...], sq_ref[...], HD_C)
            qh = (qh * scale).astype(BF16)
            sk = sink_ref[h:h + 1, 0:1]
            s = lax.dot_general(qh, kg, nt, preferred_element_type=F32)
            if latent:
                s = jnp.where(band, s, NEG_INF)
            m = jnp.maximum(jnp.max(s, axis=-1, keepdims=True), sk)
            if latent:
                sc = lax.dot_general(qh, kcg, nt, preferred_element_type=F32)
                m = jnp.maximum(m, jnp.max(sc, axis=-1, keepdims=True))
            p = jnp.exp(s - m)
            l = jnp.sum(p, axis=-1, keepdims=True) + jnp.exp(sk - m)
            o = jnp.dot(p.astype(BF16), vg, preferred_element_type=F32)
            if latent:
                pc = jnp.exp(sc - m)
                l = l + jnp.sum(pc, axis=-1, keepdims=True)
                o = o + jnp.dot(pc.astype(BF16), vcg, preferred_element_type=F32)
            o_ref[qrows, h * LANE:(h + 1) * LANE] = (o / l).astype(BF16)


def _gqa_attn(z, sink_b, *, latent, kc=None, vc=None, tables=None, prev=None):
    nb, n = (DEC_BATCH, DEC_SEQ) if latent else (BATCH // CTX_SEQS, SEQ * CTX_SEQS)
    tq = 256 if latent else n
    nq = n // tq
    off = T_CTX // n if latent else 0
    offq = T_CTX // tq if latent else 0
    wq, wk = H_C * HD_C, KVH_C * HD_C
    specs = [pl.BlockSpec((tq, wq), lambda b, i: (offq + b * nq + i, 0)),
             pl.BlockSpec((n, wk), lambda b, i: (off + b, ZC_K // wk)),
             pl.BlockSpec((n, wk), lambda b, i: (off + b, ZC_V // wk))]
    args = [z, z, z]
    if latent:
        specs += [pl.BlockSpec((PAST_LEN, wk), lambda b, i: (b, 0)),
                  pl.BlockSpec((PAST_LEN, wk), lambda b, i: (b, 0))]
        args += [kc, vc]
    specs.append(pl.BlockSpec((H_C, LANE), lambda b, i: (0, 0)))
    args.append(sink_b)
    if latent:
        cos, sin = tables
        specs += [pl.BlockSpec((tq, LANE), lambda b, i: (i, 0)),
                  pl.BlockSpec((tq, LANE), lambda b, i: (i, 0)),
                  pl.BlockSpec((n, LANE), lambda b, i: (0, 0)),
                  pl.BlockSpec((n, LANE), lambda b, i: (0, 0))]
        args += [cos, sin, cos, sin]
    specs, args, aliases = _rows_of(specs, args, prev, wq)
    return pl.pallas_call(
        functools.partial(_gqa_attn_kernel, latent=latent, tq=tq),
        grid=(nb, nq),
        in_specs=specs,
        out_specs=pl.BlockSpec((tq, wq), lambda b, i: (offq + b * nq + i, 0)),
        out_shape=jax.ShapeDtypeStruct((T_ALL, wq), BF16),
        input_output_aliases=aliases,
        compiler_params=_params("parallel", "parallel"),
        name="gqa_attn_lat" if latent else "gqa_attn_ctx",
    )(*args)


def _hgrn_tables():
    c = HGRN_CHUNK
    halves = [c >> (i + 1) for i in range(c.bit_length() - 1)]
    out = []
    for forward in (True, False):
        sums = np.zeros((len(halves) + 1, c, c), np.float32)
        level = np.full((c, c), -1, np.int32)
        level[np.arange(c), np.arange(c)] = 0
        for li, m in enumerate(halves):
            for r in range(c):
                pos = r % (2 * m)
                mid = r - pos + m
                late = pos >= m
                if forward:
                    lo, hi = (mid, r + 1) if late else (r + 1, mid)
                else:
                    lo, hi = (mid, r) if late else (r, mid)
                sums[li, r, lo:hi] = 1.0
                for s in range(r - pos, r - pos + 2 * m):
                    s_late = (s % (2 * m)) >= m
                    if (late and not s_late) if forward else (not late and s_late):
                        level[r, s] = li + 1
        for r in range(c):
            if forward:
                sums[-1, r, :r + 1] = 1.0
            else:
                sums[-1, r, r:] = 1.0
        sums = sums.reshape(-1, c)
        out.append((jnp.asarray(np.concatenate([sums, sums, sums], axis=1), BF16),
                    jnp.asarray(np.concatenate([level, level], axis=1))))
    return out


def _hgrn_kernel(*refs, n, has_s0, emit_state):
    it = iter(refs)
    q_ref, xf_ref, xb_ref, v_ref, ag_ref, lb_ref, gn_ref = (next(it) for _ in range(7))
    sums_refs = (next(it), next(it))
    level_refs = (next(it), next(it))
    s0_ref = next(it) if has_s0 else None
    next(it)
    o_ref = next(it)
    sfin_ref = next(it) if emit_state else None
    o_scr, qe_scr, u_scr, e_scr, st_scr = (next(it) for _ in range(5))

    c = HGRN_CHUNK
    nc = n // c
    nlev = c.bit_length() - 1
    nt = (((1,), (1,)), ((), ()))
    tn = (((0,), (0,)), ((), ()))
    zero = jnp.zeros((c, LANE), BF16)

    def blockdiag(x):
        return jnp.concatenate([jnp.concatenate([x[:, :LANE], zero], axis=1),
                                jnp.concatenate([zero, x[:, LANE:]], axis=1)], axis=0)

    def gates(x, lb):
        e = jnp.exp(-jnp.abs(x))
        big = 1.0 / (1.0 + e)
        small = e * big
        pos = x >= 0.0
        return jnp.log(lb + (1.0 - lb) * jnp.where(pos, big, small)), (1.0 - lb) * jnp.where(pos, small, big)

    for d in range(2):
        for hh in range(2):
            st_scr[d, hh] = s0_ref[0, d, hh].T if has_s0 else jnp.zeros((DV_A, DK_A), F32)

    group = 4

    def intra(t, carry):
        jobs = [(u, d) for u in range(group) for d in range(2)]
        chunk_of = [t * group + u for u in range(group)]
        rows = {u: pl.ds(pl.multiple_of(chunk_of[u] * c, c), c) for u in range(group)}
        q = {ci: q_ref[rows[ci], :] for ci, _ in jobs}
        v = {ci: v_ref[rows[ci], :].astype(BF16) for ci, _ in jobs}
        k = {}

        dall = {}
        for ci, d in jobs:
            g, k[ci, d] = gates((xf_ref, xb_ref)[d][rows[ci], :], lb_ref[d:d + 1, :])
            g_hi = g.astype(BF16)
            rem = g - g_hi.astype(F32)
            g_mid = rem.astype(BF16)
            g_lo = (rem - g_mid.astype(F32)).astype(BF16)
            dall[ci, d] = jnp.dot(sums_refs[d][...], jnp.concatenate([g_hi, g_mid, g_lo], axis=0),
                                  preferred_element_type=F32)

        scores = {}
        for ci, d in jobs:
            rs = [lax.dot_general(q[ci].astype(BF16), blockdiag(k[ci, d].astype(BF16)), nt,
                                  preferred_element_type=F32)]
            for li in range(nlev):
                e = jnp.exp(dall[ci, d][li * c:(li + 1) * c, :])
                rs.append(lax.dot_general((q[ci] * e).astype(BF16), blockdiag((k[ci, d] * e).astype(BF16)), nt,
                                          preferred_element_type=F32))
            scores[ci, d] = rs

        for ci, d in jobs:
            level = level_refs[d][...]
            a = jnp.where(level == 0, scores[ci, d][0], 0.0)
            for li in range(nlev):
                a = jnp.where(level == li + 1, scores[ci, d][li + 1], a)
            o_scr[d, rows[ci], :] = jnp.dot(a.astype(BF16), blockdiag(v[ci]), preferred_element_type=F32)
            gc = dall[ci, d][nlev * c:, :]
            g_end = gc[c - 1:c, :] if d == 0 else gc[0:1, :]
            qe_scr[d, rows[ci], :] = (q[ci] * jnp.exp(gc)).astype(BF16)
            kd = (k[ci, d] * jnp.exp(g_end - gc)).astype(BF16)
            e_scr[d, chunk_of[ci]] = jnp.broadcast_to(jnp.exp(g_end), (8, 2 * LANE))
            for hh in range(2):
                hl = slice(hh * LANE, (hh + 1) * LANE)
                u_scr[d, chunk_of[ci], hh] = lax.dot_general(v[ci][:, hl], kd[:, hl], tn,
                                                             preferred_element_type=F32)
        return carry

    lax.fori_loop(0, nc // group, intra, 0)

    def inter(i, carry):
        for d in range(2):
            ci = i if d == 0 else nc - 1 - i
            rows = pl.ds(pl.multiple_of(ci * c, c), c)
            e = e_scr[d, ci]
            for hh in range(2):
                hl = slice(hh * LANE, (hh + 1) * LANE)
                st = st_scr[d, hh]
                o_scr[d, rows, hl] += lax.dot_general(qe_scr[d, rows, hl], st.astype(BF16), nt,
                                                      preferred_element_type=F32)
                st_scr[d, hh] = st * e[0:1, hl] + u_scr[d, ci, hh]
        return carry

    lax.fori_loop(0, nc, inter, 0, unroll=4)

    def finish(i, carry):
        rows = pl.ds(pl.multiple_of(i * c, c), c)
        o = o_scr[0, rows, :] + o_scr[1, rows, :]
        o = jnp.concatenate([_rms(o[:, :LANE]), _rms(o[:, LANE:])], axis=1)
        ag = ag_ref[rows, :]
        o_ref[rows, :] = (o * gn_ref[...] * (ag * jax.nn.sigmoid(ag))).astype(BF16)
        return carry

    lax.fori_loop(0, nc, finish, 0, unroll=4)
    if emit_state:
        for d in range(2):
            for hh in range(2):
                sfin_ref[0, d, hh] = st_scr[d, hh].T


def _hgrn(z, lb_l, gnorm, tables, *, latent, s0=None, prev=None):
    nb, n = (DEC_BATCH, DEC_SEQ) if latent else (BATCH, SEQ)
    off = T_CTX // n if latent else 0
    emit_state = not latent
    w = 2 * LANE
    pairs = H_A // 2
    c = HGRN_CHUNK
    nc = n // c

    def zspec(k):
        return pl.BlockSpec((n, w), lambda b, p: (off + b, Z_A // w + k * pairs + p))

    def const(x):
        return pl.BlockSpec(x.shape, lambda b, p: (0, 0))

    (sums_f, level_f), (sums_b, level_b) = tables
    specs = [zspec(0), zspec(1), zspec(2), zspec(3), zspec(4),
             pl.BlockSpec((2, w), lambda b, p: (0, p)),
             pl.BlockSpec((1, w), lambda b, p: (0, 0)),
             const(sums_f), const(sums_b), const(level_f), const(level_b)]
    args = [z, z, z, z, z, lb_l, jnp.tile(gnorm.reshape(1, DV_A), (1, 2)),
            sums_f, sums_b, level_f, level_b]
    if latent:
        specs.append(pl.BlockSpec((1, 2, 2, DK_A, DV_A), lambda b, p: (b, 0, p, 0, 0)))
        args.append(s0)
    specs, args, aliases = _rows_of(specs, args, prev, H_A * DV_A)
    out_shape = [jax.ShapeDtypeStruct((T_ALL, H_A * DV_A), BF16)]
    out_specs = [pl.BlockSpec((n, w), lambda b, p: (off + b, p))]
    if emit_state:
        out_shape.append(jax.ShapeDtypeStruct((nb, 2, H_A, DK_A, DV_A), F32))
        out_specs.append(pl.BlockSpec((1, 2, 2, DK_A, DV_A), lambda b, p: (b, 0, p, 0, 0)))
    scratch = [pltpu.VMEM((2, n, w), F32),
               pltpu.VMEM((2, n, w), BF16),
               pltpu.VMEM((2, nc, 2, DV_A, DK_A), F32),
               pltpu.VMEM((2, nc, 8, w), F32),
               pltpu.VMEM((2, 2, DV_A, DK_A), F32)]
    return pl.pallas_call(
        functools.partial(_hgrn_kernel, n=n, has_s0=latent, emit_state=emit_state),
        grid=(nb, pairs),
        in_specs=specs, out_specs=out_specs, out_shape=out_shape,
        scratch_shapes=scratch,
        input_output_aliases=aliases,
        compiler_params=_params("parallel", "parallel"),
        name="hgrn_lat" if latent else "hgrn_ctx",
    )(*args)


def _merge_kernel(oa_ref, ob_ref, oc_ref, wa_ref, wb_ref, wc_ref, g0_ref, g1_ref, g2_ref, wo_ref,
                  x_ref, gate_ref, npost_ref, npre_ref, scale_ref, shift_ref, xnew_ref, h_ref, y_scr):
    j = pl.program_id(1)

    @pl.when(j == 0)
    def _():
        y_scr[...] = jnp.zeros_like(y_scr)

    def gate(g_ref):
        return jax.nn.sigmoid(g_ref[...].astype(F32))

    merged = (gate(g0_ref) * jnp.dot(oa_ref[...], wa_ref[...], preferred_element_type=F32)
              + gate(g1_ref) * jnp.dot(ob_ref[...], wb_ref[...], preferred_element_type=F32)
              + gate(g2_ref) * jnp.dot(oc_ref[...], wc_ref[...], preferred_element_type=F32))
    y_scr[...] += jnp.dot(merged.astype(BF16), wo_ref[...], preferred_element_type=F32)

    @pl.when(j == pl.num_programs(1) - 1)
    def _():
        x = x_ref[...] + gate_ref[0, 0] * (_rms(y_scr[...]) * npost_ref[...])
        xnew_ref[...] = x
        h_ref[...] = ((_rms(x) * npre_ref[...]) * (1.0 + scale_ref[0, 0]) + shift_ref[0, 0]).astype(BF16)


def _merge(oa, ob, oc, wa, wb, wc, zg, wo, l, x, mod_l, npost, npre):
    tm, tn = 512, 512
    nj = D_MODEL // tn
    kb = H_A * DV_A
    o_spec = pl.BlockSpec((tm, kb), lambda i, j: (i, 0))
    w_spec = pl.BlockSpec((None, kb, tn), lambda i, j: (l, 0, j))
    row = pl.BlockSpec((tm, D_MODEL), lambda i, j: (i, 0))
    vec = pl.BlockSpec((1, D_MODEL), lambda i, j: (0, 0))

    def gspec(k):
        return pl.BlockSpec((tm, tn), lambda i, j: (i, k * nj + j))

    def modspec(k):
        return pl.BlockSpec((1, 1, 1, D_MODEL), lambda i, j: (_mod_row(i, tm), k, 0, 0))

    return pl.pallas_call(
        _merge_kernel,
        grid=(T_ALL // tm, nj),
        in_specs=[o_spec, o_spec, o_spec, w_spec, w_spec, w_spec, gspec(0), gspec(1), gspec(2),
                  pl.BlockSpec((None, tn, D_MODEL), lambda i, j: (l, j, 0)),
                  row, modspec(2), vec, vec, modspec(4), modspec(3)],
        out_specs=[row, row],
        out_shape=[jax.ShapeDtypeStruct((T_ALL, D_MODEL), F32), jax.ShapeDtypeStruct((T_ALL, D_MODEL), BF16)],
        scratch_shapes=[pltpu.VMEM((tm, D_MODEL), F32)],
        compiler_params=_params("parallel", "arbitrary"),
        name="merge_out",
    )(oa, ob, oc, wa, wb, wc, zg, zg, zg, wo, x, mod_l, npost.reshape(1, D_MODEL), npre.reshape(1, D_MODEL),
      mod_l, mod_l)


def _ffn_kernel(h_ref, wa_ref, wg_ref, ca_ref, cg_ref, wd_ref, y_ref, *, tm):
    i = pl.program_id(0)

    @pl.when(pl.program_id(1) == 0)
    def _():
        y_ref[...] = jnp.zeros_like(y_ref)

    h = h_ref[...]
    seq_len = jnp.where(i * tm < T_CTX, SEQ, DEC_SEQ)
    pos = lax.broadcasted_iota(jnp.int32, (tm, 1), 0) & (seq_len - 1)
    has_prev = pos != 0
    has_next = pos != seq_len - 1

    def conv(u, c):
        prev = jnp.where(has_prev, pltpu.roll(u, 1, 0), 0.0)
        nxt = jnp.where(has_next, pltpu.roll(u, tm - 1, 0), 0.0)
        return c[0:1, :] * prev + c[1:2, :] * u + c[2:3, :] * nxt

    tf = wa_ref.shape[2]
    col = lax.broadcasted_iota(jnp.int32, (1, tf), 1)
    fresh = (pl.program_id(1) < pl.num_programs(1) - 1) | (col >= FFN_TILES * tf - D_FF)
    subs = (slice(0, tf // 2), slice(tf // 2, tf))
    ups = [(jnp.dot(h, wa_ref[0, :, cols], preferred_element_type=F32),
            jnp.dot(h, wg_ref[0, :, cols], preferred_element_type=F32)) for cols in subs]
    for cols, (ua, ug) in zip(subs, ups):
        act = conv(ua, ca_ref[:, cols]) * jax.nn.gelu(conv(ug, cg_ref[:, cols]))
        act = jnp.where(fresh[:, cols], act, 0.0).astype(BF16)
        y_ref[...] += jnp.dot(act, wd_ref[0, cols, :], preferred_element_type=F32)


def _ffn_tile_start(j, base=0):
    return LANE * (base // LANE + jnp.minimum(j * (FFN_TF // LANE), (D_FF - FFN_TF) // LANE))


def _ffn(h, w_up, conv_t, w_down, l):
    tm, tf, nj = 1024, FFN_TF, FFN_TILES
    one = pl.Element(1)
    return pl.pallas_call(
        functools.partial(_ffn_kernel, tm=tm),
        grid=(T_ALL // tm, nj),
        in_specs=[pl.BlockSpec((tm, D_MODEL), lambda i, j: (i, 0)),
                  pl.BlockSpec((one, pl.Element(D_MODEL), pl.Element(tf)), lambda i, j: (l, 0, _ffn_tile_start(j))),
                  pl.BlockSpec((one, pl.Element(D_MODEL), pl.Element(tf)),
                               lambda i, j: (l, 0, _ffn_tile_start(j, D_FF))),
                  pl.BlockSpec((None, CONV_W, tf), lambda i, j: (l, 0, j)),
                  pl.BlockSpec((None, CONV_W, tf), lambda i, j: (l, 0, nj + j)),
                  pl.BlockSpec((one, pl.Element(tf), pl.Element(D_MODEL)), lambda i, j: (l, _ffn_tile_start(j), 0))],
        out_specs=pl.BlockSpec((tm, D_MODEL), lambda i, j: (i, 0)),
        out_shape=jax.ShapeDtypeStruct((T_ALL, D_MODEL), F32),
        compiler_params=_params("parallel", "arbitrary"),
        name="conv_ffn",
    )(h, w_up, w_up, conv_t, conv_t, w_down)


def _pad_cols(w, n):
    return jnp.pad(w, [(0, 0)] * (w.ndim - 1) + [(0, n - w.shape[-1])])


def _prep_weights(w_in, mla_w_uq, mla_w_ukv, w_branch_a, w_branch_b, w_branch_c, w_out,
                  ffn_w_up, ffn_conv, ffn_w_down):
    w_uq = _pad_cols(mla_w_uq.reshape(DEPTH, Q_LORA, H_B, NOPE_B + ROPE_B), 2 * LANE)
    w_uq = w_uq.reshape(DEPTH, Q_LORA, H_B * 2 * LANE).astype(BF16)
    w_up = ffn_w_up.astype(BF16)
    w_down = ffn_w_down.astype(BF16)
    starts = [min(j * FFN_TF, D_FF - FFN_TF) for j in range(FFN_TILES)]
    conv = jnp.concatenate([ffn_conv[:, :, base + s:base + s + FFN_TF] for base in (0, D_FF) for s in starts],
                           axis=-1)
    return dict(w_in_t=jnp.swapaxes(w_in, 1, 2), w_uq=w_uq, w_ukv=mla_w_ukv.astype(BF16),
                w_a=w_branch_a.astype(BF16), w_b=w_branch_b.astype(BF16), w_c=w_branch_c.astype(BF16),
                w_o=w_out.astype(BF16), w_up=w_up, conv=conv, w_down=w_down)


def kernel(x_prompt, x_sample, state_hgrn, cache_mla_ckv, cache_mla_krope, cache_swa_k, cache_swa_v,
           c, c_ctx, w_mod, b_mod, norm_pre_attn, norm_post_attn, norm_pre_ffn, norm_post_ffn,
           w_in, hgrn_lb, hgrn_gnorm, mla_gq, mla_w_uq, mla_gkv, mla_w_ukv, swa_sink,
           w_branch_a, w_branch_b, w_branch_c, w_out, ffn_w_up, ffn_conv, ffn_w_down):
    wts = _prep_weights(w_in, mla_w_uq, mla_w_ukv, w_branch_a, w_branch_b, w_branch_c, w_out,
                        ffn_w_up, ffn_conv, ffn_w_down)
    cs = jnp.cumsum(jax.nn.softmax(hgrn_lb.astype(F32), axis=0), axis=0)
    lb_all = cs - cs[0]

    cvec = jnp.concatenate([c_ctx[None, :], c, jnp.zeros((MOD_ROWS - 1 - DEC_BATCH, D_MODEL), F32)], axis=0)
    mod = _modulation(cvec, w_mod, b_mod).reshape(DEPTH, MOD_ROWS, 6, 1, D_MODEL)

    hgrn_tables = _hgrn_tables()
    rope_b = _rope_tables(ROPE_B)
    rope_c = _rope_tables(HD_C)
    sink_b = jnp.broadcast_to(swa_sink[:, :, None], (DEPTH, H_C, LANE))

    x = jnp.concatenate([x_prompt.reshape(T_CTX, D_MODEL), x_sample.reshape(T_LAT, D_MODEL)], axis=0)
    new_hgrn, new_ckv, new_krope, new_k, new_v = [], [], [], [], []
    y = None
    for l in range(DEPTH):
        mod_l = mod[l]
        if l == 0:
            (h,) = _norm(x, npre=norm_pre_attn[l], mod_pre=mod_l, scale_idx=1, shift_idx=0)
        else:
            x, h = _norm(x, y=y, mod_post=mod[l - 1], gate_idx=5, npost=norm_post_ffn[l - 1],
                         npre=norm_pre_attn[l], mod_pre=mod_l, scale_idx=1, shift_idx=0)
        zab = _in_proj(h, wts["w_in_t"], l, 0, ZAB_W, "in_proj_ab")
        zcg = _in_proj(h, wts["w_in_t"], l, Z_B_END, ZC_G, "in_proj_c")
        zg = _in_proj(h, wts["w_in_t"], l, Z_B_END + ZC_G, N_BRANCH * D_MODEL, "in_proj_g", BF16)

        oa, s_ctx = _hgrn(zab, lb_all[l], hgrn_gnorm[l], hgrn_tables, latent=False)
        (oa,) = _hgrn(zab, lb_all[l], hgrn_gnorm[l], hgrn_tables, latent=True, s0=state_hgrn[:, l], prev=oa)
        new_hgrn.append(s_ctx)

        (qb,) = _norm_mm(zab, Z_B // Q_LORA, Q_LORA, mla_gq[l], wts["w_uq"], l, F32, emit_normed=False,
                         name="mla_q_proj")
        ckv, kvb = _norm_mm(zab, Z_KV // KV_LORA, KV_LORA, mla_gkv[l], wts["w_ukv"], l, BF16, emit_normed=True,
                            name="mla_kv_proj")
        kv_cache = _mm(cache_mla_ckv[:, l].reshape(DEC_BATCH * PAST_LEN, KV_LORA), wts["w_ukv"], l, BF16,
                       "mla_kv_cache")
        kr_cache = _pad_cols(cache_mla_krope[:, l].reshape(DEC_BATCH * PAST_LEN, ROPE_B), LANE)
        ob = _mla_attn(qb, kvb, zab, latent=False)
        ob = _mla_attn(qb, kvb, zab, latent=True, kvc=kv_cache, krc=kr_cache, tables=rope_b, prev=ob)
        new_ckv.append(ckv[:T_CTX].reshape(BATCH, SEQ, KV_LORA))
        new_krope.append(zab[:T_CTX, Z_KR:Z_B_END].reshape(BATCH, SEQ, ROPE_B))

        oc = _gqa_attn(zcg, sink_b[l], latent=False)
        oc = _gqa_attn(zcg, sink_b[l], latent=True,
                       kc=cache_swa_k[:, l].reshape(DEC_BATCH * PAST_LEN, KVH_C * HD_C),
                       vc=cache_swa_v[:, l].reshape(DEC_BATCH * PAST_LEN, KVH_C * HD_C), tables=rope_c, prev=oc)
        new_k.append(zcg[:T_CTX, ZC_K:ZC_V].reshape(BATCH, SEQ, KVH_C, HD_C))
        new_v.append(zcg[:T_CTX, ZC_V:ZC_G].reshape(BATCH, SEQ, KVH_C, HD_C))

        x, h = _merge(oa, ob, oc, wts["w_a"], wts["w_b"], wts["w_c"], zg, wts["w_o"], l,
                      x, mod_l, norm_post_attn[l], norm_pre_ffn[l])
        y = _ffn(h, wts["w_up"], wts["conv"], wts["w_down"], l)

    (x,) = _norm(x, y=y, mod_post=mod[DEPTH - 1], gate_idx=5, npost=norm_post_ffn[DEPTH - 1])
    return (x[:T_CTX].reshape(BATCH, SEQ, D_MODEL), x[T_CTX:].reshape(DEC_BATCH, DEC_SEQ, D_MODEL),
            jnp.stack(new_hgrn, axis=1), jnp.stack(new_ckv, axis=1), jnp.stack(new_krope, axis=1),
            jnp.stack(new_k, axis=1), jnp.stack(new_v, axis=1))
```

```python
import functools

import jax
import jax.numpy as jnp
import numpy as np
from jax import lax
from jax.experimental import pallas as pl
from jax.experimental.pallas import tpu as pltpu

F32 = jnp.float32
BF16 = jnp.bfloat16

D_MODEL = 2048
BATCH = 16
SEQ = 256
DEPTH = 4
DEC_BATCH = 4
DEC_SEQ = 1024
PAST_LEN = 256
GRID_W = 64
ROPE_BASE = 10000.0
EPS = 1e-6
NEG_INF = -1e30
H_A, DK_A, DV_A = 8, 128, 128
H_B, Q_LORA, KV_LORA, NOPE_B, ROPE_B, V_B = 8, 512, 256, 128, 64, 128
H_C, KVH_C, HD_C, WINDOW = 8, 2, 128, 128
N_BRANCH = 3
D_FF = 5504
CONV_W = 3

T_CTX = BATCH * SEQ
T_LAT = DEC_BATCH * DEC_SEQ
T_ALL = T_CTX + T_LAT
MOD_ROWS = 8
LANE = 128
CTX_SEQS = 4
FFN_TF = 512
FFN_TILES = -(-D_FF // FFN_TF)
HGRN_CHUNK = 64
HGRN_COARSE = 3
Z_A = 0
Z_B = 5 * H_A * DK_A
Z_KV = Z_B + Q_LORA
Z_KR = Z_KV + KV_LORA
Z_B_END = Z_KR + ROPE_B
ZAB_W = 6144
ZC_K = H_C * HD_C
ZC_V = ZC_K + KVH_C * HD_C
ZC_G = ZC_V + KVH_C * HD_C
ZCG_W = ZC_G + N_BRANCH * D_MODEL
VMEM_LIMIT = 56 * 1024 * 1024


def _params(*sem, flags=None):
    return pltpu.CompilerParams(dimension_semantics=sem, vmem_limit_bytes=VMEM_LIMIT, flags=flags)


def _mod_row(i, tm):
    return jnp.where(i * tm < T_CTX, 0, 1 + (i * tm - T_CTX) // DEC_SEQ)


def _rms(x):
    return x * lax.rsqrt(jnp.mean(x * x, axis=-1, keepdims=True) + EPS)


def _mod_kernel(c_ref, w_ref, b_ref, o_ref):
    cv = c_ref[...]
    s = (cv * jax.nn.sigmoid(cv)).astype(BF16)
    o_ref[0] = jnp.dot(s, w_ref[0].astype(BF16), preferred_element_type=F32) + b_ref[0]


def _modulation(cvec, w_mod, b_mod):
    tn = 1024
    n = 6 * D_MODEL
    return pl.pallas_call(
        _mod_kernel,
        grid=(DEPTH, n // tn),
        in_specs=[pl.BlockSpec((MOD_ROWS, D_MODEL), lambda l, j: (0, 0)),
                  pl.BlockSpec((1, D_MODEL, tn), lambda l, j: (l, 0, j)),
                  pl.BlockSpec((1, 1, tn), lambda l, j: (l, 0, j))],
        out_specs=pl.BlockSpec((1, MOD_ROWS, tn), lambda l, j: (l, 0, j)),
        out_shape=jax.ShapeDtypeStruct((DEPTH, MOD_ROWS, n), F32),
        compiler_params=_params("parallel", "parallel"),
        name="modulation",
    )(cvec, w_mod, b_mod.reshape(DEPTH, 1, n))


def _norm_kernel(*refs, has_y, has_h):
    it = iter(refs)
    x_ref = next(it)
    if has_y:
        y_ref, gate_ref, npost_ref = next(it), next(it), next(it)
    if has_h:
        npre_ref, scale_ref, shift_ref = next(it), next(it), next(it)
    x = x_ref[...]
    if has_y:
        xnew_ref = next(it)
        x = x + gate_ref[0, 0] * (_rms(y_ref[...]) * npost_ref[...])
        xnew_ref[...] = x
    if has_h:
        h_ref = next(it)
        h = (_rms(x) * npre_ref[...]) * (1.0 + scale_ref[0, 0]) + shift_ref[0, 0]
        h_ref[...] = h.astype(BF16)


def _norm(x, *, y=None, mod_post=None, gate_idx=None, npost=None,
          npre=None, mod_pre=None, scale_idx=None, shift_idx=None):
    tm = 512
    has_y, has_h = y is not None, npre is not None
    row = pl.BlockSpec((tm, D_MODEL), lambda i: (i, 0))
    vec = pl.BlockSpec((1, D_MODEL), lambda i: (0, 0))

    def modspec(k):
        return pl.BlockSpec((1, 1, 1, D_MODEL), lambda i: (_mod_row(i, tm), k, 0, 0))

    args, specs, out_shape, out_specs = [x], [row], [], []
    if has_y:
        args += [y, mod_post, npost.reshape(1, D_MODEL)]
        specs += [row, modspec(gate_idx), vec]
        out_shape.append(jax.ShapeDtypeStruct((T_ALL, D_MODEL), F32))
        out_specs.append(row)
    if has_h:
        args += [npre.reshape(1, D_MODEL), mod_pre, mod_pre]
        specs += [vec, modspec(scale_idx), modspec(shift_idx)]
        out_shape.append(jax.ShapeDtypeStruct((T_ALL, D_MODEL), BF16))
        out_specs.append(row)
    outs = pl.pallas_call(
        functools.partial(_norm_kernel, has_y=has_y, has_h=has_h),
        grid=(T_ALL // tm,),
        in_specs=specs, out_specs=out_specs, out_shape=out_shape,
        compiler_params=_params("parallel"),
        name="norm_y%d_h%d" % (has_y, has_h),
    )(*args)
    return outs


def _mm_kernel(x_ref, w_ref, o_ref):
    o_ref[...] = jnp.dot(x_ref[...].astype(BF16), w_ref[...].astype(BF16),
                         preferred_element_type=F32).astype(o_ref.dtype)


def _mm(x, w, l, out_dtype, name, n=None):
    m, k = x.shape
    n = w.shape[2] if n is None else n
    tm = min(m, 1024)
    tn = min(n, 512)
    return pl.pallas_call(
        _mm_kernel,
        grid=(m // tm, n // tn),
        in_specs=[pl.BlockSpec((tm, k), lambda i, j: (i, 0)),
                  pl.BlockSpec((None, k, tn), lambda i, j: (l, 0, j))],
        out_specs=pl.BlockSpec((tm, tn), lambda i, j: (i, j)),
        out_shape=jax.ShapeDtypeStruct((m, n), out_dtype),
        compiler_params=_params("parallel", "parallel"),
        name=name,
    )(x, w)


def _mm_nt_kernel(x_ref, wt_ref, o_ref):
    o_ref[...] = lax.dot_general(x_ref[...], wt_ref[...].astype(BF16), (((1,), (1,)), ((), ())),
                                 preferred_element_type=F32).astype(o_ref.dtype)


def _in_proj(h, w_t, l, col0, n, name, out_dtype=F32):
    m, k = h.shape
    tm, tn = 2048, 512
    row0 = l * w_t.shape[1] + col0
    return pl.pallas_call(
        _mm_nt_kernel,
        grid=(m // tm, n // tn),
        in_specs=[pl.BlockSpec((tm, k), lambda i, j: (i, 0)),
                  pl.BlockSpec((pl.Element(tn), pl.Element(k)),
                               lambda i, j: ((row0 // 8 + j * (tn // 8)) * 8, 0))],
        out_specs=pl.BlockSpec((tm, tn), lambda i, j: (i, j)),
        out_shape=jax.ShapeDtypeStruct((m, n), out_dtype),
        compiler_params=_params("parallel", "parallel"),
        name=name,
    )(h, w_t.reshape(-1, k))


def _norm_mm_kernel(x_ref, g_ref, w_ref, *out_refs, emit_normed):
    xn = _rms(x_ref[...]) * g_ref[...]
    if emit_normed:
        out_refs[0][...] = xn
    out_refs[-1][...] = jnp.dot(xn.astype(BF16), w_ref[...],
                                preferred_element_type=F32).astype(out_refs[-1].dtype)


def _norm_mm(z, col_block, k, gain, w, l, out_dtype, *, emit_normed, name):
    m = z.shape[0]
    n = w.shape[2]
    tm = 512
    out_shape = [jax.ShapeDtypeStruct((m, n), out_dtype)]
    out_specs = [pl.BlockSpec((tm, n), lambda i: (i, 0))]
    if emit_normed:
        out_shape.insert(0, jax.ShapeDtypeStruct((m, k), F32))
        out_specs.insert(0, pl.BlockSpec((tm, k), lambda i: (i, 0)))
    return pl.pallas_call(
        functools.partial(_norm_mm_kernel, emit_normed=emit_normed),
        grid=(m // tm,),
        in_specs=[pl.BlockSpec((tm, k), lambda i: (i, col_block)),
                  pl.BlockSpec((1, k), lambda i: (0, 0)),
                  pl.BlockSpec((None, k, n), lambda i: (l, 0, 0))],
        out_specs=out_specs, out_shape=out_shape,
        compiler_params=_params("parallel"),
        name=name,
    )(z, gain.reshape(1, k), w)


def _rope_tables(head_dim):
    n = DEC_SEQ
    half = head_dim // 2
    quarter = half // 2
    row = jnp.repeat(jnp.arange(n // GRID_W), GRID_W).astype(F32)
    col = jnp.tile(jnp.arange(GRID_W), n // GRID_W).astype(F32)
    inv = ROPE_BASE ** (-jnp.arange(0, half, 2, dtype=F32) / half)
    lane = jnp.arange(LANE)
    m = lane % half
    pos = jnp.where((lane // half)[None, :] == 0, row[:, None], col[:, None])
    ang = pos * inv[m % quarter][None, :]
    valid = (lane < head_dim)[None, :]
    cos = jnp.where(valid, jnp.cos(ang), 0.0)
    sin = jnp.where(valid, jnp.where(m < quarter, -1.0, 1.0)[None, :] * jnp.sin(ang), 0.0)
    return cos.astype(F32), sin.astype(F32)


def _rope(x, cos, sin, head_dim):
    quarter = head_dim // 4
    lane = lax.broadcasted_iota(jnp.int32, x.shape, 1)
    first = (lane % (2 * quarter)) < quarter
    partner = jnp.where(first, pltpu.roll(x, LANE - quarter, 1), pltpu.roll(x, quarter, 1))
    return x * cos + partner * sin


def _mla_attn_kernel(*refs, latent, n):
    kv_ref = refs[1]
    scale = (NOPE_B + ROPE_B) ** -0.5
    nt = (((1,), (1,)), ((), ()))
    for r0 in range(0, kv_ref.shape[0], n):
        rows = slice(r0, r0 + n)
        _mla_attn_seq(refs, latent, rows, rows if not latent else slice(None), scale, nt)


def _mla_attn_seq(refs, latent, krows, qrows, scale, nt):
    if latent:
        q_ref, kv_ref, kr_ref, kvc_ref, krc_ref, cq_ref, sq_ref, ck_ref, sk_ref, _, o_ref = refs
    else:
        q_ref, kv_ref, kr_ref, _, o_ref = refs
    kr = kr_ref[krows, :]
    if latent:
        kr = _rope(kr, ck_ref[...], sk_ref[...], ROPE_B)
        krc = krc_ref[...].astype(BF16)
    kr = kr.astype(BF16)
    for h in range(H_B):
        c0 = h * 2 * LANE
        qn = q_ref[qrows, c0:c0 + LANE]
        qr = q_ref[qrows, c0 + LANE:c0 + 2 * LANE]
        if latent:
            qr = _rope(qr, cq_ref[...], sq_ref[...], ROPE_B)
        qh = jnp.concatenate([(qn * scale).astype(BF16), (qr * scale).astype(BF16)], axis=-1)
        kh = jnp.concatenate([kv_ref[krows, c0:c0 + LANE], kr], axis=-1)
        vh = kv_ref[krows, c0 + LANE:c0 + 2 * LANE]
        s = lax.dot_general(qh, kh, nt, preferred_element_type=F32)
        m = jnp.max(s, axis=-1, keepdims=True)
        if latent:
            khc = jnp.concatenate([kvc_ref[:, c0:c0 + LANE], krc], axis=-1)
            vhc = kvc_ref[:, c0 + LANE:c0 + 2 * LANE]
            sc = lax.dot_general(qh, khc, nt, preferred_element_type=F32)
            m = jnp.maximum(m, jnp.max(sc, axis=-1, keepdims=True))
        p = jnp.exp(s - m)
        l = jnp.sum(p, axis=-1, keepdims=True)
        o = jnp.dot(p.astype(BF16), vh, preferred_element_type=F32)
        if latent:
            pc = jnp.exp(sc - m)
            l = l + jnp.sum(pc, axis=-1, keepdims=True)
            o = o + jnp.dot(pc.astype(BF16), vhc, preferred_element_type=F32)
        o_ref[qrows, h * LANE:(h + 1) * LANE] = (o / l).astype(BF16)


def _rows_of(specs, args, prev, width):
    if prev is None:
        prev = jnp.zeros((T_ALL, width), BF16)
    return specs + [pl.BlockSpec(memory_space=pl.ANY)], args + [prev], {len(args): 0}


def _mla_attn(q, kv, z, *, latent, kvc=None, krc=None, tables=None, prev=None):
    nb, n = (DEC_BATCH, DEC_SEQ) if latent else (BATCH // CTX_SEQS, SEQ)
    seqs = 1 if latent else CTX_SEQS
    tq = 256 * seqs
    nq = n * seqs // tq
    off = T_CTX // n if latent else 0
    offq = T_CTX // tq if latent else 0
    w = H_B * 2 * LANE
    specs = [pl.BlockSpec((tq, w), lambda b, i: (offq + b * nq + i, 0)),
             pl.BlockSpec((n * seqs, w), lambda b, i: (off + b, 0)),
             pl.BlockSpec((n * seqs, LANE), lambda b, i: (off + b, Z_KR // LANE))]
    args = [q, kv, z]
    if latent:
        cos, sin = tables
        specs += [pl.BlockSpec((PAST_LEN, w), lambda b, i: (b, 0)),
                  pl.BlockSpec((PAST_LEN, LANE), lambda b, i: (b, 0)),
                  pl.BlockSpec((tq, LANE), lambda b, i: (i, 0)),
                  pl.BlockSpec((tq, LANE), lambda b, i: (i, 0)),
                  pl.BlockSpec((n, LANE), lambda b, i: (0, 0)),
                  pl.BlockSpec((n, LANE), lambda b, i: (0, 0))]
        args += [kvc, krc, cos, sin, cos, sin]
    specs, args, aliases = _rows_of(specs, args, prev, H_B * V_B)
    return pl.pallas_call(
        functools.partial(_mla_attn_kernel, latent=latent, n=n),
        grid=(nb, nq),
        in_specs=specs,
        out_specs=pl.BlockSpec((tq, H_B * V_B), lambda b, i: (offq + b * nq + i, 0)),
        out_shape=jax.ShapeDtypeStruct((T_ALL, H_B * V_B), BF16),
        input_output_aliases=aliases,
        compiler_params=_params("parallel", "parallel"),
        name="mla_attn_lat" if latent else "mla_attn_ctx",
    )(*args)


def _gqa_attn_kernel(*refs, latent, tq):
    if latent:
        q_ref, k_ref, v_ref, kc_ref, vc_ref, sink_ref, cq_ref, sq_ref, ck_ref, sk_ref, _, o_ref = refs
    else:
        q_ref, k_ref, v_ref, sink_ref, _, o_ref = refs
    scale = HD_C ** -0.5
    nt = (((1,), (1,)), ((), ()))
    rep = H_C // KVH_C
    n = k_ref.shape[0]
    if latent:
        kw = tq + 2 * WINDOW
        q0 = pl.program_id(1) * tq
        k0 = pl.multiple_of(jnp.clip(q0 - WINDOW, 0, n - kw), WINDOW)
        keys = pl.ds(k0, kw)
        qpos = q0 + lax.broadcasted_iota(jnp.int32, (tq, kw), 0)
        kpos = k0 + lax.broadcasted_iota(jnp.int32, (tq, kw), 1)
        band = jnp.abs(qpos - kpos) <= WINDOW
        row_sets = [(keys, slice(None))]
    else:
        row_sets = [(slice(r0, r0 + SEQ),) * 2 for r0 in range(0, n, SEQ)]
    for keys, qrows in row_sets:
        _gqa_attn_seq(refs, latent, keys, qrows, band if latent else None, scale, nt, rep)


def _gqa_attn_seq(refs, latent, keys, qrows, band, scale, nt, rep):
    if latent:
        q_ref, k_ref, v_ref, kc_ref, vc_ref, sink_ref, cq_ref, sq_ref, ck_ref, sk_ref, _, o_ref = refs
    else:
        q_ref, k_ref, v_ref, sink_ref, _, o_ref = refs
    for g in range(KVH_C):
        kg = k_ref[keys, g * LANE:(g + 1) * LANE]
        if latent:
            kg = _rope(kg, ck_ref[keys, :], sk_ref[keys, :], HD_C)
            kcg = kc_ref[:, g * LANE:(g + 1) * LANE].astype(BF16)
            vcg = vc_ref[:, g * LANE:(g + 1) * LANE].astype(BF16)
        kg = kg.astype(BF16)
        vg = v_ref[keys, g * LANE:(g + 1) * LANE].astype(BF16)
        for r in range(rep):
            h = g * rep + r
            qh = q_ref[qrows, h * LANE:(h + 1) * LANE]
            if latent:
                qh = _rope(qh, cq_ref[...], sq_ref[...], HD_C)
            qh = (qh * scale).astype(BF16)
            sk = sink_ref[h:h + 1, 0:1]
            s = lax.dot_general(qh, kg, nt, preferred_element_type=F32)
            if latent:
                s = jnp.where(band, s, NEG_INF)
            m = jnp.maximum(jnp.max(s, axis=-1, keepdims=True), sk)
            if latent:
                sc = lax.dot_general(qh, kcg, nt, preferred_element_type=F32)
                m = jnp.maximum(m, jnp.max(sc, axis=-1, keepdims=True))
            p = jnp.exp(s - m)
            l = jnp.sum(p, axis=-1, keepdims=True) + jnp.exp(sk - m)
            o = jnp.dot(p.astype(BF16), vg, preferred_element_type=F32)
            if latent:
                pc = jnp.exp(sc - m)
                l = l + jnp.sum(pc, axis=-1, keepdims=True)
                o = o + jnp.dot(pc.astype(BF16), vcg, preferred_element_type=F32)
            o_ref[qrows, h * LANE:(h + 1) * LANE] = (o / l).astype(BF16)


def _gqa_attn(z, sink_b, *, latent, kc=None, vc=None, tables=None, prev=None):
    nb, n = (DEC_BATCH, DEC_SEQ) if latent else (BATCH // CTX_SEQS, SEQ * CTX_SEQS)
    tq = 256 if latent else n
    nq = n // tq
    off = T_CTX // n if latent else 0
    offq = T_CTX // tq if latent else 0
    wq, wk = H_C * HD_C, KVH_C * HD_C
    specs = [pl.BlockSpec((tq, wq), lambda b, i: (offq + b * nq + i, 0)),
             pl.BlockSpec((n, wk), lambda b, i: (off + b, ZC_K // wk)),
             pl.BlockSpec((n, wk), lambda b, i: (off + b, ZC_V // wk))]
    args = [z, z, z]
    if latent:
        specs += [pl.BlockSpec((PAST_LEN, wk), lambda b, i: (b, 0)),
                  pl.BlockSpec((PAST_LEN, wk), lambda b, i: (b, 0))]
        args += [kc, vc]
    specs.append(pl.BlockSpec((H_C, LANE), lambda b, i: (0, 0)))
    args.append(sink_b)
    if latent:
        cos, sin = tables
        specs += [pl.BlockSpec((tq, LANE), lambda b, i: (i, 0)),
                  pl.BlockSpec((tq, LANE), lambda b, i: (i, 0)),
                  pl.BlockSpec((n, LANE), lambda b, i: (0, 0)),
                  pl.BlockSpec((n, LANE), lambda b, i: (0, 0))]
        args += [cos, sin, cos, sin]
    specs, args, aliases = _rows_of(specs, args, prev, wq)
    return pl.pallas_call(
        functools.partial(_gqa_attn_kernel, latent=latent, tq=tq),
        grid=(nb, nq),
        in_specs=specs,
        out_specs=pl.BlockSpec((tq, wq), lambda b, i: (offq + b * nq + i, 0)),
        out_shape=jax.ShapeDtypeStruct((T_ALL, wq), BF16),
        input_output_aliases=aliases,
        compiler_params=_params("parallel", "parallel"),
        name="gqa_attn_lat" if latent else "gqa_attn_ctx",
    )(*args)


def _hgrn_tables():
    c = HGRN_CHUNK
    halves = [c >> (i + 1) for i in range(c.bit_length() - 1)]
    out = []
    for forward in (True, False):
        sums = np.zeros((len(halves) + 1, c, c), np.float32)
        level = np.full((c, c), -1, np.int32)
        level[np.arange(c), np.arange(c)] = 0
        for li, m in enumerate(halves):
            for r in range(c):
                pos = r % (2 * m)
                mid = r - pos + m
                late = pos >= m
                if forward:
                    lo, hi = (mid, r + 1) if late else (r + 1, mid)
                else:
                    lo, hi = (mid, r) if late else (r, mid)
                sums[li, r, lo:hi] = 1.0
                for s in range(r - pos, r - pos + 2 * m):
                    s_late = (s % (2 * m)) >= m
                    if (late and not s_late) if forward else (not late and s_late):
                        level[r, s] = li + 1
        for r in range(c):
            if forward:
                sums[-1, r, :r + 1] = 1.0
            else:
                sums[-1, r, r:] = 1.0
        sums = sums[HGRN_COARSE:].reshape(-1, c)
        out.append((jnp.asarray(np.concatenate([sums, sums, sums], axis=1), BF16),
                    jnp.asarray(np.concatenate([level, level], axis=1))))
    return out


def _hgrn_kernel(*refs, n, has_s0, emit_state):
    it = iter(refs)
    q_ref, xf_ref, xb_ref, v_ref, ag_ref, lb_ref, gn_ref = (next(it) for _ in range(7))
    sums_refs = (next(it), next(it))
    level_refs = (next(it), next(it))
    s0_ref = next(it) if has_s0 else None
    next(it)
    o_ref = next(it)
    sfin_ref = next(it) if emit_state else None
    o_scr, qe_scr, u_scr, e_scr, st_scr = (next(it) for _ in range(5))

    c = HGRN_CHUNK
    nc = n // c
    nlev = c.bit_length() - 1
    nt = (((1,), (1,)), ((), ()))
    tn = (((0,), (0,)), ((), ()))
    zero = jnp.zeros((c, LANE), BF16)

    def blockdiag(x):
        return jnp.concatenate([jnp.concatenate([x[:, :LANE], zero], axis=1),
                                jnp.concatenate([zero, x[:, LANE:]], axis=1)], axis=0)

    def gates(x, lb):
        e = jnp.exp(-jnp.abs(x))
        big = 1.0 / (1.0 + e)
        small = e * big
        pos = x >= 0.0
        return jnp.log(lb + (1.0 - lb) * jnp.where(pos, big, small)), (1.0 - lb) * jnp.where(pos, small, big)

    for d in range(2):
        for hh in range(2):
            st_scr[d, hh] = s0_ref[0, d, hh].T if has_s0 else jnp.zeros((DV_A, DK_A), F32)

    group = 4

    def intra(t, carry):
        jobs = [(u, d) for u in range(group) for d in range(2)]
        chunk_of = [t * group + u for u in range(group)]
        rows = {u: pl.ds(pl.multiple_of(chunk_of[u] * c, c), c) for u in range(group)}
        q = {ci: q_ref[rows[ci], :] for ci, _ in jobs}
        v = {ci: v_ref[rows[ci], :].astype(BF16) for ci, _ in jobs}
        k = {}

        dall = {}
        for ci, d in jobs:
            g, k[ci, d] = gates((xf_ref, xb_ref)[d][rows[ci], :], lb_ref[d:d + 1, :])
            g_hi = g.astype(BF16)
            rem = g - g_hi.astype(F32)
            g_mid = rem.astype(BF16)
            g_lo = (rem - g_mid.astype(F32)).astype(BF16)
            dall[ci, d] = jnp.dot(sums_refs[d][...], jnp.concatenate([g_hi, g_mid, g_lo], axis=0),
                                  preferred_element_type=F32)

        scores = {}
        for ci, d in jobs:
            rs = [lax.dot_general(q[ci].astype(BF16), blockdiag(k[ci, d].astype(BF16)), nt,
                                  preferred_element_type=F32)]
            for li in range(nlev):
                if li < HGRN_COARSE:
                    m = c >> (li + 1)
                    cum = dall[ci, d][(nlev - HGRN_COARSE) * c:, :]
                    ref = m - 1 if d == 0 else m
                    e = jnp.exp(-jnp.abs(jnp.concatenate(
                        [cum[b:b + 2 * m] - cum[b + ref:b + ref + 1] for b in range(0, c, 2 * m)], axis=0)))
                else:
                    lo = (li - HGRN_COARSE) * c
                    e = jnp.exp(dall[ci, d][lo:lo + c, :])
                rs.append(lax.dot_general((q[ci] * e).astype(BF16), blockdiag((k[ci, d] * e).astype(BF16)), nt,
                                          preferred_element_type=F32))
            scores[ci, d] = rs

        for ci, d in jobs:
            level = level_refs[d][...]
            a = jnp.where(level == 0, scores[ci, d][0], 0.0)
            for li in range(nlev):
                a = jnp.where(level == li + 1, scores[ci, d][li + 1], a)
            o_scr[d, rows[ci], :] = jnp.dot(a.astype(BF16), blockdiag(v[ci]), preferred_element_type=F32)
            gc = dall[ci, d][(nlev - HGRN_COARSE) * c:, :]
            g_end = gc[c - 1:c, :] if d == 0 else gc[0:1, :]
            qe_scr[d, rows[ci], :] = (q[ci] * jnp.exp(gc)).astype(BF16)
            kd = (k[ci, d] * jnp.exp(g_end - gc)).astype(BF16)
            e_scr[d, chunk_of[ci]] = jnp.broadcast_to(jnp.exp(g_end), (8, 2 * LANE))
            for hh in range(2):
                hl = slice(hh * LANE, (hh + 1) * LANE)
                u_scr[d, chunk_of[ci], hh] = lax.dot_general(v[ci][:, hl], kd[:, hl], tn,
                                                             preferred_element_type=F32)
        return carry

    lax.fori_loop(0, nc // group, intra, 0)

    def inter(i, carry):
        for d in range(2):
            ci = i if d == 0 else nc - 1 - i
            rows = pl.ds(pl.multiple_of(ci * c, c), c)
            e = e_scr[d, ci]
            for hh in range(2):
                hl = slice(hh * LANE, (hh + 1) * LANE)
                st = st_scr[d, hh]
                o_scr[d, rows, hl] += lax.dot_general(qe_scr[d, rows, hl], st.astype(BF16), nt,
                                                      preferred_element_type=F32)
                st_scr[d, hh] = st * e[0:1, hl] + u_scr[d, ci, hh]
        return carry

    lax.fori_loop(0, nc, inter, 0, unroll=4)

    def finish(i, carry):
        rows = pl.ds(pl.multiple_of(i * c, c), c)
        o = o_scr[0, rows, :] + o_scr[1, rows, :]
        o = jnp.concatenate([_rms(o[:, :LANE]), _rms(o[:, LANE:])], axis=1)
        ag = ag_ref[rows, :]
        o_ref[rows, :] = (o * gn_ref[...] * (ag * jax.nn.sigmoid(ag))).astype(BF16)
        return carry

    lax.fori_loop(0, nc, finish, 0, unroll=4)
    if emit_state:
        for d in range(2):
            for hh in range(2):
                sfin_ref[0, d, hh] = st_scr[d, hh].T


def _hgrn(z, lb_l, gnorm, tables, *, latent, s0=None, prev=None):
    nb, n = (DEC_BATCH, DEC_SEQ) if latent else (BATCH, SEQ)
    off = T_CTX // n if latent else 0
    emit_state = not latent
    w = 2 * LANE
    pairs = H_A // 2
    c = HGRN_CHUNK
    nc = n // c

    def zspec(k):
        return pl.BlockSpec((n, w), lambda b, p: (off + b, Z_A // w + k * pairs + p))

    def const(x):
        return pl.BlockSpec(x.shape, lambda b, p: (0, 0))

    (sums_f, level_f), (sums_b, level_b) = tables
    specs = [zspec(0), zspec(1), zspec(2), zspec(3), zspec(4),
             pl.BlockSpec((2, w), lambda b, p: (0, p)),
             pl.BlockSpec((1, w), lambda b, p: (0, 0)),
             const(sums_f), const(sums_b), const(level_f), const(level_b)]
    args = [z, z, z, z, z, lb_l, jnp.tile(gnorm.reshape(1, DV_A), (1, 2)),
            sums_f, sums_b, level_f, level_b]
    if latent:
        specs.append(pl.BlockSpec((1, 2, 2, DK_A, DV_A), lambda b, p: (b, 0, p, 0, 0)))
        args.append(s0)
    specs, args, aliases = _rows_of(specs, args, prev, H_A * DV_A)
    out_shape = [jax.ShapeDtypeStruct((T_ALL, H_A * DV_A), BF16)]
    out_specs = [pl.BlockSpec((n, w), lambda b, p: (off + b, p))]
    if emit_state:
        out_shape.append(jax.ShapeDtypeStruct((nb, 2, H_A, DK_A, DV_A), F32))
        out_specs.append(pl.BlockSpec((1, 2, 2, DK_A, DV_A), lambda b, p: (b, 0, p, 0, 0)))
    scratch = [pltpu.VMEM((2, n, w), F32),
               pltpu.VMEM((2, n, w), BF16),
               pltpu.VMEM((2, nc, 2, DV_A, DK_A), F32),
               pltpu.VMEM((2, nc, 8, w), F32),
               pltpu.VMEM((2, 2, DV_A, DK_A), F32)]
    return pl.pallas_call(
        functools.partial(_hgrn_kernel, n=n, has_s0=latent, emit_state=emit_state),
        grid=(nb, pairs),
        in_specs=specs, out_specs=out_specs, out_shape=out_shape,
        scratch_shapes=scratch,
        input_output_aliases=aliases,
        compiler_params=_params("parallel", "parallel"),
        name="hgrn_lat" if latent else "hgrn_ctx",
    )(*args)


def _merge_kernel(oa_ref, ob_ref, oc_ref, wa_ref, wb_ref, wc_ref, g0_ref, g1_ref, g2_ref, wo_ref,
                  x_ref, gate_ref, npost_ref, npre_ref, scale_ref, shift_ref, xnew_ref, h_ref, y_scr):
    j = pl.program_id(1)

    @pl.when(j == 0)
    def _():
        y_scr[...] = jnp.zeros_like(y_scr)

    def gate(g_ref):
        return jax.nn.sigmoid(g_ref[...].astype(F32))

    merged = (gate(g0_ref) * jnp.dot(oa_ref[...], wa_ref[...], preferred_element_type=F32)
              + gate(g1_ref) * jnp.dot(ob_ref[...], wb_ref[...], preferred_element_type=F32)
              + gate(g2_ref) * jnp.dot(oc_ref[...], wc_ref[...], preferred_element_type=F32))
    y_scr[...] += jnp.dot(merged.astype(BF16), wo_ref[...], preferred_element_type=F32)

    @pl.when(j == pl.num_programs(1) - 1)
    def _():
        x = x_ref[...] + gate_ref[0, 0] * (_rms(y_scr[...]) * npost_ref[...])
        xnew_ref[...] = x
        h_ref[...] = ((_rms(x) * npre_ref[...]) * (1.0 + scale_ref[0, 0]) + shift_ref[0, 0]).astype(BF16)


def _merge(oa, ob, oc, wa, wb, wc, zg, wo, l, x, mod_l, npost, npre):
    tm, tn = 512, 512
    nj = D_MODEL // tn
    kb = H_A * DV_A
    o_spec = pl.BlockSpec((tm, kb), lambda i, j: (i, 0))
    w_spec = pl.BlockSpec((None, kb, tn), lambda i, j: (l, 0, j))
    row = pl.BlockSpec((tm, D_MODEL), lambda i, j: (i, 0))
    vec = pl.BlockSpec((1, D_MODEL), lambda i, j: (0, 0))

    def gspec(k):
        return pl.BlockSpec((tm, tn), lambda i, j: (i, k * nj + j))

    def modspec(k):
        return pl.BlockSpec((1, 1, 1, D_MODEL), lambda i, j: (_mod_row(i, tm), k, 0, 0))

    return pl.pallas_call(
        _merge_kernel,
        grid=(T_ALL // tm, nj),
        in_specs=[o_spec, o_spec, o_spec, w_spec, w_spec, w_spec, gspec(0), gspec(1), gspec(2),
                  pl.BlockSpec((None, tn, D_MODEL), lambda i, j: (l, j, 0)),
                  row, modspec(2), vec, vec, modspec(4), modspec(3)],
        out_specs=[row, row],
        out_shape=[jax.ShapeDtypeStruct((T_ALL, D_MODEL), F32), jax.ShapeDtypeStruct((T_ALL, D_MODEL), BF16)],
        scratch_shapes=[pltpu.VMEM((tm, D_MODEL), F32)],
        compiler_params=_params("parallel", "arbitrary"),
        name="merge_out",
    )(oa, ob, oc, wa, wb, wc, zg, zg, zg, wo, x, mod_l, npost.reshape(1, D_MODEL), npre.reshape(1, D_MODEL),
      mod_l, mod_l)


def _ffn_kernel(h_ref, wa_ref, wg_ref, ca_ref, cg_ref, wd_ref, y_ref, *, tm):
    i = pl.program_id(0)

    @pl.when(pl.program_id(1) == 0)
    def _():
        y_ref[...] = jnp.zeros_like(y_ref)

    h = h_ref[...]
    seq_len = jnp.where(i * tm < T_CTX, SEQ, DEC_SEQ)
    pos = lax.broadcasted_iota(jnp.int32, (tm, 1), 0) & (seq_len - 1)
    has_prev = pos != 0
    has_next = pos != seq_len - 1

    def conv(u, c):
        prev = jnp.where(has_prev, pltpu.roll(u, 1, 0), 0.0)
        nxt = jnp.where(has_next, pltpu.roll(u, tm - 1, 0), 0.0)
        return c[0:1, :] * prev + c[1:2, :] * u + c[2:3, :] * nxt

    tf = wa_ref.shape[2]
    col = lax.broadcasted_iota(jnp.int32, (1, tf), 1)
    fresh = (pl.program_id(1) < pl.num_programs(1) - 1) | (col >= FFN_TILES * tf - D_FF)
    subs = (slice(0, tf // 2), slice(tf // 2, tf))
    ups = [(jnp.dot(h, wa_ref[0, :, cols], preferred_element_type=F32),
            jnp.dot(h, wg_ref[0, :, cols], preferred_element_type=F32)) for cols in subs]
    for cols, (ua, ug) in zip(subs, ups):
        act = conv(ua, ca_ref[:, cols]) * jax.nn.gelu(conv(ug, cg_ref[:, cols]))
        act = jnp.where(fresh[:, cols], act, 0.0).astype(BF16)
        y_ref[...] += jnp.dot(act, wd_ref[0, cols, :], preferred_element_type=F32)


def _ffn_tile_start(j, base=0):
    return LANE * (base // LANE + jnp.minimum(j * (FFN_TF // LANE), (D_FF - FFN_TF) // LANE))


def _ffn(h, w_up, conv_t, w_down, l):
    tm, tf, nj = 1024, FFN_TF, FFN_TILES
    one = pl.Element(1)
    return pl.pallas_call(
        functools.partial(_ffn_kernel, tm=tm),
        grid=(T_ALL // tm, nj),
        in_specs=[pl.BlockSpec((tm, D_MODEL), lambda i, j: (i, 0)),
                  pl.BlockSpec((one, pl.Element(D_MODEL), pl.Element(tf)), lambda i, j: (l, 0, _ffn_tile_start(j))),
                  pl.BlockSpec((one, pl.Element(D_MODEL), pl.Element(tf)),
                               lambda i, j: (l, 0, _ffn_tile_start(j, D_FF))),
                  pl.BlockSpec((None, CONV_W, tf), lambda i, j: (l, 0, j)),
                  pl.BlockSpec((None, CONV_W, tf), lambda i, j: (l, 0, nj + j)),
                  pl.BlockSpec((one, pl.Element(tf), pl.Element(D_MODEL)), lambda i, j: (l, _ffn_tile_start(j), 0))],
        out_specs=pl.BlockSpec((tm, D_MODEL), lambda i, j: (i, 0)),
        out_shape=jax.ShapeDtypeStruct((T_ALL, D_MODEL), F32),
        compiler_params=_params("parallel", "arbitrary"),
        name="conv_ffn",
    )(h, w_up, w_up, conv_t, conv_t, w_down)


def _pad_cols(w, n):
    return jnp.pad(w, [(0, 0)] * (w.ndim - 1) + [(0, n - w.shape[-1])])


def _prep_weights(w_in, mla_w_uq, mla_w_ukv, w_branch_a, w_branch_b, w_branch_c, w_out,
                  ffn_w_up, ffn_conv, ffn_w_down):
    w_uq = _pad_cols(mla_w_uq.reshape(DEPTH, Q_LORA, H_B, NOPE_B + ROPE_B), 2 * LANE)
    w_uq = w_uq.reshape(DEPTH, Q_LORA, H_B * 2 * LANE).astype(BF16)
    w_up = ffn_w_up.astype(BF16)
    w_down = ffn_w_down.astype(BF16)
    starts = [min(j * FFN_TF, D_FF - FFN_TF) for j in range(FFN_TILES)]
    conv = jnp.concatenate([ffn_conv[:, :, base + s:base + s + FFN_TF] for base in (0, D_FF) for s in starts],
                           axis=-1)
    return dict(w_in_t=jnp.swapaxes(w_in, 1, 2), w_uq=w_uq, w_ukv=mla_w_ukv.astype(BF16),
                w_a=w_branch_a.astype(BF16), w_b=w_branch_b.astype(BF16), w_c=w_branch_c.astype(BF16),
                w_o=w_out.astype(BF16), w_up=w_up, conv=conv, w_down=w_down)


def kernel(x_prompt, x_sample, state_hgrn, cache_mla_ckv, cache_mla_krope, cache_swa_k, cache_swa_v,
           c, c_ctx, w_mod, b_mod, norm_pre_attn, norm_post_attn, norm_pre_ffn, norm_post_ffn,
           w_in, hgrn_lb, hgrn_gnorm, mla_gq, mla_w_uq, mla_gkv, mla_w_ukv, swa_sink,
           w_branch_a, w_branch_b, w_branch_c, w_out, ffn_w_up, ffn_conv, ffn_w_down):
    wts = _prep_weights(w_in, mla_w_uq, mla_w_ukv, w_branch_a, w_branch_b, w_branch_c, w_out,
                        ffn_w_up, ffn_conv, ffn_w_down)
    cs = jnp.cumsum(jax.nn.softmax(hgrn_lb.astype(F32), axis=0), axis=0)
    lb_all = cs - cs[0]

    cvec = jnp.concatenate([c_ctx[None, :], c, jnp.zeros((MOD_ROWS - 1 - DEC_BATCH, D_MODEL), F32)], axis=0)
    mod = _modulation(cvec, w_mod, b_mod).reshape(DEPTH, MOD_ROWS, 6, 1, D_MODEL)

    hgrn_tables = _hgrn_tables()
    rope_b = _rope_tables(ROPE_B)
    rope_c = _rope_tables(HD_C)
    sink_b = jnp.broadcast_to(swa_sink[:, :, None], (DEPTH, H_C, LANE))

    x = jnp.concatenate([x_prompt.reshape(T_CTX, D_MODEL), x_sample.reshape(T_LAT, D_MODEL)], axis=0)
    new_hgrn, new_ckv, new_krope, new_k, new_v = [], [], [], [], []
    y = None
    for l in range(DEPTH):
        mod_l = mod[l]
        if l == 0:
            (h,) = _norm(x, npre=norm_pre_attn[l], mod_pre=mod_l, scale_idx=1, shift_idx=0)
        else:
            x, h = _norm(x, y=y, mod_post=mod[l - 1], gate_idx=5, npost=norm_post_ffn[l - 1],
                         npre=norm_pre_attn[l], mod_pre=mod_l, scale_idx=1, shift_idx=0)
        zab = _in_proj(h, wts["w_in_t"], l, 0, ZAB_W, "in_proj_ab")
        zcg = _in_proj(h, wts["w_in_t"], l, Z_B_END, ZC_G, "in_proj_c")
        zg = _in_proj(h, wts["w_in_t"], l, Z_B_END + ZC_G, N_BRANCH * D_MODEL, "in_proj_g", BF16)

        oa, s_ctx = _hgrn(zab, lb_all[l], hgrn_gnorm[l], hgrn_tables, latent=False)
        (oa,) = _hgrn(zab, lb_all[l], hgrn_gnorm[l], hgrn_tables, latent=True, s0=state_hgrn[:, l], prev=oa)
        new_hgrn.append(s_ctx)

        (qb,) = _norm_mm(zab, Z_B // Q_LORA, Q_LORA, mla_gq[l], wts["w_uq"], l, F32, emit_normed=False,
                         name="mla_q_proj")
        ckv, kvb = _norm_mm(zab, Z_KV // KV_LORA, KV_LORA, mla_gkv[l], wts["w_ukv"], l, BF16, emit_normed=True,
                            name="mla_kv_proj")
        kv_cache = _mm(cache_mla_ckv[:, l].reshape(DEC_BATCH * PAST_LEN, KV_LORA), wts["w_ukv"], l, BF16,
                       "mla_kv_cache")
        kr_cache = _pad_cols(cache_mla_krope[:, l].reshape(DEC_BATCH * PAST_LEN, ROPE_B), LANE)
        ob = _mla_attn(qb, kvb, zab, latent=False)
        ob = _mla_attn(qb, kvb, zab, latent=True, kvc=kv_cache, krc=kr_cache, tables=rope_b, prev=ob)
        new_ckv.append(ckv[:T_CTX].reshape(BATCH, SEQ, KV_LORA))
        new_krope.append(zab[:T_CTX, Z_KR:Z_B_END].reshape(BATCH, SEQ, ROPE_B))

        oc = _gqa_attn(zcg, sink_b[l], latent=False)
        oc = _gqa_attn(zcg, sink_b[l], latent=True,
                       kc=cache_swa_k[:, l].reshape(DEC_BATCH * PAST_LEN, KVH_C * HD_C),
                       vc=cache_swa_v[:, l].reshape(DEC_BATCH * PAST_LEN, KVH_C * HD_C), tables=rope_c, prev=oc)
        new_k.append(zcg[:T_CTX, ZC_K:ZC_V].reshape(BATCH, SEQ, KVH_C, HD_C))
        new_v.append(zcg[:T_CTX, ZC_V:ZC_G].reshape(BATCH, SEQ, KVH_C, HD_C))

        x, h = _merge(oa, ob, oc, wts["w_a"], wts["w_b"], wts["w_c"], zg, wts["w_o"], l,
                      x, mod_l, norm_post_attn[l], norm_pre_ffn[l])
        y = _ffn(h, wts["w_up"], wts["conv"], wts["w_down"], l)

    (x,) = _norm(x, y=y, mod_post=mod[DEPTH - 1], gate_idx=5, npost=norm_post_ffn[DEPTH - 1])
    return (x[:T_CTX].reshape(BATCH, SEQ, D_MODEL), x[T_CTX:].reshape(DEC_BATCH, DEC_SEQ, D_MODEL),
            jnp.stack(new_hgrn, axis=1), jnp.stack(new_ckv, axis=1), jnp.stack(new_krope, axis=1),
            jnp.stack(new_k, axis=1), jnp.stack(new_v, axis=1))
```

```python
import functools

import jax
import jax.numpy as jnp
import numpy as np
from jax import lax
from jax.experimental import pallas as pl
from jax.experimental.pallas import tpu as pltpu

F32 = jnp.float32
BF16 = jnp.bfloat16

D_MODEL = 2048
BATCH = 16
SEQ = 256
DEPTH = 4
DEC_BATCH = 4
DEC_SEQ = 1024
PAST_LEN = 256
GRID_W = 64
ROPE_BASE = 10000.0
EPS = 1e-6
NEG_INF = -1e30
H_A, DK_A, DV_A = 8, 128, 128
H_B, Q_LORA, KV_LORA, NOPE_B, ROPE_B, V_B = 8, 512, 256, 128, 64, 128
H_C, KVH_C, HD_C, WINDOW = 8, 2, 128, 128
N_BRANCH = 3
D_FF = 5504
CONV_W = 3

T_CTX = BATCH * SEQ
T_LAT = DEC_BATCH * DEC_SEQ
T_ALL = T_CTX + T_LAT
MOD_ROWS = 8
LANE = 128
CTX_SEQS = 4
FFN_TF = 512
FFN_TILES = -(-D_FF // FFN_TF)
HGRN_CHUNK = 64
HGRN_COARSE = 3
Z_A = 0
Z_B = 5 * H_A * DK_A
Z_KV = Z_B + Q_LORA
Z_KR = Z_KV + KV_LORA
Z_B_END = Z_KR + ROPE_B
ZAB_W = 6144
ZC_K = H_C * HD_C
ZC_V = ZC_K + KVH_C * HD_C
ZC_G = ZC_V + KVH_C * HD_C
ZCG_W = ZC_G + N_BRANCH * D_MODEL
VMEM_LIMIT = 56 * 1024 * 1024


def _params(*sem, flags=None):
    return pltpu.CompilerParams(dimension_semantics=sem, vmem_limit_bytes=VMEM_LIMIT, flags=flags)


def _mod_row(i, tm):
    return jnp.where(i * tm < T_CTX, 0, 1 + (i * tm - T_CTX) // DEC_SEQ)


def _rms(x):
    return x * lax.rsqrt(jnp.mean(x * x, axis=-1, keepdims=True) + EPS)


def _mod_kernel(c_ref, w_ref, b_ref, o_ref):
    cv = c_ref[...]
    s = (cv * jax.nn.sigmoid(cv)).astype(BF16)
    o_ref[0] = jnp.dot(s, w_ref[0].astype(BF16), preferred_element_type=F32) + b_ref[0]


def _modulation(cvec, w_mod, b_mod):
    tn = 1024
    n = 6 * D_MODEL
    return pl.pallas_call(
        _mod_kernel,
        grid=(DEPTH, n // tn),
        in_specs=[pl.BlockSpec((MOD_ROWS, D_MODEL), lambda l, j: (0, 0)),
                  pl.BlockSpec((1, D_MODEL, tn), lambda l, j: (l, 0, j)),
                  pl.BlockSpec((1, 1, tn), lambda l, j: (l, 0, j))],
        out_specs=pl.BlockSpec((1, MOD_ROWS, tn), lambda l, j: (l, 0, j)),
        out_shape=jax.ShapeDtypeStruct((DEPTH, MOD_ROWS, n), F32),
        compiler_params=_params("parallel", "parallel"),
        name="modulation",
    )(cvec, w_mod, b_mod.reshape(DEPTH, 1, n))


def _norm_kernel(*refs, has_y, has_h):
    it = iter(refs)
    x_ref = next(it)
    if has_y:
        y_ref, gate_ref, npost_ref = next(it), next(it), next(it)
    if has_h:
        npre_ref, scale_ref, shift_ref = next(it), next(it), next(it)
    x = x_ref[...]
    if has_y:
        xnew_ref = next(it)
        x = x + gate_ref[0, 0] * (_rms(y_ref[...]) * npost_ref[...])
        xnew_ref[...] = x
    if has_h:
        h_ref = next(it)
        h = (_rms(x) * npre_ref[...]) * (1.0 + scale_ref[0, 0]) + shift_ref[0, 0]
        h_ref[...] = h.astype(BF16)


def _norm(x, *, y=None, mod_post=None, gate_idx=None, npost=None,
          npre=None, mod_pre=None, scale_idx=None, shift_idx=None):
    tm = 512
    has_y, has_h = y is not None, npre is not None
    row = pl.BlockSpec((tm, D_MODEL), lambda i: (i, 0))
    vec = pl.BlockSpec((1, D_MODEL), lambda i: (0, 0))

    def modspec(k):
        return pl.BlockSpec((1, 1, 1, D_MODEL), lambda i: (_mod_row(i, tm), k, 0, 0))

    args, specs, out_shape, out_specs = [x], [row], [], []
    if has_y:
        args += [y, mod_post, npost.reshape(1, D_MODEL)]
        specs += [row, modspec(gate_idx), vec]
        out_shape.append(jax.ShapeDtypeStruct((T_ALL, D_MODEL), F32))
        out_specs.append(row)
    if has_h:
        args += [npre.reshape(1, D_MODEL), mod_pre, mod_pre]
        specs += [vec, modspec(scale_idx), modspec(shift_idx)]
        out_shape.append(jax.ShapeDtypeStruct((T_ALL, D_MODEL), BF16))
        out_specs.append(row)
    outs = pl.pallas_call(
        functools.partial(_norm_kernel, has_y=has_y, has_h=has_h),
        grid=(T_ALL // tm,),
        in_specs=specs, out_specs=out_specs, out_shape=out_shape,
        compiler_params=_params("parallel"),
        name="norm_y%d_h%d" % (has_y, has_h),
    )(*args)
    return outs


def _mm_kernel(x_ref, w_ref, o_ref):
    o_ref[...] = jnp.dot(x_ref[...].astype(BF16), w_ref[...].astype(BF16),
                         preferred_element_type=F32).astype(o_ref.dtype)


def _mm(x, w, l, out_dtype, name, n=None):
    m, k = x.shape
    n = w.shape[2] if n is None else n
    tm = min(m, 1024)
    tn = min(n, 512)
    return pl.pallas_call(
        _mm_kernel,
        grid=(m // tm, n // tn),
        in_specs=[pl.BlockSpec((tm, k), lambda i, j: (i, 0)),
                  pl.BlockSpec((None, k, tn), lambda i, j: (l, 0, j))],
        out_specs=pl.BlockSpec((tm, tn), lambda i, j: (i, j)),
        out_shape=jax.ShapeDtypeStruct((m, n), out_dtype),
        compiler_params=_params("parallel", "parallel"),
        name=name,
    )(x, w)


def _mm_nt_kernel(x_ref, wt_ref, o_ref):
    o_ref[...] = lax.dot_general(x_ref[...], wt_ref[...].astype(BF16), (((1,), (1,)), ((), ())),
                                 preferred_element_type=F32).astype(o_ref.dtype)


def _in_proj(h, w_t, l, col0, n, name, out_dtype=F32):
    m, k = h.shape
    tm, tn = 2048, 512
    row0 = l * w_t.shape[1] + col0
    nj = n // tn
    return pl.pallas_call(
        functools.partial(_in_proj_kernel, row0=row0, tn=tn, nj=nj, steps=(m // tm) * nj),
        grid=(m // tm, nj),
        in_specs=[pl.BlockSpec((tm, k), lambda i, j: (i, 0)),
                  pl.BlockSpec(memory_space=pl.ANY)],
        out_specs=pl.BlockSpec((tm, tn), lambda i, j: (i, j)),
        out_shape=jax.ShapeDtypeStruct((m, n), out_dtype),
        scratch_shapes=[pltpu.VMEM((W_RING, tn, k), w_t.dtype), pltpu.SemaphoreType.DMA((W_RING,))],
        compiler_params=_params("arbitrary", "arbitrary"),
        name=name,
    )(h, w_t.reshape(-1, k))


W_RING = 3


def _in_proj_kernel(x_ref, wt_hbm, o_ref, wbuf, sem, *, row0, tn, nj, steps):
    s = pl.program_id(0) * nj + pl.program_id(1)

    def tile_copy(step):
        start = (row0 // 8 + (step % nj) * (tn // 8)) * 8
        slot = step % W_RING
        return pltpu.make_async_copy(wt_hbm.at[pl.ds(start, tn), :], wbuf.at[slot], sem.at[slot])

    @pl.when(s == 0)
    def _():
        for first in range(min(W_RING - 1, steps)):
            tile_copy(first).start()

    @pl.when(s + W_RING - 1 < steps)
    def _():
        tile_copy(s + W_RING - 1).start()

    tile_copy(s).wait()
    o_ref[...] = lax.dot_general(x_ref[...], wbuf[s % W_RING].astype(BF16), (((1,), (1,)), ((), ())),
                                 preferred_element_type=F32).astype(o_ref.dtype)


def _norm_mm_kernel(x_ref, g_ref, w_ref, *out_refs, emit_normed):
    xn = _rms(x_ref[...]) * g_ref[...]
    if emit_normed:
        out_refs[0][...] = xn
    out_refs[-1][...] = jnp.dot(xn.astype(BF16), w_ref[...],
                                preferred_element_type=F32).astype(out_refs[-1].dtype)


def _norm_mm(z, col_block, k, gain, w, l, out_dtype, *, emit_normed, name):
    m = z.shape[0]
    n = w.shape[2]
    tm = 512
    out_shape = [jax.ShapeDtypeStruct((m, n), out_dtype)]
    out_specs = [pl.BlockSpec((tm, n), lambda i: (i, 0))]
    if emit_normed:
        out_shape.insert(0, jax.ShapeDtypeStruct((m, k), F32))
        out_specs.insert(0, pl.BlockSpec((tm, k), lambda i: (i, 0)))
    return pl.pallas_call(
        functools.partial(_norm_mm_kernel, emit_normed=emit_normed),
        grid=(m // tm,),
        in_specs=[pl.BlockSpec((tm, k), lambda i: (i, col_block)),
                  pl.BlockSpec((1, k), lambda i: (0, 0)),
                  pl.BlockSpec((None, k, n), lambda i: (l, 0, 0))],
        out_specs=out_specs, out_shape=out_shape,
        compiler_params=_params("parallel"),
        name=name,
    )(z, gain.reshape(1, k), w)


def _rope_tables(head_dim):
    n = DEC_SEQ
    half = head_dim // 2
    quarter = half // 2
    row = jnp.repeat(jnp.arange(n // GRID_W), GRID_W).astype(F32)
    col = jnp.tile(jnp.arange(GRID_W), n // GRID_W).astype(F32)
    inv = ROPE_BASE ** (-jnp.arange(0, half, 2, dtype=F32) / half)
    lane = jnp.arange(LANE)
    m = lane % half
    pos = jnp.where((lane // half)[None, :] == 0, row[:, None], col[:, None])
    ang = pos * inv[m % quarter][None, :]
    valid = (lane < head_dim)[None, :]
    cos = jnp.where(valid, jnp.cos(ang), 0.0)
    sin = jnp.where(valid, jnp.where(m < quarter, -1.0, 1.0)[None, :] * jnp.sin(ang), 0.0)
    return cos.astype(F32), sin.astype(F32)


def _rope(x, cos, sin, head_dim):
    quarter = head_dim // 4
    lane = lax.broadcasted_iota(jnp.int32, x.shape, 1)
    first = (lane % (2 * quarter)) < quarter
    partner = jnp.where(first, pltpu.roll(x, LANE - quarter, 1), pltpu.roll(x, quarter, 1))
    return x * cos + partner * sin


def _mla_attn_kernel(*refs, latent, n):
    kv_ref = refs[1]
    scale = (NOPE_B + ROPE_B) ** -0.5
    nt = (((1,), (1,)), ((), ()))
    for r0 in range(0, kv_ref.shape[0], n):
        rows = slice(r0, r0 + n)
        _mla_attn_seq(refs, latent, rows, rows if not latent else slice(None), scale, nt)


def _mla_attn_seq(refs, latent, krows, qrows, scale, nt):
    if latent:
        q_ref, kv_ref, kr_ref, kvc_ref, krc_ref, cq_ref, sq_ref, ck_ref, sk_ref, _, o_ref = refs
    else:
        q_ref, kv_ref, kr_ref, _, o_ref = refs
    kr = kr_ref[krows, :]
    if latent:
        kr = _rope(kr, ck_ref[...], sk_ref[...], ROPE_B)
        krc = krc_ref[...].astype(BF16)
    kr = kr.astype(BF16)
    for h in range(H_B):
        c0 = h * 2 * LANE
        qn = q_ref[qrows, c0:c0 + LANE]
        qr = q_ref[qrows, c0 + LANE:c0 + 2 * LANE]
        if latent:
            qr = _rope(qr, cq_ref[...], sq_ref[...], ROPE_B)
        qh = jnp.concatenate([(qn * scale).astype(BF16), (qr * scale).astype(BF16)], axis=-1)
        kh = jnp.concatenate([kv_ref[krows, c0:c0 + LANE], kr], axis=-1)
        vh = kv_ref[krows, c0 + LANE:c0 + 2 * LANE]
        s = lax.dot_general(qh, kh, nt, preferred_element_type=F32)
        m = jnp.max(s, axis=-1, keepdims=True)
        if latent:
            khc = jnp.concatenate([kvc_ref[:, c0:c0 + LANE], krc], axis=-1)
            vhc = kvc_ref[:, c0 + LANE:c0 + 2 * LANE]
            sc = lax.dot_general(qh, khc, nt, preferred_element_type=F32)
            m = jnp.maximum(m, jnp.max(sc, axis=-1, keepdims=True))
        p = jnp.exp(s - m)
        l = jnp.sum(p, axis=-1, keepdims=True)
        o = jnp.dot(p.astype(BF16), vh, preferred_element_type=F32)
        if latent:
            pc = jnp.exp(sc - m)
            l = l + jnp.sum(pc, axis=-1, keepdims=True)
            o = o + jnp.dot(pc.astype(BF16), vhc, preferred_element_type=F32)
        o_ref[qrows, h * LANE:(h + 1) * LANE] = (o / l).astype(BF16)


def _rows_of(specs, args, prev, width):
    if prev is None:
        prev = jnp.zeros((T_ALL, width), BF16)
    return specs + [pl.BlockSpec(memory_space=pl.ANY)], args + [prev], {len(args): 0}


def _mla_attn(q, kv, z, *, latent, kvc=None, krc=None, tables=None, prev=None):
    nb, n = (DEC_BATCH, DEC_SEQ) if latent else (BATCH // CTX_SEQS, SEQ)
    seqs = 1 if latent else CTX_SEQS
    tq = 256 * seqs
    nq = n * seqs // tq
    off = T_CTX // n if latent else 0
    offq = T_CTX // tq if latent else 0
    w = H_B * 2 * LANE
    specs = [pl.BlockSpec((tq, w), lambda b, i: (offq + b * nq + i, 0)),
             pl.BlockSpec((n * seqs, w), lambda b, i: (off + b, 0)),
             pl.BlockSpec((n * seqs, LANE), lambda b, i: (off + b, Z_KR // LANE))]
    args = [q, kv, z]
    if latent:
        cos, sin = tables
        specs += [pl.BlockSpec((PAST_LEN, w), lambda b, i: (b, 0)),
                  pl.BlockSpec((PAST_LEN, LANE), lambda b, i: (b, 0)),
                  pl.BlockSpec((tq, LANE), lambda b, i: (i, 0)),
                  pl.BlockSpec((tq, LANE), lambda b, i: (i, 0)),
                  pl.BlockSpec((n, LANE), lambda b, i: (0, 0)),
                  pl.BlockSpec((n, LANE), lambda b, i: (0, 0))]
        args += [kvc, krc, cos, sin, cos, sin]
    specs, args, aliases = _rows_of(specs, args, prev, H_B * V_B)
    return pl.pallas_call(
        functools.partial(_mla_attn_kernel, latent=latent, n=n),
        grid=(nb, nq),
        in_specs=specs,
        out_specs=pl.BlockSpec((tq, H_B * V_B), lambda b, i: (offq + b * nq + i, 0)),
        out_shape=jax.ShapeDtypeStruct((T_ALL, H_B * V_B), BF16),
        input_output_aliases=aliases,
        compiler_params=_params("parallel", "parallel"),
        name="mla_attn_lat" if latent else "mla_attn_ctx",
    )(*args)


def _gqa_attn_kernel(*refs, latent, tq):
    if latent:
        q_ref, k_ref, v_ref, kc_ref, vc_ref, sink_ref, cq_ref, sq_ref, ck_ref, sk_ref, _, o_ref = refs
    else:
        q_ref, k_ref, v_ref, sink_ref, _, o_ref = refs
    scale = HD_C ** -0.5
    nt = (((1,), (1,)), ((), ()))
    rep = H_C // KVH_C
    n = k_ref.shape[0]
    if latent:
        kw = tq + 2 * WINDOW
        q0 = pl.program_id(1) * tq
        k0 = pl.multiple_of(jnp.clip(q0 - WINDOW, 0, n - kw), WINDOW)
        keys = pl.ds(k0, kw)
        qpos = q0 + lax.broadcasted_iota(jnp.int32, (tq, kw), 0)
        kpos = k0 + lax.broadcasted_iota(jnp.int32, (tq, kw), 1)
        band = jnp.abs(qpos - kpos) <= WINDOW
        row_sets = [(keys, slice(None))]
    else:
        row_sets = [(slice(r0, r0 + SEQ),) * 2 for r0 in range(0, n, SEQ)]
    for keys, qrows in row_sets:
        _gqa_attn_seq(refs, latent, keys, qrows, band if latent else None, scale, nt, rep)


def _gqa_attn_seq(refs, latent, keys, qrows, band, scale, nt, rep):
    if latent:
        q_ref, k_ref, v_ref, kc_ref, vc_ref, sink_ref, cq_ref, sq_ref, ck_ref, sk_ref, _, o_ref = refs
    else:
        q_ref, k_ref, v_ref, sink_ref, _, o_ref = refs
    for g in range(KVH_C):
        kg = k_ref[keys, g * LANE:(g + 1) * LANE]
        if latent:
            kg = _rope(kg, ck_ref[keys, :], sk_ref[keys, :], HD_C)
            kcg = kc_ref[:, g * LANE:(g + 1) * LANE].astype(BF16)
            vcg = vc_ref[:, g * LANE:(g + 1) * LANE].astype(BF16)
        kg = kg.astype(BF16)
        vg = v_ref[keys, g * LANE:(g + 1) * LANE].astype(BF16)
        for r in range(rep):
            h = g * rep + r
            qh = q_ref[qrows, h * LANE:(h + 1) * LANE]
            if latent:
                qh = _rope(qh, cq_ref[...], sq_ref[...], HD_C)
            qh = (qh * scale).astype(BF16)
            sk = sink_ref[h:h + 1, 0:1]
            s = lax.dot_general(qh, kg, nt, preferred_element_type=F32)
            if latent:
                s = jnp.where(band, s, NEG_INF)
            m = jnp.maximum(jnp.max(s, axis=-1, keepdims=True), sk)
            if latent:
                sc = lax.dot_general(qh, kcg, nt, preferred_element_type=F32)
                m = jnp.maximum(m, jnp.max(sc, axis=-1, keepdims=True))
            p = jnp.exp(s - m)
            l = jnp.sum(p, axis=-1, keepdims=True) + jnp.exp(sk - m)
            o = jnp.dot(p.astype(BF16), vg, preferred_element_type=F32)
            if latent:
                pc = jnp.exp(sc - m)
                l = l + jnp.sum(pc, axis=-1, keepdims=True)
                o = o + jnp.dot(pc.astype(BF16), vcg, preferred_element_type=F32)
            o_ref[qrows, h * LANE:(h + 1) * LANE] = (o / l).astype(BF16)


def _gqa_attn(z, sink_b, *, latent, kc=None, vc=None, tables=None, prev=None):
    nb, n = (DEC_BATCH, DEC_SEQ) if latent else (BATCH // CTX_SEQS, SEQ * CTX_SEQS)
    tq = 256 if latent else n
    nq = n // tq
    off = T_CTX // n if latent else 0
    offq = T_CTX // tq if latent else 0
    wq, wk = H_C * HD_C, KVH_C * HD_C
    specs = [pl.BlockSpec((tq, wq), lambda b, i: (offq + b * nq + i, 0)),
             pl.BlockSpec((n, wk), lambda b, i: (off + b, ZC_K // wk)),
             pl.BlockSpec((n, wk), lambda b, i: (off + b, ZC_V // wk))]
    args = [z, z, z]
    if latent:
        specs += [pl.BlockSpec((PAST_LEN, wk), lambda b, i: (b, 0)),
                  pl.BlockSpec((PAST_LEN, wk), lambda b, i: (b, 0))]
        args += [kc, vc]
    specs.append(pl.BlockSpec((H_C, LANE), lambda b, i: (0, 0)))
    args.append(sink_b)
    if latent:
        cos, sin = tables
        specs += [pl.BlockSpec((tq, LANE), lambda b, i: (i, 0)),
                  pl.BlockSpec((tq, LANE), lambda b, i: (i, 0)),
                  pl.BlockSpec((n, LANE), lambda b, i: (0, 0)),
                  pl.BlockSpec((n, LANE), lambda b, i: (0, 0))]
        args += [cos, sin, cos, sin]
    specs, args, aliases = _rows_of(specs, args, prev, wq)
    return pl.pallas_call(
        functools.partial(_gqa_attn_kernel, latent=latent, tq=tq),
        grid=(nb, nq),
        in_specs=specs,
        out_specs=pl.BlockSpec((tq, wq), lambda b, i: (offq + b * nq + i, 0)),
        out_shape=jax.ShapeDtypeStruct((T_ALL, wq), BF16),
        input_output_aliases=aliases,
        compiler_params=_params("parallel", "parallel"),
        name="gqa_attn_lat" if latent else "gqa_attn_ctx",
    )(*args)


def _hgrn_tables():
    c = HGRN_CHUNK
    halves = [c >> (i + 1) for i in range(c.bit_length() - 1)]
    out = []
    for forward in (True, False):
        sums = np.zeros((len(halves) + 1, c, c), np.float32)
        level = np.full((c, c), -1, np.int32)
        level[np.arange(c), np.arange(c)] = 0
        for li, m in enumerate(halves):
            for r in range(c):
                pos = r % (2 * m)
                mid = r - pos + m
                late = pos >= m
                if forward:
                    lo, hi = (mid, r + 1) if late else (r + 1, mid)
                else:
                    lo, hi = (mid, r) if late else (r, mid)
                sums[li, r, lo:hi] = 1.0
                for s in range(r - pos, r - pos + 2 * m):
                    s_late = (s % (2 * m)) >= m
                    if (late and not s_late) if forward else (not late and s_late):
                        level[r, s] = li + 1
        for r in range(c):
            if forward:
                sums[-1, r, :r + 1] = 1.0
            else:
                sums[-1, r, r:] = 1.0
        sums = sums[HGRN_COARSE:].reshape(-1, c)
        out.append((jnp.asarray(np.concatenate([sums, sums, sums], axis=1), BF16),
                    jnp.asarray(np.concatenate([level, level], axis=1))))
    return out


def _hgrn_kernel(*refs, n, has_s0, emit_state):
    it = iter(refs)
    q_ref, xf_ref, xb_ref, v_ref, ag_ref, lb_ref, gn_ref = (next(it) for _ in range(7))
    sums_refs = (next(it), next(it))
    level_refs = (next(it), next(it))
    s0_ref = next(it) if has_s0 else None
    next(it)
    o_ref = next(it)
    sfin_ref = next(it) if emit_state else None
    o_scr, qe_scr, u_scr, e_scr, st_scr = (next(it) for _ in range(5))

    c = HGRN_CHUNK
    nc = n // c
    nlev = c.bit_length() - 1
    nt = (((1,), (1,)), ((), ()))
    tn = (((0,), (0,)), ((), ()))
    zero = jnp.zeros((c, LANE), BF16)

    def blockdiag(x):
        return jnp.concatenate([jnp.concatenate([x[:, :LANE], zero], axis=1),
                                jnp.concatenate([zero, x[:, LANE:]], axis=1)], axis=0)

    def gates(x, lb):
        e = jnp.exp(-jnp.abs(x))
        big = 1.0 / (1.0 + e)
        small = e * big
        pos = x >= 0.0
        return jnp.log(lb + (1.0 - lb) * jnp.where(pos, big, small)), (1.0 - lb) * jnp.where(pos, small, big)

    for d in range(2):
        for hh in range(2):
            st_scr[d, hh] = s0_ref[0, d, hh].T if has_s0 else jnp.zeros((DV_A, DK_A), F32)

    group = 4

    def intra(t, carry):
        jobs = [(u, d) for u in range(group) for d in range(2)]
        chunk_of = [t * group + u for u in range(group)]
        rows = {u: pl.ds(pl.multiple_of(chunk_of[u] * c, c), c) for u in range(group)}
        q = {ci: q_ref[rows[ci], :] for ci, _ in jobs}
        v = {ci: v_ref[rows[ci], :].astype(BF16) for ci, _ in jobs}
        k = {}

        dall = {}
        for ci, d in jobs:
            g, k[ci, d] = gates((xf_ref, xb_ref)[d][rows[ci], :], lb_ref[d:d + 1, :])
            g_hi = g.astype(BF16)
            rem = g - g_hi.astype(F32)
            g_mid = rem.astype(BF16)
            g_lo = (rem - g_mid.astype(F32)).astype(BF16)
            dall[ci, d] = jnp.dot(sums_refs[d][...], jnp.concatenate([g_hi, g_mid, g_lo], axis=0),
                                  preferred_element_type=F32)

        scores = {}
        for ci, d in jobs:
            rs = [lax.dot_general(q[ci].astype(BF16), blockdiag(k[ci, d].astype(BF16)), nt,
                                  preferred_element_type=F32)]
            for li in range(nlev):
                if li < HGRN_COARSE:
                    m = c >> (li + 1)
                    cum = dall[ci, d][(nlev - HGRN_COARSE) * c:, :]
                    ref = m - 1 if d == 0 else m
                    e = jnp.exp(-jnp.abs(jnp.concatenate(
                        [cum[b:b + 2 * m] - cum[b + ref:b + ref + 1] for b in range(0, c, 2 * m)], axis=0)))
                else:
                    lo = (li - HGRN_COARSE) * c
                    e = jnp.exp(dall[ci, d][lo:lo + c, :])
                rs.append(lax.dot_general((q[ci] * e).astype(BF16), blockdiag((k[ci, d] * e).astype(BF16)), nt,
                                          preferred_element_type=F32))
            scores[ci, d] = rs

        for ci, d in jobs:
            level = level_refs[d][...]
            a = jnp.where(level == 0, scores[ci, d][0], 0.0)
            for li in range(nlev):
                a = jnp.where(level == li + 1, scores[ci, d][li + 1], a)
            o_scr[d, rows[ci], :] = jnp.dot(a.astype(BF16), blockdiag(v[ci]), preferred_element_type=F32)
            gc = dall[ci, d][(nlev - HGRN_COARSE) * c:, :]
            g_end = gc[c - 1:c, :] if d == 0 else gc[0:1, :]
            qe_scr[d, rows[ci], :] = (q[ci] * jnp.exp(gc)).astype(BF16)
            kd = (k[ci, d] * jnp.exp(g_end - gc)).astype(BF16)
            e_scr[d, chunk_of[ci]] = jnp.broadcast_to(jnp.exp(g_end), (8, 2 * LANE))
            for hh in range(2):
                hl = slice(hh * LANE, (hh + 1) * LANE)
                u_scr[d, chunk_of[ci], hh] = lax.dot_general(v[ci][:, hl], kd[:, hl], tn,
                                                             preferred_element_type=F32)
        return carry

    lax.fori_loop(0, nc // group, intra, 0)

    def inter(i, carry):
        for d in range(2):
            ci = i if d == 0 else nc - 1 - i
            rows = pl.ds(pl.multiple_of(ci * c, c), c)
            e = e_scr[d, ci]
            for hh in range(2):
                hl = slice(hh * LANE, (hh + 1) * LANE)
                st = st_scr[d, hh]
                o_scr[d, rows, hl] += lax.dot_general(qe_scr[d, rows, hl], st.astype(BF16), nt,
                                                      preferred_element_type=F32)
                st_scr[d, hh] = st * e[0:1, hl] + u_scr[d, ci, hh]
        return carry

    lax.fori_loop(0, nc, inter, 0, unroll=4)

    def finish(i, carry):
        rows = pl.ds(pl.multiple_of(i * c, c), c)
        o = o_scr[0, rows, :] + o_scr[1, rows, :]
        o = jnp.concatenate([_rms(o[:, :LANE]), _rms(o[:, LANE:])], axis=1)
        ag = ag_ref[rows, :]
        o_ref[rows, :] = (o * gn_ref[...] * (ag * jax.nn.sigmoid(ag))).astype(BF16)
        return carry

    lax.fori_loop(0, nc, finish, 0, unroll=4)
    if emit_state:
        for d in range(2):
            for hh in range(2):
                sfin_ref[0, d, hh] = st_scr[d, hh].T


def _hgrn(z, lb_l, gnorm, tables, *, latent, s0=None, prev=None):
    nb, n = (DEC_BATCH, DEC_SEQ) if latent else (BATCH, SEQ)
    off = T_CTX // n if latent else 0
    emit_state = not latent
    w = 2 * LANE
    pairs = H_A // 2
    c = HGRN_CHUNK
    nc = n // c

    def zspec(k):
        return pl.BlockSpec((n, w), lambda b, p: (off + b, Z_A // w + k * pairs + p))

    def const(x):
        return pl.BlockSpec(x.shape, lambda b, p: (0, 0))

    (sums_f, level_f), (sums_b, level_b) = tables
    specs = [zspec(0), zspec(1), zspec(2), zspec(3), zspec(4),
             pl.BlockSpec((2, w), lambda b, p: (0, p)),
             pl.BlockSpec((1, w), lambda b, p: (0, 0)),
             const(sums_f), const(sums_b), const(level_f), const(level_b)]
    args = [z, z, z, z, z, lb_l, jnp.tile(gnorm.reshape(1, DV_A), (1, 2)),
            sums_f, sums_b, level_f, level_b]
    if latent:
        specs.append(pl.BlockSpec((1, 2, 2, DK_A, DV_A), lambda b, p: (b, 0, p, 0, 0)))
        args.append(s0)
    specs, args, aliases = _rows_of(specs, args, prev, H_A * DV_A)
    out_shape = [jax.ShapeDtypeStruct((T_ALL, H_A * DV_A), BF16)]
    out_specs = [pl.BlockSpec((n, w), lambda b, p: (off + b, p))]
    if emit_state:
        out_shape.append(jax.ShapeDtypeStruct((nb, 2, H_A, DK_A, DV_A), F32))
        out_specs.append(pl.BlockSpec((1, 2, 2, DK_A, DV_A), lambda b, p: (b, 0, p, 0, 0)))
    scratch = [pltpu.VMEM((2, n, w), F32),
               pltpu.VMEM((2, n, w), BF16),
               pltpu.VMEM((2, nc, 2, DV_A, DK_A), F32),
               pltpu.VMEM((2, nc, 8, w), F32),
               pltpu.VMEM((2, 2, DV_A, DK_A), F32)]
    return pl.pallas_call(
        functools.partial(_hgrn_kernel, n=n, has_s0=latent, emit_state=emit_state),
        grid=(nb, pairs),
        in_specs=specs, out_specs=out_specs, out_shape=out_shape,
        scratch_shapes=scratch,
        input_output_aliases=aliases,
        compiler_params=_params("parallel", "parallel"),
        name="hgrn_lat" if latent else "hgrn_ctx",
    )(*args)


def _merge_kernel(oa_ref, ob_ref, oc_ref, wa_ref, wb_ref, wc_ref, g0_ref, g1_ref, g2_ref, wo_ref,
                  x_ref, gate_ref, npost_ref, npre_ref, scale_ref, shift_ref, xnew_ref, h_ref, y_scr):
    j = pl.program_id(1)

    @pl.when(j == 0)
    def _():
        y_scr[...] = jnp.zeros_like(y_scr)

    def gate(g_ref):
        return jax.nn.sigmoid(g_ref[...].astype(F32))

    merged = (gate(g0_ref) * jnp.dot(oa_ref[...], wa_ref[...], preferred_element_type=F32)
              + gate(g1_ref) * jnp.dot(ob_ref[...], wb_ref[...], preferred_element_type=F32)
              + gate(g2_ref) * jnp.dot(oc_ref[...], wc_ref[...], preferred_element_type=F32))
    y_scr[...] += jnp.dot(merged.astype(BF16), wo_ref[...], preferred_element_type=F32)

    @pl.when(j == pl.num_programs(1) - 1)
    def _():
        x = x_ref[...] + gate_ref[0, 0] * (_rms(y_scr[...]) * npost_ref[...])
        xnew_ref[...] = x
        h_ref[...] = ((_rms(x) * npre_ref[...]) * (1.0 + scale_ref[0, 0]) + shift_ref[0, 0]).astype(BF16)


def _merge(oa, ob, oc, wa, wb, wc, zg, wo, l, x, mod_l, npost, npre):
    tm, tn = 512, 512
    nj = D_MODEL // tn
    kb = H_A * DV_A
    o_spec = pl.BlockSpec((tm, kb), lambda i, j: (i, 0))
    w_spec = pl.BlockSpec((None, kb, tn), lambda i, j: (l, 0, j))
    row = pl.BlockSpec((tm, D_MODEL), lambda i, j: (i, 0))
    vec = pl.BlockSpec((1, D_MODEL), lambda i, j: (0, 0))

    def gspec(k):
        return pl.BlockSpec((tm, tn), lambda i, j: (i, k * nj + j))

    def modspec(k):
        return pl.BlockSpec((1, 1, 1, D_MODEL), lambda i, j: (_mod_row(i, tm), k, 0, 0))

    return pl.pallas_call(
        _merge_kernel,
        grid=(T_ALL // tm, nj),
        in_specs=[o_spec, o_spec, o_spec, w_spec, w_spec, w_spec, gspec(0), gspec(1), gspec(2),
                  pl.BlockSpec((None, tn, D_MODEL), lambda i, j: (l, j, 0)),
                  row, modspec(2), vec, vec, modspec(4), modspec(3)],
        out_specs=[row, row],
        out_shape=[jax.ShapeDtypeStruct((T_ALL, D_MODEL), F32), jax.ShapeDtypeStruct((T_ALL, D_MODEL), BF16)],
        scratch_shapes=[pltpu.VMEM((tm, D_MODEL), F32)],
        compiler_params=_params("parallel", "arbitrary"),
        name="merge_out",
    )(oa, ob, oc, wa, wb, wc, zg, zg, zg, wo, x, mod_l, npost.reshape(1, D_MODEL), npre.reshape(1, D_MODEL),
      mod_l, mod_l)


def _ffn_kernel(h_ref, wa_ref, wg_ref, ca_ref, cg_ref, wd_ref, y_ref, *, tm):
    i = pl.program_id(0)

    @pl.when(pl.program_id(1) == 0)
    def _():
        y_ref[...] = jnp.zeros_like(y_ref)

    h = h_ref[...]
    seq_len = jnp.where(i * tm < T_CTX, SEQ, DEC_SEQ)
    pos = lax.broadcasted_iota(jnp.int32, (tm, 1), 0) & (seq_len - 1)
    has_prev = pos != 0
    has_next = pos != seq_len - 1

    def conv(u, c):
        prev = jnp.where(has_prev, pltpu.roll(u, 1, 0), 0.0)
        nxt = jnp.where(has_next, pltpu.roll(u, tm - 1, 0), 0.0)
        return c[0:1, :] * prev + c[1:2, :] * u + c[2:3, :] * nxt

    tf = wa_ref.shape[2]
    col = lax.broadcasted_iota(jnp.int32, (1, tf), 1)
    fresh = (pl.program_id(1) < pl.num_programs(1) - 1) | (col >= FFN_TILES * tf - D_FF)
    subs = (slice(0, tf // 2), slice(tf // 2, tf))
    ups = [(jnp.dot(h, wa_ref[0, :, cols], preferred_element_type=F32),
            jnp.dot(h, wg_ref[0, :, cols], preferred_element_type=F32)) for cols in subs]
    for cols, (ua, ug) in zip(subs, ups):
        act = conv(ua, ca_ref[:, cols]) * jax.nn.gelu(conv(ug, cg_ref[:, cols]))
        act = jnp.where(fresh[:, cols], act, 0.0).astype(BF16)
        y_ref[...] += jnp.dot(act, wd_ref[0, cols, :], preferred_element_type=F32)


def _ffn_tile_start(j, base=0):
    return LANE * (base // LANE + jnp.minimum(j * (FFN_TF // LANE), (D_FF - FFN_TF) // LANE))


def _ffn(h, w_up, conv_t, w_down, l):
    tm, tf, nj = 1024, FFN_TF, FFN_TILES
    one = pl.Element(1)
    return pl.pallas_call(
        functools.partial(_ffn_kernel, tm=tm),
        grid=(T_ALL // tm, nj),
        in_specs=[pl.BlockSpec((tm, D_MODEL), lambda i, j: (i, 0)),
                  pl.BlockSpec((one, pl.Element(D_MODEL), pl.Element(tf)), lambda i, j: (l, 0, _ffn_tile_start(j))),
                  pl.BlockSpec((one, pl.Element(D_MODEL), pl.Element(tf)),
                               lambda i, j: (l, 0, _ffn_tile_start(j, D_FF))),
                  pl.BlockSpec((None, CONV_W, tf), lambda i, j: (l, 0, j)),
                  pl.BlockSpec((None, CONV_W, tf), lambda i, j: (l, 0, nj + j)),
                  pl.BlockSpec((one, pl.Element(tf), pl.Element(D_MODEL)), lambda i, j: (l, _ffn_tile_start(j), 0))],
        out_specs=pl.BlockSpec((tm, D_MODEL), lambda i, j: (i, 0)),
        out_shape=jax.ShapeDtypeStruct((T_ALL, D_MODEL), F32),
        compiler_params=_params("parallel", "arbitrary"),
        name="conv_ffn",
    )(h, w_up, w_up, conv_t, conv_t, w_down)


def _pad_cols(w, n):
    return jnp.pad(w, [(0, 0)] * (w.ndim - 1) + [(0, n - w.shape[-1])])


def _prep_weights(w_in, mla_w_uq, mla_w_ukv, w_branch_a, w_branch_b, w_branch_c, w_out,
                  ffn_w_up, ffn_conv, ffn_w_down):
    w_uq = _pad_cols(mla_w_uq.reshape(DEPTH, Q_LORA, H_B, NOPE_B + ROPE_B), 2 * LANE)
    w_uq = w_uq.reshape(DEPTH, Q_LORA, H_B * 2 * LANE).astype(BF16)
    w_up = ffn_w_up.astype(BF16)
    w_down = ffn_w_down.astype(BF16)
    starts = [min(j * FFN_TF, D_FF - FFN_TF) for j in range(FFN_TILES)]
    conv = jnp.concatenate([ffn_conv[:, :, base + s:base + s + FFN_TF] for base in (0, D_FF) for s in starts],
                           axis=-1)
    return dict(w_in_t=jnp.swapaxes(w_in, 1, 2), w_uq=w_uq, w_ukv=mla_w_ukv.astype(BF16),
                w_a=w_branch_a.astype(BF16), w_b=w_branch_b.astype(BF16), w_c=w_branch_c.astype(BF16),
                w_o=w_out.astype(BF16), w_up=w_up, conv=conv, w_down=w_down)


def kernel(x_prompt, x_sample, state_hgrn, cache_mla_ckv, cache_mla_krope, cache_swa_k, cache_swa_v,
           c, c_ctx, w_mod, b_mod, norm_pre_attn, norm_post_attn, norm_pre_ffn, norm_post_ffn,
           w_in, hgrn_lb, hgrn_gnorm, mla_gq, mla_w_uq, mla_gkv, mla_w_ukv, swa_sink,
           w_branch_a, w_branch_b, w_branch_c, w_out, ffn_w_up, ffn_conv, ffn_w_down):
    wts = _prep_weights(w_in, mla_w_uq, mla_w_ukv, w_branch_a, w_branch_b, w_branch_c, w_out,
                        ffn_w_up, ffn_conv, ffn_w_down)
    cs = jnp.cumsum(jax.nn.softmax(hgrn_lb.astype(F32), axis=0), axis=0)
    lb_all = cs - cs[0]

    cvec = jnp.concatenate([c_ctx[None, :], c, jnp.zeros((MOD_ROWS - 1 - DEC_BATCH, D_MODEL), F32)], axis=0)
    mod = _modulation(cvec, w_mod, b_mod).reshape(DEPTH, MOD_ROWS, 6, 1, D_MODEL)

    hgrn_tables = _hgrn_tables()
    rope_b = _rope_tables(ROPE_B)
    rope_c = _rope_tables(HD_C)
    sink_b = jnp.broadcast_to(swa_sink[:, :, None], (DEPTH, H_C, LANE))

    x = jnp.concatenate([x_prompt.reshape(T_CTX, D_MODEL), x_sample.reshape(T_LAT, D_MODEL)], axis=0)
    new_hgrn, new_ckv, new_krope, new_k, new_v = [], [], [], [], []
    y = None
    for l in range(DEPTH):
        mod_l = mod[l]
        if l == 0:
            (h,) = _norm(x, npre=norm_pre_attn[l], mod_pre=mod_l, scale_idx=1, shift_idx=0)
        else:
            x, h = _norm(x, y=y, mod_post=mod[l - 1], gate_idx=5, npost=norm_post_ffn[l - 1],
                         npre=norm_pre_attn[l], mod_pre=mod_l, scale_idx=1, shift_idx=0)
        zab = _in_proj(h, wts["w_in_t"], l, 0, ZAB_W, "in_proj_ab")
        zcg = _in_proj(h, wts["w_in_t"], l, Z_B_END, ZC_G, "in_proj_c")
        zg = _in_proj(h, wts["w_in_t"], l, Z_B_END + ZC_G, N_BRANCH * D_MODEL, "in_proj_g", BF16)

        oa, s_ctx = _hgrn(zab, lb_all[l], hgrn_gnorm[l], hgrn_tables, latent=False)
        (oa,) = _hgrn(zab, lb_all[l], hgrn_gnorm[l], hgrn_tables, latent=True, s0=state_hgrn[:, l], prev=oa)
        new_hgrn.append(s_ctx)

        (qb,) = _norm_mm(zab, Z_B // Q_LORA, Q_LORA, mla_gq[l], wts["w_uq"], l, F32, emit_normed=False,
                         name="mla_q_proj")
        ckv, kvb = _norm_mm(zab, Z_KV // KV_LORA, KV_LORA, mla_gkv[l], wts["w_ukv"], l, BF16, emit_normed=True,
                            name="mla_kv_proj")
        kv_cache = _mm(cache_mla_ckv[:, l].reshape(DEC_BATCH * PAST_LEN, KV_LORA), wts["w_ukv"], l, BF16,
                       "mla_kv_cache")
        kr_cache = _pad_cols(cache_mla_krope[:, l].reshape(DEC_BATCH * PAST_LEN, ROPE_B), LANE)
        ob = _mla_attn(qb, kvb, zab, latent=False)
        ob = _mla_attn(qb, kvb, zab, latent=True, kvc=kv_cache, krc=kr_cache, tables=rope_b, prev=ob)
        new_ckv.append(ckv[:T_CTX].reshape(BATCH, SEQ, KV_LORA))
        new_krope.append(zab[:T_CTX, Z_KR:Z_B_END].reshape(BATCH, SEQ, ROPE_B))

        oc = _gqa_attn(zcg, sink_b[l], latent=False)
        oc = _gqa_attn(zcg, sink_b[l], latent=True,
                       kc=cache_swa_k[:, l].reshape(DEC_BATCH * PAST_LEN, KVH_C * HD_C),
                       vc=cache_swa_v[:, l].reshape(DEC_BATCH * PAST_LEN, KVH_C * HD_C), tables=rope_c, prev=oc)
        new_k.append(zcg[:T_CTX, ZC_K:ZC_V].reshape(BATCH, SEQ, KVH_C, HD_C))
        new_v.append(zcg[:T_CTX, ZC_V:ZC_G].reshape(BATCH, SEQ, KVH_C, HD_C))

        x, h = _merge(oa, ob, oc, wts["w_a"], wts["w_b"], wts["w_c"], zg, wts["w_o"], l,
                      x, mod_l, norm_post_attn[l], norm_pre_ffn[l])
        y = _ffn(h, wts["w_up"], wts["conv"], wts["w_down"], l)

    (x,) = _norm(x, y=y, mod_post=mod[DEPTH - 1], gate_idx=5, npost=norm_post_ffn[DEPTH - 1])
    return (x[:T_CTX].reshape(BATCH, SEQ, D_MODEL), x[T_CTX:].reshape(DEC_BATCH, DEC_SEQ, D_MODEL),
            jnp.stack(new_hgrn, axis=1), jnp.stack(new_ckv, axis=1), jnp.stack(new_krope, axis=1),
            jnp.stack(new_k, axis=1), jnp.stack(new_v, axis=1))
```

```python
import functools

import jax
import jax.numpy as jnp
import numpy as np
from jax import lax
from jax.experimental import pallas as pl
from jax.experimental.pallas import tpu as pltpu

F32 = jnp.float32
BF16 = jnp.bfloat16

D_MODEL = 2048
BATCH = 16
SEQ = 256
DEPTH = 4
DEC_BATCH = 4
DEC_SEQ = 1024
PAST_LEN = 256
GRID_W = 64
ROPE_BASE = 10000.0
EPS = 1e-6
NEG_INF = -1e30
H_A, DK_A, DV_A = 8, 128, 128
H_B, Q_LORA, KV_LORA, NOPE_B, ROPE_B, V_B = 8, 512, 256, 128, 64, 128
H_C, KVH_C, HD_C, WINDOW = 8, 2, 128, 128
N_BRANCH = 3
D_FF = 5504
CONV_W = 3

T_CTX = BATCH * SEQ
T_LAT = DEC_BATCH * DEC_SEQ
T_ALL = T_CTX + T_LAT
MOD_ROWS = 8
LANE = 128
CTX_SEQS = 4
FFN_TF = 512
FFN_TILES = -(-D_FF // FFN_TF)
HGRN_CHUNK = 64
HGRN_COARSE = 3
Z_A = 0
Z_B = 5 * H_A * DK_A
Z_KV = Z_B + Q_LORA
Z_KR = Z_KV + KV_LORA
Z_B_END = Z_KR + ROPE_B
ZAB_W = 6144
ZC_K = H_C * HD_C
ZC_V = ZC_K + KVH_C * HD_C
ZC_G = ZC_V + KVH_C * HD_C
ZCG_W = ZC_G + N_BRANCH * D_MODEL
VMEM_LIMIT = 56 * 1024 * 1024


def _params(*sem, flags=None):
    return pltpu.CompilerParams(dimension_semantics=sem, vmem_limit_bytes=VMEM_LIMIT, flags=flags)


def _mod_row(i, tm):
    return jnp.where(i * tm < T_CTX, 0, 1 + (i * tm - T_CTX) // DEC_SEQ)


def _rms(x):
    return x * lax.rsqrt(jnp.mean(x * x, axis=-1, keepdims=True) + EPS)


def _mod_kernel(c_ref, w_ref, b_ref, o_ref):
    cv = c_ref[...]
    s = (cv * jax.nn.sigmoid(cv)).astype(BF16)
    o_ref[0] = jnp.dot(s, w_ref[0].astype(BF16), preferred_element_type=F32) + b_ref[0]


def _modulation(cvec, w_mod, b_mod):
    tn = 1024
    n = 6 * D_MODEL
    return pl.pallas_call(
        _mod_kernel,
        grid=(DEPTH, n // tn),
        in_specs=[pl.BlockSpec((MOD_ROWS, D_MODEL), lambda l, j: (0, 0)),
                  pl.BlockSpec((1, D_MODEL, tn), lambda l, j: (l, 0, j)),
                  pl.BlockSpec((1, 1, tn), lambda l, j: (l, 0, j))],
        out_specs=pl.BlockSpec((1, MOD_ROWS, tn), lambda l, j: (l, 0, j)),
        out_shape=jax.ShapeDtypeStruct((DEPTH, MOD_ROWS, n), F32),
        compiler_params=_params("parallel", "parallel"),
        name="modulation",
    )(cvec, w_mod, b_mod.reshape(DEPTH, 1, n))


def _norm_kernel(*refs, has_y, has_h):
    it = iter(refs)
    x_ref = next(it)
    if has_y:
        y_ref, gate_ref, npost_ref = next(it), next(it), next(it)
    if has_h:
        npre_ref, scale_ref, shift_ref = next(it), next(it), next(it)
    x = x_ref[...]
    if has_y:
        xnew_ref = next(it)
        x = x + gate_ref[0, 0] * (_rms(y_ref[...]) * npost_ref[...])
        xnew_ref[...] = x
    if has_h:
        h_ref = next(it)
        h = (_rms(x) * npre_ref[...]) * (1.0 + scale_ref[0, 0]) + shift_ref[0, 0]
        h_ref[...] = h.astype(BF16)


def _norm(x, *, y=None, mod_post=None, gate_idx=None, npost=None,
          npre=None, mod_pre=None, scale_idx=None, shift_idx=None):
    tm = 512
    has_y, has_h = y is not None, npre is not None
    row = pl.BlockSpec((tm, D_MODEL), lambda i: (i, 0))
    vec = pl.BlockSpec((1, D_MODEL), lambda i: (0, 0))

    def modspec(k):
        return pl.BlockSpec((1, 1, 1, D_MODEL), lambda i: (_mod_row(i, tm), k, 0, 0))

    args, specs, out_shape, out_specs = [x], [row], [], []
    if has_y:
        args += [y, mod_post, npost.reshape(1, D_MODEL)]
        specs += [row, modspec(gate_idx), vec]
        out_shape.append(jax.ShapeDtypeStruct((T_ALL, D_MODEL), F32))
        out_specs.append(row)
    if has_h:
        args += [npre.reshape(1, D_MODEL), mod_pre, mod_pre]
        specs += [vec, modspec(scale_idx), modspec(shift_idx)]
        out_shape.append(jax.ShapeDtypeStruct((T_ALL, D_MODEL), BF16))
        out_specs.append(row)
    outs = pl.pallas_call(
        functools.partial(_norm_kernel, has_y=has_y, has_h=has_h),
        grid=(T_ALL // tm,),
        in_specs=specs, out_specs=out_specs, out_shape=out_shape,
        compiler_params=_params("parallel"),
        name="norm_y%d_h%d" % (has_y, has_h),
    )(*args)
    return outs


def _mm_kernel(x_ref, w_ref, o_ref):
    o_ref[...] = jnp.dot(x_ref[...].astype(BF16), w_ref[...].astype(BF16),
                         preferred_element_type=F32).astype(o_ref.dtype)


def _mm(x, w, l, out_dtype, name, n=None):
    m, k = x.shape
    n = w.shape[2] if n is None else n
    tm = min(m, 1024)
    tn = min(n, 512)
    return pl.pallas_call(
        _mm_kernel,
        grid=(m // tm, n // tn),
        in_specs=[pl.BlockSpec((tm, k), lambda i, j: (i, 0)),
                  pl.BlockSpec((None, k, tn), lambda i, j: (l, 0, j))],
        out_specs=pl.BlockSpec((tm, tn), lambda i, j: (i, j)),
        out_shape=jax.ShapeDtypeStruct((m, n), out_dtype),
        compiler_params=_params("parallel", "parallel"),
        name=name,
    )(x, w)


def _mm_nt_kernel(x_ref, wt_ref, o_ref):
    o_ref[...] = lax.dot_general(x_ref[...], wt_ref[...].astype(BF16), (((1,), (1,)), ((), ())),
                                 preferred_element_type=F32).astype(o_ref.dtype)


def _in_proj(h, w_t, l, col0, n, name, out_dtype=F32, col1=0, n1=0):
    m, k = h.shape
    tm, tn = 2048, 512
    row0 = l * w_t.shape[1] + col0
    nj0, skip = n // tn, (col1 - col0 - n) // 8
    n = n + n1
    return pl.pallas_call(
        _mm_nt_kernel,
        grid=(m // tm, n // tn),
        in_specs=[pl.BlockSpec((tm, k), lambda i, j: (i, 0)),
                  pl.BlockSpec((pl.Element(tn), pl.Element(k)),
                               lambda i, j: ((row0 // 8 + j * (tn // 8) + jnp.where(j < nj0, 0, skip)) * 8, 0))],
        out_specs=pl.BlockSpec((tm, tn), lambda i, j: (i, j)),
        out_shape=jax.ShapeDtypeStruct((m, n), out_dtype),
        compiler_params=_params("parallel", "parallel"),
        name=name,
    )(h, w_t.reshape(-1, k))


def _norm_mm_kernel(x_ref, g_ref, w_ref, *out_refs, emit_normed):
    xn = _rms(x_ref[...]) * g_ref[...]
    if emit_normed:
        out_refs[0][...] = xn
    out_refs[-1][...] = jnp.dot(xn.astype(BF16), w_ref[...],
                                preferred_element_type=F32).astype(out_refs[-1].dtype)


def _norm_mm(z, col_block, k, gain, w, l, out_dtype, *, emit_normed, name):
    m = z.shape[0]
    n = w.shape[2]
    tm = 512
    out_shape = [jax.ShapeDtypeStruct((m, n), out_dtype)]
    out_specs = [pl.BlockSpec((tm, n), lambda i: (i, 0))]
    if emit_normed:
        out_shape.insert(0, jax.ShapeDtypeStruct((m, k), F32))
        out_specs.insert(0, pl.BlockSpec((tm, k), lambda i: (i, 0)))
    return pl.pallas_call(
        functools.partial(_norm_mm_kernel, emit_normed=emit_normed),
        grid=(m // tm,),
        in_specs=[pl.BlockSpec((tm, k), lambda i: (i, col_block)),
                  pl.BlockSpec((1, k), lambda i: (0, 0)),
                  pl.BlockSpec((None, k, n), lambda i: (l, 0, 0))],
        out_specs=out_specs, out_shape=out_shape,
        compiler_params=_params("parallel"),
        name=name,
    )(z, gain.reshape(1, k), w)


def _rope_tables(head_dim):
    n = DEC_SEQ
    half = head_dim // 2
    quarter = half // 2
    row = jnp.repeat(jnp.arange(n // GRID_W), GRID_W).astype(F32)
    col = jnp.tile(jnp.arange(GRID_W), n // GRID_W).astype(F32)
    inv = ROPE_BASE ** (-jnp.arange(0, half, 2, dtype=F32) / half)
    lane = jnp.arange(LANE)
    m = lane % half
    pos = jnp.where((lane // half)[None, :] == 0, row[:, None], col[:, None])
    ang = pos * inv[m % quarter][None, :]
    valid = (lane < head_dim)[None, :]
    cos = jnp.where(valid, jnp.cos(ang), 0.0)
    sin = jnp.where(valid, jnp.where(m < quarter, -1.0, 1.0)[None, :] * jnp.sin(ang), 0.0)
    return cos.astype(F32), sin.astype(F32)


def _rope(x, cos, sin, head_dim):
    quarter = head_dim // 4
    lane = lax.broadcasted_iota(jnp.int32, x.shape, 1)
    first = (lane % (2 * quarter)) < quarter
    partner = jnp.where(first, pltpu.roll(x, LANE - quarter, 1), pltpu.roll(x, quarter, 1))
    return x * cos + partner * sin


def _mla_attn_kernel(*refs, latent, n):
    kv_ref = refs[1]
    scale = (NOPE_B + ROPE_B) ** -0.5
    nt = (((1,), (1,)), ((), ()))
    for r0 in range(0, kv_ref.shape[0], n):
        rows = slice(r0, r0 + n)
        _mla_attn_seq(refs, latent, rows, rows if not latent else slice(None), scale, nt)


def _mla_attn_seq(refs, latent, krows, qrows, scale, nt):
    if latent:
        q_ref, kv_ref, kr_ref, kvc_ref, krc_ref, cq_ref, sq_ref, ck_ref, sk_ref, _, o_ref = refs
    else:
        q_ref, kv_ref, kr_ref, _, o_ref = refs
    kr = kr_ref[krows, :]
    if latent:
        kr = _rope(kr, ck_ref[...], sk_ref[...], ROPE_B)
        krc = krc_ref[...].astype(BF16)
    kr = kr.astype(BF16)
    for h in range(H_B):
        c0 = h * 2 * LANE
        qn = q_ref[qrows, c0:c0 + LANE]
        qr = q_ref[qrows, c0 + LANE:c0 + 2 * LANE]
        if latent:
            qr = _rope(qr, cq_ref[...], sq_ref[...], ROPE_B)
        qh = jnp.concatenate([(qn * scale).astype(BF16), (qr * scale).astype(BF16)], axis=-1)
        kh = jnp.concatenate([kv_ref[krows, c0:c0 + LANE], kr], axis=-1)
        vh = kv_ref[krows, c0 + LANE:c0 + 2 * LANE]
        s = lax.dot_general(qh, kh, nt, preferred_element_type=F32)
        m = jnp.max(s, axis=-1, keepdims=True)
        if latent:
            khc = jnp.concatenate([kvc_ref[:, c0:c0 + LANE], krc], axis=-1)
            vhc = kvc_ref[:, c0 + LANE:c0 + 2 * LANE]
            sc = lax.dot_general(qh, khc, nt, preferred_element_type=F32)
            m = jnp.maximum(m, jnp.max(sc, axis=-1, keepdims=True))
        p = jnp.exp(s - m)
        l = jnp.sum(p, axis=-1, keepdims=True)
        o = jnp.dot(p.astype(BF16), vh, preferred_element_type=F32)
        if latent:
            pc = jnp.exp(sc - m)
            l = l + jnp.sum(pc, axis=-1, keepdims=True)
            o = o + jnp.dot(pc.astype(BF16), vhc, preferred_element_type=F32)
        o_ref[qrows, h * LANE:(h + 1) * LANE] = (o / l).astype(BF16)


def _rows_of(specs, args, prev, width):
    if prev is None:
        prev = jnp.zeros((T_ALL, width), BF16)
    return specs + [pl.BlockSpec(memory_space=pl.ANY)], args + [prev], {len(args): 0}


def _mla_attn(q, kv, z, *, latent, kvc=None, krc=None, tables=None, prev=None):
    nb, n = (DEC_BATCH, DEC_SEQ) if latent else (BATCH // CTX_SEQS, SEQ)
    seqs = 1 if latent else CTX_SEQS
    tq = 256 * seqs
    nq = n * seqs // tq
    off = T_CTX // n if latent else 0
    offq = T_CTX // tq if latent else 0
    w = H_B * 2 * LANE
    specs = [pl.BlockSpec((tq, w), lambda b, i: (offq + b * nq + i, 0)),
             pl.BlockSpec((n * seqs, w), lambda b, i: (off + b, 0)),
             pl.BlockSpec((n * seqs, LANE), lambda b, i: (off + b, Z_KR // LANE))]
    args = [q, kv, z]
    if latent:
        cos, sin = tables
        specs += [pl.BlockSpec((PAST_LEN, w), lambda b, i: (b, 0)),
                  pl.BlockSpec((PAST_LEN, LANE), lambda b, i: (b, 0)),
                  pl.BlockSpec((tq, LANE), lambda b, i: (i, 0)),
                  pl.BlockSpec((tq, LANE), lambda b, i: (i, 0)),
                  pl.BlockSpec((n, LANE), lambda b, i: (0, 0)),
                  pl.BlockSpec((n, LANE), lambda b, i: (0, 0))]
        args += [kvc, krc, cos, sin, cos, sin]
    specs, args, aliases = _rows_of(specs, args, prev, H_B * V_B)
    return pl.pallas_call(
        functools.partial(_mla_attn_kernel, latent=latent, n=n),
        grid=(nb, nq),
        in_specs=specs,
        out_specs=pl.BlockSpec((tq, H_B * V_B), lambda b, i: (offq + b * nq + i, 0)),
        out_shape=jax.ShapeDtypeStruct((T_ALL, H_B * V_B), BF16),
        input_output_aliases=aliases,
        compiler_params=_params("parallel", "parallel"),
        name="mla_attn_lat" if latent else "mla_attn_ctx",
    )(*args)


def _gqa_attn_kernel(*refs, latent, tq):
    if latent:
        q_ref, k_ref, v_ref, kc_ref, vc_ref, sink_ref, cq_ref, sq_ref, ck_ref, sk_ref, _, o_ref = refs
    else:
        q_ref, k_ref, v_ref, sink_ref, _, o_ref = refs
    scale = HD_C ** -0.5
    nt = (((1,), (1,)), ((), ()))
    rep = H_C // KVH_C
    n = k_ref.shape[0]
    if latent:
        kw = tq + 2 * WINDOW
        q0 = pl.program_id(1) * tq
        k0 = pl.multiple_of(jnp.clip(q0 - WINDOW, 0, n - kw), WINDOW)
        keys = pl.ds(k0, kw)
        qpos = q0 + lax.broadcasted_iota(jnp.int32, (tq, kw), 0)
        kpos = k0 + lax.broadcasted_iota(jnp.int32, (tq, kw), 1)
        band = jnp.abs(qpos - kpos) <= WINDOW
        row_sets = [(keys, slice(None))]
    else:
        row_sets = [(slice(r0, r0 + SEQ),) * 2 for r0 in range(0, n, SEQ)]
    for keys, qrows in row_sets:
        _gqa_attn_seq(refs, latent, keys, qrows, band if latent else None, scale, nt, rep)


def _gqa_attn_seq(refs, latent, keys, qrows, band, scale, nt, rep):
    if latent:
        q_ref, k_ref, v_ref, kc_ref, vc_ref, sink_ref, cq_ref, sq_ref, ck_ref, sk_ref, _, o_ref = refs
    else:
        q_ref, k_ref, v_ref, sink_ref, _, o_ref = refs
    for g in range(KVH_C):
        kg = k_ref[keys, g * LANE:(g + 1) * LANE]
        if latent:
            kg = _rope(kg, ck_ref[keys, :], sk_ref[keys, :], HD_C)
            kcg = kc_ref[:, g * LANE:(g + 1) * LANE].astype(BF16)
            vcg = vc_ref[:, g * LANE:(g + 1) * LANE].astype(BF16)
        kg = kg.astype(BF16)
        vg = v_ref[keys, g * LANE:(g + 1) * LANE].astype(BF16)
        for r in range(rep):
            h = g * rep + r
            qh = q_ref[qrows, h * LANE:(h + 1) * LANE]
            if latent:
                qh = _rope(qh, cq_ref[...], sq_ref[...], HD_C)
            qh = (qh * scale).astype(BF16)
            sk = sink_ref[h:h + 1, 0:1]
            s = lax.dot_general(qh, kg, nt, preferred_element_type=F32)
            if latent:
                s = jnp.where(band, s, NEG_INF)
            m = jnp.maximum(jnp.max(s, axis=-1, keepdims=True), sk)
            if latent:
                sc = lax.dot_general(qh, kcg, nt, preferred_element_type=F32)
                m = jnp.maximum(m, jnp.max(sc, axis=-1, keepdims=True))
            p = jnp.exp(s - m)
            l = jnp.sum(p, axis=-1, keepdims=True) + jnp.exp(sk - m)
            o = jnp.dot(p.astype(BF16), vg, preferred_element_type=F32)
            if latent:
                pc = jnp.exp(sc - m)
                l = l + jnp.sum(pc, axis=-1, keepdims=True)
                o = o + jnp.dot(pc.astype(BF16), vcg, preferred_element_type=F32)
            o_ref[qrows, h * LANE:(h + 1) * LANE] = (o / l).astype(BF16)


def _gqa_attn(z, sink_b, *, latent, kc=None, vc=None, tables=None, prev=None):
    nb, n = (DEC_BATCH, DEC_SEQ) if latent else (BATCH // CTX_SEQS, SEQ * CTX_SEQS)
    tq = 256 if latent else n
    nq = n // tq
    off = T_CTX // n if latent else 0
    offq = T_CTX // tq if latent else 0
    wq, wk = H_C * HD_C, KVH_C * HD_C
    specs = [pl.BlockSpec((tq, wq), lambda b, i: (offq + b * nq + i, ZAB_W // wq)),
             pl.BlockSpec((n, wk), lambda b, i: (off + b, (ZAB_W + ZC_K) // wk)),
             pl.BlockSpec((n, wk), lambda b, i: (off + b, (ZAB_W + ZC_V) // wk))]
    args = [z, z, z]
    if latent:
        specs += [pl.BlockSpec((PAST_LEN, wk), lambda b, i: (b, 0)),
                  pl.BlockSpec((PAST_LEN, wk), lambda b, i: (b, 0))]
        args += [kc, vc]
    specs.append(pl.BlockSpec((H_C, LANE), lambda b, i: (0, 0)))
    args.append(sink_b)
    if latent:
        cos, sin = tables
        specs += [pl.BlockSpec((tq, LANE), lambda b, i: (i, 0)),
                  pl.BlockSpec((tq, LANE), lambda b, i: (i, 0)),
                  pl.BlockSpec((n, LANE), lambda b, i: (0, 0)),
                  pl.BlockSpec((n, LANE), lambda b, i: (0, 0))]
        args += [cos, sin, cos, sin]
    specs, args, aliases = _rows_of(specs, args, prev, wq)
    return pl.pallas_call(
        functools.partial(_gqa_attn_kernel, latent=latent, tq=tq),
        grid=(nb, nq),
        in_specs=specs,
        out_specs=pl.BlockSpec((tq, wq), lambda b, i: (offq + b * nq + i, 0)),
        out_shape=jax.ShapeDtypeStruct((T_ALL, wq), BF16),
        input_output_aliases=aliases,
        compiler_params=_params("parallel", "parallel"),
        name="gqa_attn_lat" if latent else "gqa_attn_ctx",
    )(*args)


def _hgrn_tables():
    c = HGRN_CHUNK
    halves = [c >> (i + 1) for i in range(c.bit_length() - 1)]
    out = []
    for forward in (True, False):
        sums = np.zeros((len(halves) + 1, c, c), np.float32)
        level = np.full((c, c), -1, np.int32)
        level[np.arange(c), np.arange(c)] = 0
        for li, m in enumerate(halves):
            for r in range(c):
                pos = r % (2 * m)
                mid = r - pos + m
                late = pos >= m
                if forward:
                    lo, hi = (mid, r + 1) if late else (r + 1, mid)
                else:
                    lo, hi = (mid, r) if late else (r, mid)
                sums[li, r, lo:hi] = 1.0
                for s in range(r - pos, r - pos + 2 * m):
                    s_late = (s % (2 * m)) >= m
                    if (late and not s_late) if forward else (not late and s_late):
                        level[r, s] = li + 1
        for r in range(c):
            if forward:
                sums[-1, r, :r + 1] = 1.0
            else:
                sums[-1, r, r:] = 1.0
        sums = sums[HGRN_COARSE:].reshape(-1, c)
        out.append((jnp.asarray(np.concatenate([sums, sums, sums], axis=1), BF16),
                    jnp.asarray(np.concatenate([level, level], axis=1))))
    return out


def _hgrn_kernel(*refs, n, has_s0, emit_state):
    it = iter(refs)
    q_ref, xf_ref, xb_ref, v_ref, ag_ref, lb_ref, gn_ref = (next(it) for _ in range(7))
    sums_refs = (next(it), next(it))
    level_refs = (next(it), next(it))
    s0_ref = next(it) if has_s0 else None
    next(it)
    o_ref = next(it)
    sfin_ref = next(it) if emit_state else None
    o_scr, qe_scr, u_scr, e_scr, st_scr = (next(it) for _ in range(5))

    c = HGRN_CHUNK
    nc = n // c
    nlev = c.bit_length() - 1
    nt = (((1,), (1,)), ((), ()))
    tn = (((0,), (0,)), ((), ()))
    zero = jnp.zeros((c, LANE), BF16)

    def blockdiag(x):
        return jnp.concatenate([jnp.concatenate([x[:, :LANE], zero], axis=1),
                                jnp.concatenate([zero, x[:, LANE:]], axis=1)], axis=0)

    def gates(x, lb):
        e = jnp.exp(-jnp.abs(x))
        big = 1.0 / (1.0 + e)
        small = e * big
        pos = x >= 0.0
        return jnp.log(lb + (1.0 - lb) * jnp.where(pos, big, small)), (1.0 - lb) * jnp.where(pos, small, big)

    for d in range(2):
        for hh in range(2):
            st_scr[d, hh] = s0_ref[0, d, hh].T if has_s0 else jnp.zeros((DV_A, DK_A), F32)

    group = 4

    def intra(t, carry):
        jobs = [(u, d) for u in range(group) for d in range(2)]
        chunk_of = [t * group + u for u in range(group)]
        rows = {u: pl.ds(pl.multiple_of(chunk_of[u] * c, c), c) for u in range(group)}
        q = {ci: q_ref[rows[ci], :] for ci, _ in jobs}
        v = {ci: v_ref[rows[ci], :].astype(BF16) for ci, _ in jobs}
        k = {}

        dall = {}
        for ci, d in jobs:
            g, k[ci, d] = gates((xf_ref, xb_ref)[d][rows[ci], :], lb_ref[d:d + 1, :])
            g_hi = g.astype(BF16)
            rem = g - g_hi.astype(F32)
            g_mid = rem.astype(BF16)
            g_lo = (rem - g_mid.astype(F32)).astype(BF16)
            dall[ci, d] = jnp.dot(sums_refs[d][...], jnp.concatenate([g_hi, g_mid, g_lo], axis=0),
                                  preferred_element_type=F32)

        scores = {}
        for ci, d in jobs:
            rs = [lax.dot_general(q[ci].astype(BF16), blockdiag(k[ci, d].astype(BF16)), nt,
                                  preferred_element_type=F32)]
            for li in range(nlev):
                if li < HGRN_COARSE:
                    m = c >> (li + 1)
                    cum = dall[ci, d][(nlev - HGRN_COARSE) * c:, :]
                    ref = m - 1 if d == 0 else m
                    e = jnp.exp(-jnp.abs(jnp.concatenate(
                        [cum[b:b + 2 * m] - cum[b + ref:b + ref + 1] for b in range(0, c, 2 * m)], axis=0)))
                else:
                    lo = (li - HGRN_COARSE) * c
                    e = jnp.exp(dall[ci, d][lo:lo + c, :])
                rs.append(lax.dot_general((q[ci] * e).astype(BF16), blockdiag((k[ci, d] * e).astype(BF16)), nt,
                                          preferred_element_type=F32))
            scores[ci, d] = rs

        for ci, d in jobs:
            level = level_refs[d][...]
            a = jnp.where(level == 0, scores[ci, d][0], 0.0)
            for li in range(nlev):
                a = jnp.where(level == li + 1, scores[ci, d][li + 1], a)
            o_scr[d, rows[ci], :] = jnp.dot(a.astype(BF16), blockdiag(v[ci]), preferred_element_type=F32)
            gc = dall[ci, d][(nlev - HGRN_COARSE) * c:, :]
            g_end = gc[c - 1:c, :] if d == 0 else gc[0:1, :]
            qe_scr[d, rows[ci], :] = (q[ci] * jnp.exp(gc)).astype(BF16)
            kd = (k[ci, d] * jnp.exp(g_end - gc)).astype(BF16)
            e_scr[d, chunk_of[ci]] = jnp.broadcast_to(jnp.exp(g_end), (8, 2 * LANE))
            for hh in range(2):
                hl = slice(hh * LANE, (hh + 1) * LANE)
                u_scr[d, chunk_of[ci], hh] = lax.dot_general(v[ci][:, hl], kd[:, hl], tn,
                                                             preferred_element_type=F32)
        return carry

    lax.fori_loop(0, nc // group, intra, 0)

    def inter(i, carry):
        for d in range(2):
            ci = i if d == 0 else nc - 1 - i
            rows = pl.ds(pl.multiple_of(ci * c, c), c)
            e = e_scr[d, ci]
            for hh in range(2):
                hl = slice(hh * LANE, (hh + 1) * LANE)
                st = st_scr[d, hh]
                o_scr[d, rows, hl] += lax.dot_general(qe_scr[d, rows, hl], st.astype(BF16), nt,
                                                      preferred_element_type=F32)
                st_scr[d, hh] = st * e[0:1, hl] + u_scr[d, ci, hh]
        return carry

    lax.fori_loop(0, nc, inter, 0, unroll=4)

    def finish(i, carry):
        rows = pl.ds(pl.multiple_of(i * c, c), c)
        o = o_scr[0, rows, :] + o_scr[1, rows, :]
        o = jnp.concatenate([_rms(o[:, :LANE]), _rms(o[:, LANE:])], axis=1)
        ag = ag_ref[rows, :]
        o_ref[rows, :] = (o * gn_ref[...] * (ag * jax.nn.sigmoid(ag))).astype(BF16)
        return carry

    lax.fori_loop(0, nc, finish, 0, unroll=4)
    if emit_state:
        for d in range(2):
            for hh in range(2):
                sfin_ref[0, d, hh] = st_scr[d, hh].T


def _hgrn(z, lb_l, gnorm, tables, *, latent, s0=None, prev=None):
    nb, n = (DEC_BATCH, DEC_SEQ) if latent else (BATCH, SEQ)
    off = T_CTX // n if latent else 0
    emit_state = not latent
    w = 2 * LANE
    pairs = H_A // 2
    c = HGRN_CHUNK
    nc = n // c

    def zspec(k):
        return pl.BlockSpec((n, w), lambda b, p: (off + b, Z_A // w + k * pairs + p))

    def const(x):
        return pl.BlockSpec(x.shape, lambda b, p: (0, 0))

    (sums_f, level_f), (sums_b, level_b) = tables
    specs = [zspec(0), zspec(1), zspec(2), zspec(3), zspec(4),
             pl.BlockSpec((2, w), lambda b, p: (0, p)),
             pl.BlockSpec((1, w), lambda b, p: (0, 0)),
             const(sums_f), const(sums_b), const(level_f), const(level_b)]
    args = [z, z, z, z, z, lb_l, jnp.tile(gnorm.reshape(1, DV_A), (1, 2)),
            sums_f, sums_b, level_f, level_b]
    if latent:
        specs.append(pl.BlockSpec((1, 2, 2, DK_A, DV_A), lambda b, p: (b, 0, p, 0, 0)))
        args.append(s0)
    specs, args, aliases = _rows_of(specs, args, prev, H_A * DV_A)
    out_shape = [jax.ShapeDtypeStruct((T_ALL, H_A * DV_A), BF16)]
    out_specs = [pl.BlockSpec((n, w), lambda b, p: (off + b, p))]
    if emit_state:
        out_shape.append(jax.ShapeDtypeStruct((nb, 2, H_A, DK_A, DV_A), F32))
        out_specs.append(pl.BlockSpec((1, 2, 2, DK_A, DV_A), lambda b, p: (b, 0, p, 0, 0)))
    scratch = [pltpu.VMEM((2, n, w), F32),
               pltpu.VMEM((2, n, w), BF16),
               pltpu.VMEM((2, nc, 2, DV_A, DK_A), F32),
               pltpu.VMEM((2, nc, 8, w), F32),
               pltpu.VMEM((2, 2, DV_A, DK_A), F32)]
    return pl.pallas_call(
        functools.partial(_hgrn_kernel, n=n, has_s0=latent, emit_state=emit_state),
        grid=(nb, pairs),
        in_specs=specs, out_specs=out_specs, out_shape=out_shape,
        scratch_shapes=scratch,
        input_output_aliases=aliases,
        compiler_params=_params("parallel", "parallel"),
        name="hgrn_lat" if latent else "hgrn_ctx",
    )(*args)


def _merge_kernel(oa_ref, ob_ref, oc_ref, wa_ref, wb_ref, wc_ref, g0_ref, g1_ref, g2_ref, wo_ref,
                  x_ref, gate_ref, npost_ref, npre_ref, scale_ref, shift_ref, xnew_ref, h_ref, y_scr):
    j = pl.program_id(1)

    @pl.when(j == 0)
    def _():
        y_scr[...] = jnp.zeros_like(y_scr)

    def gate(g_ref):
        return jax.nn.sigmoid(g_ref[...].astype(F32))

    merged = (gate(g0_ref) * jnp.dot(oa_ref[...], wa_ref[...], preferred_element_type=F32)
              + gate(g1_ref) * jnp.dot(ob_ref[...], wb_ref[...], preferred_element_type=F32)
              + gate(g2_ref) * jnp.dot(oc_ref[...], wc_ref[...], preferred_element_type=F32))
    y_scr[...] += jnp.dot(merged.astype(BF16), wo_ref[...], preferred_element_type=F32)

    @pl.when(j == pl.num_programs(1) - 1)
    def _():
        x = x_ref[...] + gate_ref[0, 0] * (_rms(y_scr[...]) * npost_ref[...])
        xnew_ref[...] = x
        h_ref[...] = ((_rms(x) * npre_ref[...]) * (1.0 + scale_ref[0, 0]) + shift_ref[0, 0]).astype(BF16)


def _merge(oa, ob, oc, wa, wb, wc, zg, wo, l, x, mod_l, npost, npre):
    tm, tn = 512, 512
    nj = D_MODEL // tn
    kb = H_A * DV_A
    o_spec = pl.BlockSpec((tm, kb), lambda i, j: (i, 0))
    w_spec = pl.BlockSpec((None, kb, tn), lambda i, j: (l, 0, j))
    row = pl.BlockSpec((tm, D_MODEL), lambda i, j: (i, 0))
    vec = pl.BlockSpec((1, D_MODEL), lambda i, j: (0, 0))

    def gspec(k):
        return pl.BlockSpec((tm, tn), lambda i, j: (i, k * nj + j))

    def modspec(k):
        return pl.BlockSpec((1, 1, 1, D_MODEL), lambda i, j: (_mod_row(i, tm), k, 0, 0))

    return pl.pallas_call(
        _merge_kernel,
        grid=(T_ALL // tm, nj),
        in_specs=[o_spec, o_spec, o_spec, w_spec, w_spec, w_spec, gspec(0), gspec(1), gspec(2),
                  pl.BlockSpec((None, tn, D_MODEL), lambda i, j: (l, j, 0)),
                  row, modspec(2), vec, vec, modspec(4), modspec(3)],
        out_specs=[row, row],
        out_shape=[jax.ShapeDtypeStruct((T_ALL, D_MODEL), F32), jax.ShapeDtypeStruct((T_ALL, D_MODEL), BF16)],
        scratch_shapes=[pltpu.VMEM((tm, D_MODEL), F32)],
        compiler_params=_params("parallel", "arbitrary"),
        name="merge_out",
    )(oa, ob, oc, wa, wb, wc, zg, zg, zg, wo, x, mod_l, npost.reshape(1, D_MODEL), npre.reshape(1, D_MODEL),
      mod_l, mod_l)


def _ffn_kernel(h_ref, wa_ref, wg_ref, ca_ref, cg_ref, wd_ref, y_ref, *, tm):
    i = pl.program_id(0)

    @pl.when(pl.program_id(1) == 0)
    def _():
        y_ref[...] = jnp.zeros_like(y_ref)

    h = h_ref[...]
    seq_len = jnp.where(i * tm < T_CTX, SEQ, DEC_SEQ)
    pos = lax.broadcasted_iota(jnp.int32, (tm, 1), 0) & (seq_len - 1)
    has_prev = pos != 0
    has_next = pos != seq_len - 1

    def conv(u, c):
        prev = jnp.where(has_prev, pltpu.roll(u, 1, 0), 0.0)
        nxt = jnp.where(has_next, pltpu.roll(u, tm - 1, 0), 0.0)
        return c[0:1, :] * prev + c[1:2, :] * u + c[2:3, :] * nxt

    tf = wa_ref.shape[2]
    col = lax.broadcasted_iota(jnp.int32, (1, tf), 1)
    fresh = (pl.program_id(1) < pl.num_programs(1) - 1) | (col >= FFN_TILES * tf - D_FF)
    subs = (slice(0, tf // 2), slice(tf // 2, tf))
    ups = [(jnp.dot(h, wa_ref[0, :, cols], preferred_element_type=F32),
            jnp.dot(h, wg_ref[0, :, cols], preferred_element_type=F32)) for cols in subs]
    for cols, (ua, ug) in zip(subs, ups):
        act = conv(ua, ca_ref[:, cols]) * jax.nn.gelu(conv(ug, cg_ref[:, cols]))
        act = jnp.where(fresh[:, cols], act, 0.0).astype(BF16)
        y_ref[...] += jnp.dot(act, wd_ref[0, cols, :], preferred_element_type=F32)


def _ffn_tile_start(j, base=0):
    return LANE * (base // LANE + jnp.minimum(j * (FFN_TF // LANE), (D_FF - FFN_TF) // LANE))


def _ffn(h, w_up, conv_t, w_down, l):
    tm, tf, nj = 1024, FFN_TF, FFN_TILES
    one = pl.Element(1)
    return pl.pallas_call(
        functools.partial(_ffn_kernel, tm=tm),
        grid=(T_ALL // tm, nj),
        in_specs=[pl.BlockSpec((tm, D_MODEL), lambda i, j: (i, 0)),
                  pl.BlockSpec((one, pl.Element(D_MODEL), pl.Element(tf)), lambda i, j: (l, 0, _ffn_tile_start(j))),
                  pl.BlockSpec((one, pl.Element(D_MODEL), pl.Element(tf)),
                               lambda i, j: (l, 0, _ffn_tile_start(j, D_FF))),
                  pl.BlockSpec((None, CONV_W, tf), lambda i, j: (l, 0, j)),
                  pl.BlockSpec((None, CONV_W, tf), lambda i, j: (l, 0, nj + j)),
                  pl.BlockSpec((one, pl.Element(tf), pl.Element(D_MODEL)), lambda i, j: (l, _ffn_tile_start(j), 0))],
        out_specs=pl.BlockSpec((tm, D_MODEL), lambda i, j: (i, 0)),
        out_shape=jax.ShapeDtypeStruct((T_ALL, D_MODEL), F32),
        compiler_params=_params("parallel", "arbitrary"),
        name="conv_ffn",
    )(h, w_up, w_up, conv_t, conv_t, w_down)


def _pad_cols(w, n):
    return jnp.pad(w, [(0, 0)] * (w.ndim - 1) + [(0, n - w.shape[-1])])


def _prep_weights(w_in, mla_w_uq, mla_w_ukv, w_branch_a, w_branch_b, w_branch_c, w_out,
                  ffn_w_up, ffn_conv, ffn_w_down):
    w_uq = _pad_cols(mla_w_uq.reshape(DEPTH, Q_LORA, H_B, NOPE_B + ROPE_B), 2 * LANE)
    w_uq = w_uq.reshape(DEPTH, Q_LORA, H_B * 2 * LANE).astype(BF16)
    w_up = ffn_w_up.astype(BF16)
    w_down = ffn_w_down.astype(BF16)
    starts = [min(j * FFN_TF, D_FF - FFN_TF) for j in range(FFN_TILES)]
    conv = jnp.concatenate([ffn_conv[:, :, base + s:base + s + FFN_TF] for base in (0, D_FF) for s in starts],
                           axis=-1)
    return dict(w_in_t=jnp.swapaxes(w_in, 1, 2), w_uq=w_uq, w_ukv=mla_w_ukv.astype(BF16),
                w_a=w_branch_a.astype(BF16), w_b=w_branch_b.astype(BF16), w_c=w_branch_c.astype(BF16),
                w_o=w_out.astype(BF16), w_up=w_up, conv=conv, w_down=w_down)


def kernel(x_prompt, x_sample, state_hgrn, cache_mla_ckv, cache_mla_krope, cache_swa_k, cache_swa_v,
           c, c_ctx, w_mod, b_mod, norm_pre_attn, norm_post_attn, norm_pre_ffn, norm_post_ffn,
           w_in, hgrn_lb, hgrn_gnorm, mla_gq, mla_w_uq, mla_gkv, mla_w_ukv, swa_sink,
           w_branch_a, w_branch_b, w_branch_c, w_out, ffn_w_up, ffn_conv, ffn_w_down):
    wts = _prep_weights(w_in, mla_w_uq, mla_w_ukv, w_branch_a, w_branch_b, w_branch_c, w_out,
                        ffn_w_up, ffn_conv, ffn_w_down)
    cs = jnp.cumsum(jax.nn.softmax(hgrn_lb.astype(F32), axis=0), axis=0)
    lb_all = cs - cs[0]

    cvec = jnp.concatenate([c_ctx[None, :], c, jnp.zeros((MOD_ROWS - 1 - DEC_BATCH, D_MODEL), F32)], axis=0)
    mod = _modulation(cvec, w_mod, b_mod).reshape(DEPTH, MOD_ROWS, 6, 1, D_MODEL)

    hgrn_tables = _hgrn_tables()
    rope_b = _rope_tables(ROPE_B)
    rope_c = _rope_tables(HD_C)
    sink_b = jnp.broadcast_to(swa_sink[:, :, None], (DEPTH, H_C, LANE))

    x = jnp.concatenate([x_prompt.reshape(T_CTX, D_MODEL), x_sample.reshape(T_LAT, D_MODEL)], axis=0)
    new_hgrn, new_ckv, new_krope, new_k, new_v = [], [], [], [], []
    y = None
    for l in range(DEPTH):
        mod_l = mod[l]
        if l == 0:
            (h,) = _norm(x, npre=norm_pre_attn[l], mod_pre=mod_l, scale_idx=1, shift_idx=0)
        else:
            x, h = _norm(x, y=y, mod_post=mod[l - 1], gate_idx=5, npost=norm_post_ffn[l - 1],
                         npre=norm_pre_attn[l], mod_pre=mod_l, scale_idx=1, shift_idx=0)
        zab = _in_proj(h, wts["w_in_t"], l, 0, ZAB_W, "in_proj_abc", col1=Z_B_END, n1=ZC_G)
        zcg = zab
        zg = _in_proj(h, wts["w_in_t"], l, Z_B_END + ZC_G, N_BRANCH * D_MODEL, "in_proj_g", BF16)

        oa, s_ctx = _hgrn(zab, lb_all[l], hgrn_gnorm[l], hgrn_tables, latent=False)
        (oa,) = _hgrn(zab, lb_all[l], hgrn_gnorm[l], hgrn_tables, latent=True, s0=state_hgrn[:, l], prev=oa)
        new_hgrn.append(s_ctx)

        (qb,) = _norm_mm(zab, Z_B // Q_LORA, Q_LORA, mla_gq[l], wts["w_uq"], l, F32, emit_normed=False,
                         name="mla_q_proj")
        ckv, kvb = _norm_mm(zab, Z_KV // KV_LORA, KV_LORA, mla_gkv[l], wts["w_ukv"], l, BF16, emit_normed=True,
                            name="mla_kv_proj")
        kv_cache = _mm(cache_mla_ckv[:, l].reshape(DEC_BATCH * PAST_LEN, KV_LORA), wts["w_ukv"], l, BF16,
                       "mla_kv_cache")
        kr_cache = _pad_cols(cache_mla_krope[:, l].reshape(DEC_BATCH * PAST_LEN, ROPE_B), LANE)
        ob = _mla_attn(qb, kvb, zab, latent=False)
        ob = _mla_attn(qb, kvb, zab, latent=True, kvc=kv_cache, krc=kr_cache, tables=rope_b, prev=ob)
        new_ckv.append(ckv[:T_CTX].reshape(BATCH, SEQ, KV_LORA))
        new_krope.append(zab[:T_CTX, Z_KR:Z_B_END].reshape(BATCH, SEQ, ROPE_B))

        oc = _gqa_attn(zcg, sink_b[l], latent=False)
        oc = _gqa_attn(zcg, sink_b[l], latent=True,
                       kc=cache_swa_k[:, l].reshape(DEC_BATCH * PAST_LEN, KVH_C * HD_C),
                       vc=cache_swa_v[:, l].reshape(DEC_BATCH * PAST_LEN, KVH_C * HD_C), tables=rope_c, prev=oc)
        new_k.append(zcg[:T_CTX, ZAB_W + ZC_K:ZAB_W + ZC_V].reshape(BATCH, SEQ, KVH_C, HD_C))
        new_v.append(zcg[:T_CTX, ZAB_W + ZC_V:ZAB_W + ZC_G].reshape(BATCH, SEQ, KVH_C, HD_C))

        x, h = _merge(oa, ob, oc, wts["w_a"], wts["w_b"], wts["w_c"], zg, wts["w_o"], l,
                      x, mod_l, norm_post_attn[l], norm_pre_ffn[l])
        y = _ffn(h, wts["w_up"], wts["conv"], wts["w_down"], l)

    (x,) = _norm(x, y=y, mod_post=mod[DEPTH - 1], gate_idx=5, npost=norm_post_ffn[DEPTH - 1])
    return (x[:T_CTX].reshape(BATCH, SEQ, D_MODEL), x[T_CTX:].reshape(DEC_BATCH, DEC_SEQ, D_MODEL),
            jnp.stack(new_hgrn, axis=1), jnp.stack(new_ckv, axis=1), jnp.stack(new_krope, axis=1),
            jnp.stack(new_k, axis=1), jnp.stack(new_v, axis=1))
```
